```python
import math
import jax, jax.numpy as jnp
from jax import lax
import numpy as np

D_MODEL = 2048
BATCH = 2
SEQ = 4096
DEPTH = 2
DEC_BATCH = 128
DEC_SEQ = 4
PAST_LEN = 8192
PAGE_SIZE = 128

N_EVEN = (DEPTH + 1) // 2
N_ODD = DEPTH // 2
WINDOW = 128
HD_A = 64
H_A = (D_MODEL // 2) // HD_A
H_A_KV = max(1, H_A // 8)
G_A = H_A // H_A_KV
W_A = H_A * HD_A
NUM_BUCKETS = 32
MAX_DISTANCE = 128
H_B = 4
DV_B = (D_MODEL // 2) // H_B
DK_B = DV_B // 2
W_B = H_B * DV_B
GLA_RANK = 16
GLA_TAU = 16.0
GLA_CHUNK = 64
H_C = 8
DK_C = D_MODEL // H_C
DV_C = 2 * DK_C
W_C = H_C * DV_C
RET_CHUNK = 128
ROPE_BASE = 10000.0
EPS = 1e-6
NEG_INF = -1e30
EVEN_SPLITS = (W_A, H_A_KV * HD_A, H_A_KV * HD_A, W_A, H_B * DK_B, H_B * DK_B, W_B, W_B, GLA_RANK)
ODD_SPLITS = (H_C * DK_C, H_C * DK_C, W_C, W_C)

kernel_name = "hybrid_swa_gla_retention_decode_step"


def rms_norm(x, g):
    xf = x.astype(jnp.float32)
    y = xf * lax.rsqrt(jnp.mean(xf * xf, axis=-1, keepdims=True) + EPS)
    return (y * g.astype(jnp.float32)).astype(x.dtype)


def split_cols(x, sizes):
    idx = [int(i) for i in np.cumsum(sizes)[:-1]]
    return jnp.split(x, idx, axis=-1)


def ada_norm(x, c, ada_w, ada_b, norm_g):
    mod = jax.nn.silu(c) @ ada_w + ada_b
    shift, scale, gate = jnp.split(mod, 3, axis=-1)
    h = rms_norm(x, norm_g) * (1 + scale[:, None]) + shift[:, None]
    return h, gate[:, None]


def t5_bucket(dist):
    dist = jnp.maximum(dist, 0)
    max_exact = NUM_BUCKETS // 2
    log_ratio = jnp.log(jnp.maximum(dist, 1).astype(jnp.float32) / max_exact) / math.log(MAX_DISTANCE / max_exact)
    large = max_exact + (log_ratio * (NUM_BUCKETS - max_exact)).astype(jnp.int32)
    large = jnp.minimum(large, NUM_BUCKETS - 1)
    return jnp.where(dist < max_exact, dist, large)


def rel_bias_for(dist, rel_bias):
    b = rel_bias[t5_bucket(dist)].astype(jnp.float32)
    return jnp.moveaxis(b, -1, 0).reshape(H_A_KV, G_A, *dist.shape)


def sink_attention(q, k, v, bias, mask, sinks):
    s = jnp.einsum('...qkgd,...skd->...kgqs', q, k).astype(jnp.float32) * HD_A ** -0.5 + bias
    s = jnp.where(mask, s, NEG_INF)
    sink = sinks.astype(jnp.float32).reshape(H_A_KV, G_A)[:, :, None, None]
    m = jnp.maximum(jnp.max(s, axis=-1, keepdims=True), sink)
    p = jnp.exp(s - m)
    denom = jnp.sum(p, axis=-1, keepdims=True) + jnp.exp(sink - m)
    return jnp.einsum('...kgqs,...skd->...qkgd', (p / denom).astype(v.dtype), v)


def swa_prompt(q, k, v, sinks, rel_bias):
    b, length = q.shape[:2]
    nb = length // WINDOW
    qb = q.reshape(b, nb, WINDOW, H_A_KV, G_A, HD_A)
    kb = k.reshape(b, nb, WINDOW, H_A_KV, HD_A)
    vb = v.reshape(b, nb, WINDOW, H_A_KV, HD_A)
    pad = ((0, 0), (1, 0), (0, 0), (0, 0), (0, 0))
    kk = jnp.concatenate([jnp.pad(kb[:, :-1], pad), kb], axis=2)
    vv = jnp.concatenate([jnp.pad(vb[:, :-1], pad), vb], axis=2)
    i = jnp.arange(WINDOW)
    s = jnp.arange(2 * WINDOW)
    dist = WINDOW + i[:, None] - s[None, :]
    kpos = (jnp.arange(nb)[:, None] - 1) * WINDOW + s[None, :]
    mask = ((dist >= 0) & (dist <= WINDOW))[None] & (kpos >= 0)[:, None, :]
    bias = rel_bias_for(dist, rel_bias)
    o = sink_attention(qb, kk, vv, bias, mask[:, None, None], sinks)
    return o.reshape(b, length, W_A)


def swa_sample(q, k, v, win_k, win_v, sinks, rel_bias):
    bd, length = q.shape[:2]
    w_buf = win_k.shape[1]
    kk = jnp.concatenate([win_k, k], axis=1)
    vv = jnp.concatenate([win_v, v], axis=1)
    qpos = PAST_LEN + jnp.arange(length)
    kpos = PAST_LEN - w_buf + jnp.arange(w_buf + length)
    dist = qpos[:, None] - kpos[None, :]
    mask = (dist >= 0) & (dist <= WINDOW)
    bias = rel_bias_for(dist, rel_bias)
    o = sink_attention(q.reshape(bd, length, H_A_KV, G_A, HD_A), kk, vv, bias, mask, sinks)
    return o.reshape(bd, length, W_A), kk[:, -w_buf:], vv[:, -w_buf:]


def pick_chunk(length, chunk):
    return chunk if length % chunk == 0 else length


def run_chunks(step, s0, xs, chunk):
    b, length = xs[0].shape[:2]
    n = length // chunk
    xs_c = tuple(jnp.moveaxis(a.reshape(b, n, chunk, *a.shape[2:]), 1, 0) for a in xs)
    s, ys = lax.scan(lambda carry, blk: step(carry, *blk), s0, xs_c)
    ys = jnp.moveaxis(ys, 0, 1)
    return ys.reshape(b, length, *ys.shape[3:]), s


def gla_step(s, q, k, v, la):
    c = q.shape[1]
    bcum = jnp.cumsum(la, axis=1)
    qt = q * jnp.exp(bcum)
    kt = k * jnp.exp(-bcum)
    causal = jnp.tril(jnp.ones((c, c), dtype=bool))
    a = jnp.where(causal, jnp.einsum('bqhd,bshd->bhqs', qt, kt), 0.0)
    o = jnp.einsum('bhqs,bshe->bqhe', a, v) + jnp.einsum('bqhd,bhde->bqhe', qt, s)
    blast = bcum[:, -1]
    k_dec = k * jnp.exp(blast[:, None] - bcum)
    s_new = jnp.exp(blast)[..., None] * s + jnp.einsum('bshd,bshe->bhde', k_dec, v)
    return s_new, o


def retention_step(s, q, k, v, log_gamma):
    c = q.shape[1]
    i = jnp.arange(c, dtype=jnp.float32)
    dist = i[:, None] - i[None, :]
    decay = jnp.where(dist >= 0, jnp.exp(jnp.maximum(dist, 0.0)[None] * log_gamma[:, None, None]), 0.0)
    a = jnp.einsum('bqhd,bshd->bhqs', q, k) * decay
    inner = jnp.exp((i + 1.0)[:, None] * log_gamma[None, :])[None, :, :, None]
    o = jnp.einsum('bhqs,bshe->bqhe', a, v) + inner * jnp.einsum('bqhd,bhde->bqhe', q, s)
    k_dec = k * jnp.exp((c - 1.0 - i)[:, None] * log_gamma[None, :])[None, :, :, None]
    s_new = jnp.exp(c * log_gamma)[None, :, None, None] * s + jnp.einsum('bshd,bshe->bhde', k_dec, v)
    return s_new, o


def rotary(x, pos):
    half = x.shape[-1] // 2
    inv = ROPE_BASE ** (-jnp.arange(half, dtype=jnp.float32) / half)
    ang = pos.astype(jnp.float32)[:, None] * inv[None, :]
    cos = jnp.cos(ang)[None, :, None]
    sin = jnp.sin(ang)[None, :, None]
    xf = x.astype(jnp.float32)
    x1, x2 = xf[..., :half], xf[..., half:]
    return jnp.concatenate([x1 * cos - x2 * sin, x2 * cos + x1 * sin], axis=-1)


def even_layer(x, c, win_k, win_v, gla_s0, rel_bias, ada_w, ada_b, norm_g, w_in, w_lr, b_lr, qn_g, kn_g, sinks, gla_g, w_out):
    b, length = x.shape[:2]
    h, gate = ada_norm(x, c, ada_w, ada_b, norm_g)
    qa, ka, va, ga, qb, kb, vb, gb, lr = split_cols(h @ w_in, EVEN_SPLITS)
    qa = rms_norm(qa.reshape(b, length, H_A, HD_A), qn_g)
    ka = rms_norm(ka.reshape(b, length, H_A_KV, HD_A), kn_g)
    va = va.reshape(b, length, H_A_KV, HD_A)
    qb = qb.reshape(b, length, H_B, DK_B).astype(jnp.float32) * DK_B ** -0.5
    kb = kb.reshape(b, length, H_B, DK_B).astype(jnp.float32)
    vb = vb.reshape(b, length, H_B, DV_B).astype(jnp.float32)
    la = (jax.nn.log_sigmoid((lr @ w_lr + b_lr).astype(jnp.float32)) / GLA_TAU).reshape(b, length, H_B, DK_B)
    if win_k is None:
        oa = swa_prompt(qa, ka, va, sinks, rel_bias)
        new_k, new_v = ka[:, -WINDOW:], va[:, -WINDOW:]
        s0 = jnp.zeros((b, H_B, DK_B, DV_B), jnp.float32)
        state_dtype = x.dtype
    else:
        oa, new_k, new_v = swa_sample(qa, ka, va, win_k, win_v, sinks, rel_bias)
        s0 = gla_s0.astype(jnp.float32)
        state_dtype = gla_s0.dtype
    ob, s_new = run_chunks(gla_step, s0, (qb, kb, vb, la), pick_chunk(length, GLA_CHUNK))
    ob = rms_norm(ob, gla_g).astype(x.dtype).reshape(b, length, W_B)
    mixed = jnp.concatenate([oa * jax.nn.silu(ga), ob * jax.nn.silu(gb)], axis=-1)
    return x + gate * (mixed @ w_out), new_k, new_v, s_new.astype(state_dtype)


def odd_layer(x, c, ret_s0, pos, ada_w, ada_b, norm_g, w_in, ret_g, w_out):
    b, length = x.shape[:2]
    h, gate = ada_norm(x, c, ada_w, ada_b, norm_g)
    q, k, v, g = split_cols(h @ w_in, ODD_SPLITS)
    q = rotary(q.reshape(b, length, H_C, DK_C), pos)
    k = rotary(k.reshape(b, length, H_C, DK_C), pos) * DK_C ** -0.5
    v = v.reshape(b, length, H_C, DV_C).astype(jnp.float32)
    log_gamma = jnp.log1p(-jnp.exp2(-5.0 - jnp.arange(H_C, dtype=jnp.float32)))
    if ret_s0 is None:
        s0 = jnp.zeros((b, H_C, DK_C, DV_C), jnp.float32)
        state_dtype = x.dtype
    else:
        s0 = ret_s0.astype(jnp.float32)
        state_dtype = ret_s0.dtype
    step = lambda s, qc, kc, vc: retention_step(s, qc, kc, vc, log_gamma)
    o, s_new = run_chunks(step, s0, (q, k, v), pick_chunk(length, RET_CHUNK))
    o = rms_norm(o, ret_g).astype(x.dtype).reshape(b, length, W_C) * jax.nn.silu(g)
    return x + gate * (o @ w_out), s_new.astype(state_dtype)


def setup_inputs(seed: int = 0) -> dict:
    key = jax.random.key(seed)
    ks = iter(jax.random.split(key, 32))

    def nrm(shape, scale=1.0):
        return jax.random.normal(next(ks), shape, jnp.float32) * scale

    def gain(shape):
        return 1.0 + nrm(shape, 0.02)

    w_buf = min(WINDOW, PAST_LEN)
    d_in_even = sum(EVEN_SPLITS)
    d_in_odd = sum(ODD_SPLITS)
    return {
        "x_prompt": nrm((BATCH, SEQ, D_MODEL)),
        "x_sample": nrm((DEC_BATCH, DEC_SEQ, D_MODEL)),
        "cache_swa_k": nrm((N_EVEN, DEC_BATCH, w_buf, H_A_KV, HD_A)),
        "cache_swa_v": nrm((N_EVEN, DEC_BATCH, w_buf, H_A_KV, HD_A)),
        "state_gla": nrm((N_EVEN, DEC_BATCH, H_B, DK_B, DV_B), 0.5),
        "state_ret": nrm((N_ODD, DEC_BATCH, H_C, DK_C, DV_C), 0.5),
        "c_prompt": nrm((BATCH, D_MODEL)),
        "c_sample": nrm((DEC_BATCH, D_MODEL)),
        "rel_bias": nrm((NUM_BUCKETS, H_A), 0.5),
        "ada_w_even": nrm((N_EVEN, D_MODEL, 3 * D_MODEL), 0.5 * D_MODEL ** -0.5),
        "ada_b_even": nrm((N_EVEN, 3 * D_MODEL), 0.02),
        "norm_g_even": gain((N_EVEN, D_MODEL)),
        "w_in_even": nrm((N_EVEN, D_MODEL, d_in_even), D_MODEL ** -0.5),
        "w_lr_even": nrm((N_EVEN, GLA_RANK, H_B * DK_B), GLA_RANK ** -0.5),
        "b_lr_even": nrm((N_EVEN, H_B * DK_B), 0.02),
        "qn_g_even": gain((N_EVEN, HD_A)),
        "kn_g_even": gain((N_EVEN, HD_A)),
        "sinks_even": nrm((N_EVEN, H_A), 0.5),
        "gla_g_even": gain((N_EVEN, DV_B)),
        "w_out_even": nrm((N_EVEN, W_A + W_B, D_MODEL), (W_A + W_B) ** -0.5),
        "ada_w_odd": nrm((N_ODD, D_MODEL, 3 * D_MODEL), 0.5 * D_MODEL ** -0.5),
        "ada_b_odd": nrm((N_ODD, 3 * D_MODEL), 0.02),
        "norm_g_odd": gain((N_ODD, D_MODEL)),
        "w_in_odd": nrm((N_ODD, D_MODEL, d_in_odd), D_MODEL ** -0.5),
        "ret_g_odd": gain((N_ODD, DV_C)),
        "w_out_odd": nrm((N_ODD, W_C, D_MODEL), W_C ** -0.5),
    }


def reference(x_prompt, x_sample, cache_swa_k, cache_swa_v, state_gla, state_ret, c_prompt, c_sample, rel_bias,
              ada_w_even, ada_b_even, norm_g_even, w_in_even, w_lr_even, b_lr_even, qn_g_even, kn_g_even,
              sinks_even, gla_g_even, w_out_even, ada_w_odd, ada_b_odd, norm_g_odd, w_in_odd, ret_g_odd, w_out_odd):
    pos_prompt = jnp.arange(x_prompt.shape[1])
    pos_sample = PAST_LEN + jnp.arange(x_sample.shape[1])
    y_prompt, y_sample = x_prompt, x_sample
    swa_k_p, swa_v_p, gla_p, ret_p = [], [], [], []
    swa_k_s, swa_v_s, gla_s, ret_s = [], [], [], []
    for layer in range(DEPTH):
        j = layer // 2
        if layer % 2 == 0:
            ew = (ada_w_even[j], ada_b_even[j], norm_g_even[j], w_in_even[j], w_lr_even[j], b_lr_even[j],
                  qn_g_even[j], kn_g_even[j], sinks_even[j], gla_g_even[j], w_out_even[j])
            y_prompt, kp, vp, sp = even_layer(y_prompt, c_prompt, None, None, None, rel_bias, *ew)
            y_sample, ksm, vsm, ssm = even_layer(y_sample, c_sample, cache_swa_k[j], cache_swa_v[j], state_gla[j], rel_bias, *ew)
            swa_k_p.append(kp); swa_v_p.append(vp); gla_p.append(sp)
            swa_k_s.append(ksm); swa_v_s.append(vsm); gla_s.append(ssm)
        else:
            ow = (ada_w_odd[j], ada_b_odd[j], norm_g_odd[j], w_in_odd[j], ret_g_odd[j], w_out_odd[j])
            y_prompt, rp = odd_layer(y_prompt, c_prompt, None, pos_prompt, *ow)
            y_sample, rsm = odd_layer(y_sample, c_sample, state_ret[j], pos_sample, *ow)
            ret_p.append(rp); ret_s.append(rsm)
    return (y_prompt, y_sample, jnp.stack(swa_k_p), jnp.stack(swa_v_p), jnp.stack(gla_p), jnp.stack(ret_p),
            jnp.stack(swa_k_s), jnp.stack(swa_v_s), jnp.stack(gla_s), jnp.stack(ret_s))
```

```python
import functools
import math

import numpy as np
import jax
import jax.numpy as jnp
from jax import lax
from jax.experimental import pallas as pl
from jax.experimental.pallas import tpu as pltpu

F32 = jnp.float32
BF16 = jnp.bfloat16

D_MODEL = 2048
WINDOW = 128
HD_A = 64
H_A = 16
H_A_KV = 2
G_A = 8
W_A = 1024
NUM_BUCKETS = 32
MAX_DISTANCE = 128
H_B = 4
DV_B = 256
DK_B = 128
W_B = 1024
GLA_RANK = 16
GLA_TAU = 16.0
GLA_CHUNK = 64
H_C = 8
DK_C = 256
DV_C = 512
W_C = 4096
RET_CHUNK = 128
ROPE_BASE = 10000.0
EPS = 1e-6
NEG_INF = -1e30
PAST_LEN = 8192

EV_QA, EV_GA, EV_VB, EV_GB, EV_QB, EV_KB, EV_KV = 0, 1024, 2048, 3072, 4096, 4608, 5120
EV_N = 5376
OD_N = 12288
SAMPLE_PAD = 8
VMEM_LIMIT = 56 * 1024 * 1024

LOG_GAMMA = [float(np.log1p(-np.exp2(np.float32(-5.0 - h)))) for h in range(H_C)]


def _cparams(sem):
    return pltpu.CompilerParams(dimension_semantics=sem, vmem_limit_bytes=VMEM_LIMIT)


def _silu(x):
    return x / (1.0 + jnp.exp(-x))


def _dot(a, b):
    return jnp.dot(a, b, preferred_element_type=F32)


def _dot_nt(a, b):
    return lax.dot_general(a, b, (((1,), (1,)), ((), ())), preferred_element_type=F32)


def _dot_tn(a, b):
    return lax.dot_general(a, b, (((0,), (0,)), ((), ())), preferred_element_type=F32)


def _cumsum_rows(x):
    c = x.shape[0]
    r = lax.broadcasted_iota(jnp.int32, (c, c), 0)
    s = lax.broadcasted_iota(jnp.int32, (c, c), 1)
    tri = (r >= s).astype(BF16)
    hi = x.astype(BF16)
    r1 = x - hi.astype(F32)
    mid = r1.astype(BF16)
    lo = (r1 - mid.astype(F32)).astype(BF16)
    return _dot(tri, hi) + _dot(tri, mid) + _dot(tri, lo)


def _row_to_col(r):
    n = r.shape[1]
    ri = lax.broadcasted_iota(jnp.int32, (n, n), 0)
    ci = lax.broadcasted_iota(jnp.int32, (n, n), 1)
    return jnp.sum(jnp.where(ri == ci, jnp.broadcast_to(r, (n, n)), 0.0), axis=1, keepdims=True)


def _ada_kernel(c_ref, w_ref, b_ref, o_ref):
    sc = _silu(c_ref[...]).astype(BF16)
    o_ref[...] = _dot(sc, w_ref[...].astype(BF16)) + b_ref[...]


def _ada_mod(c_all, w, b):
    m = c_all.shape[0]
    n = w.shape[1]
    tn = 768
    return pl.pallas_call(
        _ada_kernel,
        grid=(n // tn,),
        in_specs=[pl.BlockSpec((m, D_MODEL), lambda j: (0, 0)),
                  pl.BlockSpec((D_MODEL, tn), lambda j: (0, j)),
                  pl.BlockSpec((1, tn), lambda j: (0, j))],
        out_specs=pl.BlockSpec((m, tn), lambda j: (0, j)),
        out_shape=jax.ShapeDtypeStruct((m, n), F32),
        compiler_params=_cparams(("arbitrary",)),
        name="ada_mod",
    )(c_all, w, b.reshape(1, n))


def _norm_mod(x, g, scale, shift):
    ms = jnp.mean(x * x, axis=-1, keepdims=True)
    y = x * lax.rsqrt(ms + EPS) * g
    return y * (1.0 + scale) + shift


def _log_sigmoid(z):
    return jnp.minimum(z, 0.0) - jnp.log(1.0 + jnp.exp(-jnp.abs(z)))


def _inproj_even_kernel(x_ref, scale_ref, shift_ref, g_ref, w_ref, wlri_ref, wlr_ref, blr_ref,
                        o_ref, la_ref, h_ref):
    @pl.when(pl.program_id(1) == 0)
    def _():
        hb = _norm_mod(x_ref[...], g_ref[...], scale_ref[...], shift_ref[...]).astype(BF16)
        h_ref[...] = hb
        lr = _dot(hb, wlri_ref[...])
        z = _dot(lr.astype(BF16), wlr_ref[...]) + blr_ref[...]
        la_ref[...] = _log_sigmoid(z) / GLA_TAU

    o_ref[...] = _dot(h_ref[...], w_ref[...]).astype(o_ref.dtype)


def _rotary_tile(acc, cos, sin, mult):
    half = DK_C // 2
    parts = []
    for hh in range(acc.shape[1] // DK_C):
        x1 = acc[:, hh * DK_C: hh * DK_C + half]
        x2 = acc[:, hh * DK_C + half: (hh + 1) * DK_C]
        parts.append((x1 * cos - x2 * sin) * mult)
        parts.append((x2 * cos + x1 * sin) * mult)
    return jnp.concatenate(parts, axis=1)


def _inproj_odd_kernel(x_ref, scale_ref, shift_ref, g_ref, w_ref, cos_ref, sin_ref, o_ref, h_ref, *, tn):
    j = pl.program_id(1)

    @pl.when(j == 0)
    def _():
        h_ref[...] = _norm_mod(x_ref[...], g_ref[...], scale_ref[...], shift_ref[...]).astype(BF16)

    acc = _dot(h_ref[...], w_ref[...])
    nq = (H_C * DK_C) // tn

    @pl.when(j < nq)
    def _():
        o_ref[...] = _rotary_tile(acc, cos_ref[...], sin_ref[...], 1.0).astype(o_ref.dtype)

    @pl.when((j >= nq) & (j < 2 * nq))
    def _():
        o_ref[...] = _rotary_tile(acc, cos_ref[...], sin_ref[...], DK_C ** -0.5).astype(o_ref.dtype)

    @pl.when(j >= 2 * nq)
    def _():
        o_ref[...] = acc.astype(o_ref.dtype)


def _mod_spec(arr, tm, rows_per_batch):
    if arr.ndim == 3:
        return pl.BlockSpec((None, 1, D_MODEL), lambda i, j: (i // (rows_per_batch // tm), 0, 0))
    return pl.BlockSpec((tm, D_MODEL), lambda i, j: (i, 0))


def _inproj_even(x, scale, shift, g, w, wlri, wlr, blr, *, tm, tn, rows_per_batch, out_dtype):
    m = x.shape[0]
    return pl.pallas_call(
        _inproj_even_kernel,
        grid=(m // tm, EV_N // tn),
        in_specs=[pl.BlockSpec((tm, D_MODEL), lambda i, j: (i, 0)),
                  _mod_spec(scale, tm, rows_per_batch),
                  _mod_spec(shift, tm, rows_per_batch),
                  pl.BlockSpec((1, D_MODEL), lambda i, j: (0, 0)),
                  pl.BlockSpec((D_MODEL, tn), lambda i, j: (0, j)),
                  pl.BlockSpec((D_MODEL, 128), lambda i, j: (0, 0)),
                  pl.BlockSpec((128, H_B * DK_B), lambda i, j: (0, 0)),
                  pl.BlockSpec((1, H_B * DK_B), lambda i, j: (0, 0))],
        out_specs=[pl.BlockSpec((tm, tn), lambda i, j: (i, j)),
                   pl.BlockSpec((tm, H_B * DK_B), lambda i, j: (i, 0))],
        out_shape=[jax.ShapeDtypeStruct((m, EV_N), out_dtype),
                   jax.ShapeDtypeStruct((m, H_B * DK_B), F32)],
        scratch_shapes=[pltpu.VMEM((tm, D_MODEL), BF16)],
        compiler_params=_cparams(("arbitrary", "arbitrary")),
        name="inproj_even",
    )(x, scale, shift, g, w, wlri, wlr, blr)


def _inproj_odd(x, scale, shift, g, w, cos, sin, *, tm, tn, rows_per_batch, out_dtype):
    m = x.shape[0]
    return pl.pallas_call(
        functools.partial(_inproj_odd_kernel, tn=tn),
        grid=(m // tm, OD_N // tn),
        in_specs=[pl.BlockSpec((tm, D_MODEL), lambda i, j: (i, 0)),
                  _mod_spec(scale, tm, rows_per_batch),
                  _mod_spec(shift, tm, rows_per_batch),
                  pl.BlockSpec((1, D_MODEL), lambda i, j: (0, 0)),
                  pl.BlockSpec((D_MODEL, tn), lambda i, j: (0, j)),
                  pl.BlockSpec((tm, DK_C // 2), lambda i, j: (i % (cos.shape[0] // tm), 0)),
                  pl.BlockSpec((tm, DK_C // 2), lambda i, j: (i % (cos.shape[0] // tm), 0))],
        out_specs=pl.BlockSpec((tm, tn), lambda i, j: (i, j)),
        out_shape=jax.ShapeDtypeStruct((m, OD_N), out_dtype),
        scratch_shapes=[pltpu.VMEM((tm, D_MODEL), BF16)],
        compiler_params=_cparams(("arbitrary", "arbitrary")),
        name="inproj_odd",
    )(x, scale, shift, g, w, cos, sin)


def _outproj_kernel(m_ref, w_ref, x_ref, gate_ref, o_ref):
    o_ref[...] = x_ref[...] + gate_ref[...] * _dot(m_ref[...], w_ref[...])


def _outproj(mixed, w, x, gate, *, tm, tn, rows_per_batch):
    m, k = mixed.shape
    if gate.ndim == 3:
        gate_spec = pl.BlockSpec((None, 1, tn), lambda i, j: (i // (rows_per_batch // tm), 0, j))
    else:
        gate_spec = pl.BlockSpec((tm, tn), lambda i, j: (i, j))
    return pl.pallas_call(
        _outproj_kernel,
        grid=(m // tm, D_MODEL // tn),
        in_specs=[pl.BlockSpec((tm, k), lambda i, j: (i, 0)),
                  pl.BlockSpec((k, tn), lambda i, j: (0, j)),
                  pl.BlockSpec((tm, tn), lambda i, j: (i, j)),
                  gate_spec],
        out_specs=pl.BlockSpec((tm, tn), lambda i, j: (i, j)),
        out_shape=jax.ShapeDtypeStruct((m, D_MODEL), F32),
        compiler_params=_cparams(("arbitrary", "arbitrary")),
        name="outproj",
    )(mixed, w, x, gate)


def _head_rms(x, g):
    return x * lax.rsqrt(jnp.mean(x * x, axis=-1, keepdims=True) + EPS) * g


def _softmax_sink_pv(s, sink, v):
    m = jnp.maximum(jnp.max(s, axis=-1, keepdims=True), sink)
    p = jnp.exp(s - m)
    denom = jnp.sum(p, axis=-1, keepdims=True) + jnp.exp(sink - m)
    return _dot((p / denom).astype(BF16), v)


def _swa_prompt_kernel(relb_ref, sinks_ref, bucket_ref, q_ref, ga_ref, kvp_ref, kvc_ref, qn_ref, kn_ref,
                       o_ref, knew_ref, bias_ref):
    b = pl.program_id(0)
    i = pl.program_id(1)

    @pl.when((b == 0) & (i == 0))
    def _():
        bk = bucket_ref[...]
        for h in range(H_A):
            bias_ref[h] = jnp.full((WINDOW, 2 * WINDOW), NEG_INF, F32)
        for bb in range(NUM_BUCKETS):
            hit = bk == bb
            for h in range(H_A):
                bias_ref[h] = jnp.where(hit, relb_ref[bb, h], bias_ref[h])

    qn_g = qn_ref[...]
    kn_g = kn_ref[...]
    kvp = kvp_ref[...].astype(F32)
    kvc = kvc_ref[...].astype(F32)
    col = lax.broadcasted_iota(jnp.int32, (WINDOW, 2 * WINDOW), 1)
    col_ok = (col >= WINDOW) | (i > 0)
    kn_cur = []
    outs = []
    for g in range(H_A_KV):
        kc = _head_rms(kvc[:, g * HD_A:(g + 1) * HD_A], kn_g)
        kp = _head_rms(kvp[:, g * HD_A:(g + 1) * HD_A], kn_g)
        kn_cur.append(kc)
        kk = jnp.concatenate([kp, kc], axis=0).astype(BF16)
        vv = jnp.concatenate([kvp[:, W_K + g * HD_A: W_K + (g + 1) * HD_A],
                              kvc[:, W_K + g * HD_A: W_K + (g + 1) * HD_A]], axis=0).astype(BF16)
        for hh in range(G_A):
            h = g * G_A + hh
            qh = _head_rms(q_ref[:, h * HD_A:(h + 1) * HD_A].astype(F32), qn_g).astype(BF16)
            bias = bias_ref[h]
            s = _dot_nt(qh, kk) * HD_A ** -0.5 + bias
            s = jnp.where((bias > 0.5 * NEG_INF) & col_ok, s, NEG_INF)
            outs.append(_softmax_sink_pv(s, sinks_ref[h], vv))
    oa = jnp.concatenate(outs, axis=1)
    o_ref[...] = (oa * _silu(ga_ref[...].astype(F32))).astype(o_ref.dtype)

    @pl.when(i == pl.num_programs(1) - 1)
    def _():
        knew_ref[...] = jnp.concatenate(kn_cur, axis=1)


W_K = H_A_KV * HD_A


def _swa_prompt(proj, rel_bias, sinks, qn_g, kn_g, bucket, *, batch, seq):
    nb = seq // WINDOW
    rb = lambda b, i: b * nb + i
    smem = pl.BlockSpec(memory_space=pltpu.SMEM)
    return pl.pallas_call(
        _swa_prompt_kernel,
        grid=(batch, nb),
        in_specs=[smem, smem,
                  pl.BlockSpec((WINDOW, 2 * WINDOW), lambda b, i: (0, 0)),
                  pl.BlockSpec((WINDOW, W_A), lambda b, i: (rb(b, i), EV_QA // W_A)),
                  pl.BlockSpec((WINDOW, W_A), lambda b, i: (rb(b, i), EV_GA // W_A)),
                  pl.BlockSpec((WINDOW, 2 * W_K), lambda b, i: (rb(b, jnp.maximum(i - 1, 0)), EV_KV // (2 * W_K))),
                  pl.BlockSpec((WINDOW, 2 * W_K), lambda b, i: (rb(b, i), EV_KV // (2 * W_K))),
                  pl.BlockSpec((1, HD_A), lambda b, i: (0, 0)),
                  pl.BlockSpec((1, HD_A), lambda b, i: (0, 0))],
        out_specs=[pl.BlockSpec((WINDOW, W_A), lambda b, i: (rb(b, i), 0)),
                   pl.BlockSpec((None, WINDOW, W_K), lambda b, i: (b, 0, 0))],
        out_shape=[jax.ShapeDtypeStruct((batch * seq, W_A), BF16),
                   jax.ShapeDtypeStruct((batch, WINDOW, W_K), F32)],
        scratch_shapes=[pltpu.VMEM((H_A, WINDOW, 2 * WINDOW), F32)],
        compiler_params=_cparams(("arbitrary", "arbitrary")),
        name="swa_prompt",
    )(rel_bias, sinks, bucket, proj, proj, proj, proj, qn_g, kn_g)


def _swa_sample_kernel(bkc_ref, bkn_ref, relrows_ref, sinkrows_ref, q_ref, ga_ref, kn_in_ref, vn_in_ref,
                       ck_ref, cv_ref, qn_ref, kn_ref, o_ref, ko_ref, vo_ref, biasc_ref, biasn_ref, *, sb, ntok):
    @pl.when(pl.program_id(0) == 0)
    def _():
        bkc = bkc_ref[...]
        bkn = bkn_ref[...]
        for g in range(H_A_KV):
            rr = relrows_ref[g]
            bc = jnp.full(bkc.shape, NEG_INF, F32)
            bn = jnp.full(bkn.shape, NEG_INF, F32)
            for bb in range(NUM_BUCKETS):
                val = rr[:, bb:bb + 1]
                bc = jnp.where(bkc == bb, val, bc)
                bn = jnp.where(bkn == bb, val, bn)
            biasc_ref[g] = bc
            biasn_ref[g] = bn

    qn_g = qn_ref[...]
    kn_g = kn_ref[...]
    nkeep = WINDOW - ntok

    def body(s, carry):
        ck = ck_ref[s]
        cv = cv_ref[s]
        kn_in = kn_in_ref[s]
        vn_in = vn_in_ref[s]
        knew = []
        for g in range(H_A_KV):
            sl = slice(g * HD_A, (g + 1) * HD_A)
            kng = _head_rms(kn_in[:, sl], kn_g)
            knew.append(kng)
            q = _head_rms(q_ref[s, g], qn_g).astype(BF16)
            biasc = biasc_ref[g]
            biasn = biasn_ref[g][:, :ntok]
            sc = _dot_nt(q, ck[:, sl].astype(BF16)) * HD_A ** -0.5 + biasc
            sc = jnp.where(biasc > 0.5 * NEG_INF, sc, NEG_INF)
            sn = _dot_nt(q, kng.astype(BF16)) * HD_A ** -0.5 + biasn
            sn = jnp.where(biasn > 0.5 * NEG_INF, sn, NEG_INF)
            sink = sinkrows_ref[g][:, 0:1]
            m = jnp.maximum(jnp.maximum(jnp.max(sc, axis=-1, keepdims=True),
                                        jnp.max(sn, axis=-1, keepdims=True)), sink)
            pc = jnp.exp(sc - m)
            pn = jnp.exp(sn - m)
            denom = (jnp.sum(pc, axis=-1, keepdims=True) + jnp.sum(pn, axis=-1, keepdims=True)
                     + jnp.exp(sink - m))
            o = (_dot((pc / denom).astype(BF16), cv[:, sl].astype(BF16))
                 + _dot((pn / denom).astype(BF16), vn_in[:, sl].astype(BF16)))
            o_ref[s, g] = (o * _silu(ga_ref[s, g])).astype(o_ref.dtype)
        ko_ref[s, 0:nkeep, :] = ck_ref[s, ntok:WINDOW, :]
        ko_ref[s, nkeep:WINDOW, :] = jnp.concatenate(knew, axis=1)
        vo_ref[s, 0:nkeep, :] = cv_ref[s, ntok:WINDOW, :]
        vo_ref[s, nkeep:WINDOW, :] = vn_in
        return carry

    lax.fori_loop(0, sb, body, 0)


def _swa_sample(q_t, ga_t, k_new, v_new, cache_k, cache_v, bkc, bkn, relrows, sinkrows, qn_g, kn_g, *, sb):
    nseq, _, nrow, _ = q_t.shape
    ntok = k_new.shape[1]
    full = lambda shape: pl.BlockSpec(shape, lambda i: tuple(0 for _ in shape))
    seq4 = pl.BlockSpec((sb, H_A_KV, nrow, HD_A), lambda i: (i, 0, 0, 0))
    tok = pl.BlockSpec((sb, ntok, W_K), lambda i: (i, 0, 0))
    cache = pl.BlockSpec((sb, WINDOW, W_K), lambda i: (i, 0, 0))
    return pl.pallas_call(
        functools.partial(_swa_sample_kernel, sb=sb, ntok=ntok),
        grid=(nseq // sb,),
        in_specs=[full(bkc.shape), full(bkn.shape), full(relrows.shape), full(sinkrows.shape),
                  seq4, seq4, tok, tok, cache, cache, full((1, HD_A)), full((1, HD_A))],
        out_specs=[seq4, cache, cache],
        out_shape=[jax.ShapeDtypeStruct(q_t.shape, BF16),
                   jax.ShapeDtypeStruct(cache_k.shape, F32),
                   jax.ShapeDtypeStruct(cache_v.shape, F32)],
        scratch_shapes=[pltpu.VMEM((H_A_KV,) + bkc.shape, F32), pltpu.VMEM((H_A_KV,) + bkn.shape, F32)],
        compiler_params=_cparams(("arbitrary",)),
        name="swa_sample",
    )(bkc, bkn, relrows, sinkrows, q_t, ga_t, k_new, v_new, cache_k, cache_v, qn_g, kn_g)


def _gla_chunk(q, k, v, gb, la, gla_g, state_ref, state_idx, n_valid):
    c = q.shape[0]
    bcum = _cumsum_rows(la)
    rr = lax.broadcasted_iota(jnp.int32, (c, c), 0)
    cc = lax.broadcasted_iota(jnp.int32, (c, c), 1)
    causal = rr >= cc
    row = lax.broadcasted_iota(jnp.int32, (c, 1), 0)
    outs = []
    for h in range(H_B):
        ks = slice(h * DK_B, (h + 1) * DK_B)
        vs = slice(h * DV_B, (h + 1) * DV_B)
        bc = bcum[:, ks]
        qt = (q[:, ks] * DK_B ** -0.5) * jnp.exp(bc)
        kt = k[:, ks] * jnp.exp(-bc)
        a = jnp.where(causal, _dot_nt(qt.astype(BF16), kt.astype(BF16)), 0.0)
        st = state_ref[state_idx + (h,)]
        vb = v[:, vs].astype(BF16)
        qtb = qt.astype(BF16)
        o = _dot(a.astype(BF16), vb) + _dot(qtb, st.astype(BF16))
        blast = bc[n_valid - 1:n_valid, :]
        kd = k[:, ks] * jnp.exp(blast - bc)
        if n_valid < c:
            kd = jnp.where(row < n_valid, kd, 0.0)
        upd = _dot_tn(kd.astype(BF16), vb)
        state_ref[state_idx + (h,)] = _row_to_col(jnp.exp(blast)) * st + upd
        on = o * lax.rsqrt(jnp.mean(o * o, axis=-1, keepdims=True) + EPS) * gla_g
        outs.append(on * _silu(gb[:, vs]))
    return jnp.concatenate(outs, axis=1)


def _gla_prompt_kernel(q_ref, k_ref, v_ref, gb_ref, la_ref, g_ref, o_ref, s_ref, *, nchunk):
    @pl.when(pl.program_id(1) == 0)
    def _():
        s_ref[...] = jnp.zeros(s_ref.shape, F32)

    gla_g = g_ref[...]
    for cidx in range(nchunk):
        rows = slice(cidx * GLA_CHUNK, (cidx + 1) * GLA_CHUNK)
        out = _gla_chunk(q_ref[rows, :].astype(F32), k_ref[rows, :].astype(F32), v_ref[rows, :].astype(F32),
                         gb_ref[rows, :].astype(F32), la_ref[rows, :], gla_g, s_ref, (), GLA_CHUNK)
        o_ref[rows, :] = out.astype(o_ref.dtype)


def _gla_prompt(proj, la, gla_g, *, batch, seq, rows):
    nstep = seq // rows
    rb = lambda b, i: b * nstep + i
    return pl.pallas_call(
        functools.partial(_gla_prompt_kernel, nchunk=rows // GLA_CHUNK),
        grid=(batch, nstep),
        in_specs=[pl.BlockSpec((rows, H_B * DK_B), lambda b, i: (rb(b, i), EV_QB // (H_B * DK_B))),
                  pl.BlockSpec((rows, H_B * DK_B), lambda b, i: (rb(b, i), EV_KB // (H_B * DK_B))),
                  pl.BlockSpec((rows, W_B), lambda b, i: (rb(b, i), EV_VB // W_B)),
                  pl.BlockSpec((rows, W_B), lambda b, i: (rb(b, i), EV_GB // W_B)),
                  pl.BlockSpec((rows, H_B * DK_B), lambda b, i: (rb(b, i), 0)),
                  pl.BlockSpec((1, DV_B), lambda b, i: (0, 0))],
        out_specs=[pl.BlockSpec((rows, W_B), lambda b, i: (rb(b, i), 0)),
                   pl.BlockSpec((None, H_B, DK_B, DV_B), lambda b, i: (b, 0, 0, 0))],
        out_shape=[jax.ShapeDtypeStruct((batch * seq, W_B), BF16),
                   jax.ShapeDtypeStruct((batch, H_B, DK_B, DV_B), F32)],
        compiler_params=_cparams(("arbitrary", "arbitrary")),
        name="gla_prompt",
    )(proj, proj, proj, proj, la, gla_g)


def _gla_sample_kernel(q_ref, k_ref, v_ref, gb_ref, la_ref, g_ref, s_in_ref, o_ref, s_ref, *, sb, ntok):
    gla_g = g_ref[...]
    s_ref[...] = s_in_ref[...]

    def body(s, carry):
        rows = pl.ds(pl.multiple_of(s * SAMPLE_PAD, SAMPLE_PAD), SAMPLE_PAD)
        out = _gla_chunk(q_ref[rows, :], k_ref[rows, :], v_ref[rows, :], gb_ref[rows, :], la_ref[rows, :],
                         gla_g, s_ref, (s,), ntok)
        o_ref[rows, :] = out.astype(o_ref.dtype)
        return carry

    lax.fori_loop(0, sb, body, 0)


def _gla_sample(proj, la, gla_g, state, *, sb, ntok):
    nseq = state.shape[0]
    rows = sb * SAMPLE_PAD
    return pl.pallas_call(
        functools.partial(_gla_sample_kernel, sb=sb, ntok=ntok),
        grid=(nseq // sb,),
        in_specs=[pl.BlockSpec((rows, H_B * DK_B), lambda i: (i, EV_QB // (H_B * DK_B))),
                  pl.BlockSpec((rows, H_B * DK_B), lambda i: (i, EV_KB // (H_B * DK_B))),
                  pl.BlockSpec((rows, W_B), lambda i: (i, EV_VB // W_B)),
                  pl.BlockSpec((rows, W_B), lambda i: (i, EV_GB // W_B)),
                  pl.BlockSpec((rows, H_B * DK_B), lambda i: (i, 0)),
                  pl.BlockSpec((1, DV_B), lambda i: (0, 0)),
                  pl.BlockSpec((sb, H_B, DK_B, DV_B), lambda i: (i, 0, 0, 0))],
        out_specs=[pl.BlockSpec((rows, W_B), lambda i: (i, 0)),
                   pl.BlockSpec((sb, H_B, DK_B, DV_B), lambda i: (i, 0, 0, 0))],
        out_shape=[jax.ShapeDtypeStruct((nseq * SAMPLE_PAD, W_B), BF16),
                   jax.ShapeDtypeStruct(state.shape, F32)],
        compiler_params=_cparams(("arbitrary",)),
        name="gla_sample",
    )(proj, proj, proj, proj, la, gla_g, state)


def _ret_chunk(q_ref, k_ref, v_ref, g_ref, retg, state_ref, o_ref, n_valid):
    c = q_ref.shape[0]
    ri = lax.broadcasted_iota(jnp.int32, (c, c), 0)
    ci = lax.broadcasted_iota(jnp.int32, (c, c), 1)
    dist = (ri - ci).astype(F32)
    row = lax.broadcasted_iota(jnp.int32, (c, 1), 0)
    rowf = row.astype(F32)
    for h in range(H_C):
        lg = LOG_GAMMA[h]
        ks = slice(h * DK_C, (h + 1) * DK_C)
        vs = slice(h * DV_C, (h + 1) * DV_C)
        decay = jnp.where(ri >= ci, jnp.exp(jnp.maximum(dist, 0.0) * lg), 0.0)
        qb = q_ref[:, ks].astype(BF16)
        kf = k_ref[:, ks].astype(F32)
        vb = v_ref[:, vs].astype(BF16)
        a = _dot_nt(qb, kf.astype(BF16)) * decay
        st = state_ref[h]
        inner = jnp.exp((rowf + 1.0) * lg)
        o = _dot(a.astype(BF16), vb) + inner * _dot(qb, st.astype(BF16))
        kd = kf * jnp.exp((n_valid - 1.0 - rowf) * lg)
        if n_valid < c:
            kd = jnp.where(row < n_valid, kd, 0.0)
        state_ref[h] = math.exp(n_valid * lg) * st + _dot_tn(kd.astype(BF16), vb)
        on = o * lax.rsqrt(jnp.mean(o * o, axis=-1, keepdims=True) + EPS) * retg
        o_ref[:, vs] = (on * _silu(g_ref[:, vs].astype(F32))).astype(o_ref.dtype)


def _ret_prompt_kernel(q_ref, k_ref, v_ref, g_ref, retg_ref, o_ref, s_ref):
    @pl.when(pl.program_id(1) == 0)
    def _():
        s_ref[...] = jnp.zeros(s_ref.shape, F32)

    _ret_chunk(q_ref, k_ref, v_ref, g_ref, retg_ref[...], s_ref, o_ref, RET_CHUNK)


def _ret_prompt(proj, ret_g, *, batch, seq):
    nstep = seq // RET_CHUNK
    rb = lambda b, i: b * nstep + i
    qk = H_C * DK_C
    return pl.pallas_call(
        _ret_prompt_kernel,
        grid=(batch, nstep),
        in_specs=[pl.BlockSpec((RET_CHUNK, qk), lambda b, i: (rb(b, i), 0)),
                  pl.BlockSpec((RET_CHUNK, qk), lambda b, i: (rb(b, i), 1)),
                  pl.BlockSpec((RET_CHUNK, W_C), lambda b, i: (rb(b, i), 1)),
                  pl.BlockSpec((RET_CHUNK, W_C), lambda b, i: (rb(b, i), 2)),
                  pl.BlockSpec((1, DV_C), lambda b, i: (0, 0))],
        out_specs=[pl.BlockSpec((RET_CHUNK, W_C), lambda b, i: (rb(b, i), 0)),
                   pl.BlockSpec((None, H_C, DK_C, DV_C), lambda b, i: (b, 0, 0, 0))],
        out_shape=[jax.ShapeDtypeStruct((batch * seq, W_C), BF16),
                   jax.ShapeDtypeStruct((batch, H_C, DK_C, DV_C), F32)],
        compiler_params=_cparams(("arbitrary", "arbitrary")),
        name="ret_prompt",
    )(proj, proj, proj, proj, ret_g)


def _ret_sample_kernel(q_ref, k_ref, v_ref, g_ref, retg_ref, s_in_ref, o_ref, s_ref, *, ntok):
    s_ref[...] = s_in_ref[...]
    _ret_chunk(q_ref, k_ref, v_ref, g_ref, retg_ref[...], s_ref, o_ref, ntok)


def _ret_sample(proj, ret_g, state, *, ntok):
    nseq = state.shape[0]
    qk = H_C * DK_C
    return pl.pallas_call(
        functools.partial(_ret_sample_kernel, ntok=ntok),
        grid=(nseq,),
        in_specs=[pl.BlockSpec((SAMPLE_PAD, qk), lambda i: (i, 0)),
                  pl.BlockSpec((SAMPLE_PAD, qk), lambda i: (i, 1)),
                  pl.BlockSpec((SAMPLE_PAD, W_C), lambda i: (i, 1)),
                  pl.BlockSpec((SAMPLE_PAD, W_C), lambda i: (i, 2)),
                  pl.BlockSpec((1, DV_C), lambda i: (0, 0)),
                  pl.BlockSpec((None, H_C, DK_C, DV_C), lambda i: (i, 0, 0, 0))],
        out_specs=[pl.BlockSpec((SAMPLE_PAD, W_C), lambda i: (i, 0)),
                   pl.BlockSpec((None, H_C, DK_C, DV_C), lambda i: (i, 0, 0, 0))],
        out_shape=[jax.ShapeDtypeStruct((nseq * SAMPLE_PAD, W_C), BF16),
                   jax.ShapeDtypeStruct(state.shape, F32)],
        compiler_params=_cparams(("arbitrary",)),
        name="ret_sample",
    )(proj, proj, proj, proj, ret_g, state)


def _t5_bucket(dist):
    dist = jnp.maximum(dist, 0)
    max_exact = NUM_BUCKETS // 2
    log_ratio = jnp.log(jnp.maximum(dist, 1).astype(F32) / max_exact) / math.log(MAX_DISTANCE / max_exact)
    large = jnp.minimum(max_exact + (log_ratio * (NUM_BUCKETS - max_exact)).astype(jnp.int32), NUM_BUCKETS - 1)
    return jnp.where(dist < max_exact, dist, large)


def _bucket_or_masked(dist):
    return jnp.where((dist >= 0) & (dist <= WINDOW), _t5_bucket(dist), -1).astype(jnp.int32)


def _rotary_tables(pos):
    half = DK_C // 2
    inv = ROPE_BASE ** (-jnp.arange(half, dtype=F32) / half)
    ang = pos.astype(F32)[:, None] * inv[None, :]
    return jnp.cos(ang), jnp.sin(ang)


def _even_weight_layout(w_in):
    qa, ka, va, ga, qb, kb, vb, gb, lr = jnp.split(w_in, [1024, 1152, 1280, 2304, 2816, 3328, 4352, 5376], axis=1)
    main = jnp.concatenate([qa, ga, vb, gb, qb, kb, ka, va], axis=1).astype(BF16)
    lr_pad = jnp.pad(lr, ((0, 0), (0, 128 - GLA_RANK))).astype(BF16)
    return main, lr_pad


def _pad_tokens(a, ntok):
    pad = [(0, 0), (0, SAMPLE_PAD - ntok)] + [(0, 0)] * (a.ndim - 2)
    a = jnp.pad(a, pad)
    return a.reshape((a.shape[0] * SAMPLE_PAD,) + a.shape[2:])


def _rows(v, reps):
    return jnp.repeat(v, reps, axis=0)


def kernel(x_prompt, x_sample, cache_swa_k, cache_swa_v, state_gla, state_ret, c_prompt, c_sample, rel_bias,
           ada_w_even, ada_b_even, norm_g_even, w_in_even, w_lr_even, b_lr_even, qn_g_even, kn_g_even,
           sinks_even, gla_g_even, w_out_even, ada_w_odd, ada_b_odd, norm_g_odd, w_in_odd, ret_g_odd, w_out_odd):
    batch, seq, _ = x_prompt.shape
    nseq, ntok, _ = x_sample.shape
    mp = batch * seq

    c_all = jnp.concatenate([c_prompt, c_sample], axis=0)
    mod_e = _ada_mod(c_all, ada_w_even[0], ada_b_even[0])
    mod_o = _ada_mod(c_all, ada_w_odd[0], ada_b_odd[0])

    def split_mod(mod):
        shift, scale, gate = jnp.split(mod, 3, axis=1)
        p = tuple(a[:batch].reshape(batch, 1, D_MODEL) for a in (shift, scale, gate))
        s = tuple(_rows(a[batch:], SAMPLE_PAD) for a in (shift, scale, gate))
        return p, s

    (shift_ep, scale_ep, gate_ep), (shift_es, scale_es, gate_es) = split_mod(mod_e)
    (shift_op, scale_op, gate_op), (shift_os, scale_os, gate_os) = split_mod(mod_o)

    xp = x_prompt.reshape(mp, D_MODEL)
    xs = _pad_tokens(x_sample, ntok)
    ms = xs.shape[0]

    w_e, w_lri = _even_weight_layout(w_in_even[0])
    w_lr = jnp.pad(w_lr_even[0], ((0, 128 - GLA_RANK), (0, 0))).astype(BF16)
    b_lr = b_lr_even[0].reshape(1, -1)
    g_e = norm_g_even[0].reshape(1, D_MODEL)
    qn_g = qn_g_even[0].reshape(1, HD_A)
    kn_g = kn_g_even[0].reshape(1, HD_A)
    gla_g = gla_g_even[0].reshape(1, DV_B)
    w_out_e = w_out_even[0].astype(BF16)

    proj_p, la_p = _inproj_even(xp, scale_ep, shift_ep, g_e, w_e, w_lri, w_lr, b_lr,
                                tm=512, tn=768, rows_per_batch=seq, out_dtype=F32)
    proj_s, la_s = _inproj_even(xs, scale_es, shift_es, g_e, w_e, w_lri, w_lr, b_lr,
                                tm=512, tn=768, rows_per_batch=ms, out_dtype=F32)

    ii = jnp.arange(WINDOW)
    ss = jnp.arange(2 * WINDOW)
    bucket_p = _bucket_or_masked(WINDOW + ii[:, None] - ss[None, :])
    mixed_a_p, k_last = _swa_prompt(proj_p, rel_bias, sinks_even[0], qn_g, kn_g, bucket_p, batch=batch, seq=seq)
    mixed_b_p, gla_p = _gla_prompt(proj_p, la_p, gla_g, batch=batch, seq=seq, rows=256)
    mixed_p = jnp.concatenate([mixed_a_p, mixed_b_p], axis=1)
    y1_p = _outproj(mixed_p, w_out_e, xp, gate_ep, tm=512, tn=1024, rows_per_batch=seq)

    proj_p3 = proj_p.reshape(batch, seq, EV_N)
    swa_k_p = k_last.reshape(1, batch, WINDOW, H_A_KV, HD_A)
    swa_v_p = proj_p3[:, seq - WINDOW:, EV_KV + W_K: EV_KV + 2 * W_K].astype(F32).reshape(
        1, batch, WINDOW, H_A_KV, HD_A)

    proj_s3 = proj_s.reshape(nseq, SAMPLE_PAD, EV_N)[:, :ntok]
    nrow = ntok * G_A

    def to_heads(a):
        return a.reshape(nseq, ntok, H_A_KV, G_A, HD_A).transpose(0, 2, 1, 3, 4).reshape(nseq, H_A_KV, nrow, HD_A)

    q_t = to_heads(proj_s3[:, :, EV_QA:EV_QA + W_A])
    ga_t = to_heads(proj_s3[:, :, EV_GA:EV_GA + W_A])
    k_new = proj_s3[:, :, EV_KV:EV_KV + W_K]
    v_new = proj_s3[:, :, EV_KV + W_K:EV_KV + 2 * W_K]
    tt = jnp.repeat(jnp.arange(ntok), G_A)
    jj = jnp.arange(WINDOW)
    bkc = _bucket_or_masked(WINDOW + tt[:, None] - jj[None, :])
    bkn = _bucket_or_masked(tt[:, None] - jj[None, :])
    bkn = jnp.where(jj[None, :] < ntok, bkn, -1)
    relrows = jnp.tile(rel_bias.T.reshape(H_A_KV, 1, G_A, NUM_BUCKETS), (1, ntok, 1, 1)).reshape(
        H_A_KV, nrow, NUM_BUCKETS)
    sinkrows = jnp.broadcast_to(
        jnp.tile(sinks_even[0].reshape(H_A_KV, 1, G_A), (1, ntok, 1)).reshape(H_A_KV, nrow, 1), (H_A_KV, nrow, 128))
    w_buf = cache_swa_k.shape[2]
    mixed_a_t, k_cache_s, v_cache_s = _swa_sample(
        q_t, ga_t, k_new, v_new, cache_swa_k[0].reshape(nseq, w_buf, W_K), cache_swa_v[0].reshape(nseq, w_buf, W_K),
        bkc, bkn, relrows, sinkrows, qn_g, kn_g, sb=16)
    mixed_a_s = mixed_a_t.reshape(nseq, H_A_KV, ntok, G_A, HD_A).transpose(0, 2, 1, 3, 4).reshape(nseq, ntok, W_A)
    mixed_a_s = _pad_tokens(mixed_a_s, ntok)
    mixed_b_s, gla_s = _gla_sample(proj_s, la_s, gla_g, state_gla[0], sb=8, ntok=ntok)
    mixed_s = jnp.concatenate([mixed_a_s, mixed_b_s], axis=1)
    y1_s = _outproj(mixed_s, w_out_e, xs, gate_es, tm=512, tn=1024, rows_per_batch=ms)

    w_o = w_in_odd[0].astype(BF16)
    g_o = norm_g_odd[0].reshape(1, D_MODEL)
    ret_g = ret_g_odd[0].reshape(1, DV_C)
    w_out_o = w_out_odd[0].astype(BF16)
    cos_p, sin_p = _rotary_tables(jnp.arange(seq))
    pos_s = PAST_LEN + jnp.minimum(jnp.arange(SAMPLE_PAD), ntok - 1)
    cos_s, sin_s = _rotary_tables(jnp.tile(pos_s, nseq))

    projo_p = _inproj_odd(y1_p, scale_op, shift_op, g_o, w_o, cos_p, sin_p,
                          tm=512, tn=1024, rows_per_batch=seq, out_dtype=F32)
    o_p, ret_p = _ret_prompt(projo_p, ret_g, batch=batch, seq=seq)
    y2_p = _outproj(o_p, w_out_o, y1_p, gate_op, tm=512, tn=1024, rows_per_batch=seq)

    projo_s = _inproj_odd(y1_s, scale_os, shift_os, g_o, w_o, cos_s, sin_s,
                          tm=512, tn=1024, rows_per_batch=ms, out_dtype=F32)
    o_s, ret_s = _ret_sample(projo_s, ret_g, state_ret[0], ntok=ntok)
    y2_s = _outproj(o_s, w_out_o, y1_s, gate_os, tm=512, tn=1024, rows_per_batch=ms)

    y_prompt = y2_p.reshape(batch, seq, D_MODEL)
    y_sample = y2_s.reshape(nseq, SAMPLE_PAD, D_MODEL)[:, :ntok]
    return (y_prompt, y_sample, swa_k_p, swa_v_p, gla_p[None], ret_p[None],
            k_cache_s.reshape(1, nseq, w_buf, H_A_KV, HD_A), v_cache_s.reshape(1, nseq, w_buf, H_A_KV, HD_A),
            gla_s[None], ret_s[None])
```

```python
import functools
import math

import numpy as np
import jax
import jax.numpy as jnp
from jax import lax
from jax.experimental import pallas as pl
from jax.experimental.pallas import tpu as pltpu

F32 = jnp.float32
BF16 = jnp.bfloat16

D_MODEL = 2048
WINDOW = 128
HD_A = 64
H_A = 16
H_A_KV = 2
G_A = 8
W_A = 1024
NUM_BUCKETS = 32
MAX_DISTANCE = 128
H_B = 4
DV_B = 256
DK_B = 128
W_B = 1024
GLA_RANK = 16
GLA_TAU = 16.0
GLA_CHUNK = 64
H_C = 8
DK_C = 256
DV_C = 512
W_C = 4096
RET_CHUNK = 128
ROPE_BASE = 10000.0
EPS = 1e-6
NEG_INF = -1e30
PAST_LEN = 8192

EV_QA, EV_GA, EV_VB, EV_GB, EV_QB, EV_KB, EV_KV = 0, 1024, 2048, 3072, 4096, 4608, 5120
EV_N = 5376
OD_N = 12288
W_K = H_A_KV * HD_A
SAMPLE_PAD = 8
VMEM_LIMIT = 56 * 1024 * 1024

LOG_GAMMA = [float(np.log1p(-np.exp2(np.float32(-5.0 - h)))) for h in range(H_C)]


def _cparams(sem):
    return pltpu.CompilerParams(dimension_semantics=sem, vmem_limit_bytes=VMEM_LIMIT)


def _silu(x):
    return x / (1.0 + jnp.exp(-x))


def _dot(a, b):
    return jnp.dot(a, b, preferred_element_type=F32)


def _dot_nt(a, b):
    return lax.dot_general(a, b, (((1,), (1,)), ((), ())), preferred_element_type=F32)


def _dot_tn(a, b):
    return lax.dot_general(a, b, (((0,), (0,)), ((), ())), preferred_element_type=F32)


def _cumsum_rows(x):
    c = x.shape[0]
    r = lax.broadcasted_iota(jnp.int32, (c, c), 0)
    s = lax.broadcasted_iota(jnp.int32, (c, c), 1)
    tri = (r >= s).astype(BF16)
    hi = x.astype(BF16)
    r1 = x - hi.astype(F32)
    mid = r1.astype(BF16)
    lo = (r1 - mid.astype(F32)).astype(BF16)
    return _dot(tri, hi) + _dot(tri, mid) + _dot(tri, lo)


def _row_to_col(r):
    n = r.shape[1]
    ri = lax.broadcasted_iota(jnp.int32, (n, n), 0)
    ci = lax.broadcasted_iota(jnp.int32, (n, n), 1)
    return jnp.sum(jnp.where(ri == ci, jnp.broadcast_to(r, (n, n)), 0.0), axis=1, keepdims=True)


def _ada_kernel(c_ref, w_ref, b_ref, o_ref):
    sc = _silu(c_ref[...]).astype(BF16)
    o_ref[...] = _dot(sc, w_ref[...].astype(BF16)) + b_ref[...]


def _ada_mod(c_all, w, b):
    m = c_all.shape[0]
    n = w.shape[1]
    tn = 768
    return pl.pallas_call(
        _ada_kernel,
        grid=(n // tn,),
        in_specs=[pl.BlockSpec((m, D_MODEL), lambda j: (0, 0)),
                  pl.BlockSpec((D_MODEL, tn), lambda j: (0, j)),
                  pl.BlockSpec((1, tn), lambda j: (0, j))],
        out_specs=pl.BlockSpec((m, tn), lambda j: (0, j)),
        out_shape=jax.ShapeDtypeStruct((m, n), F32),
        compiler_params=_cparams(("arbitrary",)),
        name="ada_mod",
    )(c_all, w, b.reshape(1, n))


def _norm_mod(x, g, scale, shift):
    ms = jnp.mean(x * x, axis=-1, keepdims=True)
    y = x * lax.rsqrt(ms + EPS) * g
    return y * (1.0 + scale) + shift


def _log_sigmoid(z):
    return jnp.minimum(z, 0.0) - jnp.log(1.0 + jnp.exp(-jnp.abs(z)))


def _inproj_even_kernel(x_ref, scale_ref, shift_ref, g_ref, w_ref, wlri_ref, wlr_ref, blr_ref,
                        o_ref, la_ref, h_ref):
    @pl.when(pl.program_id(1) == 0)
    def _():
        hb = _norm_mod(x_ref[...], g_ref[...], scale_ref[...], shift_ref[...]).astype(BF16)
        h_ref[...] = hb
        lr = _dot(hb, wlri_ref[...])
        z = _dot(lr.astype(BF16), wlr_ref[...]) + blr_ref[...]
        la_ref[...] = _log_sigmoid(z) / GLA_TAU

    o_ref[...] = _dot(h_ref[...], w_ref[...]).astype(o_ref.dtype)


def _rotary_tile(acc, cos, sin, mult):
    half = DK_C // 2
    parts = []
    for hh in range(acc.shape[1] // DK_C):
        x1 = acc[:, hh * DK_C: hh * DK_C + half]
        x2 = acc[:, hh * DK_C + half: (hh + 1) * DK_C]
        parts.append((x1 * cos - x2 * sin) * mult)
        parts.append((x2 * cos + x1 * sin) * mult)
    return jnp.concatenate(parts, axis=1)


def _inproj_odd_kernel(x_ref, scale_ref, shift_ref, g_ref, w_ref, cos_ref, sin_ref, o_ref, h_ref, *, tn):
    j = pl.program_id(1)

    @pl.when(j == 0)
    def _():
        h_ref[...] = _norm_mod(x_ref[...], g_ref[...], scale_ref[...], shift_ref[...]).astype(BF16)

    acc = _dot(h_ref[...], w_ref[...])
    nq = (H_C * DK_C) // tn

    @pl.when(j < nq)
    def _():
        o_ref[...] = _rotary_tile(acc, cos_ref[...], sin_ref[...], 1.0).astype(o_ref.dtype)

    @pl.when((j >= nq) & (j < 2 * nq))
    def _():
        o_ref[...] = _rotary_tile(acc, cos_ref[...], sin_ref[...], DK_C ** -0.5).astype(o_ref.dtype)

    @pl.when(j >= 2 * nq)
    def _():
        o_ref[...] = acc.astype(o_ref.dtype)


def _mod_spec(arr, tm, rows_per_batch):
    if arr.ndim == 3:
        return pl.BlockSpec((None, 1, D_MODEL), lambda i, j: (i // (rows_per_batch // tm), 0, 0))
    return pl.BlockSpec((tm, D_MODEL), lambda i, j: (i, 0))


def _inproj_even(x, scale, shift, g, w, wlri, wlr, blr, *, tm, tn, rows_per_batch, out_dtype):
    m = x.shape[0]
    return pl.pallas_call(
        _inproj_even_kernel,
        grid=(m // tm, EV_N // tn),
        in_specs=[pl.BlockSpec((tm, D_MODEL), lambda i, j: (i, 0)),
                  _mod_spec(scale, tm, rows_per_batch),
                  _mod_spec(shift, tm, rows_per_batch),
                  pl.BlockSpec((1, D_MODEL), lambda i, j: (0, 0)),
                  pl.BlockSpec((D_MODEL, tn), lambda i, j: (0, j)),
                  pl.BlockSpec((D_MODEL, 128), lambda i, j: (0, 0)),
                  pl.BlockSpec((128, H_B * DK_B), lambda i, j: (0, 0)),
                  pl.BlockSpec((1, H_B * DK_B), lambda i, j: (0, 0))],
        out_specs=[pl.BlockSpec((tm, tn), lambda i, j: (i, j)),
                   pl.BlockSpec((tm, H_B * DK_B), lambda i, j: (i, 0))],
        out_shape=[jax.ShapeDtypeStruct((m, EV_N), out_dtype),
                   jax.ShapeDtypeStruct((m, H_B * DK_B), F32)],
        scratch_shapes=[pltpu.VMEM((tm, D_MODEL), BF16)],
        compiler_params=_cparams(("arbitrary", "arbitrary")),
        name="inproj_even",
    )(x, scale, shift, g, w, wlri, wlr, blr)


def _inproj_odd(x, scale, shift, g, w, cos, sin, *, tm, tn, rows_per_batch, out_dtype):
    m = x.shape[0]
    return pl.pallas_call(
        functools.partial(_inproj_odd_kernel, tn=tn),
        grid=(m // tm, OD_N // tn),
        in_specs=[pl.BlockSpec((tm, D_MODEL), lambda i, j: (i, 0)),
                  _mod_spec(scale, tm, rows_per_batch),
                  _mod_spec(shift, tm, rows_per_batch),
                  pl.BlockSpec((1, D_MODEL), lambda i, j: (0, 0)),
                  pl.BlockSpec((D_MODEL, tn), lambda i, j: (0, j)),
                  pl.BlockSpec((tm, DK_C // 2), lambda i, j: (i % (cos.shape[0] // tm), 0)),
                  pl.BlockSpec((tm, DK_C // 2), lambda i, j: (i % (cos.shape[0] // tm), 0))],
        out_specs=pl.BlockSpec((tm, tn), lambda i, j: (i, j)),
        out_shape=jax.ShapeDtypeStruct((m, OD_N), out_dtype),
        scratch_shapes=[pltpu.VMEM((tm, D_MODEL), BF16)],
        compiler_params=_cparams(("arbitrary", "arbitrary")),
        name="inproj_odd",
    )(x, scale, shift, g, w, cos, sin)


def _outproj_kernel(*refs, nin):
    m_refs, w_refs = refs[:nin], refs[nin:2 * nin]
    x_ref, gate_ref, o_ref = refs[2 * nin:]
    acc = _dot(m_refs[0][...].astype(BF16), w_refs[0][...])
    for m_ref, w_ref in zip(m_refs[1:], w_refs[1:]):
        acc += _dot(m_ref[...].astype(BF16), w_ref[...])
    o_ref[...] = x_ref[...] + gate_ref[...] * acc


def _outproj(mixed, ws, x, gate, *, tm, tn, rows_per_batch):
    m = x.shape[0]
    if gate.ndim == 3:
        gate_spec = pl.BlockSpec((None, 1, tn), lambda i, j: (i // (rows_per_batch // tm), 0, j))
    else:
        gate_spec = pl.BlockSpec((tm, tn), lambda i, j: (i, j))
    return pl.pallas_call(
        functools.partial(_outproj_kernel, nin=len(mixed)),
        grid=(m // tm, D_MODEL // tn),
        in_specs=([pl.BlockSpec((tm, a.shape[1]), lambda i, j: (i, 0)) for a in mixed]
                  + [pl.BlockSpec((w.shape[0], tn), lambda i, j: (0, j)) for w in ws]
                  + [pl.BlockSpec((tm, tn), lambda i, j: (i, j)), gate_spec]),
        out_specs=pl.BlockSpec((tm, tn), lambda i, j: (i, j)),
        out_shape=jax.ShapeDtypeStruct((m, D_MODEL), F32),
        compiler_params=_cparams(("arbitrary", "arbitrary")),
        name="outproj",
    )(*mixed, *ws, x, gate)


def _seg_rms(x, g2, seg_ones):
    x2 = x * x
    hi = x2.astype(BF16)
    lo = (x2 - hi.astype(F32)).astype(BF16)
    ss = _dot(hi, seg_ones) + _dot(lo, seg_ones)
    return x * lax.rsqrt(ss * (1.0 / HD_A) + EPS) * g2


def _seg_ones():
    r = lax.broadcasted_iota(jnp.int32, (2 * HD_A, 2 * HD_A), 0)
    c = lax.broadcasted_iota(jnp.int32, (2 * HD_A, 2 * HD_A), 1)
    return ((r < HD_A) == (c < HD_A)).astype(BF16)


def _swa_prompt_kernel(relb_ref, sinks_ref, bucket_ref, q_ref, ga_ref, kv_ref, qn_ref, kn_ref,
                       o_ref, knew_ref, bias_ref, kprev_ref, vprev_ref, s_ref, pe_ref):
    b = pl.program_id(0)
    i = pl.program_id(1)

    @pl.when((b == 0) & (i == 0))
    def _():
        bk = bucket_ref[...]
        for h in range(H_A):
            bias_ref[h] = jnp.full((WINDOW, 2 * WINDOW), NEG_INF, F32)
        for bb in range(NUM_BUCKETS):
            hit = bk == bb
            for h in range(H_A):
                bias_ref[h] = jnp.where(hit, relb_ref[bb, h], bias_ref[h])
        col = lax.broadcasted_iota(jnp.int32, (WINDOW, 2 * WINDOW), 1)
        for h in range(H_A):
            bias_ref[H_A + h] = jnp.where(col >= WINDOW, bias_ref[h], NEG_INF)

    @pl.when(i == 0)
    def _():
        kprev_ref[...] = jnp.zeros(kprev_ref.shape, BF16)
        vprev_ref[...] = jnp.zeros(vprev_ref.shape, BF16)

    seg_ones = _seg_ones()
    lt = 2 * HD_A
    kv = kv_ref[...].astype(F32)
    kc = _seg_rms(kv[:, :W_K], kn_ref[...], seg_ones)
    kcb = kc.astype(BF16)
    vcb = kv[:, W_K:].astype(BF16)
    kk = jnp.concatenate([kprev_ref[...], kcb], axis=0)
    vv = jnp.concatenate([vprev_ref[...], vcb], axis=0)
    vv_ext = jnp.concatenate([vv, jnp.ones(vv.shape, BF16)], axis=1)
    base = jnp.where(i == 0, H_A, 0)
    qn2 = qn_ref[...] * HD_A ** -0.5
    x8 = jnp.concatenate([q_ref[:, p * lt:(p + 1) * lt] for p in range(G_A)], axis=0).astype(F32)
    xn8 = _seg_rms(x8, qn2, seg_ones)
    lo8 = lax.broadcasted_iota(jnp.int32, xn8.shape, 1) < HD_A
    q_all = jnp.concatenate([jnp.where(lo8, xn8, 0.0), jnp.where(lo8, 0.0, xn8)], axis=0).astype(BF16)
    s_ref[...] = _dot_nt(q_all, kk)
    sink_terms = []
    for h in range(H_A):
        rows = slice(h * WINDOW, (h + 1) * WINDOW)
        s = s_ref[rows, :] + bias_ref[base + h]
        sink = sinks_ref[h]
        m = jnp.maximum(jnp.max(s, axis=-1, keepdims=True), sink)
        pe_ref[rows, :] = jnp.exp(s - m).astype(BF16)
        sink_terms.append(jnp.exp(sink - m))
    o_ext = _dot(pe_ref[...], vv_ext)
    lo_half = lax.broadcasted_iota(jnp.int32, (WINDOW, lt), 1) < HD_A
    for p in range(G_A):
        halves = []
        for a in range(H_A_KV):
            h = a * G_A + p
            rows = slice(h * WINDOW, (h + 1) * WINDOW)
            halves.append(o_ext[rows, :lt] / (o_ext[rows, lt:] + sink_terms[h]))
        oa = jnp.where(lo_half, halves[0], halves[1])
        cols = slice(p * lt, (p + 1) * lt)
        o_ref[:, cols] = (oa * _silu(ga_ref[:, cols].astype(F32))).astype(o_ref.dtype)
    kprev_ref[...] = kcb
    vprev_ref[...] = vcb

    @pl.when(i == pl.num_programs(1) - 1)
    def _():
        knew_ref[...] = kc


def _swa_prompt(proj, rel_bias, sinks, qn2, kn2, bucket, *, batch, seq):
    nb = seq // WINDOW
    rb = lambda b, i: b * nb + i
    smem = pl.BlockSpec(memory_space=pltpu.SMEM)
    return pl.pallas_call(
        _swa_prompt_kernel,
        grid=(batch, nb),
        in_specs=[smem, smem,
                  pl.BlockSpec((WINDOW, 2 * WINDOW), lambda b, i: (0, 0)),
                  pl.BlockSpec((WINDOW, W_A), lambda b, i: (rb(b, i), EV_QA // W_A)),
                  pl.BlockSpec((WINDOW, W_A), lambda b, i: (rb(b, i), EV_GA // W_A)),
                  pl.BlockSpec((WINDOW, 2 * W_K), lambda b, i: (rb(b, i), EV_KV // (2 * W_K))),
                  pl.BlockSpec((1, 2 * HD_A), lambda b, i: (0, 0)),
                  pl.BlockSpec((1, 2 * HD_A), lambda b, i: (0, 0))],
        out_specs=[pl.BlockSpec((WINDOW, W_A), lambda b, i: (rb(b, i), 0)),
                   pl.BlockSpec((None, WINDOW, W_K), lambda b, i: (b, 0, 0))],
        out_shape=[jax.ShapeDtypeStruct((batch * seq, W_A), BF16),
                   jax.ShapeDtypeStruct((batch, WINDOW, W_K), F32)],
        scratch_shapes=[pltpu.VMEM((2 * H_A, WINDOW, 2 * WINDOW), F32),
                        pltpu.VMEM((WINDOW, W_K), BF16), pltpu.VMEM((WINDOW, W_K), BF16),
                        pltpu.VMEM((H_A * WINDOW, 2 * WINDOW), F32), pltpu.VMEM((H_A * WINDOW, 2 * WINDOW), BF16)],
        compiler_params=_cparams(("arbitrary", "arbitrary")),
        name="swa_prompt",
    )(rel_bias, sinks, bucket, proj, proj, proj, qn2, kn2)


def _swa_sample_kernel(bkc_ref, bkn_ref, relrows_ref, sinkrows_ref, q_ref, ga_ref, kvn_ref, ck_ref, cv_ref,
                       qn_ref, kn_ref, o_ref, ko_ref, vo_ref, biasc_ref, biasn_ref, *, sb, ntok, unroll):
    @pl.when(pl.program_id(0) == 0)
    def _():
        bkc = bkc_ref[...]
        bkn = bkn_ref[...]
        rr = relrows_ref[...]
        bc = jnp.full(bkc.shape, NEG_INF, F32)
        bn = jnp.full(bkn.shape, NEG_INF, F32)
        for bb in range(NUM_BUCKETS):
            val = rr[:, bb:bb + 1]
            bc = jnp.where(bkc == bb, val, bc)
            bn = jnp.where(bkn == bb, val, bn)
        biasc_ref[...] = bc
        biasn_ref[...] = bn

    seg_ones = _seg_ones()
    lt = 2 * HD_A
    nrow = G_A * SAMPLE_PAD
    lo_half = lax.broadcasted_iota(jnp.int32, (nrow, lt), 1) < HD_A
    qn2 = qn_ref[...] * HD_A ** -0.5
    kn2 = kn_ref[...]
    sink = sinkrows_ref[:, 0:1]
    nkeep = WINDOW - ntok

    def body(it, carry):
        seqs = [it * unroll + u for u in range(unroll)]
        rows = [pl.ds(pl.multiple_of(s * SAMPLE_PAD, SAMPLE_PAD), SAMPLE_PAD) for s in seqs]
        kvn = [kvn_ref[r, :] for r in rows]
        kn_all = _seg_rms(jnp.concatenate([x[:, :W_K] for x in kvn], axis=0), kn2, seg_ones)
        q2_all = jnp.concatenate([q_ref[r, p * lt:(p + 1) * lt] for r in rows for p in range(G_A)], axis=0)
        xn_all = _seg_rms(q2_all, qn2, seg_ones)
        sc, sn, kn = [], [], []
        for u, s in enumerate(seqs):
            xn = xn_all[u * nrow:(u + 1) * nrow]
            q4 = jnp.concatenate([jnp.where(lo_half, xn, 0.0), jnp.where(lo_half, 0.0, xn)], axis=0).astype(BF16)
            kn.append(kn_all[u * SAMPLE_PAD:(u + 1) * SAMPLE_PAD])
            sc.append(_dot_nt(q4, ck_ref[s].astype(BF16)))
            sn.append(_dot_nt(q4, kn[u].astype(BF16)))
        pc, pn, sink_terms = [], [], []
        for u in range(unroll):
            scu = sc[u] + biasc_ref[...]
            snu = sn[u] + biasn_ref[:, :SAMPLE_PAD]
            m = jnp.maximum(jnp.maximum(jnp.max(scu, axis=-1, keepdims=True),
                                        jnp.max(snu, axis=-1, keepdims=True)), sink)
            pc.append(jnp.exp(scu - m).astype(BF16))
            pn.append(jnp.exp(snu - m).astype(BF16))
            sink_terms.append(jnp.exp(sink - m))
        o_ext = []
        for u, s in enumerate(seqs):
            cv_ext = jnp.concatenate([cv_ref[s].astype(BF16), jnp.ones((WINDOW, lt), BF16)], axis=1)
            vn_ext = jnp.concatenate([kvn[u][:, W_K:].astype(BF16), jnp.ones((SAMPLE_PAD, lt), BF16)], axis=1)
            o_ext.append(_dot(pc[u], cv_ext) + _dot(pn[u], vn_ext))
        for u, s in enumerate(seqs):
            o4 = o_ext[u][:, :lt] / (o_ext[u][:, lt:] + sink_terms[u])
            o2 = jnp.where(lo_half, o4[:nrow], o4[nrow:])
            xg = ga_ref[rows[u], :]
            for p in range(G_A):
                o_ref[rows[u], p * lt:(p + 1) * lt] = (
                    o2[p * SAMPLE_PAD:(p + 1) * SAMPLE_PAD] * _silu(xg[:, p * lt:(p + 1) * lt])).astype(o_ref.dtype)
            ko_ref[s, 0:nkeep, :] = ck_ref[s, ntok:WINDOW, :]
            ko_ref[s, nkeep:WINDOW, :] = kn[u][0:ntok]
            vo_ref[s, 0:nkeep, :] = cv_ref[s, ntok:WINDOW, :]
            vo_ref[s, nkeep:WINDOW, :] = kvn[u][0:ntok, W_K:]
        return carry

    lax.fori_loop(0, sb // unroll, body, 0)


def _swa_sample(proj, cache_k, cache_v, bkc, bkn, relrows, sinkrows, qn2, kn2, *, sb, ntok, unroll):
    nseq = cache_k.shape[0]
    rows = sb * SAMPLE_PAD
    full = lambda shape: pl.BlockSpec(shape, lambda i: tuple(0 for _ in shape))
    cache = pl.BlockSpec((sb, WINDOW, W_K), lambda i: (i, 0, 0))
    return pl.pallas_call(
        functools.partial(_swa_sample_kernel, sb=sb, ntok=ntok, unroll=unroll),
        grid=(nseq // sb,),
        in_specs=[full(bkc.shape), full(bkn.shape), full(relrows.shape), full(sinkrows.shape),
                  pl.BlockSpec((rows, W_A), lambda i: (i, EV_QA // W_A)),
                  pl.BlockSpec((rows, W_A), lambda i: (i, EV_GA // W_A)),
                  pl.BlockSpec((rows, 2 * W_K), lambda i: (i, EV_KV // (2 * W_K))),
                  cache, cache, full((1, 2 * HD_A)), full((1, 2 * HD_A))],
        out_specs=[pl.BlockSpec((rows, W_A), lambda i: (i, 0)), cache, cache],
        out_shape=[jax.ShapeDtypeStruct((nseq * SAMPLE_PAD, W_A), F32),
                   jax.ShapeDtypeStruct(cache_k.shape, F32),
                   jax.ShapeDtypeStruct(cache_v.shape, F32)],
        scratch_shapes=[pltpu.VMEM(bkc.shape, F32), pltpu.VMEM(bkn.shape, F32)],
        compiler_params=_cparams(("arbitrary",)),
        name="swa_sample",
    )(bkc, bkn, relrows, sinkrows, proj, proj, proj, cache_k, cache_v, qn2, kn2)


def _gla_chunk(q, k, v, gb, la, gla_g, state_ref, state_idx, n_valid):
    c = q.shape[0]
    bcum = _cumsum_rows(la)
    rr = lax.broadcasted_iota(jnp.int32, (c, c), 0)
    cc = lax.broadcasted_iota(jnp.int32, (c, c), 1)
    causal = rr >= cc
    row = lax.broadcasted_iota(jnp.int32, (c, 1), 0)
    outs = []
    for h in range(H_B):
        ks = slice(h * DK_B, (h + 1) * DK_B)
        vs = slice(h * DV_B, (h + 1) * DV_B)
        bc = bcum[:, ks]
        qt = (q[:, ks] * DK_B ** -0.5) * jnp.exp(bc)
        kt = k[:, ks] * jnp.exp(-bc)
        a = jnp.where(causal, _dot_nt(qt.astype(BF16), kt.astype(BF16)), 0.0)
        st = state_ref[state_idx + (h,)]
        vb = v[:, vs].astype(BF16)
        qtb = qt.astype(BF16)
        o = _dot(a.astype(BF16), vb) + _dot(qtb, st.astype(BF16))
        blast = bc[n_valid - 1:n_valid, :]
        kd = k[:, ks] * jnp.exp(blast - bc)
        if n_valid < c:
            kd = jnp.where(row < n_valid, kd, 0.0)
        upd = _dot_tn(kd.astype(BF16), vb)
        state_ref[state_idx + (h,)] = _row_to_col(jnp.exp(blast)) * st + upd
        on = o * lax.rsqrt(jnp.mean(o * o, axis=-1, keepdims=True) + EPS) * gla_g
        outs.append(on * _silu(gb[:, vs]))
    return jnp.concatenate(outs, axis=1)


def _gla_prompt_kernel(q_ref, k_ref, v_ref, gb_ref, la_ref, g_ref, o_ref, s_ref, *, nchunk):
    @pl.when(pl.program_id(1) == 0)
    def _():
        s_ref[...] = jnp.zeros(s_ref.shape, F32)

    gla_g = g_ref[...]
    for cidx in range(nchunk):
        rows = slice(cidx * GLA_CHUNK, (cidx + 1) * GLA_CHUNK)
        out = _gla_chunk(q_ref[rows, :].astype(F32), k_ref[rows, :].astype(F32), v_ref[rows, :].astype(F32),
                         gb_ref[rows, :].astype(F32), la_ref[rows, :], gla_g, s_ref, (), GLA_CHUNK)
        o_ref[rows, :] = out.astype(o_ref.dtype)


def _gla_prompt(proj, la, gla_g, *, batch, seq, rows):
    nstep = seq // rows
    rb = lambda b, i: b * nstep + i
    return pl.pallas_call(
        functools.partial(_gla_prompt_kernel, nchunk=rows // GLA_CHUNK),
        grid=(batch, nstep),
        in_specs=[pl.BlockSpec((rows, H_B * DK_B), lambda b, i: (rb(b, i), EV_QB // (H_B * DK_B))),
                  pl.BlockSpec((rows, H_B * DK_B), lambda b, i: (rb(b, i), EV_KB // (H_B * DK_B))),
                  pl.BlockSpec((rows, W_B), lambda b, i: (rb(b, i), EV_VB // W_B)),
                  pl.BlockSpec((rows, W_B), lambda b, i: (rb(b, i), EV_GB // W_B)),
                  pl.BlockSpec((rows, H_B * DK_B), lambda b, i: (rb(b, i), 0)),
                  pl.BlockSpec((1, DV_B), lambda b, i: (0, 0))],
        out_specs=[pl.BlockSpec((rows, W_B), lambda b, i: (rb(b, i), 0)),
                   pl.BlockSpec((None, H_B, DK_B, DV_B), lambda b, i: (b, 0, 0, 0))],
        out_shape=[jax.ShapeDtypeStruct((batch * seq, W_B), BF16),
                   jax.ShapeDtypeStruct((batch, H_B, DK_B, DV_B), F32)],
        compiler_params=_cparams(("arbitrary", "arbitrary")),
        name="gla_prompt",
    )(proj, proj, proj, proj, la, gla_g)


def _gla_sample_kernel(q_ref, k_ref, v_ref, gb_ref, la_ref, g_ref, s_in_ref, o_ref, s_ref, *, sb, ntok):
    gla_g = g_ref[...]
    s_ref[...] = s_in_ref[...]

    def body(s, carry):
        rows = pl.ds(pl.multiple_of(s * SAMPLE_PAD, SAMPLE_PAD), SAMPLE_PAD)
        out = _gla_chunk(q_ref[rows, :], k_ref[rows, :], v_ref[rows, :], gb_ref[rows, :], la_ref[rows, :],
                         gla_g, s_ref, (s,), ntok)
        o_ref[rows, :] = out.astype(o_ref.dtype)
        return carry

    lax.fori_loop(0, sb, body, 0)


def _gla_sample(proj, la, gla_g, state, *, sb, ntok):
    nseq = state.shape[0]
    rows = sb * SAMPLE_PAD
    return pl.pallas_call(
        functools.partial(_gla_sample_kernel, sb=sb, ntok=ntok),
        grid=(nseq // sb,),
        in_specs=[pl.BlockSpec((rows, H_B * DK_B), lambda i: (i, EV_QB // (H_B * DK_B))),
                  pl.BlockSpec((rows, H_B * DK_B), lambda i: (i, EV_KB // (H_B * DK_B))),
                  pl.BlockSpec((rows, W_B), lambda i: (i, EV_VB // W_B)),
                  pl.BlockSpec((rows, W_B), lambda i: (i, EV_GB // W_B)),
                  pl.BlockSpec((rows, H_B * DK_B), lambda i: (i, 0)),
                  pl.BlockSpec((1, DV_B), lambda i: (0, 0)),
                  pl.BlockSpec((sb, H_B, DK_B, DV_B), lambda i: (i, 0, 0, 0))],
        out_specs=[pl.BlockSpec((rows, W_B), lambda i: (i, 0)),
                   pl.BlockSpec((sb, H_B, DK_B, DV_B), lambda i: (i, 0, 0, 0))],
        out_shape=[jax.ShapeDtypeStruct((nseq * SAMPLE_PAD, W_B), BF16),
                   jax.ShapeDtypeStruct(state.shape, F32)],
        compiler_params=_cparams(("arbitrary",)),
        name="gla_sample",
    )(proj, proj, proj, proj, la, gla_g, state)


def _ret_chunk(q_ref, k_ref, v_ref, g_ref, retg, state_ref, o_ref, n_valid):
    c = q_ref.shape[0]
    ri = lax.broadcasted_iota(jnp.int32, (c, c), 0)
    ci = lax.broadcasted_iota(jnp.int32, (c, c), 1)
    dist = (ri - ci).astype(F32)
    row = lax.broadcasted_iota(jnp.int32, (c, 1), 0)
    rowf = row.astype(F32)
    for h in range(H_C):
        lg = LOG_GAMMA[h]
        ks = slice(h * DK_C, (h + 1) * DK_C)
        vs = slice(h * DV_C, (h + 1) * DV_C)
        decay = jnp.where(ri >= ci, jnp.exp(jnp.maximum(dist, 0.0) * lg), 0.0)
        qb = q_ref[:, ks].astype(BF16)
        kf = k_ref[:, ks].astype(F32)
        vb = v_ref[:, vs].astype(BF16)
        a = _dot_nt(qb, kf.astype(BF16)) * decay
        st = state_ref[h]
        inner = jnp.exp((rowf + 1.0) * lg)
        o = _dot(a.astype(BF16), vb) + inner * _dot(qb, st.astype(BF16))
        kd = kf * jnp.exp((n_valid - 1.0 - rowf) * lg)
        if n_valid < c:
            kd = jnp.where(row < n_valid, kd, 0.0)
        state_ref[h] = math.exp(n_valid * lg) * st + _dot_tn(kd.astype(BF16), vb)
        on = o * lax.rsqrt(jnp.mean(o * o, axis=-1, keepdims=True) + EPS) * retg
        o_ref[:, vs] = (on * _silu(g_ref[:, vs].astype(F32))).astype(o_ref.dtype)


def _ret_prompt_kernel(q_ref, k_ref, v_ref, g_ref, retg_ref, o_ref, s_ref):
    @pl.when(pl.program_id(1) == 0)
    def _():
        s_ref[...] = jnp.zeros(s_ref.shape, F32)

    _ret_chunk(q_ref, k_ref, v_ref, g_ref, retg_ref[...], s_ref, o_ref, RET_CHUNK)


def _ret_prompt(proj, ret_g, *, batch, seq):
    nstep = seq // RET_CHUNK
    rb = lambda b, i: b * nstep + i
    qk = H_C * DK_C
    return pl.pallas_call(
        _ret_prompt_kernel,
        grid=(batch, nstep),
        in_specs=[pl.BlockSpec((RET_CHUNK, qk), lambda b, i: (rb(b, i), 0)),
                  pl.BlockSpec((RET_CHUNK, qk), lambda b, i: (rb(b, i), 1)),
                  pl.BlockSpec((RET_CHUNK, W_C), lambda b, i: (rb(b, i), 1)),
                  pl.BlockSpec((RET_CHUNK, W_C), lambda b, i: (rb(b, i), 2)),
                  pl.BlockSpec((1, DV_C), lambda b, i: (0, 0))],
        out_specs=[pl.BlockSpec((RET_CHUNK, W_C), lambda b, i: (rb(b, i), 0)),
                   pl.BlockSpec((None, H_C, DK_C, DV_C), lambda b, i: (b, 0, 0, 0))],
        out_shape=[jax.ShapeDtypeStruct((batch * seq, W_C), BF16),
                   jax.ShapeDtypeStruct((batch, H_C, DK_C, DV_C), F32)],
        compiler_params=_cparams(("arbitrary", "arbitrary")),
        name="ret_prompt",
    )(proj, proj, proj, proj, ret_g)


def _ret_sample_kernel(q_ref, k_ref, v_ref, g_ref, retg_ref, s_in_ref, o_ref, s_ref, *, ntok):
    s_ref[...] = s_in_ref[...]
    _ret_chunk(q_ref, k_ref, v_ref, g_ref, retg_ref[...], s_ref, o_ref, ntok)


def _ret_sample(proj, ret_g, state, *, ntok):
    nseq = state.shape[0]
    qk = H_C * DK_C
    return pl.pallas_call(
        functools.partial(_ret_sample_kernel, ntok=ntok),
        grid=(nseq,),
        in_specs=[pl.BlockSpec((SAMPLE_PAD, qk), lambda i: (i, 0)),
                  pl.BlockSpec((SAMPLE_PAD, qk), lambda i: (i, 1)),
                  pl.BlockSpec((SAMPLE_PAD, W_C), lambda i: (i, 1)),
                  pl.BlockSpec((SAMPLE_PAD, W_C), lambda i: (i, 2)),
                  pl.BlockSpec((1, DV_C), lambda i: (0, 0)),
                  pl.BlockSpec((None, H_C, DK_C, DV_C), lambda i: (i, 0, 0, 0))],
        out_specs=[pl.BlockSpec((SAMPLE_PAD, W_C), lambda i: (i, 0)),
                   pl.BlockSpec((None, H_C, DK_C, DV_C), lambda i: (i, 0, 0, 0))],
        out_shape=[jax.ShapeDtypeStruct((nseq * SAMPLE_PAD, W_C), BF16),
                   jax.ShapeDtypeStruct(state.shape, F32)],
        compiler_params=_cparams(("arbitrary",)),
        name="ret_sample",
    )(proj, proj, proj, proj, ret_g, state)


def _t5_bucket(dist):
    dist = jnp.maximum(dist, 0)
    max_exact = NUM_BUCKETS // 2
    log_ratio = jnp.log(jnp.maximum(dist, 1).astype(F32) / max_exact) / math.log(MAX_DISTANCE / max_exact)
    large = jnp.minimum(max_exact + (log_ratio * (NUM_BUCKETS - max_exact)).astype(jnp.int32), NUM_BUCKETS - 1)
    return jnp.where(dist < max_exact, dist, large)


def _bucket_or_masked(dist):
    return jnp.where((dist >= 0) & (dist <= WINDOW), _t5_bucket(dist), -1).astype(jnp.int32)


def _rotary_tables(pos):
    half = DK_C // 2
    inv = ROPE_BASE ** (-jnp.arange(half, dtype=F32) / half)
    ang = pos.astype(F32)[:, None] * inv[None, :]
    return jnp.cos(ang), jnp.sin(ang)


def _pair_heads(a, axis):
    shp = a.shape
    a = a.reshape(shp[:axis] + (H_A_KV, G_A, HD_A) + shp[axis + 1:])
    return jnp.swapaxes(a, axis, axis + 1).reshape(shp)


def _even_weight_layout(w_in):
    qa, ka, va, ga, qb, kb, vb, gb, lr = jnp.split(w_in, [1024, 1152, 1280, 2304, 2816, 3328, 4352, 5376], axis=1)
    main = jnp.concatenate([_pair_heads(qa, 1), _pair_heads(ga, 1), vb, gb, qb, kb, ka, va], axis=1).astype(BF16)
    lr_pad = jnp.pad(lr, ((0, 0), (0, 128 - GLA_RANK))).astype(BF16)
    return main, lr_pad


def _pad_tokens(a, ntok):
    pad = [(0, 0), (0, SAMPLE_PAD - ntok)] + [(0, 0)] * (a.ndim - 2)
    a = jnp.pad(a, pad)
    return a.reshape((a.shape[0] * SAMPLE_PAD,) + a.shape[2:])


def _rows(v, reps):
    return jnp.repeat(v, reps, axis=0)


def kernel(x_prompt, x_sample, cache_swa_k, cache_swa_v, state_gla, state_ret, c_prompt, c_sample, rel_bias,
           ada_w_even, ada_b_even, norm_g_even, w_in_even, w_lr_even, b_lr_even, qn_g_even, kn_g_even,
           sinks_even, gla_g_even, w_out_even, ada_w_odd, ada_b_odd, norm_g_odd, w_in_odd, ret_g_odd, w_out_odd):
    batch, seq, _ = x_prompt.shape
    nseq, ntok, _ = x_sample.shape
    mp = batch * seq

    c_all = jnp.concatenate([c_prompt, c_sample], axis=0)
    mod_e = _ada_mod(c_all, ada_w_even[0], ada_b_even[0])
    mod_o = _ada_mod(c_all, ada_w_odd[0], ada_b_odd[0])

    def split_mod(mod):
        shift, scale, gate = jnp.split(mod, 3, axis=1)
        p = tuple(a[:batch].reshape(batch, 1, D_MODEL) for a in (shift, scale, gate))
        s = tuple(_rows(a[batch:], SAMPLE_PAD) for a in (shift, scale, gate))
        return p, s

    (shift_ep, scale_ep, gate_ep), (shift_es, scale_es, gate_es) = split_mod(mod_e)
    (shift_op, scale_op, gate_op), (shift_os, scale_os, gate_os) = split_mod(mod_o)

    xp = x_prompt.reshape(mp, D_MODEL)
    xs = _pad_tokens(x_sample, ntok)
    ms = xs.shape[0]

    w_e, w_lri = _even_weight_layout(w_in_even[0])
    w_lr = jnp.pad(w_lr_even[0], ((0, 128 - GLA_RANK), (0, 0))).astype(BF16)
    b_lr = b_lr_even[0].reshape(1, -1)
    g_e = norm_g_even[0].reshape(1, D_MODEL)
    qn2 = jnp.tile(qn_g_even[0].reshape(1, HD_A), (1, 2))
    kn2 = jnp.tile(kn_g_even[0].reshape(1, HD_A), (1, 2))
    gla_g = gla_g_even[0].reshape(1, DV_B)
    w_out_a = _pair_heads(w_out_even[0][:W_A], 0).astype(BF16)
    w_out_b = w_out_even[0][W_A:].astype(BF16)

    proj_p, la_p = _inproj_even(xp, scale_ep, shift_ep, g_e, w_e, w_lri, w_lr, b_lr,
                                tm=512, tn=768, rows_per_batch=seq, out_dtype=BF16)
    proj_s, la_s = _inproj_even(xs, scale_es, shift_es, g_e, w_e, w_lri, w_lr, b_lr,
                                tm=512, tn=768, rows_per_batch=ms, out_dtype=F32)

    ii = jnp.arange(WINDOW)
    ss = jnp.arange(2 * WINDOW)
    bucket_p = _bucket_or_masked(WINDOW + ii[:, None] - ss[None, :])
    mixed_a_p, k_last = _swa_prompt(proj_p, rel_bias, sinks_even[0], qn2, kn2, bucket_p, batch=batch, seq=seq)
    mixed_b_p, gla_p = _gla_prompt(proj_p, la_p, gla_g, batch=batch, seq=seq, rows=256)
    y1_p = _outproj([mixed_a_p, mixed_b_p], [w_out_a, w_out_b], xp, gate_ep, tm=512, tn=1024, rows_per_batch=seq)

    proj_p3 = proj_p.reshape(batch, seq, EV_N)
    swa_k_p = k_last.reshape(1, batch, WINDOW, H_A_KV, HD_A)
    swa_v_p = proj_p3[:, seq - WINDOW:, EV_KV + W_K: EV_KV + 2 * W_K].astype(F32).reshape(
        1, batch, WINDOW, H_A_KV, HD_A)

    tt = jnp.tile(jnp.arange(SAMPLE_PAD), H_A)
    jj = jnp.arange(WINDOW)
    live = (tt < ntok)[:, None]
    bkc = jnp.where(live, _bucket_or_masked(WINDOW + tt[:, None] - jj[None, :]), -1)
    bkn = jnp.where(live & (jj[None, :] < ntok), _bucket_or_masked(tt[:, None] - jj[None, :]), -1)
    relrows = jnp.repeat(rel_bias.T, SAMPLE_PAD, axis=0)
    sinkrows = jnp.broadcast_to(jnp.repeat(sinks_even[0], SAMPLE_PAD)[:, None], (H_A * SAMPLE_PAD, 128))
    w_buf = cache_swa_k.shape[2]
    mixed_a_s, k_cache_s, v_cache_s = _swa_sample(
        proj_s, cache_swa_k[0].reshape(nseq, w_buf, W_K), cache_swa_v[0].reshape(nseq, w_buf, W_K),
        bkc, bkn, relrows, sinkrows, qn2, kn2, sb=16, ntok=ntok, unroll=4)
    mixed_b_s, gla_s = _gla_sample(proj_s, la_s, gla_g, state_gla[0], sb=8, ntok=ntok)
    y1_s = _outproj([mixed_a_s, mixed_b_s], [w_out_a, w_out_b], xs, gate_es, tm=512, tn=1024, rows_per_batch=ms)

    w_o = w_in_odd[0].astype(BF16)
    g_o = norm_g_odd[0].reshape(1, D_MODEL)
    ret_g = ret_g_odd[0].reshape(1, DV_C)
    w_out_o = w_out_odd[0].astype(BF16)
    cos_p, sin_p = _rotary_tables(jnp.arange(seq))
    pos_s = PAST_LEN + jnp.minimum(jnp.arange(SAMPLE_PAD), ntok - 1)
    cos_s, sin_s = _rotary_tables(jnp.tile(pos_s, nseq))

    projo_p = _inproj_odd(y1_p, scale_op, shift_op, g_o, w_o, cos_p, sin_p,
                          tm=512, tn=1024, rows_per_batch=seq, out_dtype=BF16)
    o_p, ret_p = _ret_prompt(projo_p, ret_g, batch=batch, seq=seq)
    y2_p = _outproj([o_p], [w_out_o], y1_p, gate_op, tm=512, tn=1024, rows_per_batch=seq)

    projo_s = _inproj_odd(y1_s, scale_os, shift_os, g_o, w_o, cos_s, sin_s,
                          tm=512, tn=1024, rows_per_batch=ms, out_dtype=F32)
    o_s, ret_s = _ret_sample(projo_s, ret_g, state_ret[0], ntok=ntok)
    y2_s = _outproj([o_s], [w_out_o], y1_s, gate_os, tm=512, tn=1024, rows_per_batch=ms)

    y_prompt = y2_p.reshape(batch, seq, D_MODEL)
    y_sample = y2_s.reshape(nseq, SAMPLE_PAD, D_MODEL)[:, :ntok]
    return (y_prompt, y_sample, swa_k_p, swa_v_p, gla_p[None], ret_p[None],
            k_cache_s.reshape(1, nseq, w_buf, H_A_KV, HD_A), v_cache_s.reshape(1, nseq, w_buf, H_A_KV, HD_A),
            gla_s[None], ret_s[None])
```

```python
import functools
import math

import numpy as np
import jax
import jax.numpy as jnp
from jax import lax
from jax.experimental import pallas as pl
from jax.experimental.pallas import tpu as pltpu

F32 = jnp.float32
BF16 = jnp.bfloat16

D_MODEL = 2048
WINDOW = 128
HD_A = 64
H_A = 16
H_A_KV = 2
G_A = 8
W_A = 1024
NUM_BUCKETS = 32
MAX_DISTANCE = 128
H_B = 4
DV_B = 256
DK_B = 128
W_B = 1024
GLA_RANK = 16
GLA_TAU = 16.0
GLA_CHUNK = 64
H_C = 8
DK_C = 256
DV_C = 512
W_C = 4096
RET_CHUNK = 128
ROPE_BASE = 10000.0
EPS = 1e-6
NEG_INF = -1e30
PAST_LEN = 8192

EV_QA, EV_GA, EV_VB, EV_GB, EV_QB, EV_KB, EV_KV = 0, 1024, 2048, 3072, 4096, 4608, 5120
EV_N = 5376
OD_N = 12288
W_K = H_A_KV * HD_A
SAMPLE_PAD = 8
MXU_N = 256
VMEM_LIMIT = 56 * 1024 * 1024

LOG_GAMMA = [float(np.log1p(-np.exp2(np.float32(-5.0 - h)))) for h in range(H_C)]


def _cparams(sem):
    return pltpu.CompilerParams(dimension_semantics=sem, vmem_limit_bytes=VMEM_LIMIT)


def _silu(x):
    return x / (1.0 + jnp.exp(-x))


def _dot(a, b):
    return jnp.dot(a, b, preferred_element_type=F32)


def _dot_nt(a, b):
    return lax.dot_general(a, b, (((1,), (1,)), ((), ())), preferred_element_type=F32)


def _dot_tn(a, b):
    return lax.dot_general(a, b, (((0,), (0,)), ((), ())), preferred_element_type=F32)


def _cumsum_rows(x):
    c = x.shape[0]
    r = lax.broadcasted_iota(jnp.int32, (c, c), 0)
    s = lax.broadcasted_iota(jnp.int32, (c, c), 1)
    tri = (r >= s).astype(BF16)
    hi = x.astype(BF16)
    r1 = x - hi.astype(F32)
    mid = r1.astype(BF16)
    lo = (r1 - mid.astype(F32)).astype(BF16)
    return _dot(tri, hi) + _dot(tri, mid) + _dot(tri, lo)


def _row_to_col(r):
    n = r.shape[1]
    ri = lax.broadcasted_iota(jnp.int32, (n, n), 0)
    ci = lax.broadcasted_iota(jnp.int32, (n, n), 1)
    return jnp.sum(jnp.where(ri == ci, jnp.broadcast_to(r, (n, n)), 0.0), axis=1, keepdims=True)


def _ada_kernel(c_ref, w_ref, b_ref, o_ref):
    sc = _silu(c_ref[...]).astype(BF16)
    o_ref[...] = _dot(sc, w_ref[...].astype(BF16)) + b_ref[...]


def _ada_mod(c_all, w, b):
    m = c_all.shape[0]
    n = w.shape[1]
    tn = 768
    return pl.pallas_call(
        _ada_kernel,
        grid=(n // tn,),
        in_specs=[pl.BlockSpec((m, D_MODEL), lambda j: (0, 0)),
                  pl.BlockSpec((D_MODEL, tn), lambda j: (0, j)),
                  pl.BlockSpec((1, tn), lambda j: (0, j))],
        out_specs=pl.BlockSpec((m, tn), lambda j: (0, j)),
        out_shape=jax.ShapeDtypeStruct((m, n), F32),
        compiler_params=_cparams(("arbitrary",)),
        name="ada_mod",
    )(c_all, w, b.reshape(1, n))


def _norm_mod(x, g, scale, shift):
    ms = jnp.mean(x * x, axis=-1, keepdims=True)
    y = x * lax.rsqrt(ms + EPS) * g
    return y * (1.0 + scale) + shift


def _log_sigmoid(z):
    return jnp.minimum(z, 0.0) - jnp.log(1.0 + jnp.exp(-jnp.abs(z)))


def _inproj_even_kernel(x_ref, scale_ref, shift_ref, g_ref, w_ref, wlri_ref, wlr_ref, blr_ref,
                        o_ref, la_ref, h_ref):
    @pl.when(pl.program_id(1) == 0)
    def _():
        hb = _norm_mod(x_ref[...], g_ref[...], scale_ref[...], shift_ref[...]).astype(BF16)
        h_ref[...] = hb
        lr = _dot(hb, wlri_ref[...])
        z = _dot(lr.astype(BF16), wlr_ref[...]) + blr_ref[...]
        la_ref[...] = _log_sigmoid(z) / GLA_TAU

    for c in range(o_ref.shape[1] // MXU_N):
        cs = slice(c * MXU_N, (c + 1) * MXU_N)
        o_ref[:, cs] = _dot(h_ref[...], w_ref[:, cs]).astype(o_ref.dtype)


def _inproj_odd_kernel(x_ref, scale_ref, shift_ref, g_ref, w_ref, cos_ref, sin_ref, o_ref, h_ref, *, tn):
    j = pl.program_id(1)

    @pl.when(j == 0)
    def _():
        h_ref[...] = _norm_mod(x_ref[...], g_ref[...], scale_ref[...], shift_ref[...]).astype(BF16)

    nq = (H_C * DK_C) // tn
    half = DK_C // 2

    @pl.when(j < 2 * nq)
    def _():
        mult = jnp.where(j < nq, 1.0, DK_C ** -0.5)
        cos = cos_ref[...] * mult
        sin = sin_ref[...] * mult
        for c in range(tn // DK_C):
            acc = _dot(h_ref[...], w_ref[:, c * DK_C:(c + 1) * DK_C])
            x1 = acc[:, :half]
            x2 = acc[:, half:]
            o_ref[:, c * DK_C:c * DK_C + half] = (x1 * cos - x2 * sin).astype(o_ref.dtype)
            o_ref[:, c * DK_C + half:(c + 1) * DK_C] = (x2 * cos + x1 * sin).astype(o_ref.dtype)

    @pl.when(j >= 2 * nq)
    def _():
        for c in range(tn // MXU_N):
            cs = slice(c * MXU_N, (c + 1) * MXU_N)
            o_ref[:, cs] = _dot(h_ref[...], w_ref[:, cs]).astype(o_ref.dtype)


def _mod_spec(arr, tm, rows_per_batch):
    if arr.ndim == 3:
        return pl.BlockSpec((None, 1, D_MODEL), lambda i, j: (i // (rows_per_batch // tm), 0, 0))
    return pl.BlockSpec((tm, D_MODEL), lambda i, j: (i, 0))


def _inproj_even(x, scale, shift, g, w, wlri, wlr, blr, *, tm, tn, rows_per_batch, out_dtype):
    m = x.shape[0]
    return pl.pallas_call(
        _inproj_even_kernel,
        grid=(m // tm, EV_N // tn),
        in_specs=[pl.BlockSpec((tm, D_MODEL), lambda i, j: (i, 0)),
                  _mod_spec(scale, tm, rows_per_batch),
                  _mod_spec(shift, tm, rows_per_batch),
                  pl.BlockSpec((1, D_MODEL), lambda i, j: (0, 0)),
                  pl.BlockSpec((D_MODEL, tn), lambda i, j: (0, j)),
                  pl.BlockSpec((D_MODEL, 128), lambda i, j: (0, 0)),
                  pl.BlockSpec((128, H_B * DK_B), lambda i, j: (0, 0)),
                  pl.BlockSpec((1, H_B * DK_B), lambda i, j: (0, 0))],
        out_specs=[pl.BlockSpec((tm, tn), lambda i, j: (i, j)),
                   pl.BlockSpec((tm, H_B * DK_B), lambda i, j: (i, 0))],
        out_shape=[jax.ShapeDtypeStruct((m, EV_N), out_dtype),
                   jax.ShapeDtypeStruct((m, H_B * DK_B), F32)],
        scratch_shapes=[pltpu.VMEM((tm, D_MODEL), BF16)],
        compiler_params=_cparams(("arbitrary", "arbitrary")),
        name="inproj_even",
    )(x, scale, shift, g, w, wlri, wlr, blr)


def _inproj_odd(x, scale, shift, g, w, cos, sin, *, tm, tn, rows_per_batch, out_dtype):
    m = x.shape[0]
    return pl.pallas_call(
        functools.partial(_inproj_odd_kernel, tn=tn),
        grid=(m // tm, OD_N // tn),
        in_specs=[pl.BlockSpec((tm, D_MODEL), lambda i, j: (i, 0)),
                  _mod_spec(scale, tm, rows_per_batch),
                  _mod_spec(shift, tm, rows_per_batch),
                  pl.BlockSpec((1, D_MODEL), lambda i, j: (0, 0)),
                  pl.BlockSpec((D_MODEL, tn), lambda i, j: (0, j)),
                  pl.BlockSpec((tm, DK_C // 2), lambda i, j: (i % (cos.shape[0] // tm), 0)),
                  pl.BlockSpec((tm, DK_C // 2), lambda i, j: (i % (cos.shape[0] // tm), 0))],
        out_specs=pl.BlockSpec((tm, tn), lambda i, j: (i, j)),
        out_shape=jax.ShapeDtypeStruct((m, OD_N), out_dtype),
        scratch_shapes=[pltpu.VMEM((tm, D_MODEL), BF16)],
        compiler_params=_cparams(("arbitrary", "arbitrary")),
        name="inproj_odd",
    )(x, scale, shift, g, w, cos, sin)


def _outproj_kernel(*refs, nin):
    m_refs, w_refs = refs[:nin], refs[nin:2 * nin]
    x_ref, gate_ref, o_ref = refs[2 * nin:]
    for c in range(o_ref.shape[1] // MXU_N):
        cs = slice(c * MXU_N, (c + 1) * MXU_N)
        acc = _dot(m_refs[0][...].astype(BF16), w_refs[0][:, cs])
        for m_ref, w_ref in zip(m_refs[1:], w_refs[1:]):
            acc += _dot(m_ref[...].astype(BF16), w_ref[:, cs])
        o_ref[:, cs] = x_ref[:, cs] + gate_ref[:, cs] * acc


def _outproj(mixed, ws, x, gate, *, tm, tn, rows_per_batch):
    m = x.shape[0]
    if gate.ndim == 3:
        gate_spec = pl.BlockSpec((None, 1, tn), lambda i, j: (i // (rows_per_batch // tm), 0, j))
    else:
        gate_spec = pl.BlockSpec((tm, tn), lambda i, j: (i, j))
    return pl.pallas_call(
        functools.partial(_outproj_kernel, nin=len(mixed)),
        grid=(m // tm, D_MODEL // tn),
        in_specs=([pl.BlockSpec((tm, a.shape[1]), lambda i, j: (i, 0)) for a in mixed]
                  + [pl.BlockSpec((w.shape[0], tn), lambda i, j: (0, j)) for w in ws]
                  + [pl.BlockSpec((tm, tn), lambda i, j: (i, j)), gate_spec]),
        out_specs=pl.BlockSpec((tm, tn), lambda i, j: (i, j)),
        out_shape=jax.ShapeDtypeStruct((m, D_MODEL), F32),
        compiler_params=_cparams(("arbitrary", "arbitrary")),
        name="outproj",
    )(*mixed, *ws, x, gate)


def _seg_rms(x, g2, seg_ones):
    x2 = x * x
    hi = x2.astype(BF16)
    lo = (x2 - hi.astype(F32)).astype(BF16)
    ss = _dot(hi, seg_ones) + _dot(lo, seg_ones)
    return x * lax.rsqrt(ss * (1.0 / HD_A) + EPS) * g2


def _dup_halves(x, lo_half):
    sw = pltpu.roll(x, HD_A, 1)
    return [jnp.where(lo_half, x, sw), jnp.where(lo_half, sw, x)]


def _seg_ones():
    r = lax.broadcasted_iota(jnp.int32, (2 * HD_A, 2 * HD_A), 0)
    c = lax.broadcasted_iota(jnp.int32, (2 * HD_A, 2 * HD_A), 1)
    return ((r < HD_A) == (c < HD_A)).astype(BF16)


def _swa_prompt_kernel(relb_ref, sinks_ref, bucket_ref, q_ref, ga_ref, kv_ref, qn_ref, kn_ref,
                       o_ref, knew_ref, bias_ref, kprev_ref, vprev_ref, s_ref, pe_ref):
    b = pl.program_id(0)
    i = pl.program_id(1)

    @pl.when((b == 0) & (i == 0))
    def _():
        bk = bucket_ref[...]
        for h in range(H_A):
            bias_ref[h] = jnp.full((WINDOW, 2 * WINDOW), NEG_INF, F32)
        for bb in range(NUM_BUCKETS):
            hit = bk == bb
            for h in range(H_A):
                bias_ref[h] = jnp.where(hit, relb_ref[bb, h], bias_ref[h])
        col = lax.broadcasted_iota(jnp.int32, (WINDOW, 2 * WINDOW), 1)
        for h in range(H_A):
            bias_ref[H_A + h] = jnp.where(col >= WINDOW, bias_ref[h], NEG_INF)

    @pl.when(i == 0)
    def _():
        kprev_ref[...] = jnp.zeros(kprev_ref.shape, BF16)
        vprev_ref[...] = jnp.zeros(vprev_ref.shape, BF16)

    seg_ones = _seg_ones()
    lt = 2 * HD_A
    tpg = G_A // 2
    lo_half = lax.broadcasted_iota(jnp.int32, (WINDOW, lt), 1) < HD_A
    kv = kv_ref[...].astype(F32)
    kc = _seg_rms(kv[:, :W_K], kn_ref[...], seg_ones)
    k_dup = _dup_halves(kc, lo_half)
    v_dup = _dup_halves(kv[:, W_K:], lo_half)
    base = jnp.where(i == 0, H_A, 0)
    qn2 = qn_ref[...] * HD_A ** -0.5
    gq = tpg * WINDOW
    lo_g = lax.broadcasted_iota(jnp.int32, (gq, lt), 1) < HD_A
    grows = [slice(g * 2 * gq, (g + 1) * 2 * gq) for g in range(H_A_KV)]
    for g in range(H_A_KV):
        kk = jnp.concatenate([kprev_ref[g], k_dup[g].astype(BF16)], axis=0)
        xg = jnp.concatenate([q_ref[:, (g * tpg + p) * lt:(g * tpg + p + 1) * lt] for p in range(tpg)],
                             axis=0).astype(F32)
        xn = _seg_rms(xg, qn2, seg_ones)
        q_all = jnp.concatenate([jnp.where(lo_g, xn, 0.0), jnp.where(lo_g, 0.0, xn)], axis=0).astype(BF16)
        s_ref[grows[g], :] = _dot_nt(q_all, kk)
    sink_terms = []
    for g in range(H_A_KV):
        for n in range(G_A):
            h = g * G_A + 2 * (n % tpg) + n // tpg
            rows = slice((g * G_A + n) * WINDOW, (g * G_A + n + 1) * WINDOW)
            s = s_ref[rows, :] + bias_ref[base + h]
            sink = sinks_ref[h]
            m = jnp.maximum(jnp.max(s, axis=-1, keepdims=True), sink)
            pe_ref[rows, :] = jnp.exp(s - m).astype(BF16)
            sink_terms.append(jnp.exp(sink - m))
    o_ext = []
    for g in range(H_A_KV):
        vv = jnp.concatenate([vprev_ref[g], v_dup[g].astype(BF16)], axis=0)
        vv_ext = jnp.concatenate([vv, jnp.ones(vv.shape, BF16)], axis=1)
        o_ext.append(_dot(pe_ref[grows[g], :], vv_ext))
    for g in range(H_A_KV):
        for p in range(tpg):
            halves = []
            for a in range(2):
                n = a * tpg + p
                rows = slice(n * WINDOW, (n + 1) * WINDOW)
                halves.append(o_ext[g][rows, :lt] / (o_ext[g][rows, lt:] + sink_terms[g * G_A + n]))
            oa = jnp.where(lo_half, halves[0], halves[1])
            cols = slice((g * tpg + p) * lt, (g * tpg + p + 1) * lt)
            o_ref[:, cols] = (oa * _silu(ga_ref[:, cols].astype(F32))).astype(o_ref.dtype)
    for g in range(H_A_KV):
        kprev_ref[g] = k_dup[g].astype(BF16)
        vprev_ref[g] = v_dup[g].astype(BF16)

    @pl.when(i == pl.num_programs(1) - 1)
    def _():
        knew_ref[...] = kc


def _swa_prompt(proj, rel_bias, sinks, qn2, kn2, bucket, *, batch, seq):
    nb = seq // WINDOW
    rb = lambda b, i: b * nb + i
    smem = pl.BlockSpec(memory_space=pltpu.SMEM)
    return pl.pallas_call(
        _swa_prompt_kernel,
        grid=(batch, nb),
        in_specs=[smem, smem,
                  pl.BlockSpec((WINDOW, 2 * WINDOW), lambda b, i: (0, 0)),
                  pl.BlockSpec((WINDOW, W_A), lambda b, i: (rb(b, i), EV_QA // W_A)),
                  pl.BlockSpec((WINDOW, W_A), lambda b, i: (rb(b, i), EV_GA // W_A)),
                  pl.BlockSpec((WINDOW, 2 * W_K), lambda b, i: (rb(b, i), EV_KV // (2 * W_K))),
                  pl.BlockSpec((1, 2 * HD_A), lambda b, i: (0, 0)),
                  pl.BlockSpec((1, 2 * HD_A), lambda b, i: (0, 0))],
        out_specs=[pl.BlockSpec((WINDOW, W_A), lambda b, i: (rb(b, i), 0)),
                   pl.BlockSpec((None, WINDOW, W_K), lambda b, i: (b, 0, 0))],
        out_shape=[jax.ShapeDtypeStruct((batch * seq, W_A), BF16),
                   jax.ShapeDtypeStruct((batch, WINDOW, W_K), F32)],
        scratch_shapes=[pltpu.VMEM((2 * H_A, WINDOW, 2 * WINDOW), F32),
                        pltpu.VMEM((H_A_KV, WINDOW, W_K), BF16), pltpu.VMEM((H_A_KV, WINDOW, W_K), BF16),
                        pltpu.VMEM((H_A * WINDOW, 2 * WINDOW), F32), pltpu.VMEM((H_A * WINDOW, 2 * WINDOW), BF16)],
        compiler_params=_cparams(("arbitrary", "arbitrary")),
        name="swa_prompt",
    )(rel_bias, sinks, bucket, proj, proj, proj, qn2, kn2)


def _swa_sample_kernel(bkc_ref, bkn_ref, relrows_ref, sinkrows_ref, q_ref, ga_ref, kvn_ref, ck_ref, cv_ref,
                       qn_ref, kn_ref, o_ref, ko_ref, vo_ref, biasc_ref, biasn_ref, *, sb, ntok, unroll):
    @pl.when(pl.program_id(0) == 0)
    def _():
        bkc = bkc_ref[...]
        bkn = bkn_ref[...]
        rr = relrows_ref[...]
        bc = jnp.full(bkc.shape, NEG_INF, F32)
        bn = jnp.full(bkn.shape, NEG_INF, F32)
        for bb in range(NUM_BUCKETS):
            val = rr[:, bb:bb + 1]
            bc = jnp.where(bkc == bb, val, bc)
            bn = jnp.where(bkn == bb, val, bn)
        biasc_ref[...] = bc
        biasn_ref[...] = bn

    seg_ones = _seg_ones()
    lt = 2 * HD_A
    nrow = G_A * SAMPLE_PAD
    grow = nrow // H_A_KV
    lo_q = lax.broadcasted_iota(jnp.int32, (grow, lt), 1) < HD_A
    lo_k = lax.broadcasted_iota(jnp.int32, (WINDOW, lt), 1) < HD_A
    lo_n = lax.broadcasted_iota(jnp.int32, (SAMPLE_PAD, lt), 1) < HD_A
    qn2 = qn_ref[...] * HD_A ** -0.5
    kn2 = kn_ref[...]
    sink = sinkrows_ref[:, 0:1]
    nkeep = WINDOW - ntok

    def body(it, carry):
        seqs = [it * unroll + u for u in range(unroll)]
        rows = [pl.ds(pl.multiple_of(s * SAMPLE_PAD, SAMPLE_PAD), SAMPLE_PAD) for s in seqs]
        kvn = [kvn_ref[r, :] for r in rows]
        kn_all = _seg_rms(jnp.concatenate([x[:, :W_K] for x in kvn], axis=0), kn2, seg_ones)
        q2_all = jnp.concatenate([q_ref[r, p * lt:(p + 1) * lt] for r in rows for p in range(G_A)], axis=0)
        xn_all = _seg_rms(q2_all, qn2, seg_ones)
        sc, sn, kn = [], [], []
        for u, s in enumerate(seqs):
            kn.append(kn_all[u * SAMPLE_PAD:(u + 1) * SAMPLE_PAD])
            ck_dup = _dup_halves(ck_ref[s], lo_k)
            kn_dup = _dup_halves(kn[u], lo_n)
            scg, sng = [], []
            for g in range(H_A_KV):
                xn = xn_all[u * nrow + g * grow:u * nrow + (g + 1) * grow]
                q4 = jnp.concatenate([jnp.where(lo_q, xn, 0.0), jnp.where(lo_q, 0.0, xn)], axis=0).astype(BF16)
                scg.append(_dot_nt(q4, ck_dup[g].astype(BF16)))
                sng.append(_dot_nt(q4, kn_dup[g].astype(BF16)))
            sc.append(jnp.concatenate(scg, axis=0))
            sn.append(jnp.concatenate(sng, axis=0))
        pc, pn, sink_terms = [], [], []
        for u in range(unroll):
            scu = sc[u] + biasc_ref[...]
            snu = sn[u] + biasn_ref[:, :SAMPLE_PAD]
            m = jnp.maximum(jnp.maximum(jnp.max(scu, axis=-1, keepdims=True),
                                        jnp.max(snu, axis=-1, keepdims=True)), sink)
            pc.append(jnp.exp(scu - m).astype(BF16))
            pn.append(jnp.exp(snu - m).astype(BF16))
            sink_terms.append(jnp.exp(sink - m))
        o_ext = []
        for u, s in enumerate(seqs):
            cv_dup = _dup_halves(cv_ref[s], lo_k)
            vn_dup = _dup_halves(kvn[u][:, W_K:], lo_n)
            og = []
            for g in range(H_A_KV):
                cv_ext = jnp.concatenate([cv_dup[g].astype(BF16), jnp.ones((WINDOW, lt), BF16)], axis=1)
                vn_ext = jnp.concatenate([vn_dup[g].astype(BF16), jnp.ones((SAMPLE_PAD, lt), BF16)], axis=1)
                gr = slice(g * 2 * grow, (g + 1) * 2 * grow)
                og.append(_dot(pc[u][gr], cv_ext) + _dot(pn[u][gr], vn_ext))
            o_ext.append(jnp.concatenate(og, axis=0))
        for u, s in enumerate(seqs):
            o4 = o_ext[u][:, :lt] / (o_ext[u][:, lt:] + sink_terms[u])
            xg = ga_ref[rows[u], :]
            for p in range(G_A):
                g, pp = divmod(p, G_A // 2)
                r0 = g * 2 * grow + pp * SAMPLE_PAD
                o2 = jnp.where(lo_n, o4[r0:r0 + SAMPLE_PAD], o4[r0 + grow:r0 + grow + SAMPLE_PAD])
                o_ref[rows[u], p * lt:(p + 1) * lt] = (o2 * _silu(xg[:, p * lt:(p + 1) * lt])).astype(o_ref.dtype)
            ko_ref[s, 0:nkeep, :] = ck_ref[s, ntok:WINDOW, :]
            ko_ref[s, nkeep:WINDOW, :] = kn[u][0:ntok]
            vo_ref[s, 0:nkeep, :] = cv_ref[s, ntok:WINDOW, :]
            vo_ref[s, nkeep:WINDOW, :] = kvn[u][0:ntok, W_K:]
        return carry

    lax.fori_loop(0, sb // unroll, body, 0)


def _swa_sample(proj, cache_k, cache_v, bkc, bkn, relrows, sinkrows, qn2, kn2, *, sb, ntok, unroll):
    nseq = cache_k.shape[0]
    rows = sb * SAMPLE_PAD
    full = lambda shape: pl.BlockSpec(shape, lambda i: tuple(0 for _ in shape))
    cache = pl.BlockSpec((sb, WINDOW, W_K), lambda i: (i, 0, 0))
    return pl.pallas_call(
        functools.partial(_swa_sample_kernel, sb=sb, ntok=ntok, unroll=unroll),
        grid=(nseq // sb,),
        in_specs=[full(bkc.shape), full(bkn.shape), full(relrows.shape), full(sinkrows.shape),
                  pl.BlockSpec((rows, W_A), lambda i: (i, EV_QA // W_A)),
                  pl.BlockSpec((rows, W_A), lambda i: (i, EV_GA // W_A)),
                  pl.BlockSpec((rows, 2 * W_K), lambda i: (i, EV_KV // (2 * W_K))),
                  cache, cache, full((1, 2 * HD_A)), full((1, 2 * HD_A))],
        out_specs=[pl.BlockSpec((rows, W_A), lambda i: (i, 0)), cache, cache],
        out_shape=[jax.ShapeDtypeStruct((nseq * SAMPLE_PAD, W_A), F32),
                   jax.ShapeDtypeStruct(cache_k.shape, F32),
                   jax.ShapeDtypeStruct(cache_v.shape, F32)],
        scratch_shapes=[pltpu.VMEM(bkc.shape, F32), pltpu.VMEM(bkn.shape, F32)],
        compiler_params=_cparams(("arbitrary",)),
        name="swa_sample",
    )(bkc, bkn, relrows, sinkrows, proj, proj, proj, cache_k, cache_v, qn2, kn2)


def _gla_chunk(q, k, v, gb, la, gla_g, state_ref, state_idx, n_valid):
    c = q.shape[0]
    bcum = _cumsum_rows(la)
    rr = lax.broadcasted_iota(jnp.int32, (c, c), 0)
    cc = lax.broadcasted_iota(jnp.int32, (c, c), 1)
    causal = rr >= cc
    row = lax.broadcasted_iota(jnp.int32, (c, 1), 0)
    outs = []
    for h in range(H_B):
        ks = slice(h * DK_B, (h + 1) * DK_B)
        vs = slice(h * DV_B, (h + 1) * DV_B)
        bc = bcum[:, ks]
        qt = (q[:, ks] * DK_B ** -0.5) * jnp.exp(bc)
        kt = k[:, ks] * jnp.exp(-bc)
        a = jnp.where(causal, _dot_nt(qt.astype(BF16), kt.astype(BF16)), 0.0)
        st = state_ref[state_idx + (h,)]
        vb = v[:, vs].astype(BF16)
        qtb = qt.astype(BF16)
        o = _dot(a.astype(BF16), vb) + _dot(qtb, st.astype(BF16))
        blast = bc[n_valid - 1:n_valid, :]
        kd = k[:, ks] * jnp.exp(blast - bc)
        if n_valid < c:
            kd = jnp.where(row < n_valid, kd, 0.0)
        upd = _dot_tn(kd.astype(BF16), vb)
        state_ref[state_idx + (h,)] = _row_to_col(jnp.exp(blast)) * st + upd
        on = o * lax.rsqrt(jnp.mean(o * o, axis=-1, keepdims=True) + EPS) * gla_g
        outs.append(on * _silu(gb[:, vs]))
    return jnp.concatenate(outs, axis=1)


def _gla_prompt_kernel(q_ref, k_ref, v_ref, gb_ref, la_ref, g_ref, o_ref, s_ref, *, nchunk):
    @pl.when(pl.program_id(1) == 0)
    def _():
        s_ref[...] = jnp.zeros(s_ref.shape, F32)

    gla_g = g_ref[...]
    for cidx in range(nchunk):
        rows = slice(cidx * GLA_CHUNK, (cidx + 1) * GLA_CHUNK)
        out = _gla_chunk(q_ref[rows, :].astype(F32), k_ref[rows, :].astype(F32), v_ref[rows, :].astype(F32),
                         gb_ref[rows, :].astype(F32), la_ref[rows, :], gla_g, s_ref, (), GLA_CHUNK)
        o_ref[rows, :] = out.astype(o_ref.dtype)


def _gla_prompt(proj, la, gla_g, *, batch, seq, rows):
    nstep = seq // rows
    rb = lambda b, i: b * nstep + i
    return pl.pallas_call(
        functools.partial(_gla_prompt_kernel, nchunk=rows // GLA_CHUNK),
        grid=(batch, nstep),
        in_specs=[pl.BlockSpec((rows, H_B * DK_B), lambda b, i: (rb(b, i), EV_QB // (H_B * DK_B))),
                  pl.BlockSpec((rows, H_B * DK_B), lambda b, i: (rb(b, i), EV_KB // (H_B * DK_B))),
                  pl.BlockSpec((rows, W_B), lambda b, i: (rb(b, i), EV_VB // W_B)),
                  pl.BlockSpec((rows, W_B), lambda b, i: (rb(b, i), EV_GB // W_B)),
                  pl.BlockSpec((rows, H_B * DK_B), lambda b, i: (rb(b, i), 0)),
                  pl.BlockSpec((1, DV_B), lambda b, i: (0, 0))],
        out_specs=[pl.BlockSpec((rows, W_B), lambda b, i: (rb(b, i), 0)),
                   pl.BlockSpec((None, H_B, DK_B, DV_B), lambda b, i: (b, 0, 0, 0))],
        out_shape=[jax.ShapeDtypeStruct((batch * seq, W_B), BF16),
                   jax.ShapeDtypeStruct((batch, H_B, DK_B, DV_B), F32)],
        compiler_params=_cparams(("arbitrary", "arbitrary")),
        name="gla_prompt",
    )(proj, proj, proj, proj, la, gla_g)


def _gla_sample_kernel(q_ref, k_ref, v_ref, gb_ref, la_ref, g_ref, s_in_ref, o_ref, s_ref, *, sb, ntok):
    gla_g = g_ref[...]
    s_ref[...] = s_in_ref[...]

    def body(s, carry):
        rows = pl.ds(pl.multiple_of(s * SAMPLE_PAD, SAMPLE_PAD), SAMPLE_PAD)
        out = _gla_chunk(q_ref[rows, :], k_ref[rows, :], v_ref[rows, :], gb_ref[rows, :], la_ref[rows, :],
                         gla_g, s_ref, (s,), ntok)
        o_ref[rows, :] = out.astype(o_ref.dtype)
        return carry

    lax.fori_loop(0, sb, body, 0)


def _gla_sample(proj, la, gla_g, state, *, sb, ntok):
    nseq = state.shape[0]
    rows = sb * SAMPLE_PAD
    return pl.pallas_call(
        functools.partial(_gla_sample_kernel, sb=sb, ntok=ntok),
        grid=(nseq // sb,),
        in_specs=[pl.BlockSpec((rows, H_B * DK_B), lambda i: (i, EV_QB // (H_B * DK_B))),
                  pl.BlockSpec((rows, H_B * DK_B), lambda i: (i, EV_KB // (H_B * DK_B))),
                  pl.BlockSpec((rows, W_B), lambda i: (i, EV_VB // W_B)),
                  pl.BlockSpec((rows, W_B), lambda i: (i, EV_GB // W_B)),
                  pl.BlockSpec((rows, H_B * DK_B), lambda i: (i, 0)),
                  pl.BlockSpec((1, DV_B), lambda i: (0, 0)),
                  pl.BlockSpec((sb, H_B, DK_B, DV_B), lambda i: (i, 0, 0, 0))],
        out_specs=[pl.BlockSpec((rows, W_B), lambda i: (i, 0)),
                   pl.BlockSpec((sb, H_B, DK_B, DV_B), lambda i: (i, 0, 0, 0))],
        out_shape=[jax.ShapeDtypeStruct((nseq * SAMPLE_PAD, W_B), BF16),
                   jax.ShapeDtypeStruct(state.shape, F32)],
        compiler_params=_cparams(("arbitrary",)),
        name="gla_sample",
    )(proj, proj, proj, proj, la, gla_g, state)


def _ret_chunk(q_ref, k_ref, v_ref, g_ref, retg, state_ref, o_ref, n_valid):
    c = q_ref.shape[0]
    ri = lax.broadcasted_iota(jnp.int32, (c, c), 0)
    ci = lax.broadcasted_iota(jnp.int32, (c, c), 1)
    dist = (ri - ci).astype(F32)
    row = lax.broadcasted_iota(jnp.int32, (c, 1), 0)
    rowf = row.astype(F32)
    a_heads = []
    for h in range(H_C):
        lg = LOG_GAMMA[h]
        ks = slice(h * DK_C, (h + 1) * DK_C)
        decay = jnp.where(ri >= ci, jnp.exp(jnp.maximum(dist, 0.0) * lg), 0.0)
        a_heads.append((_dot_nt(q_ref[:, ks].astype(BF16), k_ref[:, ks].astype(BF16)) * decay).astype(BF16))
    for h in range(H_C):
        lg = LOG_GAMMA[h]
        ks = slice(h * DK_C, (h + 1) * DK_C)
        vs = slice(h * DV_C, (h + 1) * DV_C)
        qb = q_ref[:, ks].astype(BF16)
        kf = k_ref[:, ks].astype(F32)
        vb = v_ref[:, vs].astype(BF16)
        st = state_ref[h]
        inner = jnp.exp((rowf + 1.0) * lg)
        o = _dot(a_heads[h], vb) + inner * _dot(qb, st.astype(BF16))
        kd = kf * jnp.exp((n_valid - 1.0 - rowf) * lg)
        if n_valid < c:
            kd = jnp.where(row < n_valid, kd, 0.0)
        state_ref[h] = math.exp(n_valid * lg) * st + _dot_tn(kd.astype(BF16), vb)
        on = o * lax.rsqrt(jnp.mean(o * o, axis=-1, keepdims=True) + EPS) * retg
        o_ref[:, vs] = (on * _silu(g_ref[:, vs].astype(F32))).astype(o_ref.dtype)


def _ret_prompt_kernel(q_ref, k_ref, v_ref, g_ref, retg_ref, o_ref, s_ref):
    @pl.when(pl.program_id(1) == 0)
    def _():
        s_ref[...] = jnp.zeros(s_ref.shape, F32)

    _ret_chunk(q_ref, k_ref, v_ref, g_ref, retg_ref[...], s_ref, o_ref, RET_CHUNK)


def _ret_prompt(proj, ret_g, *, batch, seq):
    nstep = seq // RET_CHUNK
    rb = lambda b, i: b * nstep + i
    qk = H_C * DK_C
    return pl.pallas_call(
        _ret_prompt_kernel,
        grid=(batch, nstep),
        in_specs=[pl.BlockSpec((RET_CHUNK, qk), lambda b, i: (rb(b, i), 0)),
                  pl.BlockSpec((RET_CHUNK, qk), lambda b, i: (rb(b, i), 1)),
                  pl.BlockSpec((RET_CHUNK, W_C), lambda b, i: (rb(b, i), 1)),
                  pl.BlockSpec((RET_CHUNK, W_C), lambda b, i: (rb(b, i), 2)),
                  pl.BlockSpec((1, DV_C), lambda b, i: (0, 0))],
        out_specs=[pl.BlockSpec((RET_CHUNK, W_C), lambda b, i: (rb(b, i), 0)),
                   pl.BlockSpec((None, H_C, DK_C, DV_C), lambda b, i: (b, 0, 0, 0))],
        out_shape=[jax.ShapeDtypeStruct((batch * seq, W_C), BF16),
                   jax.ShapeDtypeStruct((batch, H_C, DK_C, DV_C), F32)],
        compiler_params=_cparams(("arbitrary", "arbitrary")),
        name="ret_prompt",
    )(proj, proj, proj, proj, ret_g)


def _ret_sample_kernel(q_ref, k_ref, v_ref, g_ref, retg_ref, s_in_ref, o_ref, s_ref, *, ntok):
    s_ref[...] = s_in_ref[...]
    _ret_chunk(q_ref, k_ref, v_ref, g_ref, retg_ref[...], s_ref, o_ref, ntok)


def _ret_sample(proj, ret_g, state, *, ntok):
    nseq = state.shape[0]
    qk = H_C * DK_C
    return pl.pallas_call(
        functools.partial(_ret_sample_kernel, ntok=ntok),
        grid=(nseq,),
        in_specs=[pl.BlockSpec((SAMPLE_PAD, qk), lambda i: (i, 0)),
                  pl.BlockSpec((SAMPLE_PAD, qk), lambda i: (i, 1)),
                  pl.BlockSpec((SAMPLE_PAD, W_C), lambda i: (i, 1)),
                  pl.BlockSpec((SAMPLE_PAD, W_C), lambda i: (i, 2)),
                  pl.BlockSpec((1, DV_C), lambda i: (0, 0)),
                  pl.BlockSpec((None, H_C, DK_C, DV_C), lambda i: (i, 0, 0, 0))],
        out_specs=[pl.BlockSpec((SAMPLE_PAD, W_C), lambda i: (i, 0)),
                   pl.BlockSpec((None, H_C, DK_C, DV_C), lambda i: (i, 0, 0, 0))],
        out_shape=[jax.ShapeDtypeStruct((nseq * SAMPLE_PAD, W_C), BF16),
                   jax.ShapeDtypeStruct(state.shape, F32)],
        compiler_params=_cparams(("arbitrary",)),
        name="ret_sample",
    )(proj, proj, proj, proj, ret_g, state)


def _t5_bucket(dist):
    dist = jnp.maximum(dist, 0)
    max_exact = NUM_BUCKETS // 2
    log_ratio = jnp.log(jnp.maximum(dist, 1).astype(F32) / max_exact) / math.log(MAX_DISTANCE / max_exact)
    large = jnp.minimum(max_exact + (log_ratio * (NUM_BUCKETS - max_exact)).astype(jnp.int32), NUM_BUCKETS - 1)
    return jnp.where(dist < max_exact, dist, large)


def _bucket_or_masked(dist):
    return jnp.where((dist >= 0) & (dist <= WINDOW), _t5_bucket(dist), -1).astype(jnp.int32)


def _rotary_tables(pos):
    half = DK_C // 2
    inv = ROPE_BASE ** (-jnp.arange(half, dtype=F32) / half)
    ang = pos.astype(F32)[:, None] * inv[None, :]
    return jnp.cos(ang), jnp.sin(ang)


def _even_weight_layout(w_in):
    qa, ka, va, ga, qb, kb, vb, gb, lr = jnp.split(w_in, [1024, 1152, 1280, 2304, 2816, 3328, 4352, 5376], axis=1)
    main = jnp.concatenate([qa, ga, vb, gb, qb, kb, ka, va], axis=1).astype(BF16)
    lr_pad = jnp.pad(lr, ((0, 0), (0, 128 - GLA_RANK))).astype(BF16)
    return main, lr_pad


def _pad_tokens(a, ntok):
    pad = [(0, 0), (0, SAMPLE_PAD - ntok)] + [(0, 0)] * (a.ndim - 2)
    a = jnp.pad(a, pad)
    return a.reshape((a.shape[0] * SAMPLE_PAD,) + a.shape[2:])


def _rows(v, reps):
    return jnp.repeat(v, reps, axis=0)


def kernel(x_prompt, x_sample, cache_swa_k, cache_swa_v, state_gla, state_ret, c_prompt, c_sample, rel_bias,
           ada_w_even, ada_b_even, norm_g_even, w_in_even, w_lr_even, b_lr_even, qn_g_even, kn_g_even,
           sinks_even, gla_g_even, w_out_even, ada_w_odd, ada_b_odd, norm_g_odd, w_in_odd, ret_g_odd, w_out_odd):
    batch, seq, _ = x_prompt.shape
    nseq, ntok, _ = x_sample.shape
    mp = batch * seq

    c_all = jnp.concatenate([c_prompt, c_sample], axis=0)
    mod_e = _ada_mod(c_all, ada_w_even[0], ada_b_even[0])
    mod_o = _ada_mod(c_all, ada_w_odd[0], ada_b_odd[0])

    def split_mod(mod):
        shift, scale, gate = jnp.split(mod, 3, axis=1)
        p = tuple(a[:batch].reshape(batch, 1, D_MODEL) for a in (shift, scale, gate))
        s = tuple(_rows(a[batch:], SAMPLE_PAD) for a in (shift, scale, gate))
        return p, s

    (shift_ep, scale_ep, gate_ep), (shift_es, scale_es, gate_es) = split_mod(mod_e)
    (shift_op, scale_op, gate_op), (shift_os, scale_os, gate_os) = split_mod(mod_o)

    xp = x_prompt.reshape(mp, D_MODEL)
    xs = _pad_tokens(x_sample, ntok)
    ms = xs.shape[0]

    w_e, w_lri = _even_weight_layout(w_in_even[0])
    w_lr = jnp.pad(w_lr_even[0], ((0, 128 - GLA_RANK), (0, 0))).astype(BF16)
    b_lr = b_lr_even[0].reshape(1, -1)
    g_e = norm_g_even[0].reshape(1, D_MODEL)
    qn2 = jnp.tile(qn_g_even[0].reshape(1, HD_A), (1, 2))
    kn2 = jnp.tile(kn_g_even[0].reshape(1, HD_A), (1, 2))
    gla_g = gla_g_even[0].reshape(1, DV_B)
    w_out_e = w_out_even[0].astype(BF16)
    w_out_a, w_out_b = w_out_e[:W_A], w_out_e[W_A:]

    proj_p, la_p = _inproj_even(xp, scale_ep, shift_ep, g_e, w_e, w_lri, w_lr, b_lr,
                                tm=512, tn=768, rows_per_batch=seq, out_dtype=BF16)
    proj_s, la_s = _inproj_even(xs, scale_es, shift_es, g_e, w_e, w_lri, w_lr, b_lr,
                                tm=512, tn=768, rows_per_batch=ms, out_dtype=F32)

    ii = jnp.arange(WINDOW)
    ss = jnp.arange(2 * WINDOW)
    bucket_p = _bucket_or_masked(WINDOW + ii[:, None] - ss[None, :])
    mixed_a_p, k_last = _swa_prompt(proj_p, rel_bias, sinks_even[0], qn2, kn2, bucket_p, batch=batch, seq=seq)
    mixed_b_p, gla_p = _gla_prompt(proj_p, la_p, gla_g, batch=batch, seq=seq, rows=256)
    y1_p = _outproj([mixed_a_p, mixed_b_p], [w_out_a, w_out_b], xp, gate_ep, tm=512, tn=1024, rows_per_batch=seq)

    proj_p3 = proj_p.reshape(batch, seq, EV_N)
    swa_k_p = k_last.reshape(1, batch, WINDOW, H_A_KV, HD_A)
    swa_v_p = proj_p3[:, seq - WINDOW:, EV_KV + W_K: EV_KV + 2 * W_K].astype(F32).reshape(
        1, batch, WINDOW, H_A_KV, HD_A)

    rr = np.arange(H_A * SAMPLE_PAD)
    row_head = 8 * (rr // 64) + 2 * ((rr % 32) // 8) + (rr % 64) // 32
    tt = jnp.asarray(rr % SAMPLE_PAD)
    jj = jnp.arange(WINDOW)
    live = (tt < ntok)[:, None]
    bkc = jnp.where(live, _bucket_or_masked(WINDOW + tt[:, None] - jj[None, :]), -1)
    bkn = jnp.where(live & (jj[None, :] < ntok), _bucket_or_masked(tt[:, None] - jj[None, :]), -1)
    relrows = rel_bias.T[row_head]
    sinkrows = jnp.broadcast_to(sinks_even[0][row_head][:, None], (H_A * SAMPLE_PAD, 128))
    w_buf = cache_swa_k.shape[2]
    mixed_a_s, k_cache_s, v_cache_s = _swa_sample(
        proj_s, cache_swa_k[0].reshape(nseq, w_buf, W_K), cache_swa_v[0].reshape(nseq, w_buf, W_K),
        bkc, bkn, relrows, sinkrows, qn2, kn2, sb=16, ntok=ntok, unroll=4)
    mixed_b_s, gla_s = _gla_sample(proj_s, la_s, gla_g, state_gla[0], sb=8, ntok=ntok)
    y1_s = _outproj([mixed_a_s, mixed_b_s], [w_out_a, w_out_b], xs, gate_es, tm=512, tn=1024, rows_per_batch=ms)

    w_o = w_in_odd[0].astype(BF16)
    g_o = norm_g_odd[0].reshape(1, D_MODEL)
    ret_g = ret_g_odd[0].reshape(1, DV_C)
    w_out_o = w_out_odd[0].astype(BF16)
    cos_p, sin_p = _rotary_tables(jnp.arange(seq))
    pos_s = PAST_LEN + jnp.minimum(jnp.arange(SAMPLE_PAD), ntok - 1)
    cos_s, sin_s = _rotary_tables(jnp.tile(pos_s, nseq))

    projo_p = _inproj_odd(y1_p, scale_op, shift_op, g_o, w_o, cos_p, sin_p,
                          tm=512, tn=1024, rows_per_batch=seq, out_dtype=BF16)
    o_p, ret_p = _ret_prompt(projo_p, ret_g, batch=batch, seq=seq)
    y2_p = _outproj([o_p], [w_out_o], y1_p, gate_op, tm=512, tn=1024, rows_per_batch=seq)

    projo_s = _inproj_odd(y1_s, scale_os, shift_os, g_o, w_o, cos_s, sin_s,
                          tm=512, tn=1024, rows_per_batch=ms, out_dtype=F32)
    o_s, ret_s = _ret_sample(projo_s, ret_g, state_ret[0], ntok=ntok)
    y2_s = _outproj([o_s], [w_out_o], y1_s, gate_os, tm=512, tn=1024, rows_per_batch=ms)

    y_prompt = y2_p.reshape(batch, seq, D_MODEL)
    y_sample = y2_s.reshape(nseq, SAMPLE_PAD, D_MODEL)[:, :ntok]
    return (y_prompt, y_sample, swa_k_p, swa_v_p, gla_p[None], ret_p[None],
            k_cache_s.reshape(1, nseq, w_buf, H_A_KV, HD_A), v_cache_s.reshape(1, nseq, w_buf, H_A_KV, HD_A),
            gla_s[None], ret_s[None])
```

```python
import functools
import math

import numpy as np
import jax
import jax.numpy as jnp
from jax import lax
from jax.experimental import pallas as pl
from jax.experimental.pallas import tpu as pltpu

F32 = jnp.float32
BF16 = jnp.bfloat16

D_MODEL = 2048
WINDOW = 128
HD_A = 64
H_A = 16
H_A_KV = 2
G_A = 8
W_A = 1024
NUM_BUCKETS = 32
MAX_DISTANCE = 128
H_B = 4
DV_B = 256
DK_B = 128
W_B = 1024
GLA_RANK = 16
GLA_TAU = 16.0
GLA_CHUNK = 64
H_C = 8
DK_C = 256
DV_C = 512
W_C = 4096
RET_CHUNK = 128
ROPE_BASE = 10000.0
EPS = 1e-6
NEG_INF = -1e30
PAST_LEN = 8192

EV_QA, EV_GA, EV_VB, EV_GB, EV_QB, EV_KB, EV_KV = 0, 1024, 2048, 3072, 4096, 4608, 5120
EV_N = 5376
OD_N = 12288
W_K = H_A_KV * HD_A
SAMPLE_PAD = 8
MXU_N = 256
VMEM_LIMIT = 56 * 1024 * 1024

LOG_GAMMA = [float(np.log1p(-np.exp2(np.float32(-5.0 - h)))) for h in range(H_C)]


def _cparams(sem):
    return pltpu.CompilerParams(dimension_semantics=sem, vmem_limit_bytes=VMEM_LIMIT)


def _silu(x):
    return x / (1.0 + jnp.exp(-x))


def _dot(a, b):
    return jnp.dot(a, b, preferred_element_type=F32)


def _dot_nt(a, b):
    return lax.dot_general(a, b, (((1,), (1,)), ((), ())), preferred_element_type=F32)


def _dot_tn(a, b):
    return lax.dot_general(a, b, (((0,), (0,)), ((), ())), preferred_element_type=F32)


def _cumsum_rows(x):
    c = x.shape[0]
    r = lax.broadcasted_iota(jnp.int32, (c, c), 0)
    s = lax.broadcasted_iota(jnp.int32, (c, c), 1)
    tri = (r >= s).astype(BF16)
    hi = x.astype(BF16)
    r1 = x - hi.astype(F32)
    mid = r1.astype(BF16)
    lo = (r1 - mid.astype(F32)).astype(BF16)
    return _dot(tri, hi) + _dot(tri, mid) + _dot(tri, lo)


def _row_to_col(r):
    n = r.shape[1]
    ri = lax.broadcasted_iota(jnp.int32, (n, n), 0)
    ci = lax.broadcasted_iota(jnp.int32, (n, n), 1)
    return jnp.sum(jnp.where(ri == ci, jnp.broadcast_to(r, (n, n)), 0.0), axis=1, keepdims=True)


def _ada_kernel(c_ref, w_ref, b_ref, o_ref):
    sc = _silu(c_ref[...]).astype(BF16)
    o_ref[...] = _dot(sc, w_ref[...].astype(BF16)) + b_ref[...]


def _ada_mod(c_all, w, b):
    m = c_all.shape[0]
    n = w.shape[1]
    tn = 768
    return pl.pallas_call(
        _ada_kernel,
        grid=(n // tn,),
        in_specs=[pl.BlockSpec((m, D_MODEL), lambda j: (0, 0)),
                  pl.BlockSpec((D_MODEL, tn), lambda j: (0, j)),
                  pl.BlockSpec((1, tn), lambda j: (0, j))],
        out_specs=pl.BlockSpec((m, tn), lambda j: (0, j)),
        out_shape=jax.ShapeDtypeStruct((m, n), F32),
        compiler_params=_cparams(("arbitrary",)),
        name="ada_mod",
    )(c_all, w, b.reshape(1, n))


def _norm_mod(x, g, scale, shift):
    ms = jnp.mean(x * x, axis=-1, keepdims=True)
    y = x * lax.rsqrt(ms + EPS) * g
    return y * (1.0 + scale) + shift


def _log_sigmoid(z):
    return jnp.minimum(z, 0.0) - jnp.log(1.0 + jnp.exp(-jnp.abs(z)))


def _mod_spec(arr, tm, rows_per_batch):
    if arr.ndim == 3:
        return pl.BlockSpec((None, 1, D_MODEL), lambda i: (i // (rows_per_batch // tm), 0, 0))
    return pl.BlockSpec((tm, D_MODEL), lambda i: (i, 0))


def _prenorm_even_kernel(x_ref, scale_ref, shift_ref, g_ref, wlri_ref, wlr_ref, blr_ref, h_ref, la_ref):
    hb = _norm_mod(x_ref[...], g_ref[...], scale_ref[...], shift_ref[...]).astype(BF16)
    h_ref[...] = hb
    lr = _dot(hb, wlri_ref[...])
    z = _dot(lr.astype(BF16), wlr_ref[...]) + blr_ref[...]
    la_ref[...] = _log_sigmoid(z) / GLA_TAU


def _prenorm_even(x, scale, shift, g, wlri, wlr, blr, *, tm, rows_per_batch):
    m = x.shape[0]
    const = lambda shape: pl.BlockSpec(shape, lambda i: (0, 0))
    return pl.pallas_call(
        _prenorm_even_kernel,
        grid=(m // tm,),
        in_specs=[pl.BlockSpec((tm, D_MODEL), lambda i: (i, 0)),
                  _mod_spec(scale, tm, rows_per_batch), _mod_spec(shift, tm, rows_per_batch),
                  const((1, D_MODEL)), const((D_MODEL, 128)), const((128, H_B * DK_B)), const((1, H_B * DK_B))],
        out_specs=[pl.BlockSpec((tm, D_MODEL), lambda i: (i, 0)),
                   pl.BlockSpec((tm, H_B * DK_B), lambda i: (i, 0))],
        out_shape=[jax.ShapeDtypeStruct((m, D_MODEL), BF16),
                   jax.ShapeDtypeStruct((m, H_B * DK_B), F32)],
        compiler_params=_cparams(("arbitrary",)),
        name="prenorm_even",
    )(x, scale, shift, g, wlri, wlr, blr)


def _inproj_kernel(h_ref, w_ref, *rest, rot_tiles):
    o_ref = rest[-1]
    tn = o_ref.shape[1]

    def plain():
        for c in range(tn // MXU_N):
            cs = slice(c * MXU_N, (c + 1) * MXU_N)
            o_ref[:, cs] = _dot(h_ref[...], w_ref[:, cs]).astype(o_ref.dtype)

    if rot_tiles == 0:
        plain()
        return
    cos_ref, sin_ref = rest[:2]
    j = pl.program_id(0)
    half = DK_C // 2

    @pl.when(j < rot_tiles)
    def _():
        mult = jnp.where(j < rot_tiles // 2, 1.0, DK_C ** -0.5)
        cos = cos_ref[...] * mult
        sin = sin_ref[...] * mult
        for c in range(tn // DK_C):
            acc = _dot(h_ref[...], w_ref[:, c * DK_C:(c + 1) * DK_C])
            x1 = acc[:, :half]
            x2 = acc[:, half:]
            o_ref[:, c * DK_C:c * DK_C + half] = (x1 * cos - x2 * sin).astype(o_ref.dtype)
            o_ref[:, c * DK_C + half:(c + 1) * DK_C] = (x2 * cos + x1 * sin).astype(o_ref.dtype)

    pl.when(j >= rot_tiles)(plain)


def _inproj(h, w, rot=None, *, tm, tn, out_dtype, name):
    m = h.shape[0]
    n = w.shape[1]
    in_specs = [pl.BlockSpec((tm, D_MODEL), lambda j, i: (i, 0)),
                pl.BlockSpec((D_MODEL, tn), lambda j, i: (0, j))]
    args = [h, w]
    rot_tiles = 0
    if rot is not None:
        nblk = rot[0].shape[0] // tm
        in_specs += [pl.BlockSpec((tm, DK_C // 2), lambda j, i: (i % nblk, 0))] * 2
        args += list(rot)
        rot_tiles = 2 * (H_C * DK_C) // tn
    return pl.pallas_call(
        functools.partial(_inproj_kernel, rot_tiles=rot_tiles),
        grid=(n // tn, m // tm),
        in_specs=in_specs,
        out_specs=pl.BlockSpec((tm, tn), lambda j, i: (i, j)),
        out_shape=jax.ShapeDtypeStruct((m, n), out_dtype),
        compiler_params=_cparams(("arbitrary", "arbitrary")),
        name=name,
    )(*args)


def _outproj_kernel(*refs, nin, with_next):
    m_refs, w_refs = refs[:nin], refs[nin:2 * nin]
    if with_next:
        x_ref, gate_ref, nscale_ref, nshift_ref, ng_ref, o_ref, h_ref = refs[2 * nin:]
    else:
        x_ref, gate_ref, o_ref = refs[2 * nin:]
    for c in range(o_ref.shape[1] // MXU_N):
        cs = slice(c * MXU_N, (c + 1) * MXU_N)
        acc = _dot(m_refs[0][...].astype(BF16), w_refs[0][:, cs])
        for m_ref, w_ref in zip(m_refs[1:], w_refs[1:]):
            acc += _dot(m_ref[...].astype(BF16), w_ref[:, cs])
        o_ref[:, cs] = x_ref[:, cs] + gate_ref[:, cs] * acc
    if with_next:
        h_ref[...] = _norm_mod(o_ref[...], ng_ref[...], nscale_ref[...], nshift_ref[...]).astype(BF16)


def _outproj(mixed, ws, x, gate, next_mod=None, *, tm, rows_per_batch):
    m = x.shape[0]
    row = pl.BlockSpec((tm, D_MODEL), lambda i: (i, 0))
    in_specs = ([pl.BlockSpec((tm, a.shape[1]), lambda i: (i, 0)) for a in mixed]
                + [pl.BlockSpec(w.shape, lambda i: (0, 0), pipeline_mode=pl.Buffered(1)) for w in ws]
                + [row, _mod_spec(gate, tm, rows_per_batch)])
    args = [*mixed, *ws, x, gate]
    out_specs = [row]
    out_shape = [jax.ShapeDtypeStruct((m, D_MODEL), F32)]
    if next_mod is not None:
        nscale, nshift, ng = next_mod
        in_specs += [_mod_spec(nscale, tm, rows_per_batch), _mod_spec(nshift, tm, rows_per_batch),
                     pl.BlockSpec((1, D_MODEL), lambda i: (0, 0))]
        args += [nscale, nshift, ng]
        out_specs.append(row)
        out_shape.append(jax.ShapeDtypeStruct((m, D_MODEL), BF16))
    return pl.pallas_call(
        functools.partial(_outproj_kernel, nin=len(mixed), with_next=next_mod is not None),
        grid=(m // tm,),
        in_specs=in_specs,
        out_specs=out_specs,
        out_shape=out_shape,
        compiler_params=_cparams(("arbitrary",)),
        name="outproj",
    )(*args)


def _seg_rms(x, g2, seg_ones):
    x2 = x * x
    hi = x2.astype(BF16)
    lo = (x2 - hi.astype(F32)).astype(BF16)
    ss = _dot(hi, seg_ones) + _dot(lo, seg_ones)
    return x * lax.rsqrt(ss * (1.0 / HD_A) + EPS) * g2


def _dup_halves(x, lo_half):
    sw = pltpu.roll(x, HD_A, 1)
    return [jnp.where(lo_half, x, sw), jnp.where(lo_half, sw, x)]


def _seg_ones():
    r = lax.broadcasted_iota(jnp.int32, (2 * HD_A, 2 * HD_A), 0)
    c = lax.broadcasted_iota(jnp.int32, (2 * HD_A, 2 * HD_A), 1)
    return ((r < HD_A) == (c < HD_A)).astype(BF16)


def _swa_prompt_kernel(relb_ref, sinks_ref, bucket_ref, q_ref, ga_ref, kv_ref, qn_ref, kn_ref,
                       o_ref, knew_ref, bias_ref, kprev_ref, vprev_ref, s_ref, pe_ref):
    b = pl.program_id(0)
    i = pl.program_id(1)

    @pl.when((b == 0) & (i == 0))
    def _():
        bk = bucket_ref[...]
        for h in range(H_A):
            bias_ref[h] = jnp.full((WINDOW, 2 * WINDOW), NEG_INF, F32)
        for bb in range(NUM_BUCKETS):
            hit = bk == bb
            for h in range(H_A):
                bias_ref[h] = jnp.where(hit, relb_ref[bb, h], bias_ref[h])
        col = lax.broadcasted_iota(jnp.int32, (WINDOW, 2 * WINDOW), 1)
        for h in range(H_A):
            bias_ref[H_A + h] = jnp.where(col >= WINDOW, bias_ref[h], NEG_INF)

    @pl.when(i == 0)
    def _():
        kprev_ref[...] = jnp.zeros(kprev_ref.shape, BF16)
        vprev_ref[...] = jnp.zeros(vprev_ref.shape, BF16)

    seg_ones = _seg_ones()
    lt = 2 * HD_A
    tpg = G_A // 2
    lo_half = lax.broadcasted_iota(jnp.int32, (WINDOW, lt), 1) < HD_A
    kv = kv_ref[...].astype(F32)
    kc = _seg_rms(kv[:, :W_K], kn_ref[...], seg_ones)
    k_dup = _dup_halves(kc, lo_half)
    v_dup = _dup_halves(kv[:, W_K:], lo_half)
    base = jnp.where(i == 0, H_A, 0)
    qn2 = qn_ref[...] * HD_A ** -0.5
    gq = tpg * WINDOW
    lo_g = lax.broadcasted_iota(jnp.int32, (gq, lt), 1) < HD_A
    grows = [slice(g * 2 * gq, (g + 1) * 2 * gq) for g in range(H_A_KV)]
    for g in range(H_A_KV):
        kk = jnp.concatenate([kprev_ref[g], k_dup[g].astype(BF16)], axis=0)
        xg = jnp.concatenate([q_ref[:, (g * tpg + p) * lt:(g * tpg + p + 1) * lt] for p in range(tpg)],
                             axis=0).astype(F32)
        xn = _seg_rms(xg, qn2, seg_ones)
        q_all = jnp.concatenate([jnp.where(lo_g, xn, 0.0), jnp.where(lo_g, 0.0, xn)], axis=0).astype(BF16)
        s_ref[grows[g], :] = _dot_nt(q_all, kk)
    sink_terms = []
    for g in range(H_A_KV):
        for n in range(G_A):
            h = g * G_A + 2 * (n % tpg) + n // tpg
            rows = slice((g * G_A + n) * WINDOW, (g * G_A + n + 1) * WINDOW)
            s = s_ref[rows, :] + bias_ref[base + h]
            sink = sinks_ref[h]
            m = jnp.maximum(jnp.max(s, axis=-1, keepdims=True), sink)
            pe_ref[rows, :] = jnp.exp(s - m).astype(BF16)
            sink_terms.append(jnp.exp(sink - m))
    o_ext = []
    for g in range(H_A_KV):
        vv = jnp.concatenate([vprev_ref[g], v_dup[g].astype(BF16)], axis=0)
        vv_ext = jnp.concatenate([vv, jnp.ones(vv.shape, BF16)], axis=1)
        o_ext.append(_dot(pe_ref[grows[g], :], vv_ext))
    for g in range(H_A_KV):
        for p in range(tpg):
            halves = []
            for a in range(2):
                n = a * tpg + p
                rows = slice(n * WINDOW, (n + 1) * WINDOW)
                halves.append(o_ext[g][rows, :lt] / (o_ext[g][rows, lt:] + sink_terms[g * G_A + n]))
            oa = jnp.where(lo_half, halves[0], halves[1])
            cols = slice((g * tpg + p) * lt, (g * tpg + p + 1) * lt)
            o_ref[:, cols] = (oa * _silu(ga_ref[:, cols].astype(F32))).astype(o_ref.dtype)
    for g in range(H_A_KV):
        kprev_ref[g] = k_dup[g].astype(BF16)
        vprev_ref[g] = v_dup[g].astype(BF16)

    @pl.when(i == pl.num_programs(1) - 1)
    def _():
        knew_ref[...] = kc


def _swa_prompt(proj, rel_bias, sinks, qn2, kn2, bucket, *, batch, seq):
    nb = seq // WINDOW
    rb = lambda b, i: b * nb + i
    smem = pl.BlockSpec(memory_space=pltpu.SMEM)
    return pl.pallas_call(
        _swa_prompt_kernel,
        grid=(batch, nb),
        in_specs=[smem, smem,
                  pl.BlockSpec((WINDOW, 2 * WINDOW), lambda b, i: (0, 0)),
                  pl.BlockSpec((WINDOW, W_A), lambda b, i: (rb(b, i), EV_QA // W_A)),
                  pl.BlockSpec((WINDOW, W_A), lambda b, i: (rb(b, i), EV_GA // W_A)),
                  pl.BlockSpec((WINDOW, 2 * W_K), lambda b, i: (rb(b, i), EV_KV // (2 * W_K))),
                  pl.BlockSpec((1, 2 * HD_A), lambda b, i: (0, 0)),
                  pl.BlockSpec((1, 2 * HD_A), lambda b, i: (0, 0))],
        out_specs=[pl.BlockSpec((WINDOW, W_A), lambda b, i: (rb(b, i), 0)),
                   pl.BlockSpec((None, WINDOW, W_K), lambda b, i: (b, 0, 0))],
        out_shape=[jax.ShapeDtypeStruct((batch * seq, W_A), BF16),
                   jax.ShapeDtypeStruct((batch, WINDOW, W_K), F32)],
        scratch_shapes=[pltpu.VMEM((2 * H_A, WINDOW, 2 * WINDOW), F32),
                        pltpu.VMEM((H_A_KV, WINDOW, W_K), BF16), pltpu.VMEM((H_A_KV, WINDOW, W_K), BF16),
                        pltpu.VMEM((H_A * WINDOW, 2 * WINDOW), F32), pltpu.VMEM((H_A * WINDOW, 2 * WINDOW), BF16)],
        compiler_params=_cparams(("arbitrary", "arbitrary")),
        name="swa_prompt",
    )(rel_bias, sinks, bucket, proj, proj, proj, qn2, kn2)


def _swa_sample_kernel(bkc_ref, bkn_ref, relrows_ref, sinkrows_ref, q_ref, ga_ref, kvn_ref, ck_ref, cv_ref,
                       qn_ref, kn_ref, o_ref, ko_ref, vo_ref, biasc_ref, biasn_ref, *, sb, ntok, unroll):
    @pl.when(pl.program_id(0) == 0)
    def _():
        bkc = bkc_ref[...]
        bkn = bkn_ref[...]
        rr = relrows_ref[...]
        bc = jnp.full(bkc.shape, NEG_INF, F32)
        bn = jnp.full(bkn.shape, NEG_INF, F32)
        for bb in range(NUM_BUCKETS):
            val = rr[:, bb:bb + 1]
            bc = jnp.where(bkc == bb, val, bc)
            bn = jnp.where(bkn == bb, val, bn)
        biasc_ref[...] = bc
        biasn_ref[...] = bn

    seg_ones = _seg_ones()
    lt = 2 * HD_A
    nrow = G_A * SAMPLE_PAD
    grow = nrow // H_A_KV
    lo_q = lax.broadcasted_iota(jnp.int32, (grow, lt), 1) < HD_A
    lo_k = lax.broadcasted_iota(jnp.int32, (WINDOW, lt), 1) < HD_A
    lo_n = lax.broadcasted_iota(jnp.int32, (SAMPLE_PAD, lt), 1) < HD_A
    qn2 = qn_ref[...] * HD_A ** -0.5
    kn2 = kn_ref[...]
    sink = sinkrows_ref[:, 0:1]
    nkeep = WINDOW - ntok

    def body(it, carry):
        seqs = [it * unroll + u for u in range(unroll)]
        rows = [pl.ds(pl.multiple_of(s * SAMPLE_PAD, SAMPLE_PAD), SAMPLE_PAD) for s in seqs]
        kvn = [kvn_ref[r, :] for r in rows]
        kn_all = _seg_rms(jnp.concatenate([x[:, :W_K] for x in kvn], axis=0), kn2, seg_ones)
        q2_all = jnp.concatenate([q_ref[r, p * lt:(p + 1) * lt] for r in rows for p in range(G_A)], axis=0)
        xn_all = _seg_rms(q2_all, qn2, seg_ones)
        sc, sn, kn = [], [], []
        for u, s in enumerate(seqs):
            kn.append(kn_all[u * SAMPLE_PAD:(u + 1) * SAMPLE_PAD])
            ck_dup = _dup_halves(ck_ref[s], lo_k)
            kn_dup = _dup_halves(kn[u], lo_n)
            scg, sng = [], []
            for g in range(H_A_KV):
                xn = xn_all[u * nrow + g * grow:u * nrow + (g + 1) * grow]
                q4 = jnp.concatenate([jnp.where(lo_q, xn, 0.0), jnp.where(lo_q, 0.0, xn)], axis=0).astype(BF16)
                scg.append(_dot_nt(q4, ck_dup[g].astype(BF16)))
                sng.append(_dot_nt(q4, kn_dup[g].astype(BF16)))
            sc.append(jnp.concatenate(scg, axis=0))
            sn.append(jnp.concatenate(sng, axis=0))
        pc, pn, sink_terms = [], [], []
        for u in range(unroll):
            scu = sc[u] + biasc_ref[...]
            snu = sn[u] + biasn_ref[:, :SAMPLE_PAD]
            m = jnp.maximum(jnp.maximum(jnp.max(scu, axis=-1, keepdims=True),
                                        jnp.max(snu, axis=-1, keepdims=True)), sink)
            pc.append(jnp.exp(scu - m).astype(BF16))
            pn.append(jnp.exp(snu - m).astype(BF16))
            sink_terms.append(jnp.exp(sink - m))
        o_ext = []
        for u, s in enumerate(seqs):
            cv_dup = _dup_halves(cv_ref[s], lo_k)
            vn_dup = _dup_halves(kvn[u][:, W_K:], lo_n)
            og = []
            for g in range(H_A_KV):
                cv_ext = jnp.concatenate([cv_dup[g].astype(BF16), jnp.ones((WINDOW, lt), BF16)], axis=1)
                vn_ext = jnp.concatenate([vn_dup[g].astype(BF16), jnp.ones((SAMPLE_PAD, lt), BF16)], axis=1)
                gr = slice(g * 2 * grow, (g + 1) * 2 * grow)
                og.append(_dot(pc[u][gr], cv_ext) + _dot(pn[u][gr], vn_ext))
            o_ext.append(jnp.concatenate(og, axis=0))
        for u, s in enumerate(seqs):
            o4 = o_ext[u][:, :lt] / (o_ext[u][:, lt:] + sink_terms[u])
            xg = ga_ref[rows[u], :]
            for p in range(G_A):
                g, pp = divmod(p, G_A // 2)
                r0 = g * 2 * grow + pp * SAMPLE_PAD
                o2 = jnp.where(lo_n, o4[r0:r0 + SAMPLE_PAD], o4[r0 + grow:r0 + grow + SAMPLE_PAD])
                o_ref[rows[u], p * lt:(p + 1) * lt] = (o2 * _silu(xg[:, p * lt:(p + 1) * lt])).astype(o_ref.dtype)
            ko_ref[s, 0:nkeep, :] = ck_ref[s, ntok:WINDOW, :]
            ko_ref[s, nkeep:WINDOW, :] = kn[u][0:ntok]
            vo_ref[s, 0:nkeep, :] = cv_ref[s, ntok:WINDOW, :]
            vo_ref[s, nkeep:WINDOW, :] = kvn[u][0:ntok, W_K:]
        return carry

    lax.fori_loop(0, sb // unroll, body, 0)


def _swa_sample(proj, cache_k, cache_v, bkc, bkn, relrows, sinkrows, qn2, kn2, *, sb, ntok, unroll):
    nseq = cache_k.shape[0]
    rows = sb * SAMPLE_PAD
    full = lambda shape: pl.BlockSpec(shape, lambda i: tuple(0 for _ in shape))
    cache = pl.BlockSpec((sb, WINDOW, W_K), lambda i: (i, 0, 0))
    return pl.pallas_call(
        functools.partial(_swa_sample_kernel, sb=sb, ntok=ntok, unroll=unroll),
        grid=(nseq // sb,),
        in_specs=[full(bkc.shape), full(bkn.shape), full(relrows.shape), full(sinkrows.shape),
                  pl.BlockSpec((rows, W_A), lambda i: (i, EV_QA // W_A)),
                  pl.BlockSpec((rows, W_A), lambda i: (i, EV_GA // W_A)),
                  pl.BlockSpec((rows, 2 * W_K), lambda i: (i, EV_KV // (2 * W_K))),
                  cache, cache, full((1, 2 * HD_A)), full((1, 2 * HD_A))],
        out_specs=[pl.BlockSpec((rows, W_A), lambda i: (i, 0)), cache, cache],
        out_shape=[jax.ShapeDtypeStruct((nseq * SAMPLE_PAD, W_A), F32),
                   jax.ShapeDtypeStruct(cache_k.shape, F32),
                   jax.ShapeDtypeStruct(cache_v.shape, F32)],
        scratch_shapes=[pltpu.VMEM(bkc.shape, F32), pltpu.VMEM(bkn.shape, F32)],
        compiler_params=_cparams(("arbitrary",)),
        name="swa_sample",
    )(bkc, bkn, relrows, sinkrows, proj, proj, proj, cache_k, cache_v, qn2, kn2)


def _gla_chunk(q, k, v, gb, la, gla_g, state_ref, state_idx, n_valid):
    c = q.shape[0]
    bcum = _cumsum_rows(la)
    rr = lax.broadcasted_iota(jnp.int32, (c, c), 0)
    cc = lax.broadcasted_iota(jnp.int32, (c, c), 1)
    causal = rr >= cc
    row = lax.broadcasted_iota(jnp.int32, (c, 1), 0)
    outs = []
    for h in range(H_B):
        ks = slice(h * DK_B, (h + 1) * DK_B)
        vs = slice(h * DV_B, (h + 1) * DV_B)
        bc = bcum[:, ks]
        qt = (q[:, ks] * DK_B ** -0.5) * jnp.exp(bc)
        kt = k[:, ks] * jnp.exp(-bc)
        a = jnp.where(causal, _dot_nt(qt.astype(BF16), kt.astype(BF16)), 0.0)
        st = state_ref[state_idx + (h,)]
        vb = v[:, vs].astype(BF16)
        qtb = qt.astype(BF16)
        o = _dot(a.astype(BF16), vb) + _dot(qtb, st.astype(BF16))
        blast = bc[n_valid - 1:n_valid, :]
        kd = k[:, ks] * jnp.exp(blast - bc)
        if n_valid < c:
            kd = jnp.where(row < n_valid, kd, 0.0)
        upd = _dot_tn(kd.astype(BF16), vb)
        state_ref[state_idx + (h,)] = _row_to_col(jnp.exp(blast)) * st + upd
        on = o * lax.rsqrt(jnp.mean(o * o, axis=-1, keepdims=True) + EPS) * gla_g
        outs.append(on * _silu(gb[:, vs]))
    return jnp.concatenate(outs, axis=1)


def _gla_prompt_kernel(q_ref, k_ref, v_ref, gb_ref, la_ref, g_ref, o_ref, s_ref, *, nchunk):
    @pl.when(pl.program_id(1) == 0)
    def _():
        s_ref[...] = jnp.zeros(s_ref.shape, F32)

    gla_g = g_ref[...]
    for cidx in range(nchunk):
        rows = slice(cidx * GLA_CHUNK, (cidx + 1) * GLA_CHUNK)
        out = _gla_chunk(q_ref[rows, :].astype(F32), k_ref[rows, :].astype(F32), v_ref[rows, :].astype(F32),
                         gb_ref[rows, :].astype(F32), la_ref[rows, :], gla_g, s_ref, (), GLA_CHUNK)
        o_ref[rows, :] = out.astype(o_ref.dtype)


def _gla_prompt(proj, la, gla_g, *, batch, seq, rows):
    nstep = seq // rows
    rb = lambda b, i: b * nstep + i
    return pl.pallas_call(
        functools.partial(_gla_prompt_kernel, nchunk=rows // GLA_CHUNK),
        grid=(batch, nstep),
        in_specs=[pl.BlockSpec((rows, H_B * DK_B), lambda b, i: (rb(b, i), EV_QB // (H_B * DK_B))),
                  pl.BlockSpec((rows, H_B * DK_B), lambda b, i: (rb(b, i), EV_KB // (H_B * DK_B))),
                  pl.BlockSpec((rows, W_B), lambda b, i: (rb(b, i), EV_VB // W_B)),
                  pl.BlockSpec((rows, W_B), lambda b, i: (rb(b, i), EV_GB // W_B)),
                  pl.BlockSpec((rows, H_B * DK_B), lambda b, i: (rb(b, i), 0)),
                  pl.BlockSpec((1, DV_B), lambda b, i: (0, 0))],
        out_specs=[pl.BlockSpec((rows, W_B), lambda b, i: (rb(b, i), 0)),
                   pl.BlockSpec((None, H_B, DK_B, DV_B), lambda b, i: (b, 0, 0, 0))],
        out_shape=[jax.ShapeDtypeStruct((batch * seq, W_B), BF16),
                   jax.ShapeDtypeStruct((batch, H_B, DK_B, DV_B), F32)],
        compiler_params=_cparams(("arbitrary", "arbitrary")),
        name="gla_prompt",
    )(proj, proj, proj, proj, la, gla_g)


def _gla_sample_kernel(q_ref, k_ref, v_ref, gb_ref, la_ref, g_ref, s_in_ref, o_ref, s_ref, *, sb, ntok):
    gla_g = g_ref[...]
    s_ref[...] = s_in_ref[...]

    def body(s, carry):
        rows = pl.ds(pl.multiple_of(s * SAMPLE_PAD, SAMPLE_PAD), SAMPLE_PAD)
        out = _gla_chunk(q_ref[rows, :], k_ref[rows, :], v_ref[rows, :], gb_ref[rows, :], la_ref[rows, :],
                         gla_g, s_ref, (s,), ntok)
        o_ref[rows, :] = out.astype(o_ref.dtype)
        return carry

    lax.fori_loop(0, sb, body, 0)


def _gla_sample(proj, la, gla_g, state, *, sb, ntok):
    nseq = state.shape[0]
    rows = sb * SAMPLE_PAD
    return pl.pallas_call(
        functools.partial(_gla_sample_kernel, sb=sb, ntok=ntok),
        grid=(nseq // sb,),
        in_specs=[pl.BlockSpec((rows, H_B * DK_B), lambda i: (i, EV_QB // (H_B * DK_B))),
                  pl.BlockSpec((rows, H_B * DK_B), lambda i: (i, EV_KB // (H_B * DK_B))),
                  pl.BlockSpec((rows, W_B), lambda i: (i, EV_VB // W_B)),
                  pl.BlockSpec((rows, W_B), lambda i: (i, EV_GB // W_B)),
                  pl.BlockSpec((rows, H_B * DK_B), lambda i: (i, 0)),
                  pl.BlockSpec((1, DV_B), lambda i: (0, 0)),
                  pl.BlockSpec((sb, H_B, DK_B, DV_B), lambda i: (i, 0, 0, 0))],
        out_specs=[pl.BlockSpec((rows, W_B), lambda i: (i, 0)),
                   pl.BlockSpec((sb, H_B, DK_B, DV_B), lambda i: (i, 0, 0, 0))],
        out_shape=[jax.ShapeDtypeStruct((nseq * SAMPLE_PAD, W_B), BF16),
                   jax.ShapeDtypeStruct(state.shape, F32)],
        compiler_params=_cparams(("arbitrary",)),
        name="gla_sample",
    )(proj, proj, proj, proj, la, gla_g, state)


def _ret_chunk(q_ref, k_ref, v_ref, g_ref, retg, state_ref, o_ref, n_valid):
    c = q_ref.shape[0]
    ri = lax.broadcasted_iota(jnp.int32, (c, c), 0)
    ci = lax.broadcasted_iota(jnp.int32, (c, c), 1)
    dist = (ri - ci).astype(F32)
    row = lax.broadcasted_iota(jnp.int32, (c, 1), 0)
    rowf = row.astype(F32)
    a_heads = []
    for h in range(H_C):
        lg = LOG_GAMMA[h]
        ks = slice(h * DK_C, (h + 1) * DK_C)
        decay = jnp.where(ri >= ci, jnp.exp(jnp.maximum(dist, 0.0) * lg), 0.0)
        a_heads.append((_dot_nt(q_ref[:, ks].astype(BF16), k_ref[:, ks].astype(BF16)) * decay).astype(BF16))
    for h in range(H_C):
        lg = LOG_GAMMA[h]
        ks = slice(h * DK_C, (h + 1) * DK_C)
        vs = slice(h * DV_C, (h + 1) * DV_C)
        qb = q_ref[:, ks].astype(BF16)
        kf = k_ref[:, ks].astype(F32)
        vb = v_ref[:, vs].astype(BF16)
        st = state_ref[h]
        inner = jnp.exp((rowf + 1.0) * lg)
        o = _dot(a_heads[h], vb) + inner * _dot(qb, st.astype(BF16))
        kd = kf * jnp.exp((n_valid - 1.0 - rowf) * lg)
        if n_valid < c:
            kd = jnp.where(row < n_valid, kd, 0.0)
        state_ref[h] = math.exp(n_valid * lg) * st + _dot_tn(kd.astype(BF16), vb)
        on = o * lax.rsqrt(jnp.mean(o * o, axis=-1, keepdims=True) + EPS) * retg
        o_ref[:, vs] = (on * _silu(g_ref[:, vs].astype(F32))).astype(o_ref.dtype)


def _ret_prompt_kernel(q_ref, k_ref, v_ref, g_ref, retg_ref, o_ref, s_ref):
    @pl.when(pl.program_id(1) == 0)
    def _():
        s_ref[...] = jnp.zeros(s_ref.shape, F32)

    _ret_chunk(q_ref, k_ref, v_ref, g_ref, retg_ref[...], s_ref, o_ref, RET_CHUNK)


def _ret_prompt(proj, ret_g, *, batch, seq):
    nstep = seq // RET_CHUNK
    rb = lambda b, i: b * nstep + i
    qk = H_C * DK_C
    return pl.pallas_call(
        _ret_prompt_kernel,
        grid=(batch, nstep),
        in_specs=[pl.BlockSpec((RET_CHUNK, qk), lambda b, i: (rb(b, i), 0)),
                  pl.BlockSpec((RET_CHUNK, qk), lambda b, i: (rb(b, i), 1)),
                  pl.BlockSpec((RET_CHUNK, W_C), lambda b, i: (rb(b, i), 1)),
                  pl.BlockSpec((RET_CHUNK, W_C), lambda b, i: (rb(b, i), 2)),
                  pl.BlockSpec((1, DV_C), lambda b, i: (0, 0))],
        out_specs=[pl.BlockSpec((RET_CHUNK, W_C), lambda b, i: (rb(b, i), 0)),
                   pl.BlockSpec((None, H_C, DK_C, DV_C), lambda b, i: (b, 0, 0, 0))],
        out_shape=[jax.ShapeDtypeStruct((batch * seq, W_C), BF16),
                   jax.ShapeDtypeStruct((batch, H_C, DK_C, DV_C), F32)],
        compiler_params=_cparams(("arbitrary", "arbitrary")),
        name="ret_prompt",
    )(proj, proj, proj, proj, ret_g)


def _ret_sample_kernel(q_ref, k_ref, v_ref, g_ref, retg_ref, s_in_ref, o_ref, s_ref, *, ntok):
    s_ref[...] = s_in_ref[...]
    _ret_chunk(q_ref, k_ref, v_ref, g_ref, retg_ref[...], s_ref, o_ref, ntok)


def _ret_sample(proj, ret_g, state, *, ntok):
    nseq = state.shape[0]
    qk = H_C * DK_C
    return pl.pallas_call(
        functools.partial(_ret_sample_kernel, ntok=ntok),
        grid=(nseq,),
        in_specs=[pl.BlockSpec((SAMPLE_PAD, qk), lambda i: (i, 0)),
                  pl.BlockSpec((SAMPLE_PAD, qk), lambda i: (i, 1)),
                  pl.BlockSpec((SAMPLE_PAD, W_C), lambda i: (i, 1)),
                  pl.BlockSpec((SAMPLE_PAD, W_C), lambda i: (i, 2)),
                  pl.BlockSpec((1, DV_C), lambda i: (0, 0)),
                  pl.BlockSpec((None, H_C, DK_C, DV_C), lambda i: (i, 0, 0, 0))],
        out_specs=[pl.BlockSpec((SAMPLE_PAD, W_C), lambda i: (i, 0)),
                   pl.BlockSpec((None, H_C, DK_C, DV_C), lambda i: (i, 0, 0, 0))],
        out_shape=[jax.ShapeDtypeStruct((nseq * SAMPLE_PAD, W_C), BF16),
                   jax.ShapeDtypeStruct(state.shape, F32)],
        compiler_params=_cparams(("arbitrary",)),
        name="ret_sample",
    )(proj, proj, proj, proj, ret_g, state)


def _t5_bucket(dist):
    dist = jnp.maximum(dist, 0)
    max_exact = NUM_BUCKETS // 2
    log_ratio = jnp.log(jnp.maximum(dist, 1).astype(F32) / max_exact) / math.log(MAX_DISTANCE / max_exact)
    large = jnp.minimum(max_exact + (log_ratio * (NUM_BUCKETS - max_exact)).astype(jnp.int32), NUM_BUCKETS - 1)
    return jnp.where(dist < max_exact, dist, large)


def _bucket_or_masked(dist):
    return jnp.where((dist >= 0) & (dist <= WINDOW), _t5_bucket(dist), -1).astype(jnp.int32)


def _rotary_tables(pos):
    half = DK_C // 2
    inv = ROPE_BASE ** (-jnp.arange(half, dtype=F32) / half)
    ang = pos.astype(F32)[:, None] * inv[None, :]
    return jnp.cos(ang), jnp.sin(ang)


def _even_weight_layout(w_in):
    qa, ka, va, ga, qb, kb, vb, gb, lr = jnp.split(w_in, [1024, 1152, 1280, 2304, 2816, 3328, 4352, 5376], axis=1)
    main = jnp.concatenate([qa, ga, vb, gb, qb, kb, ka, va], axis=1).astype(BF16)
    lr_pad = jnp.pad(lr, ((0, 0), (0, 128 - GLA_RANK))).astype(BF16)
    return main, lr_pad


def _pad_tokens(a, ntok):
    pad = [(0, 0), (0, SAMPLE_PAD - ntok)] + [(0, 0)] * (a.ndim - 2)
    a = jnp.pad(a, pad)
    return a.reshape((a.shape[0] * SAMPLE_PAD,) + a.shape[2:])


def _rows(v, reps):
    return jnp.repeat(v, reps, axis=0)


def kernel(x_prompt, x_sample, cache_swa_k, cache_swa_v, state_gla, state_ret, c_prompt, c_sample, rel_bias,
           ada_w_even, ada_b_even, norm_g_even, w_in_even, w_lr_even, b_lr_even, qn_g_even, kn_g_even,
           sinks_even, gla_g_even, w_out_even, ada_w_odd, ada_b_odd, norm_g_odd, w_in_odd, ret_g_odd, w_out_odd):
    batch, seq, _ = x_prompt.shape
    nseq, ntok, _ = x_sample.shape
    mp = batch * seq

    c_all = jnp.concatenate([c_prompt, c_sample], axis=0)
    mod_e = _ada_mod(c_all, ada_w_even[0], ada_b_even[0])
    mod_o = _ada_mod(c_all, ada_w_odd[0], ada_b_odd[0])

    def split_mod(mod):
        shift, scale, gate = jnp.split(mod, 3, axis=1)
        p = tuple(a[:batch].reshape(batch, 1, D_MODEL) for a in (shift, scale, gate))
        s = tuple(_rows(a[batch:], SAMPLE_PAD) for a in (shift, scale, gate))
        return p, s

    (shift_ep, scale_ep, gate_ep), (shift_es, scale_es, gate_es) = split_mod(mod_e)
    (shift_op, scale_op, gate_op), (shift_os, scale_os, gate_os) = split_mod(mod_o)

    xp = x_prompt.reshape(mp, D_MODEL)
    xs = _pad_tokens(x_sample, ntok)
    ms = xs.shape[0]

    w_e, w_lri = _even_weight_layout(w_in_even[0])
    w_lr = jnp.pad(w_lr_even[0], ((0, 128 - GLA_RANK), (0, 0))).astype(BF16)
    b_lr = b_lr_even[0].reshape(1, -1)
    g_e = norm_g_even[0].reshape(1, D_MODEL)
    qn2 = jnp.tile(qn_g_even[0].reshape(1, HD_A), (1, 2))
    kn2 = jnp.tile(kn_g_even[0].reshape(1, HD_A), (1, 2))
    gla_g = gla_g_even[0].reshape(1, DV_B)
    w_out_e = w_out_even[0].astype(BF16)
    w_out_a, w_out_b = w_out_e[:W_A], w_out_e[W_A:]

    h0_p, la_p = _prenorm_even(xp, scale_ep, shift_ep, g_e, w_lri, w_lr, b_lr, tm=512, rows_per_batch=seq)
    h0_s, la_s = _prenorm_even(xs, scale_es, shift_es, g_e, w_lri, w_lr, b_lr, tm=512, rows_per_batch=ms)
    proj_p = _inproj(h0_p, w_e, tm=1024, tn=EV_N // 3, out_dtype=BF16, name="inproj_even")
    proj_s = _inproj(h0_s, w_e, tm=512, tn=EV_N // 3, out_dtype=F32, name="inproj_even")

    ii = jnp.arange(WINDOW)
    ss = jnp.arange(2 * WINDOW)
    bucket_p = _bucket_or_masked(WINDOW + ii[:, None] - ss[None, :])
    mixed_a_p, k_last = _swa_prompt(proj_p, rel_bias, sinks_even[0], qn2, kn2, bucket_p, batch=batch, seq=seq)
    mixed_b_p, gla_p = _gla_prompt(proj_p, la_p, gla_g, batch=batch, seq=seq, rows=256)
    g_o = norm_g_odd[0].reshape(1, D_MODEL)
    y1_p, h1_p = _outproj([mixed_a_p, mixed_b_p], [w_out_a, w_out_b], xp, gate_ep, (scale_op, shift_op, g_o),
                          tm=512, rows_per_batch=seq)

    proj_p3 = proj_p.reshape(batch, seq, EV_N)
    swa_k_p = k_last.reshape(1, batch, WINDOW, H_A_KV, HD_A)
    swa_v_p = proj_p3[:, seq - WINDOW:, EV_KV + W_K: EV_KV + 2 * W_K].astype(F32).reshape(
        1, batch, WINDOW, H_A_KV, HD_A)

    rr = np.arange(H_A * SAMPLE_PAD)
    row_head = 8 * (rr // 64) + 2 * ((rr % 32) // 8) + (rr % 64) // 32
    tt = jnp.asarray(rr % SAMPLE_PAD)
    jj = jnp.arange(WINDOW)
    live = (tt < ntok)[:, None]
    bkc = jnp.where(live, _bucket_or_masked(WINDOW + tt[:, None] - jj[None, :]), -1)
    bkn = jnp.where(live & (jj[None, :] < ntok), _bucket_or_masked(tt[:, None] - jj[None, :]), -1)
    relrows = rel_bias.T[row_head]
    sinkrows = jnp.broadcast_to(sinks_even[0][row_head][:, None], (H_A * SAMPLE_PAD, 128))
    w_buf = cache_swa_k.shape[2]
    mixed_a_s, k_cache_s, v_cache_s = _swa_sample(
        proj_s, cache_swa_k[0].reshape(nseq, w_buf, W_K), cache_swa_v[0].reshape(nseq, w_buf, W_K),
        bkc, bkn, relrows, sinkrows, qn2, kn2, sb=16, ntok=ntok, unroll=4)
    mixed_b_s, gla_s = _gla_sample(proj_s, la_s, gla_g, state_gla[0], sb=8, ntok=ntok)
    y1_s, h1_s = _outproj([mixed_a_s, mixed_b_s], [w_out_a, w_out_b], xs, gate_es, (scale_os, shift_os, g_o),
                          tm=256, rows_per_batch=ms)

    w_o = w_in_odd[0].astype(BF16)
    ret_g = ret_g_odd[0].reshape(1, DV_C)
    w_out_o = w_out_odd[0].astype(BF16)
    cos_p, sin_p = _rotary_tables(jnp.arange(seq))
    pos_s = PAST_LEN + jnp.minimum(jnp.arange(SAMPLE_PAD), ntok - 1)
    cos_s, sin_s = _rotary_tables(jnp.tile(pos_s, nseq))

    projo_p = _inproj(h1_p, w_o, (cos_p, sin_p), tm=1024, tn=H_C * DK_C, out_dtype=BF16, name="inproj_odd")
    o_p, ret_p = _ret_prompt(projo_p, ret_g, batch=batch, seq=seq)
    (y2_p,) = _outproj([o_p], [w_out_o], y1_p, gate_op, tm=256, rows_per_batch=seq)

    projo_s = _inproj(h1_s, w_o, (cos_s, sin_s), tm=512, tn=H_C * DK_C, out_dtype=F32, name="inproj_odd")
    o_s, ret_s = _ret_sample(projo_s, ret_g, state_ret[0], ntok=ntok)
    (y2_s,) = _outproj([o_s], [w_out_o], y1_s, gate_os, tm=256, rows_per_batch=ms)

    y_prompt = y2_p.reshape(batch, seq, D_MODEL)
    y_sample = y2_s.reshape(nseq, SAMPLE_PAD, D_MODEL)[:, :ntok]
    return (y_prompt, y_sample, swa_k_p, swa_v_p, gla_p[None], ret_p[None],
            k_cache_s.reshape(1, nseq, w_buf, H_A_KV, HD_A), v_cache_s.reshape(1, nseq, w_buf, H_A_KV, HD_A),
            gla_s[None], ret_s[None])
```

```python
import functools
import math
from typing import Callable, NamedTuple

import numpy as np
import jax
import jax.numpy as jnp
from jax import lax
from jax.experimental import pallas as pl
from jax.experimental.pallas import tpu as pltpu

F32 = jnp.float32
BF16 = jnp.bfloat16

D_MODEL = 2048
WINDOW = 128
HD_A = 64
H_A = 16
H_A_KV = 2
G_A = 8
W_A = 1024
NUM_BUCKETS = 32
MAX_DISTANCE = 128
H_B = 4
DV_B = 256
DK_B = 128
W_B = 1024
GLA_RANK = 16
GLA_TAU = 16.0
GLA_CHUNK = 64
H_C = 8
DK_C = 256
DV_C = 512
W_C = 4096
RET_CHUNK = 128
ROPE_BASE = 10000.0
EPS = 1e-6
NEG_INF = -1e30
PAST_LEN = 8192

EV_QA, EV_GA, EV_VB, EV_GB, EV_QB, EV_KB, EV_KV = 0, 1024, 2048, 3072, 4096, 4608, 5120
EV_N = 5376
OD_N = 12288
W_K = H_A_KV * HD_A
SAMPLE_PAD = 8
MXU_N = 256
VMEM_LIMIT = 56 * 1024 * 1024

LOG_GAMMA = [float(np.log1p(-np.exp2(np.float32(-5.0 - h)))) for h in range(H_C)]


def _cparams(sem):
    return pltpu.CompilerParams(dimension_semantics=sem, vmem_limit_bytes=VMEM_LIMIT)


def _silu(x):
    return x / (1.0 + jnp.exp(-x))


def _dot(a, b):
    return jnp.dot(a, b, preferred_element_type=F32)


def _dot_nt(a, b):
    return lax.dot_general(a, b, (((1,), (1,)), ((), ())), preferred_element_type=F32)


def _dot_tn(a, b):
    return lax.dot_general(a, b, (((0,), (0,)), ((), ())), preferred_element_type=F32)


def _cumsum_rows(x):
    c = x.shape[0]
    r = lax.broadcasted_iota(jnp.int32, (c, c), 0)
    s = lax.broadcasted_iota(jnp.int32, (c, c), 1)
    tri = (r >= s).astype(BF16)
    hi = x.astype(BF16)
    r1 = x - hi.astype(F32)
    mid = r1.astype(BF16)
    lo = (r1 - mid.astype(F32)).astype(BF16)
    return _dot(tri, hi) + _dot(tri, mid) + _dot(tri, lo)


def _row_to_col(r):
    n = r.shape[1]
    ri = lax.broadcasted_iota(jnp.int32, (n, n), 0)
    ci = lax.broadcasted_iota(jnp.int32, (n, n), 1)
    return jnp.sum(jnp.where(ri == ci, jnp.broadcast_to(r, (n, n)), 0.0), axis=1, keepdims=True)


def _ada_kernel(c_ref, w_ref, b_ref, o_ref):
    sc = _silu(c_ref[...]).astype(BF16)
    o_ref[...] = _dot(sc, w_ref[...].astype(BF16)) + b_ref[...]


def _ada_mod(c_all, w, b):
    m = c_all.shape[0]
    n = w.shape[1]
    tn = 768
    return pl.pallas_call(
        _ada_kernel,
        grid=(n // tn,),
        in_specs=[pl.BlockSpec((m, D_MODEL), lambda j: (0, 0)),
                  pl.BlockSpec((D_MODEL, tn), lambda j: (0, j)),
                  pl.BlockSpec((1, tn), lambda j: (0, j))],
        out_specs=pl.BlockSpec((m, tn), lambda j: (0, j)),
        out_shape=jax.ShapeDtypeStruct((m, n), F32),
        compiler_params=_cparams(("arbitrary",)),
        name="ada_mod",
    )(c_all, w, b.reshape(1, n))


def _norm_mod(x, g, scale, shift):
    ms = jnp.mean(x * x, axis=-1, keepdims=True)
    y = x * lax.rsqrt(ms + EPS) * g
    return y * (1.0 + scale) + shift


def _log_sigmoid(z):
    return jnp.minimum(z, 0.0) - jnp.log(1.0 + jnp.exp(-jnp.abs(z)))


def _mod_spec(arr, tm, rows_per_batch):
    if arr.ndim == 3:
        return pl.BlockSpec((None, 1, D_MODEL), lambda i: (i // (rows_per_batch // tm), 0, 0))
    return pl.BlockSpec((tm, D_MODEL), lambda i: (i, 0))


def _prenorm_even_kernel(x_ref, scale_ref, shift_ref, g_ref, wlri_ref, wlr_ref, blr_ref, h_ref, la_ref):
    hb = _norm_mod(x_ref[...], g_ref[...], scale_ref[...], shift_ref[...]).astype(BF16)
    h_ref[...] = hb
    lr = _dot(hb, wlri_ref[...])
    z = _dot(lr.astype(BF16), wlr_ref[...]) + blr_ref[...]
    la_ref[...] = _log_sigmoid(z) / GLA_TAU


def _prenorm_even(x, scale, shift, g, wlri, wlr, blr, *, tm, rows_per_batch):
    m = x.shape[0]
    const = lambda shape: pl.BlockSpec(shape, lambda i: (0, 0))
    return pl.pallas_call(
        _prenorm_even_kernel,
        grid=(m // tm,),
        in_specs=[pl.BlockSpec((tm, D_MODEL), lambda i: (i, 0)),
                  _mod_spec(scale, tm, rows_per_batch), _mod_spec(shift, tm, rows_per_batch),
                  const((1, D_MODEL)), const((D_MODEL, 128)), const((128, H_B * DK_B)), const((1, H_B * DK_B))],
        out_specs=[pl.BlockSpec((tm, D_MODEL), lambda i: (i, 0)),
                   pl.BlockSpec((tm, H_B * DK_B), lambda i: (i, 0))],
        out_shape=[jax.ShapeDtypeStruct((m, D_MODEL), BF16),
                   jax.ShapeDtypeStruct((m, H_B * DK_B), F32)],
        compiler_params=_cparams(("arbitrary",)),
        name="prenorm_even",
    )(x, scale, shift, g, wlri, wlr, blr)


class _SideJob(NamedTuple):
    njobs: int
    args: list
    in_specs: Callable
    out_specs: Callable
    out_shape: list
    body: Callable


def _inproj_kernel(*refs, rot_tiles, side, nsteps):
    nmain = 4 if rot_tiles else 2
    if side is not None:
        njobs, nsi, nso, body = side
        side_in = refs[nmain:nmain + nsi]
        side_out = refs[nmain + nsi + 1:nmain + nsi + 1 + nso]
        step = pl.program_id(0) * pl.num_programs(1) + pl.program_id(1)
        block = (step * njobs) // nsteps
        prev_block = ((step - 1) * njobs) // nsteps

        @pl.when((step == 0) | (block != prev_block))
        def _():
            body(side_in, side_out)

        refs = refs[:nmain] + (refs[nmain + nsi],)
    h_ref, w_ref, *rest = refs
    o_ref = rest[-1]
    tn = o_ref.shape[1]

    def plain():
        for c in range(tn // MXU_N):
            cs = slice(c * MXU_N, (c + 1) * MXU_N)
            o_ref[:, cs] = _dot(h_ref[...], w_ref[:, cs]).astype(o_ref.dtype)

    if rot_tiles == 0:
        plain()
        return
    cos_ref, sin_ref = rest[:2]
    j = pl.program_id(0)
    half = DK_C // 2

    @pl.when(j < rot_tiles)
    def _():
        mult = jnp.where(j < rot_tiles // 2, 1.0, DK_C ** -0.5)
        cos = cos_ref[...] * mult
        sin = sin_ref[...] * mult
        for c in range(tn // DK_C):
            acc = _dot(h_ref[...], w_ref[:, c * DK_C:(c + 1) * DK_C])
            x1 = acc[:, :half]
            x2 = acc[:, half:]
            o_ref[:, c * DK_C:c * DK_C + half] = (x1 * cos - x2 * sin).astype(o_ref.dtype)
            o_ref[:, c * DK_C + half:(c + 1) * DK_C] = (x2 * cos + x1 * sin).astype(o_ref.dtype)

    pl.when(j >= rot_tiles)(plain)


def _inproj(h, w, rot=None, side=None, *, tm, tn, out_dtype, name):
    m = h.shape[0]
    n = w.shape[1]
    ni = m // tm
    nsteps = (n // tn) * ni
    in_specs = [pl.BlockSpec((tm, D_MODEL), lambda j, i: (i, 0)),
                pl.BlockSpec((D_MODEL, tn), lambda j, i: (0, j))]
    args = [h, w]
    rot_tiles = 0
    if rot is not None:
        nblk = rot[0].shape[0] // tm
        in_specs += [pl.BlockSpec((tm, DK_C // 2), lambda j, i: (i % nblk, 0))] * 2
        args += list(rot)
        rot_tiles = 2 * (H_C * DK_C) // tn
    out_specs = [pl.BlockSpec((tm, tn), lambda j, i: (i, j))]
    out_shape = [jax.ShapeDtypeStruct((m, n), out_dtype)]
    side_static = None
    if side is not None:
        assert side.njobs <= nsteps
        block = lambda j, i: ((j * ni + i) * side.njobs) // nsteps
        in_specs += side.in_specs(block)
        args += side.args
        out_specs += side.out_specs(block)
        out_shape += side.out_shape
        side_static = (side.njobs, len(side.args), len(side.out_shape), side.body)
    return pl.pallas_call(
        functools.partial(_inproj_kernel, rot_tiles=rot_tiles, side=side_static, nsteps=nsteps),
        grid=(n // tn, ni),
        in_specs=in_specs,
        out_specs=out_specs,
        out_shape=out_shape,
        compiler_params=_cparams(("arbitrary", "arbitrary")),
        name=name,
    )(*args)


def _outproj_kernel(*refs, nin, with_next):
    m_refs, w_refs = refs[:nin], refs[nin:2 * nin]
    if with_next:
        x_ref, gate_ref, nscale_ref, nshift_ref, ng_ref, o_ref, h_ref = refs[2 * nin:]
    else:
        x_ref, gate_ref, o_ref = refs[2 * nin:]
    for c in range(o_ref.shape[1] // MXU_N):
        cs = slice(c * MXU_N, (c + 1) * MXU_N)
        acc = _dot(m_refs[0][...].astype(BF16), w_refs[0][:, cs])
        for m_ref, w_ref in zip(m_refs[1:], w_refs[1:]):
            acc += _dot(m_ref[...].astype(BF16), w_ref[:, cs])
        o_ref[:, cs] = x_ref[:, cs] + gate_ref[:, cs] * acc
    if with_next:
        h_ref[...] = _norm_mod(o_ref[...], ng_ref[...], nscale_ref[...], nshift_ref[...]).astype(BF16)


def _outproj(mixed, ws, x, gate, next_mod=None, *, tm, rows_per_batch):
    m = x.shape[0]
    row = pl.BlockSpec((tm, D_MODEL), lambda i: (i, 0))
    in_specs = ([pl.BlockSpec((tm, a.shape[1]), lambda i: (i, 0)) for a in mixed]
                + [pl.BlockSpec(w.shape, lambda i: (0, 0), pipeline_mode=pl.Buffered(1)) for w in ws]
                + [row, _mod_spec(gate, tm, rows_per_batch)])
    args = [*mixed, *ws, x, gate]
    out_specs = [row]
    out_shape = [jax.ShapeDtypeStruct((m, D_MODEL), F32)]
    if next_mod is not None:
        nscale, nshift, ng = next_mod
        in_specs += [_mod_spec(nscale, tm, rows_per_batch), _mod_spec(nshift, tm, rows_per_batch),
                     pl.BlockSpec((1, D_MODEL), lambda i: (0, 0))]
        args += [nscale, nshift, ng]
        out_specs.append(row)
        out_shape.append(jax.ShapeDtypeStruct((m, D_MODEL), BF16))
    return pl.pallas_call(
        functools.partial(_outproj_kernel, nin=len(mixed), with_next=next_mod is not None),
        grid=(m // tm,),
        in_specs=in_specs,
        out_specs=out_specs,
        out_shape=out_shape,
        compiler_params=_cparams(("arbitrary",)),
        name="outproj",
    )(*args)


def _seg_rms(x, g2, seg_ones):
    x2 = x * x
    hi = x2.astype(BF16)
    lo = (x2 - hi.astype(F32)).astype(BF16)
    ss = _dot(hi, seg_ones) + _dot(lo, seg_ones)
    return x * lax.rsqrt(ss * (1.0 / HD_A) + EPS) * g2


def _dup_halves(x, lo_half):
    sw = pltpu.roll(x, HD_A, 1)
    return [jnp.where(lo_half, x, sw), jnp.where(lo_half, sw, x)]


def _seg_ones():
    r = lax.broadcasted_iota(jnp.int32, (2 * HD_A, 2 * HD_A), 0)
    c = lax.broadcasted_iota(jnp.int32, (2 * HD_A, 2 * HD_A), 1)
    return ((r < HD_A) == (c < HD_A)).astype(BF16)


def _swa_prompt_kernel(relb_ref, sinks_ref, bucket_ref, q_ref, ga_ref, kv_ref, qn_ref, kn_ref,
                       o_ref, knew_ref, bias_ref, kprev_ref, vprev_ref, s_ref, pe_ref):
    b = pl.program_id(0)
    i = pl.program_id(1)

    @pl.when((b == 0) & (i == 0))
    def _():
        bk = bucket_ref[...]
        for h in range(H_A):
            bias_ref[h] = jnp.full((WINDOW, 2 * WINDOW), NEG_INF, F32)
        for bb in range(NUM_BUCKETS):
            hit = bk == bb
            for h in range(H_A):
                bias_ref[h] = jnp.where(hit, relb_ref[bb, h], bias_ref[h])
        col = lax.broadcasted_iota(jnp.int32, (WINDOW, 2 * WINDOW), 1)
        for h in range(H_A):
            bias_ref[H_A + h] = jnp.where(col >= WINDOW, bias_ref[h], NEG_INF)

    @pl.when(i == 0)
    def _():
        kprev_ref[...] = jnp.zeros(kprev_ref.shape, BF16)
        vprev_ref[...] = jnp.zeros(vprev_ref.shape, BF16)

    seg_ones = _seg_ones()
    lt = 2 * HD_A
    tpg = G_A // 2
    lo_half = lax.broadcasted_iota(jnp.int32, (WINDOW, lt), 1) < HD_A
    kv = kv_ref[...].astype(F32)
    kc = _seg_rms(kv[:, :W_K], kn_ref[...], seg_ones)
    k_dup = _dup_halves(kc, lo_half)
    v_dup = _dup_halves(kv[:, W_K:], lo_half)
    base = jnp.where(i == 0, H_A, 0)
    qn2 = qn_ref[...] * HD_A ** -0.5
    gq = tpg * WINDOW
    lo_g = lax.broadcasted_iota(jnp.int32, (gq, lt), 1) < HD_A
    grows = [slice(g * 2 * gq, (g + 1) * 2 * gq) for g in range(H_A_KV)]
    for g in range(H_A_KV):
        kk = jnp.concatenate([kprev_ref[g], k_dup[g].astype(BF16)], axis=0)
        xg = jnp.concatenate([q_ref[:, (g * tpg + p) * lt:(g * tpg + p + 1) * lt] for p in range(tpg)],
                             axis=0).astype(F32)
        xn = _seg_rms(xg, qn2, seg_ones)
        q_all = jnp.concatenate([jnp.where(lo_g, xn, 0.0), jnp.where(lo_g, 0.0, xn)], axis=0).astype(BF16)
        s_ref[grows[g], :] = _dot_nt(q_all, kk)
    sink_terms = []
    for g in range(H_A_KV):
        for n in range(G_A):
            h = g * G_A + 2 * (n % tpg) + n // tpg
            rows = slice((g * G_A + n) * WINDOW, (g * G_A + n + 1) * WINDOW)
            s = s_ref[rows, :] + bias_ref[base + h]
            sink = sinks_ref[h]
            m = jnp.maximum(jnp.max(s, axis=-1, keepdims=True), sink)
            pe_ref[rows, :] = jnp.exp(s - m).astype(BF16)
            sink_terms.append(jnp.exp(sink - m))
    o_ext = []
    for g in range(H_A_KV):
        vv = jnp.concatenate([vprev_ref[g], v_dup[g].astype(BF16)], axis=0)
        vv_ext = jnp.concatenate([vv, jnp.ones(vv.shape, BF16)], axis=1)
        o_ext.append(_dot(pe_ref[grows[g], :], vv_ext))
    for g in range(H_A_KV):
        for p in range(tpg):
            halves = []
            for a in range(2):
                n = a * tpg + p
                rows = slice(n * WINDOW, (n + 1) * WINDOW)
                halves.append(o_ext[g][rows, :lt] / (o_ext[g][rows, lt:] + sink_terms[g * G_A + n]))
            oa = jnp.where(lo_half, halves[0], halves[1])
            cols = slice((g * tpg + p) * lt, (g * tpg + p + 1) * lt)
            o_ref[:, cols] = (oa * _silu(ga_ref[:, cols].astype(F32))).astype(o_ref.dtype)
    for g in range(H_A_KV):
        kprev_ref[g] = k_dup[g].astype(BF16)
        vprev_ref[g] = v_dup[g].astype(BF16)

    @pl.when(i == pl.num_programs(1) - 1)
    def _():
        knew_ref[...] = kc


def _swa_prompt(proj, rel_bias, sinks, qn2, kn2, bucket, *, batch, seq):
    nb = seq // WINDOW
    rb = lambda b, i: b * nb + i
    smem = pl.BlockSpec(memory_space=pltpu.SMEM)
    return pl.pallas_call(
        _swa_prompt_kernel,
        grid=(batch, nb),
        in_specs=[smem, smem,
                  pl.BlockSpec((WINDOW, 2 * WINDOW), lambda b, i: (0, 0)),
                  pl.BlockSpec((WINDOW, W_A), lambda b, i: (rb(b, i), EV_QA // W_A)),
                  pl.BlockSpec((WINDOW, W_A), lambda b, i: (rb(b, i), EV_GA // W_A)),
                  pl.BlockSpec((WINDOW, 2 * W_K), lambda b, i: (rb(b, i), EV_KV // (2 * W_K))),
                  pl.BlockSpec((1, 2 * HD_A), lambda b, i: (0, 0)),
                  pl.BlockSpec((1, 2 * HD_A), lambda b, i: (0, 0))],
        out_specs=[pl.BlockSpec((WINDOW, W_A), lambda b, i: (rb(b, i), 0)),
                   pl.BlockSpec((None, WINDOW, W_K), lambda b, i: (b, 0, 0))],
        out_shape=[jax.ShapeDtypeStruct((batch * seq, W_A), BF16),
                   jax.ShapeDtypeStruct((batch, WINDOW, W_K), F32)],
        scratch_shapes=[pltpu.VMEM((2 * H_A, WINDOW, 2 * WINDOW), F32),
                        pltpu.VMEM((H_A_KV, WINDOW, W_K), BF16), pltpu.VMEM((H_A_KV, WINDOW, W_K), BF16),
                        pltpu.VMEM((H_A * WINDOW, 2 * WINDOW), F32), pltpu.VMEM((H_A * WINDOW, 2 * WINDOW), BF16)],
        compiler_params=_cparams(("arbitrary", "arbitrary")),
        name="swa_prompt",
    )(rel_bias, sinks, bucket, proj, proj, proj, qn2, kn2)


def _swa_sample_kernel(bkc_ref, bkn_ref, relrows_ref, sinkrows_ref, q_ref, ga_ref, kvn_ref, ck_ref, cv_ref,
                       qn_ref, kn_ref, o_ref, ko_ref, vo_ref, biasc_ref, biasn_ref, *, sb, ntok, unroll):
    @pl.when(pl.program_id(0) == 0)
    def _():
        bkc = bkc_ref[...]
        bkn = bkn_ref[...]
        rr = relrows_ref[...]
        bc = jnp.full(bkc.shape, NEG_INF, F32)
        bn = jnp.full(bkn.shape, NEG_INF, F32)
        for bb in range(NUM_BUCKETS):
            val = rr[:, bb:bb + 1]
            bc = jnp.where(bkc == bb, val, bc)
            bn = jnp.where(bkn == bb, val, bn)
        biasc_ref[...] = bc
        biasn_ref[...] = bn

    seg_ones = _seg_ones()
    lt = 2 * HD_A
    nrow = G_A * SAMPLE_PAD
    grow = nrow // H_A_KV
    lo_q = lax.broadcasted_iota(jnp.int32, (grow, lt), 1) < HD_A
    lo_k = lax.broadcasted_iota(jnp.int32, (WINDOW, lt), 1) < HD_A
    lo_n = lax.broadcasted_iota(jnp.int32, (SAMPLE_PAD, lt), 1) < HD_A
    qn2 = qn_ref[...] * HD_A ** -0.5
    kn2 = kn_ref[...]
    sink = sinkrows_ref[:, 0:1]
    nkeep = WINDOW - ntok

    def body(it, carry):
        seqs = [it * unroll + u for u in range(unroll)]
        rows = [pl.ds(pl.multiple_of(s * SAMPLE_PAD, SAMPLE_PAD), SAMPLE_PAD) for s in seqs]
        kvn = [kvn_ref[r, :] for r in rows]
        kn_all = _seg_rms(jnp.concatenate([x[:, :W_K] for x in kvn], axis=0), kn2, seg_ones)
        q2_all = jnp.concatenate([q_ref[r, p * lt:(p + 1) * lt] for r in rows for p in range(G_A)], axis=0)
        xn_all = _seg_rms(q2_all, qn2, seg_ones)
        sc, sn, kn = [], [], []
        for u, s in enumerate(seqs):
            kn.append(kn_all[u * SAMPLE_PAD:(u + 1) * SAMPLE_PAD])
            ck_dup = _dup_halves(ck_ref[s], lo_k)
            kn_dup = _dup_halves(kn[u], lo_n)
            scg, sng = [], []
            for g in range(H_A_KV):
                xn = xn_all[u * nrow + g * grow:u * nrow + (g + 1) * grow]
                q4 = jnp.concatenate([jnp.where(lo_q, xn, 0.0), jnp.where(lo_q, 0.0, xn)], axis=0).astype(BF16)
                scg.append(_dot_nt(q4, ck_dup[g].astype(BF16)))
                sng.append(_dot_nt(q4, kn_dup[g].astype(BF16)))
            sc.append(jnp.concatenate(scg, axis=0))
            sn.append(jnp.concatenate(sng, axis=0))
        pc, pn, sink_terms = [], [], []
        for u in range(unroll):
            scu = sc[u] + biasc_ref[...]
            snu = sn[u] + biasn_ref[:, :SAMPLE_PAD]
            m = jnp.maximum(jnp.maximum(jnp.max(scu, axis=-1, keepdims=True),
                                        jnp.max(snu, axis=-1, keepdims=True)), sink)
            pc.append(jnp.exp(scu - m).astype(BF16))
            pn.append(jnp.exp(snu - m).astype(BF16))
            sink_terms.append(jnp.exp(sink - m))
        o_ext = []
        for u, s in enumerate(seqs):
            cv_dup = _dup_halves(cv_ref[s], lo_k)
            vn_dup = _dup_halves(kvn[u][:, W_K:], lo_n)
            og = []
            for g in range(H_A_KV):
                cv_ext = jnp.concatenate([cv_dup[g].astype(BF16), jnp.ones((WINDOW, lt), BF16)], axis=1)
                vn_ext = jnp.concatenate([vn_dup[g].astype(BF16), jnp.ones((SAMPLE_PAD, lt), BF16)], axis=1)
                gr = slice(g * 2 * grow, (g + 1) * 2 * grow)
                og.append(_dot(pc[u][gr], cv_ext) + _dot(pn[u][gr], vn_ext))
            o_ext.append(jnp.concatenate(og, axis=0))
        for u, s in enumerate(seqs):
            o4 = o_ext[u][:, :lt] / (o_ext[u][:, lt:] + sink_terms[u])
            xg = ga_ref[rows[u], :]
            for p in range(G_A):
                g, pp = divmod(p, G_A // 2)
                r0 = g * 2 * grow + pp * SAMPLE_PAD
                o2 = jnp.where(lo_n, o4[r0:r0 + SAMPLE_PAD], o4[r0 + grow:r0 + grow + SAMPLE_PAD])
                o_ref[rows[u], p * lt:(p + 1) * lt] = (o2 * _silu(xg[:, p * lt:(p + 1) * lt])).astype(o_ref.dtype)
            ko_ref[s, 0:nkeep, :] = ck_ref[s, ntok:WINDOW, :]
            ko_ref[s, nkeep:WINDOW, :] = kn[u][0:ntok]
            vo_ref[s, 0:nkeep, :] = cv_ref[s, ntok:WINDOW, :]
            vo_ref[s, nkeep:WINDOW, :] = kvn[u][0:ntok, W_K:]
        return carry

    lax.fori_loop(0, sb // unroll, body, 0)


def _swa_sample(proj, cache_k, cache_v, bkc, bkn, relrows, sinkrows, qn2, kn2, *, sb, ntok, unroll):
    nseq = cache_k.shape[0]
    rows = sb * SAMPLE_PAD
    full = lambda shape: pl.BlockSpec(shape, lambda i: tuple(0 for _ in shape))
    cache = pl.BlockSpec((sb, WINDOW, W_K), lambda i: (i, 0, 0))
    return pl.pallas_call(
        functools.partial(_swa_sample_kernel, sb=sb, ntok=ntok, unroll=unroll),
        grid=(nseq // sb,),
        in_specs=[full(bkc.shape), full(bkn.shape), full(relrows.shape), full(sinkrows.shape),
                  pl.BlockSpec((rows, W_A), lambda i: (i, EV_QA // W_A)),
                  pl.BlockSpec((rows, W_A), lambda i: (i, EV_GA // W_A)),
                  pl.BlockSpec((rows, 2 * W_K), lambda i: (i, EV_KV // (2 * W_K))),
                  cache, cache, full((1, 2 * HD_A)), full((1, 2 * HD_A))],
        out_specs=[pl.BlockSpec((rows, W_A), lambda i: (i, 0)), cache, cache],
        out_shape=[jax.ShapeDtypeStruct((nseq * SAMPLE_PAD, W_A), F32),
                   jax.ShapeDtypeStruct(cache_k.shape, F32),
                   jax.ShapeDtypeStruct(cache_v.shape, F32)],
        scratch_shapes=[pltpu.VMEM(bkc.shape, F32), pltpu.VMEM(bkn.shape, F32)],
        compiler_params=_cparams(("arbitrary",)),
        name="swa_sample",
    )(bkc, bkn, relrows, sinkrows, proj, proj, proj, cache_k, cache_v, qn2, kn2)


def _gla_chunk(q, k, v, gb, la, gla_g, state_ref, state_idx, n_valid):
    c = q.shape[0]
    bcum = _cumsum_rows(la)
    rr = lax.broadcasted_iota(jnp.int32, (c, c), 0)
    cc = lax.broadcasted_iota(jnp.int32, (c, c), 1)
    causal = rr >= cc
    row = lax.broadcasted_iota(jnp.int32, (c, 1), 0)
    outs = []
    for h in range(H_B):
        ks = slice(h * DK_B, (h + 1) * DK_B)
        vs = slice(h * DV_B, (h + 1) * DV_B)
        bc = bcum[:, ks]
        qt = (q[:, ks] * DK_B ** -0.5) * jnp.exp(bc)
        kt = k[:, ks] * jnp.exp(-bc)
        a = jnp.where(causal, _dot_nt(qt.astype(BF16), kt.astype(BF16)), 0.0)
        st = state_ref[state_idx + (h,)]
        vb = v[:, vs].astype(BF16)
        qtb = qt.astype(BF16)
        o = _dot(a.astype(BF16), vb) + _dot(qtb, st.astype(BF16))
        blast = bc[n_valid - 1:n_valid, :]
        kd = k[:, ks] * jnp.exp(blast - bc)
        if n_valid < c:
            kd = jnp.where(row < n_valid, kd, 0.0)
        upd = _dot_tn(kd.astype(BF16), vb)
        state_ref[state_idx + (h,)] = _row_to_col(jnp.exp(blast)) * st + upd
        on = o * lax.rsqrt(jnp.mean(o * o, axis=-1, keepdims=True) + EPS) * gla_g
        outs.append(on * _silu(gb[:, vs]))
    return jnp.concatenate(outs, axis=1)


def _gla_prompt_kernel(q_ref, k_ref, v_ref, gb_ref, la_ref, g_ref, o_ref, s_ref, *, nchunk):
    @pl.when(pl.program_id(1) == 0)
    def _():
        s_ref[...] = jnp.zeros(s_ref.shape, F32)

    gla_g = g_ref[...]
    for cidx in range(nchunk):
        rows = slice(cidx * GLA_CHUNK, (cidx + 1) * GLA_CHUNK)
        out = _gla_chunk(q_ref[rows, :].astype(F32), k_ref[rows, :].astype(F32), v_ref[rows, :].astype(F32),
                         gb_ref[rows, :].astype(F32), la_ref[rows, :], gla_g, s_ref, (), GLA_CHUNK)
        o_ref[rows, :] = out.astype(o_ref.dtype)


def _gla_prompt(proj, la, gla_g, *, batch, seq, rows):
    nstep = seq // rows
    rb = lambda b, i: b * nstep + i
    return pl.pallas_call(
        functools.partial(_gla_prompt_kernel, nchunk=rows // GLA_CHUNK),
        grid=(batch, nstep),
        in_specs=[pl.BlockSpec((rows, H_B * DK_B), lambda b, i: (rb(b, i), EV_QB // (H_B * DK_B))),
                  pl.BlockSpec((rows, H_B * DK_B), lambda b, i: (rb(b, i), EV_KB // (H_B * DK_B))),
                  pl.BlockSpec((rows, W_B), lambda b, i: (rb(b, i), EV_VB // W_B)),
                  pl.BlockSpec((rows, W_B), lambda b, i: (rb(b, i), EV_GB // W_B)),
                  pl.BlockSpec((rows, H_B * DK_B), lambda b, i: (rb(b, i), 0)),
                  pl.BlockSpec((1, DV_B), lambda b, i: (0, 0))],
        out_specs=[pl.BlockSpec((rows, W_B), lambda b, i: (rb(b, i), 0)),
                   pl.BlockSpec((None, H_B, DK_B, DV_B), lambda b, i: (b, 0, 0, 0))],
        out_shape=[jax.ShapeDtypeStruct((batch * seq, W_B), BF16),
                   jax.ShapeDtypeStruct((batch, H_B, DK_B, DV_B), F32)],
        compiler_params=_cparams(("arbitrary", "arbitrary")),
        name="gla_prompt",
    )(proj, proj, proj, proj, la, gla_g)


def _gla_sample_kernel(q_ref, k_ref, v_ref, gb_ref, la_ref, g_ref, s_in_ref, o_ref, s_ref, *, sb, ntok):
    gla_g = g_ref[...]
    s_ref[...] = s_in_ref[...]

    def body(s, carry):
        rows = pl.ds(pl.multiple_of(s * SAMPLE_PAD, SAMPLE_PAD), SAMPLE_PAD)
        out = _gla_chunk(q_ref[rows, :], k_ref[rows, :], v_ref[rows, :], gb_ref[rows, :], la_ref[rows, :],
                         gla_g, s_ref, (s,), ntok)
        o_ref[rows, :] = out.astype(o_ref.dtype)
        return carry

    lax.fori_loop(0, sb, body, 0)


def _gla_sample(proj, la, gla_g, state, *, sb, ntok):
    nseq = state.shape[0]
    rows = sb * SAMPLE_PAD
    return pl.pallas_call(
        functools.partial(_gla_sample_kernel, sb=sb, ntok=ntok),
        grid=(nseq // sb,),
        in_specs=[pl.BlockSpec((rows, H_B * DK_B), lambda i: (i, EV_QB // (H_B * DK_B))),
                  pl.BlockSpec((rows, H_B * DK_B), lambda i: (i, EV_KB // (H_B * DK_B))),
                  pl.BlockSpec((rows, W_B), lambda i: (i, EV_VB // W_B)),
                  pl.BlockSpec((rows, W_B), lambda i: (i, EV_GB // W_B)),
                  pl.BlockSpec((rows, H_B * DK_B), lambda i: (i, 0)),
                  pl.BlockSpec((1, DV_B), lambda i: (0, 0)),
                  pl.BlockSpec((sb, H_B, DK_B, DV_B), lambda i: (i, 0, 0, 0))],
        out_specs=[pl.BlockSpec((rows, W_B), lambda i: (i, 0)),
                   pl.BlockSpec((sb, H_B, DK_B, DV_B), lambda i: (i, 0, 0, 0))],
        out_shape=[jax.ShapeDtypeStruct((nseq * SAMPLE_PAD, W_B), BF16),
                   jax.ShapeDtypeStruct(state.shape, F32)],
        compiler_params=_cparams(("arbitrary",)),
        name="gla_sample",
    )(proj, proj, proj, proj, la, gla_g, state)


def _ret_chunk(q_ref, k_ref, v_ref, g_ref, retg, state_ref, o_ref, n_valid):
    c = q_ref.shape[0]
    ri = lax.broadcasted_iota(jnp.int32, (c, c), 0)
    ci = lax.broadcasted_iota(jnp.int32, (c, c), 1)
    dist = (ri - ci).astype(F32)
    row = lax.broadcasted_iota(jnp.int32, (c, 1), 0)
    rowf = row.astype(F32)
    a_heads = []
    for h in range(H_C):
        lg = LOG_GAMMA[h]
        ks = slice(h * DK_C, (h + 1) * DK_C)
        decay = jnp.where(ri >= ci, jnp.exp(jnp.maximum(dist, 0.0) * lg), 0.0)
        a_heads.append((_dot_nt(q_ref[:, ks].astype(BF16), k_ref[:, ks].astype(BF16)) * decay).astype(BF16))
    for h in range(H_C):
        lg = LOG_GAMMA[h]
        ks = slice(h * DK_C, (h + 1) * DK_C)
        vs = slice(h * DV_C, (h + 1) * DV_C)
        qb = q_ref[:, ks].astype(BF16)
        kf = k_ref[:, ks].astype(F32)
        vb = v_ref[:, vs].astype(BF16)
        st = state_ref[h]
        inner = jnp.exp((rowf + 1.0) * lg)
        o = _dot(a_heads[h], vb) + inner * _dot(qb, st.astype(BF16))
        kd = kf * jnp.exp((n_valid - 1.0 - rowf) * lg)
        if n_valid < c:
            kd = jnp.where(row < n_valid, kd, 0.0)
        state_ref[h] = math.exp(n_valid * lg) * st + _dot_tn(kd.astype(BF16), vb)
        on = o * lax.rsqrt(jnp.mean(o * o, axis=-1, keepdims=True) + EPS) * retg
        o_ref[:, vs] = (on * _silu(g_ref[:, vs].astype(F32))).astype(o_ref.dtype)


def _ret_prompt_kernel(q_ref, k_ref, v_ref, g_ref, retg_ref, o_ref, s_ref):
    @pl.when(pl.program_id(1) == 0)
    def _():
        s_ref[...] = jnp.zeros(s_ref.shape, F32)

    _ret_chunk(q_ref, k_ref, v_ref, g_ref, retg_ref[...], s_ref, o_ref, RET_CHUNK)


def _ret_prompt(proj, ret_g, *, batch, seq):
    nstep = seq // RET_CHUNK
    rb = lambda b, i: b * nstep + i
    qk = H_C * DK_C
    return pl.pallas_call(
        _ret_prompt_kernel,
        grid=(batch, nstep),
        in_specs=[pl.BlockSpec((RET_CHUNK, qk), lambda b, i: (rb(b, i), 0)),
                  pl.BlockSpec((RET_CHUNK, qk), lambda b, i: (rb(b, i), 1)),
                  pl.BlockSpec((RET_CHUNK, W_C), lambda b, i: (rb(b, i), 1)),
                  pl.BlockSpec((RET_CHUNK, W_C), lambda b, i: (rb(b, i), 2)),
                  pl.BlockSpec((1, DV_C), lambda b, i: (0, 0))],
        out_specs=[pl.BlockSpec((RET_CHUNK, W_C), lambda b, i: (rb(b, i), 0)),
                   pl.BlockSpec((None, H_C, DK_C, DV_C), lambda b, i: (b, 0, 0, 0))],
        out_shape=[jax.ShapeDtypeStruct((batch * seq, W_C), BF16),
                   jax.ShapeDtypeStruct((batch, H_C, DK_C, DV_C), F32)],
        compiler_params=_cparams(("arbitrary", "arbitrary")),
        name="ret_prompt",
    )(proj, proj, proj, proj, ret_g)


def _ret_sample_job(proj, ret_g, state, *, ntok):
    nseq = state.shape[0]
    qk = H_C * DK_C
    state_spec = lambda blk: pl.BlockSpec((None, H_C, DK_C, DV_C), lambda j, i: (blk(j, i), 0, 0, 0))

    def body(in_refs, out_refs):
        q_ref, k_ref, v_ref, g_ref, retg_ref, s_in_ref = in_refs
        o_ref, s_ref = out_refs
        s_ref[...] = s_in_ref[...]
        _ret_chunk(q_ref, k_ref, v_ref, g_ref, retg_ref[...], s_ref, o_ref, ntok)

    return _SideJob(
        njobs=nseq,
        args=[proj, proj, proj, proj, ret_g, state],
        in_specs=lambda blk: [pl.BlockSpec((SAMPLE_PAD, qk), lambda j, i: (blk(j, i), 0)),
                              pl.BlockSpec((SAMPLE_PAD, qk), lambda j, i: (blk(j, i), 1)),
                              pl.BlockSpec((SAMPLE_PAD, W_C), lambda j, i: (blk(j, i), 1)),
                              pl.BlockSpec((SAMPLE_PAD, W_C), lambda j, i: (blk(j, i), 2)),
                              pl.BlockSpec((1, DV_C), lambda j, i: (0, 0)),
                              state_spec(blk)],
        out_specs=lambda blk: [pl.BlockSpec((SAMPLE_PAD, W_C), lambda j, i: (blk(j, i), 0)), state_spec(blk)],
        out_shape=[jax.ShapeDtypeStruct((nseq * SAMPLE_PAD, W_C), BF16),
                   jax.ShapeDtypeStruct(state.shape, F32)],
        body=body)


def _t5_bucket(dist):
    dist = jnp.maximum(dist, 0)
    max_exact = NUM_BUCKETS // 2
    log_ratio = jnp.log(jnp.maximum(dist, 1).astype(F32) / max_exact) / math.log(MAX_DISTANCE / max_exact)
    large = jnp.minimum(max_exact + (log_ratio * (NUM_BUCKETS - max_exact)).astype(jnp.int32), NUM_BUCKETS - 1)
    return jnp.where(dist < max_exact, dist, large)


def _bucket_or_masked(dist):
    return jnp.where((dist >= 0) & (dist <= WINDOW), _t5_bucket(dist), -1).astype(jnp.int32)


def _rotary_tables(pos):
    half = DK_C // 2
    inv = ROPE_BASE ** (-jnp.arange(half, dtype=F32) / half)
    ang = pos.astype(F32)[:, None] * inv[None, :]
    return jnp.cos(ang), jnp.sin(ang)


def _even_weight_layout(w_in):
    qa, ka, va, ga, qb, kb, vb, gb, lr = jnp.split(w_in, [1024, 1152, 1280, 2304, 2816, 3328, 4352, 5376], axis=1)
    main = jnp.concatenate([qa, ga, vb, gb, qb, kb, ka, va], axis=1).astype(BF16)
    lr_pad = jnp.pad(lr, ((0, 0), (0, 128 - GLA_RANK))).astype(BF16)
    return main, lr_pad


def _pad_tokens(a, ntok):
    pad = [(0, 0), (0, SAMPLE_PAD - ntok)] + [(0, 0)] * (a.ndim - 2)
    a = jnp.pad(a, pad)
    return a.reshape((a.shape[0] * SAMPLE_PAD,) + a.shape[2:])


def _rows(v, reps):
    return jnp.repeat(v, reps, axis=0)


def kernel(x_prompt, x_sample, cache_swa_k, cache_swa_v, state_gla, state_ret, c_prompt, c_sample, rel_bias,
           ada_w_even, ada_b_even, norm_g_even, w_in_even, w_lr_even, b_lr_even, qn_g_even, kn_g_even,
           sinks_even, gla_g_even, w_out_even, ada_w_odd, ada_b_odd, norm_g_odd, w_in_odd, ret_g_odd, w_out_odd):
    batch, seq, _ = x_prompt.shape
    nseq, ntok, _ = x_sample.shape
    mp = batch * seq

    c_all = jnp.concatenate([c_prompt, c_sample], axis=0)
    mod_e = _ada_mod(c_all, ada_w_even[0], ada_b_even[0])
    mod_o = _ada_mod(c_all, ada_w_odd[0], ada_b_odd[0])

    def split_mod(mod):
        shift, scale, gate = jnp.split(mod, 3, axis=1)
        p = tuple(a[:batch].reshape(batch, 1, D_MODEL) for a in (shift, scale, gate))
        s = tuple(_rows(a[batch:], SAMPLE_PAD) for a in (shift, scale, gate))
        return p, s

    (shift_ep, scale_ep, gate_ep), (shift_es, scale_es, gate_es) = split_mod(mod_e)
    (shift_op, scale_op, gate_op), (shift_os, scale_os, gate_os) = split_mod(mod_o)

    xp = x_prompt.reshape(mp, D_MODEL)
    xs = _pad_tokens(x_sample, ntok)
    ms = xs.shape[0]

    w_e, w_lri = _even_weight_layout(w_in_even[0])
    w_lr = jnp.pad(w_lr_even[0], ((0, 128 - GLA_RANK), (0, 0))).astype(BF16)
    b_lr = b_lr_even[0].reshape(1, -1)
    g_e = norm_g_even[0].reshape(1, D_MODEL)
    qn2 = jnp.tile(qn_g_even[0].reshape(1, HD_A), (1, 2))
    kn2 = jnp.tile(kn_g_even[0].reshape(1, HD_A), (1, 2))
    gla_g = gla_g_even[0].reshape(1, DV_B)
    w_out_e = w_out_even[0].astype(BF16)
    w_out_a, w_out_b = w_out_e[:W_A], w_out_e[W_A:]

    h0_p, la_p = _prenorm_even(xp, scale_ep, shift_ep, g_e, w_lri, w_lr, b_lr, tm=512, rows_per_batch=seq)
    h0_s, la_s = _prenorm_even(xs, scale_es, shift_es, g_e, w_lri, w_lr, b_lr, tm=512, rows_per_batch=ms)
    (proj_p,) = _inproj(h0_p, w_e, tm=1024, tn=EV_N // 3, out_dtype=BF16, name="inproj_even")
    (proj_s,) = _inproj(h0_s, w_e, tm=512, tn=EV_N // 3, out_dtype=F32, name="inproj_even")

    ii = jnp.arange(WINDOW)
    ss = jnp.arange(2 * WINDOW)
    bucket_p = _bucket_or_masked(WINDOW + ii[:, None] - ss[None, :])
    mixed_a_p, k_last = _swa_prompt(proj_p, rel_bias, sinks_even[0], qn2, kn2, bucket_p, batch=batch, seq=seq)
    mixed_b_p, gla_p = _gla_prompt(proj_p, la_p, gla_g, batch=batch, seq=seq, rows=256)
    g_o = norm_g_odd[0].reshape(1, D_MODEL)
    y1_p, h1_p = _outproj([mixed_a_p, mixed_b_p], [w_out_a, w_out_b], xp, gate_ep, (scale_op, shift_op, g_o),
                          tm=512, rows_per_batch=seq)

    proj_p3 = proj_p.reshape(batch, seq, EV_N)
    swa_k_p = k_last.reshape(1, batch, WINDOW, H_A_KV, HD_A)
    swa_v_p = proj_p3[:, seq - WINDOW:, EV_KV + W_K: EV_KV + 2 * W_K].astype(F32).reshape(
        1, batch, WINDOW, H_A_KV, HD_A)

    rr = np.arange(H_A * SAMPLE_PAD)
    row_head = 8 * (rr // 64) + 2 * ((rr % 32) // 8) + (rr % 64) // 32
    tt = jnp.asarray(rr % SAMPLE_PAD)
    jj = jnp.arange(WINDOW)
    live = (tt < ntok)[:, None]
    bkc = jnp.where(live, _bucket_or_masked(WINDOW + tt[:, None] - jj[None, :]), -1)
    bkn = jnp.where(live & (jj[None, :] < ntok), _bucket_or_masked(tt[:, None] - jj[None, :]), -1)
    relrows = rel_bias.T[row_head]
    sinkrows = jnp.broadcast_to(sinks_even[0][row_head][:, None], (H_A * SAMPLE_PAD, 128))
    w_buf = cache_swa_k.shape[2]
    mixed_a_s, k_cache_s, v_cache_s = _swa_sample(
        proj_s, cache_swa_k[0].reshape(nseq, w_buf, W_K), cache_swa_v[0].reshape(nseq, w_buf, W_K),
        bkc, bkn, relrows, sinkrows, qn2, kn2, sb=16, ntok=ntok, unroll=4)
    mixed_b_s, gla_s = _gla_sample(proj_s, la_s, gla_g, state_gla[0], sb=8, ntok=ntok)
    y1_s, h1_s = _outproj([mixed_a_s, mixed_b_s], [w_out_a, w_out_b], xs, gate_es, (scale_os, shift_os, g_o),
                          tm=256, rows_per_batch=ms)

    w_o = w_in_odd[0].astype(BF16)
    ret_g = ret_g_odd[0].reshape(1, DV_C)
    w_out_o = w_out_odd[0].astype(BF16)
    cos_p, sin_p = _rotary_tables(jnp.arange(seq))
    pos_s = PAST_LEN + jnp.minimum(jnp.arange(SAMPLE_PAD), ntok - 1)
    cos_s, sin_s = _rotary_tables(jnp.tile(pos_s, nseq))

    (projo_s,) = _inproj(h1_s, w_o, (cos_s, sin_s), tm=512, tn=H_C * DK_C, out_dtype=F32, name="inproj_odd")
    ret_job = _ret_sample_job(projo_s, ret_g, state_ret[0], ntok=ntok)
    projo_p, o_s, ret_s = _inproj(h1_p, w_o, (cos_p, sin_p), ret_job, tm=256, tn=H_C * DK_C, out_dtype=BF16,
                                  name="inproj_odd_ret_sample")
    o_p, ret_p = _ret_prompt(projo_p, ret_g, batch=batch, seq=seq)
    (y2_p,) = _outproj([o_p], [w_out_o], y1_p, gate_op, tm=256, rows_per_batch=seq)
    (y2_s,) = _outproj([o_s], [w_out_o], y1_s, gate_os, tm=256, rows_per_batch=ms)

    y_prompt = y2_p.reshape(batch, seq, D_MODEL)
    y_sample = y2_s.reshape(nseq, SAMPLE_PAD, D_MODEL)[:, :ntok]
    return (y_prompt, y_sample, swa_k_p, swa_v_p, gla_p[None], ret_p[None],
            k_cache_s.reshape(1, nseq, w_buf, H_A_KV, HD_A), v_cache_s.reshape(1, nseq, w_buf, H_A_KV, HD_A),
            gla_s[None], ret_s[None])
```

```python
import functools
import math
from typing import Callable, NamedTuple

import numpy as np
import jax
import jax.numpy as jnp
from jax import lax
from jax.experimental import pallas as pl
from jax.experimental.pallas import tpu as pltpu

F32 = jnp.float32
BF16 = jnp.bfloat16

D_MODEL = 2048
WINDOW = 128
HD_A = 64
H_A = 16
H_A_KV = 2
G_A = 8
W_A = 1024
NUM_BUCKETS = 32
MAX_DISTANCE = 128
H_B = 4
DV_B = 256
DK_B = 128
W_B = 1024
GLA_RANK = 16
GLA_TAU = 16.0
GLA_CHUNK = 64
H_C = 8
DK_C = 256
DV_C = 512
W_C = 4096
RET_CHUNK = 128
ROPE_BASE = 10000.0
EPS = 1e-6
NEG_INF = -1e30
PAST_LEN = 8192

EV_QA, EV_GA, EV_VB, EV_GB, EV_QB, EV_KB, EV_KV = 0, 1024, 2048, 3072, 4096, 4608, 5120
EV_N = 5376
OD_N = 12288
W_K = H_A_KV * HD_A
SAMPLE_PAD = 8
MXU_N = 256
VMEM_LIMIT = 56 * 1024 * 1024

LOG_GAMMA = [float(np.log1p(-np.exp2(np.float32(-5.0 - h)))) for h in range(H_C)]


def _cparams(sem):
    return pltpu.CompilerParams(dimension_semantics=sem, vmem_limit_bytes=VMEM_LIMIT)


def _silu(x):
    return x / (1.0 + jnp.exp(-x))


def _dot(a, b):
    return jnp.dot(a, b, preferred_element_type=F32)


def _dot_nt(a, b):
    return lax.dot_general(a, b, (((1,), (1,)), ((), ())), preferred_element_type=F32)


def _dot_tn(a, b):
    return lax.dot_general(a, b, (((0,), (0,)), ((), ())), preferred_element_type=F32)


def _cumsum_rows(x):
    c = x.shape[0]
    r = lax.broadcasted_iota(jnp.int32, (c, c), 0)
    s = lax.broadcasted_iota(jnp.int32, (c, c), 1)
    tri = (r >= s).astype(BF16)
    hi = x.astype(BF16)
    r1 = x - hi.astype(F32)
    mid = r1.astype(BF16)
    lo = (r1 - mid.astype(F32)).astype(BF16)
    return _dot(tri, hi) + _dot(tri, mid) + _dot(tri, lo)


def _row_to_col(r):
    n = r.shape[1]
    ri = lax.broadcasted_iota(jnp.int32, (n, n), 0)
    ci = lax.broadcasted_iota(jnp.int32, (n, n), 1)
    return jnp.sum(jnp.where(ri == ci, jnp.broadcast_to(r, (n, n)), 0.0), axis=1, keepdims=True)


def _ada_kernel(c_ref, w_ref, b_ref, o_ref):
    sc = _silu(c_ref[...]).astype(BF16)
    o_ref[...] = _dot(sc, w_ref[...].astype(BF16)) + b_ref[...]


def _ada_mod(c_all, w, b):
    m = c_all.shape[0]
    n = w.shape[1]
    tn = 768
    return pl.pallas_call(
        _ada_kernel,
        grid=(n // tn,),
        in_specs=[pl.BlockSpec((m, D_MODEL), lambda j: (0, 0)),
                  pl.BlockSpec((D_MODEL, tn), lambda j: (0, j)),
                  pl.BlockSpec((1, tn), lambda j: (0, j))],
        out_specs=pl.BlockSpec((m, tn), lambda j: (0, j)),
        out_shape=jax.ShapeDtypeStruct((m, n), F32),
        compiler_params=_cparams(("arbitrary",)),
        name="ada_mod",
    )(c_all, w, b.reshape(1, n))


def _norm_mod(x, g, scale, shift):
    ms = jnp.mean(x * x, axis=-1, keepdims=True)
    y = x * lax.rsqrt(ms + EPS) * g
    return y * (1.0 + scale) + shift


def _log_sigmoid(z):
    return jnp.minimum(z, 0.0) - jnp.log(1.0 + jnp.exp(-jnp.abs(z)))


def _mod_spec(arr, tm, rows_per_batch):
    if arr.ndim == 3:
        return pl.BlockSpec((None, 1, D_MODEL), lambda i: (i // (rows_per_batch // tm), 0, 0))
    return pl.BlockSpec((tm, D_MODEL), lambda i: (i, 0))


def _prenorm_even_kernel(x_ref, scale_ref, shift_ref, g_ref, wlri_ref, wlr_ref, blr_ref, h_ref, la_ref):
    hb = _norm_mod(x_ref[...], g_ref[...], scale_ref[...], shift_ref[...]).astype(BF16)
    h_ref[...] = hb
    lr = _dot(hb, wlri_ref[...])
    z = _dot(lr.astype(BF16), wlr_ref[...]) + blr_ref[...]
    la_ref[...] = _log_sigmoid(z) / GLA_TAU


def _prenorm_even(x, scale, shift, g, wlri, wlr, blr, *, tm, rows_per_batch):
    m = x.shape[0]
    const = lambda shape: pl.BlockSpec(shape, lambda i: (0, 0))
    return pl.pallas_call(
        _prenorm_even_kernel,
        grid=(m // tm,),
        in_specs=[pl.BlockSpec((tm, D_MODEL), lambda i: (i, 0)),
                  _mod_spec(scale, tm, rows_per_batch), _mod_spec(shift, tm, rows_per_batch),
                  const((1, D_MODEL)), const((D_MODEL, 128)), const((128, H_B * DK_B)), const((1, H_B * DK_B))],
        out_specs=[pl.BlockSpec((tm, D_MODEL), lambda i: (i, 0)),
                   pl.BlockSpec((tm, H_B * DK_B), lambda i: (i, 0))],
        out_shape=[jax.ShapeDtypeStruct((m, D_MODEL), BF16),
                   jax.ShapeDtypeStruct((m, H_B * DK_B), F32)],
        compiler_params=_cparams(("arbitrary",)),
        name="prenorm_even",
    )(x, scale, shift, g, wlri, wlr, blr)


class _SideJob(NamedTuple):
    njobs: int
    args: list
    in_specs: Callable
    out_specs: Callable
    out_shape: list
    body: Callable


def _inproj_kernel(*refs, rot_tiles, side, nsteps):
    nmain = 4 if rot_tiles else 2
    if side is not None:
        njobs, nsi, nso, body = side
        side_in = refs[nmain:nmain + nsi]
        side_out = refs[nmain + nsi + 1:nmain + nsi + 1 + nso]
        step = pl.program_id(0) * pl.num_programs(1) + pl.program_id(1)
        block = (step * njobs) // nsteps
        prev_block = ((step - 1) * njobs) // nsteps

        @pl.when((step == 0) | (block != prev_block))
        def _():
            body(side_in, side_out)

        refs = refs[:nmain] + (refs[nmain + nsi],)
    h_ref, w_ref, *rest = refs
    o_ref = rest[-1]
    tn = o_ref.shape[1]

    def plain():
        for c in range(tn // MXU_N):
            cs = slice(c * MXU_N, (c + 1) * MXU_N)
            o_ref[:, cs] = _dot(h_ref[...], w_ref[:, cs]).astype(o_ref.dtype)

    if rot_tiles == 0:
        plain()
        return
    cos_ref, sin_ref = rest[:2]
    j = pl.program_id(0)
    half = DK_C // 2

    @pl.when(j < rot_tiles)
    def _():
        mult = jnp.where(j < rot_tiles // 2, 1.0, DK_C ** -0.5)
        cos = cos_ref[...] * mult
        sin = sin_ref[...] * mult
        for c in range(tn // DK_C):
            acc = _dot(h_ref[...], w_ref[:, c * DK_C:(c + 1) * DK_C])
            x1 = acc[:, :half]
            x2 = acc[:, half:]
            o_ref[:, c * DK_C:c * DK_C + half] = (x1 * cos - x2 * sin).astype(o_ref.dtype)
            o_ref[:, c * DK_C + half:(c + 1) * DK_C] = (x2 * cos + x1 * sin).astype(o_ref.dtype)

    pl.when(j >= rot_tiles)(plain)


def _inproj(h, w, rot=None, side=None, *, tm, tn, out_dtype, name):
    m = h.shape[0]
    n = w.shape[1]
    ni = m // tm
    nsteps = (n // tn) * ni
    in_specs = [pl.BlockSpec((tm, D_MODEL), lambda j, i: (i, 0)),
                pl.BlockSpec((D_MODEL, tn), lambda j, i: (0, j))]
    args = [h, w]
    rot_tiles = 0
    if rot is not None:
        nblk = rot[0].shape[0] // tm
        in_specs += [pl.BlockSpec((tm, DK_C // 2), lambda j, i: (i % nblk, 0))] * 2
        args += list(rot)
        rot_tiles = 2 * (H_C * DK_C) // tn
    out_specs = [pl.BlockSpec((tm, tn), lambda j, i: (i, j))]
    out_shape = [jax.ShapeDtypeStruct((m, n), out_dtype)]
    side_static = None
    if side is not None:
        assert side.njobs <= nsteps
        block = lambda j, i: ((j * ni + i) * side.njobs) // nsteps
        in_specs += side.in_specs(block)
        args += side.args
        out_specs += side.out_specs(block)
        out_shape += side.out_shape
        side_static = (side.njobs, len(side.args), len(side.out_shape), side.body)
    return pl.pallas_call(
        functools.partial(_inproj_kernel, rot_tiles=rot_tiles, side=side_static, nsteps=nsteps),
        grid=(n // tn, ni),
        in_specs=in_specs,
        out_specs=out_specs,
        out_shape=out_shape,
        compiler_params=_cparams(("arbitrary", "arbitrary")),
        name=name,
    )(*args)


def _outproj_kernel(*refs, nin, with_next):
    m_refs, w_refs = refs[:nin], refs[nin:2 * nin]
    if with_next:
        x_ref, gate_ref, nscale_ref, nshift_ref, ng_ref, o_ref, h_ref = refs[2 * nin:]
    else:
        x_ref, gate_ref, o_ref = refs[2 * nin:]
    for c in range(o_ref.shape[1] // MXU_N):
        cs = slice(c * MXU_N, (c + 1) * MXU_N)
        acc = _dot(m_refs[0][...].astype(BF16), w_refs[0][:, cs])
        for m_ref, w_ref in zip(m_refs[1:], w_refs[1:]):
            acc += _dot(m_ref[...].astype(BF16), w_ref[:, cs])
        o_ref[:, cs] = x_ref[:, cs] + gate_ref[:, cs] * acc
    if with_next:
        h_ref[...] = _norm_mod(o_ref[...], ng_ref[...], nscale_ref[...], nshift_ref[...]).astype(BF16)


def _outproj(mixed, ws, x, gate, next_mod=None, *, tm, rows_per_batch):
    m = x.shape[0]
    row = pl.BlockSpec((tm, D_MODEL), lambda i: (i, 0))
    in_specs = ([pl.BlockSpec((tm, a.shape[1]), lambda i: (i, 0)) for a in mixed]
                + [pl.BlockSpec(w.shape, lambda i: (0, 0), pipeline_mode=pl.Buffered(1)) for w in ws]
                + [row, _mod_spec(gate, tm, rows_per_batch)])
    args = [*mixed, *ws, x, gate]
    out_specs = [row]
    out_shape = [jax.ShapeDtypeStruct((m, D_MODEL), F32)]
    if next_mod is not None:
        nscale, nshift, ng = next_mod
        in_specs += [_mod_spec(nscale, tm, rows_per_batch), _mod_spec(nshift, tm, rows_per_batch),
                     pl.BlockSpec((1, D_MODEL), lambda i: (0, 0))]
        args += [nscale, nshift, ng]
        out_specs.append(row)
        out_shape.append(jax.ShapeDtypeStruct((m, D_MODEL), BF16))
    return pl.pallas_call(
        functools.partial(_outproj_kernel, nin=len(mixed), with_next=next_mod is not None),
        grid=(m // tm,),
        in_specs=in_specs,
        out_specs=out_specs,
        out_shape=out_shape,
        compiler_params=_cparams(("arbitrary",)),
        name="outproj",
    )(*args)


def _seg_rms(x, g2, seg_ones):
    x2 = x * x
    hi = x2.astype(BF16)
    lo = (x2 - hi.astype(F32)).astype(BF16)
    ss = _dot(hi, seg_ones) + _dot(lo, seg_ones)
    return x * lax.rsqrt(ss * (1.0 / HD_A) + EPS) * g2


def _dup_halves(x, lo_half):
    sw = pltpu.roll(x, HD_A, 1)
    return [jnp.where(lo_half, x, sw), jnp.where(lo_half, sw, x)]


def _seg_ones():
    r = lax.broadcasted_iota(jnp.int32, (2 * HD_A, 2 * HD_A), 0)
    c = lax.broadcasted_iota(jnp.int32, (2 * HD_A, 2 * HD_A), 1)
    return ((r < HD_A) == (c < HD_A)).astype(BF16)


def _swa_prompt_kernel(relb_ref, sinks_ref, bucket_ref, q_ref, ga_ref, kv_ref, qn_ref, kn_ref,
                       o_ref, knew_ref, bias_ref, kprev_ref, vprev_ref, s_ref, pe_ref):
    b = pl.program_id(0)
    i = pl.program_id(1)

    @pl.when((b == 0) & (i == 0))
    def _():
        bk = bucket_ref[...]
        for h in range(H_A):
            bias_ref[h] = jnp.full((WINDOW, 2 * WINDOW), NEG_INF, F32)
        for bb in range(NUM_BUCKETS):
            hit = bk == bb
            for h in range(H_A):
                bias_ref[h] = jnp.where(hit, relb_ref[bb, h], bias_ref[h])
        col = lax.broadcasted_iota(jnp.int32, (WINDOW, 2 * WINDOW), 1)
        for h in range(H_A):
            bias_ref[H_A + h] = jnp.where(col >= WINDOW, bias_ref[h], NEG_INF)

    @pl.when(i == 0)
    def _():
        kprev_ref[...] = jnp.zeros(kprev_ref.shape, BF16)
        vprev_ref[...] = jnp.zeros(vprev_ref.shape, BF16)

    seg_ones = _seg_ones()
    lt = 2 * HD_A
    tpg = G_A // 2
    lo_half = lax.broadcasted_iota(jnp.int32, (WINDOW, lt), 1) < HD_A
    kv = kv_ref[...].astype(F32)
    kc = _seg_rms(kv[:, :W_K], kn_ref[...], seg_ones)
    k_dup = _dup_halves(kc, lo_half)
    v_dup = _dup_halves(kv[:, W_K:], lo_half)
    base = jnp.where(i == 0, H_A, 0)
    qn2 = qn_ref[...] * HD_A ** -0.5
    gq = tpg * WINDOW
    lo_g = lax.broadcasted_iota(jnp.int32, (gq, lt), 1) < HD_A
    grows = [slice(g * 2 * gq, (g + 1) * 2 * gq) for g in range(H_A_KV)]
    for g in range(H_A_KV):
        kk = jnp.concatenate([kprev_ref[g], k_dup[g].astype(BF16)], axis=0)
        xg = jnp.concatenate([q_ref[:, (g * tpg + p) * lt:(g * tpg + p + 1) * lt] for p in range(tpg)],
                             axis=0).astype(F32)
        xn = _seg_rms(xg, qn2, seg_ones)
        q_all = jnp.concatenate([jnp.where(lo_g, xn, 0.0), jnp.where(lo_g, 0.0, xn)], axis=0).astype(BF16)
        s_ref[grows[g], :] = _dot_nt(q_all, kk)
    sink_terms = []
    for g in range(H_A_KV):
        for n in range(G_A):
            h = g * G_A + 2 * (n % tpg) + n // tpg
            rows = slice((g * G_A + n) * WINDOW, (g * G_A + n + 1) * WINDOW)
            s = s_ref[rows, :] + bias_ref[base + h]
            sink = sinks_ref[h]
            m = jnp.maximum(jnp.max(s, axis=-1, keepdims=True), sink)
            pe_ref[rows, :] = jnp.exp(s - m).astype(BF16)
            sink_terms.append(jnp.exp(sink - m))
    o_ext = []
    for g in range(H_A_KV):
        vv = jnp.concatenate([vprev_ref[g], v_dup[g].astype(BF16)], axis=0)
        vv_ext = jnp.concatenate([vv, jnp.ones(vv.shape, BF16)], axis=1)
        o_ext.append(_dot(pe_ref[grows[g], :], vv_ext))
    for g in range(H_A_KV):
        for p in range(tpg):
            halves = []
            for a in range(2):
                n = a * tpg + p
                rows = slice(n * WINDOW, (n + 1) * WINDOW)
                halves.append(o_ext[g][rows, :lt] / (o_ext[g][rows, lt:] + sink_terms[g * G_A + n]))
            oa = jnp.where(lo_half, halves[0], halves[1])
            cols = slice((g * tpg + p) * lt, (g * tpg + p + 1) * lt)
            o_ref[:, cols] = (oa * _silu(ga_ref[:, cols].astype(F32))).astype(o_ref.dtype)
    for g in range(H_A_KV):
        kprev_ref[g] = k_dup[g].astype(BF16)
        vprev_ref[g] = v_dup[g].astype(BF16)

    @pl.when(i == pl.num_programs(1) - 1)
    def _():
        knew_ref[...] = kc


def _swa_prompt(proj, rel_bias, sinks, qn2, kn2, bucket, *, batch, seq):
    nb = seq // WINDOW
    rb = lambda b, i: b * nb + i
    smem = pl.BlockSpec(memory_space=pltpu.SMEM)
    return pl.pallas_call(
        _swa_prompt_kernel,
        grid=(batch, nb),
        in_specs=[smem, smem,
                  pl.BlockSpec((WINDOW, 2 * WINDOW), lambda b, i: (0, 0)),
                  pl.BlockSpec((WINDOW, W_A), lambda b, i: (rb(b, i), EV_QA // W_A)),
                  pl.BlockSpec((WINDOW, W_A), lambda b, i: (rb(b, i), EV_GA // W_A)),
                  pl.BlockSpec((WINDOW, 2 * W_K), lambda b, i: (rb(b, i), EV_KV // (2 * W_K))),
                  pl.BlockSpec((1, 2 * HD_A), lambda b, i: (0, 0)),
                  pl.BlockSpec((1, 2 * HD_A), lambda b, i: (0, 0))],
        out_specs=[pl.BlockSpec((WINDOW, W_A), lambda b, i: (rb(b, i), 0)),
                   pl.BlockSpec((None, WINDOW, W_K), lambda b, i: (b, 0, 0))],
        out_shape=[jax.ShapeDtypeStruct((batch * seq, W_A), BF16),
                   jax.ShapeDtypeStruct((batch, WINDOW, W_K), F32)],
        scratch_shapes=[pltpu.VMEM((2 * H_A, WINDOW, 2 * WINDOW), F32),
                        pltpu.VMEM((H_A_KV, WINDOW, W_K), BF16), pltpu.VMEM((H_A_KV, WINDOW, W_K), BF16),
                        pltpu.VMEM((H_A * WINDOW, 2 * WINDOW), F32), pltpu.VMEM((H_A * WINDOW, 2 * WINDOW), BF16)],
        compiler_params=_cparams(("arbitrary", "arbitrary")),
        name="swa_prompt",
    )(rel_bias, sinks, bucket, proj, proj, proj, qn2, kn2)


def _swa_sample_kernel(bkc_ref, bkn_ref, relrows_ref, sinkrows_ref, q_ref, ga_ref, kvn_ref, ck_ref, cv_ref,
                       qn_ref, kn_ref, o_ref, ko_ref, vo_ref, biasc_ref, biasn_ref, *, sb, ntok, unroll):
    @pl.when(pl.program_id(0) == 0)
    def _():
        bkc = bkc_ref[...]
        bkn = bkn_ref[...]
        rr = relrows_ref[...]
        bc = jnp.full(bkc.shape, NEG_INF, F32)
        bn = jnp.full(bkn.shape, NEG_INF, F32)
        for bb in range(NUM_BUCKETS):
            val = rr[:, bb:bb + 1]
            bc = jnp.where(bkc == bb, val, bc)
            bn = jnp.where(bkn == bb, val, bn)
        biasc_ref[...] = bc
        biasn_ref[...] = bn

    seg_ones = _seg_ones()
    lt = 2 * HD_A
    nrow = G_A * SAMPLE_PAD
    grow = nrow // H_A_KV
    lo_q = lax.broadcasted_iota(jnp.int32, (grow, lt), 1) < HD_A
    lo_k = lax.broadcasted_iota(jnp.int32, (WINDOW, lt), 1) < HD_A
    lo_n = lax.broadcasted_iota(jnp.int32, (SAMPLE_PAD, lt), 1) < HD_A
    qn2 = qn_ref[...] * HD_A ** -0.5
    kn2 = kn_ref[...]
    sink = sinkrows_ref[:, 0:1]
    nkeep = WINDOW - ntok

    def body(it, carry):
        seqs = [it * unroll + u for u in range(unroll)]
        rows = [pl.ds(pl.multiple_of(s * SAMPLE_PAD, SAMPLE_PAD), SAMPLE_PAD) for s in seqs]
        kvn = [kvn_ref[r, :] for r in rows]
        kn_all = _seg_rms(jnp.concatenate([x[:, :W_K] for x in kvn], axis=0), kn2, seg_ones)
        q2_all = jnp.concatenate([q_ref[r, p * lt:(p + 1) * lt] for r in rows for p in range(G_A)], axis=0)
        xn_all = _seg_rms(q2_all, qn2, seg_ones)
        sc, sn, kn = [], [], []
        for u, s in enumerate(seqs):
            kn.append(kn_all[u * SAMPLE_PAD:(u + 1) * SAMPLE_PAD])
            ck_dup = _dup_halves(ck_ref[s], lo_k)
            kn_dup = _dup_halves(kn[u], lo_n)
            scg, sng = [], []
            for g in range(H_A_KV):
                xn = xn_all[u * nrow + g * grow:u * nrow + (g + 1) * grow]
                q4 = jnp.concatenate([jnp.where(lo_q, xn, 0.0), jnp.where(lo_q, 0.0, xn)], axis=0).astype(BF16)
                scg.append(_dot_nt(q4, ck_dup[g].astype(BF16)))
                sng.append(_dot_nt(q4, kn_dup[g].astype(BF16)))
            sc.append(jnp.concatenate(scg, axis=0))
            sn.append(jnp.concatenate(sng, axis=0))
        pc, pn, sink_terms = [], [], []
        for u in range(unroll):
            scu = sc[u] + biasc_ref[...]
            snu = sn[u] + biasn_ref[:, :SAMPLE_PAD]
            m = jnp.maximum(jnp.maximum(jnp.max(scu, axis=-1, keepdims=True),
                                        jnp.max(snu, axis=-1, keepdims=True)), sink)
            pc.append(jnp.exp(scu - m).astype(BF16))
            pn.append(jnp.exp(snu - m).astype(BF16))
            sink_terms.append(jnp.exp(sink - m))
        o_ext = []
        for u, s in enumerate(seqs):
            cv_dup = _dup_halves(cv_ref[s], lo_k)
            vn_dup = _dup_halves(kvn[u][:, W_K:], lo_n)
            og = []
            for g in range(H_A_KV):
                cv_ext = jnp.concatenate([cv_dup[g].astype(BF16), jnp.ones((WINDOW, lt), BF16)], axis=1)
                vn_ext = jnp.concatenate([vn_dup[g].astype(BF16), jnp.ones((SAMPLE_PAD, lt), BF16)], axis=1)
                gr = slice(g * 2 * grow, (g + 1) * 2 * grow)
                og.append(_dot(pc[u][gr], cv_ext) + _dot(pn[u][gr], vn_ext))
            o_ext.append(jnp.concatenate(og, axis=0))
        for u, s in enumerate(seqs):
            o4 = o_ext[u][:, :lt] / (o_ext[u][:, lt:] + sink_terms[u])
            xg = ga_ref[rows[u], :]
            for p in range(G_A):
                g, pp = divmod(p, G_A // 2)
                r0 = g * 2 * grow + pp * SAMPLE_PAD
                o2 = jnp.where(lo_n, o4[r0:r0 + SAMPLE_PAD], o4[r0 + grow:r0 + grow + SAMPLE_PAD])
                o_ref[rows[u], p * lt:(p + 1) * lt] = (o2 * _silu(xg[:, p * lt:(p + 1) * lt])).astype(o_ref.dtype)
            ko_ref[s, 0:nkeep, :] = ck_ref[s, ntok:WINDOW, :]
            ko_ref[s, nkeep:WINDOW, :] = kn[u][0:ntok]
            vo_ref[s, 0:nkeep, :] = cv_ref[s, ntok:WINDOW, :]
            vo_ref[s, nkeep:WINDOW, :] = kvn[u][0:ntok, W_K:]
        return carry

    lax.fori_loop(0, sb // unroll, body, 0)


def _swa_sample(proj, cache_k, cache_v, bkc, bkn, relrows, sinkrows, qn2, kn2, *, sb, ntok, unroll):
    nseq = cache_k.shape[0]
    rows = sb * SAMPLE_PAD
    full = lambda shape: pl.BlockSpec(shape, lambda i: tuple(0 for _ in shape))
    cache = pl.BlockSpec((sb, WINDOW, W_K), lambda i: (i, 0, 0))
    return pl.pallas_call(
        functools.partial(_swa_sample_kernel, sb=sb, ntok=ntok, unroll=unroll),
        grid=(nseq // sb,),
        in_specs=[full(bkc.shape), full(bkn.shape), full(relrows.shape), full(sinkrows.shape),
                  pl.BlockSpec((rows, W_A), lambda i: (i, EV_QA // W_A)),
                  pl.BlockSpec((rows, W_A), lambda i: (i, EV_GA // W_A)),
                  pl.BlockSpec((rows, 2 * W_K), lambda i: (i, EV_KV // (2 * W_K))),
                  cache, cache, full((1, 2 * HD_A)), full((1, 2 * HD_A))],
        out_specs=[pl.BlockSpec((rows, W_A), lambda i: (i, 0)), cache, cache],
        out_shape=[jax.ShapeDtypeStruct((nseq * SAMPLE_PAD, W_A), F32),
                   jax.ShapeDtypeStruct(cache_k.shape, F32),
                   jax.ShapeDtypeStruct(cache_v.shape, F32)],
        scratch_shapes=[pltpu.VMEM(bkc.shape, F32), pltpu.VMEM(bkn.shape, F32)],
        compiler_params=_cparams(("arbitrary",)),
        name="swa_sample",
    )(bkc, bkn, relrows, sinkrows, proj, proj, proj, cache_k, cache_v, qn2, kn2)


def _gla_pre(q, k, v, la, n_valid):
    c = q.shape[0]
    bcum = _cumsum_rows(la)
    rr = lax.broadcasted_iota(jnp.int32, (c, c), 0)
    cc = lax.broadcasted_iota(jnp.int32, (c, c), 1)
    causal = rr >= cc
    row = lax.broadcasted_iota(jnp.int32, (c, 1), 0)
    qts, kts, kds, vbs, dcols = [], [], [], [], []
    for h in range(H_B):
        ks = slice(h * DK_B, (h + 1) * DK_B)
        bc = bcum[:, ks]
        qts.append(((q[:, ks] * DK_B ** -0.5) * jnp.exp(bc)).astype(BF16))
        kts.append((k[:, ks] * jnp.exp(-bc)).astype(BF16))
        blast = bc[n_valid - 1:n_valid, :]
        kd = k[:, ks] * jnp.exp(blast - bc)
        if n_valid < c:
            kd = jnp.where(row < n_valid, kd, 0.0)
        kds.append(kd.astype(BF16))
        vbs.append(v[:, h * DV_B:(h + 1) * DV_B].astype(BF16))
        dcols.append(_row_to_col(jnp.exp(blast)))
    scores = [_dot_nt(qts[h], kts[h]) for h in range(H_B)]
    upds = [_dot_tn(kds[h], vbs[h]) for h in range(H_B)]
    return [(qts[h], jnp.where(causal, scores[h], 0.0).astype(BF16), vbs[h], upds[h], dcols[h])
            for h in range(H_B)]


def _gla_post(pre, gb, gla_g, states):
    outs = [_dot(a, vb) + _dot(qt, st.astype(BF16)) for (qt, a, vb, _, _), st in zip(pre, states)]
    new_states = [dcol * st + upd for (_, _, _, upd, dcol), st in zip(pre, states)]
    gated = []
    for h, o in enumerate(outs):
        on = o * lax.rsqrt(jnp.mean(o * o, axis=-1, keepdims=True) + EPS) * gla_g
        gated.append(on * _silu(gb[:, h * DV_B:(h + 1) * DV_B]))
    return jnp.concatenate(gated, axis=1), new_states


def _gla_prompt_kernel(q_ref, k_ref, v_ref, gb_ref, la_ref, g_ref, o_ref, s_ref, *, nchunk):
    @pl.when(pl.program_id(1) == 0)
    def _():
        s_ref[...] = jnp.zeros(s_ref.shape, F32)

    gla_g = g_ref[...]
    chunk_rows = [slice(c * GLA_CHUNK, (c + 1) * GLA_CHUNK) for c in range(nchunk)]
    pres = [_gla_pre(q_ref[r, :].astype(F32), k_ref[r, :].astype(F32), v_ref[r, :].astype(F32), la_ref[r, :],
                     GLA_CHUNK) for r in chunk_rows]
    states = [s_ref[h] for h in range(H_B)]
    for r, pre in zip(chunk_rows, pres):
        out, states = _gla_post(pre, gb_ref[r, :].astype(F32), gla_g, states)
        o_ref[r, :] = out.astype(o_ref.dtype)
    for h in range(H_B):
        s_ref[h] = states[h]


def _gla_prompt(proj, la, gla_g, *, batch, seq, rows):
    nstep = seq // rows
    rb = lambda b, i: b * nstep + i
    return pl.pallas_call(
        functools.partial(_gla_prompt_kernel, nchunk=rows // GLA_CHUNK),
        grid=(batch, nstep),
        in_specs=[pl.BlockSpec((rows, H_B * DK_B), lambda b, i: (rb(b, i), EV_QB // (H_B * DK_B))),
                  pl.BlockSpec((rows, H_B * DK_B), lambda b, i: (rb(b, i), EV_KB // (H_B * DK_B))),
                  pl.BlockSpec((rows, W_B), lambda b, i: (rb(b, i), EV_VB // W_B)),
                  pl.BlockSpec((rows, W_B), lambda b, i: (rb(b, i), EV_GB // W_B)),
                  pl.BlockSpec((rows, H_B * DK_B), lambda b, i: (rb(b, i), 0)),
                  pl.BlockSpec((1, DV_B), lambda b, i: (0, 0))],
        out_specs=[pl.BlockSpec((rows, W_B), lambda b, i: (rb(b, i), 0)),
                   pl.BlockSpec((None, H_B, DK_B, DV_B), lambda b, i: (b, 0, 0, 0))],
        out_shape=[jax.ShapeDtypeStruct((batch * seq, W_B), BF16),
                   jax.ShapeDtypeStruct((batch, H_B, DK_B, DV_B), F32)],
        compiler_params=_cparams(("arbitrary", "arbitrary")),
        name="gla_prompt",
    )(proj, proj, proj, proj, la, gla_g)


def _gla_sample_kernel(q_ref, k_ref, v_ref, gb_ref, la_ref, g_ref, s_in_ref, o_ref, s_ref, *, sb, ntok, unroll):
    gla_g = g_ref[...]

    def body(it, carry):
        seqs = [it * unroll + u for u in range(unroll)]
        rows = [pl.ds(pl.multiple_of(s * SAMPLE_PAD, SAMPLE_PAD), SAMPLE_PAD) for s in seqs]
        pres = [_gla_pre(q_ref[r, :], k_ref[r, :], v_ref[r, :], la_ref[r, :], ntok) for r in rows]
        for s, r, pre in zip(seqs, rows, pres):
            out, new_states = _gla_post(pre, gb_ref[r, :], gla_g, [s_in_ref[s, h] for h in range(H_B)])
            o_ref[r, :] = out.astype(o_ref.dtype)
            for h in range(H_B):
                s_ref[s, h] = new_states[h]
        return carry

    lax.fori_loop(0, sb // unroll, body, 0)


def _gla_sample(proj, la, gla_g, state, *, sb, ntok, unroll):
    nseq = state.shape[0]
    rows = sb * SAMPLE_PAD
    return pl.pallas_call(
        functools.partial(_gla_sample_kernel, sb=sb, ntok=ntok, unroll=unroll),
        grid=(nseq // sb,),
        in_specs=[pl.BlockSpec((rows, H_B * DK_B), lambda i: (i, EV_QB // (H_B * DK_B))),
                  pl.BlockSpec((rows, H_B * DK_B), lambda i: (i, EV_KB // (H_B * DK_B))),
                  pl.BlockSpec((rows, W_B), lambda i: (i, EV_VB // W_B)),
                  pl.BlockSpec((rows, W_B), lambda i: (i, EV_GB // W_B)),
                  pl.BlockSpec((rows, H_B * DK_B), lambda i: (i, 0)),
                  pl.BlockSpec((1, DV_B), lambda i: (0, 0)),
                  pl.BlockSpec((sb, H_B, DK_B, DV_B), lambda i: (i, 0, 0, 0))],
        out_specs=[pl.BlockSpec((rows, W_B), lambda i: (i, 0)),
                   pl.BlockSpec((sb, H_B, DK_B, DV_B), lambda i: (i, 0, 0, 0))],
        out_shape=[jax.ShapeDtypeStruct((nseq * SAMPLE_PAD, W_B), BF16),
                   jax.ShapeDtypeStruct(state.shape, F32)],
        compiler_params=_cparams(("arbitrary",)),
        name="gla_sample",
    )(proj, proj, proj, proj, la, gla_g, state)


def _ret_chunk(q_ref, k_ref, v_ref, g_ref, retg, state_in_ref, state_ref, o_ref, n_valid):
    c = q_ref.shape[0]
    ri = lax.broadcasted_iota(jnp.int32, (c, c), 0)
    ci = lax.broadcasted_iota(jnp.int32, (c, c), 1)
    dist = (ri - ci).astype(F32)
    row = lax.broadcasted_iota(jnp.int32, (c, 1), 0)
    rowf = row.astype(F32)
    a_heads = []
    for h in range(H_C):
        lg = LOG_GAMMA[h]
        ks = slice(h * DK_C, (h + 1) * DK_C)
        decay = jnp.where(ri >= ci, jnp.exp(jnp.maximum(dist, 0.0) * lg), 0.0)
        a_heads.append((_dot_nt(q_ref[:, ks].astype(BF16), k_ref[:, ks].astype(BF16)) * decay).astype(BF16))
    for h in range(H_C):
        lg = LOG_GAMMA[h]
        ks = slice(h * DK_C, (h + 1) * DK_C)
        vs = slice(h * DV_C, (h + 1) * DV_C)
        qb = q_ref[:, ks].astype(BF16)
        kf = k_ref[:, ks].astype(F32)
        vb = v_ref[:, vs].astype(BF16)
        st = state_in_ref[h]
        inner = jnp.exp((rowf + 1.0) * lg)
        o = _dot(a_heads[h], vb) + inner * _dot(qb, st.astype(BF16))
        kd = kf * jnp.exp((n_valid - 1.0 - rowf) * lg)
        if n_valid < c:
            kd = jnp.where(row < n_valid, kd, 0.0)
        state_ref[h] = math.exp(n_valid * lg) * st + _dot_tn(kd.astype(BF16), vb)
        on = o * lax.rsqrt(jnp.mean(o * o, axis=-1, keepdims=True) + EPS) * retg
        o_ref[:, vs] = (on * _silu(g_ref[:, vs].astype(F32))).astype(o_ref.dtype)


def _ret_prompt_kernel(q_ref, k_ref, v_ref, g_ref, retg_ref, o_ref, s_ref):
    @pl.when(pl.program_id(1) == 0)
    def _():
        s_ref[...] = jnp.zeros(s_ref.shape, F32)

    _ret_chunk(q_ref, k_ref, v_ref, g_ref, retg_ref[...], s_ref, s_ref, o_ref, RET_CHUNK)


def _ret_prompt(proj, ret_g, *, batch, seq):
    nstep = seq // RET_CHUNK
    rb = lambda b, i: b * nstep + i
    qk = H_C * DK_C
    return pl.pallas_call(
        _ret_prompt_kernel,
        grid=(batch, nstep),
        in_specs=[pl.BlockSpec((RET_CHUNK, qk), lambda b, i: (rb(b, i), 0)),
                  pl.BlockSpec((RET_CHUNK, qk), lambda b, i: (rb(b, i), 1)),
                  pl.BlockSpec((RET_CHUNK, W_C), lambda b, i: (rb(b, i), 1)),
                  pl.BlockSpec((RET_CHUNK, W_C), lambda b, i: (rb(b, i), 2)),
                  pl.BlockSpec((1, DV_C), lambda b, i: (0, 0))],
        out_specs=[pl.BlockSpec((RET_CHUNK, W_C), lambda b, i: (rb(b, i), 0)),
                   pl.BlockSpec((None, H_C, DK_C, DV_C), lambda b, i: (b, 0, 0, 0))],
        out_shape=[jax.ShapeDtypeStruct((batch * seq, W_C), BF16),
                   jax.ShapeDtypeStruct((batch, H_C, DK_C, DV_C), F32)],
        compiler_params=_cparams(("arbitrary", "arbitrary")),
        name="ret_prompt",
    )(proj, proj, proj, proj, ret_g)


def _ret_sample_job(proj, ret_g, state, *, ntok):
    nseq = state.shape[0]
    qk = H_C * DK_C
    state_spec = lambda blk: pl.BlockSpec((None, H_C, DK_C, DV_C), lambda j, i: (blk(j, i), 0, 0, 0))

    def body(in_refs, out_refs):
        q_ref, k_ref, v_ref, g_ref, retg_ref, s_in_ref = in_refs
        o_ref, s_ref = out_refs
        _ret_chunk(q_ref, k_ref, v_ref, g_ref, retg_ref[...], s_in_ref, s_ref, o_ref, ntok)

    return _SideJob(
        njobs=nseq,
        args=[proj, proj, proj, proj, ret_g, state],
        in_specs=lambda blk: [pl.BlockSpec((SAMPLE_PAD, qk), lambda j, i: (blk(j, i), 0)),
                              pl.BlockSpec((SAMPLE_PAD, qk), lambda j, i: (blk(j, i), 1)),
                              pl.BlockSpec((SAMPLE_PAD, W_C), lambda j, i: (blk(j, i), 1)),
                              pl.BlockSpec((SAMPLE_PAD, W_C), lambda j, i: (blk(j, i), 2)),
                              pl.BlockSpec((1, DV_C), lambda j, i: (0, 0)),
                              state_spec(blk)],
        out_specs=lambda blk: [pl.BlockSpec((SAMPLE_PAD, W_C), lambda j, i: (blk(j, i), 0)), state_spec(blk)],
        out_shape=[jax.ShapeDtypeStruct((nseq * SAMPLE_PAD, W_C), BF16),
                   jax.ShapeDtypeStruct(state.shape, F32)],
        body=body)


def _t5_bucket(dist):
    dist = jnp.maximum(dist, 0)
    max_exact = NUM_BUCKETS // 2
    log_ratio = jnp.log(jnp.maximum(dist, 1).astype(F32) / max_exact) / math.log(MAX_DISTANCE / max_exact)
    large = jnp.minimum(max_exact + (log_ratio * (NUM_BUCKETS - max_exact)).astype(jnp.int32), NUM_BUCKETS - 1)
    return jnp.where(dist < max_exact, dist, large)


def _bucket_or_masked(dist):
    return jnp.where((dist >= 0) & (dist <= WINDOW), _t5_bucket(dist), -1).astype(jnp.int32)


def _rotary_tables(pos):
    half = DK_C // 2
    inv = ROPE_BASE ** (-jnp.arange(half, dtype=F32) / half)
    ang = pos.astype(F32)[:, None] * inv[None, :]
    return jnp.cos(ang), jnp.sin(ang)


def _regroup_kernel(w_ref, o_ref):
    o_ref[...] = w_ref[...].astype(o_ref.dtype)


def _even_weight_layout(w_in):
    src = lambda j: jnp.where(j < 4, j, jnp.where(j < 8, j + 1, jnp.where(j < 16, j + 5,
                                                                         jnp.where(j < 20, j - 7, 4))))
    main = pl.pallas_call(
        _regroup_kernel,
        grid=(EV_N // MXU_N,),
        in_specs=[pl.BlockSpec((D_MODEL, MXU_N), lambda j: (0, src(j)))],
        out_specs=pl.BlockSpec((D_MODEL, MXU_N), lambda j: (0, j)),
        out_shape=jax.ShapeDtypeStruct((D_MODEL, EV_N), BF16),
        compiler_params=_cparams(("arbitrary",)),
        name="regroup_even_weights",
    )(w_in)
    lr_pad = jnp.pad(w_in[:, EV_N:], ((0, 0), (0, 128 - GLA_RANK))).astype(BF16)
    return main, lr_pad


def _pad_tokens(a, ntok):
    pad = [(0, 0), (0, SAMPLE_PAD - ntok)] + [(0, 0)] * (a.ndim - 2)
    a = jnp.pad(a, pad)
    return a.reshape((a.shape[0] * SAMPLE_PAD,) + a.shape[2:])


def _rows(v, reps):
    return jnp.repeat(v, reps, axis=0)


def kernel(x_prompt, x_sample, cache_swa_k, cache_swa_v, state_gla, state_ret, c_prompt, c_sample, rel_bias,
           ada_w_even, ada_b_even, norm_g_even, w_in_even, w_lr_even, b_lr_even, qn_g_even, kn_g_even,
           sinks_even, gla_g_even, w_out_even, ada_w_odd, ada_b_odd, norm_g_odd, w_in_odd, ret_g_odd, w_out_odd):
    batch, seq, _ = x_prompt.shape
    nseq, ntok, _ = x_sample.shape
    mp = batch * seq

    c_all = jnp.concatenate([c_prompt, c_sample], axis=0)
    mod_e = _ada_mod(c_all, ada_w_even[0], ada_b_even[0])
    mod_o = _ada_mod(c_all, ada_w_odd[0], ada_b_odd[0])

    def split_mod(mod):
        shift, scale, gate = jnp.split(mod, 3, axis=1)
        p = tuple(a[:batch].reshape(batch, 1, D_MODEL) for a in (shift, scale, gate))
        s = tuple(_rows(a[batch:], SAMPLE_PAD) for a in (shift, scale, gate))
        return p, s

    (shift_ep, scale_ep, gate_ep), (shift_es, scale_es, gate_es) = split_mod(mod_e)
    (shift_op, scale_op, gate_op), (shift_os, scale_os, gate_os) = split_mod(mod_o)

    xp = x_prompt.reshape(mp, D_MODEL)
    xs = _pad_tokens(x_sample, ntok)
    ms = xs.shape[0]

    w_e, w_lri = _even_weight_layout(w_in_even[0])
    w_lr = jnp.pad(w_lr_even[0], ((0, 128 - GLA_RANK), (0, 0))).astype(BF16)
    b_lr = b_lr_even[0].reshape(1, -1)
    g_e = norm_g_even[0].reshape(1, D_MODEL)
    qn2 = jnp.tile(qn_g_even[0].reshape(1, HD_A), (1, 2))
    kn2 = jnp.tile(kn_g_even[0].reshape(1, HD_A), (1, 2))
    gla_g = gla_g_even[0].reshape(1, DV_B)
    w_out_e = w_out_even[0].astype(BF16)
    w_out_a, w_out_b = w_out_e[:W_A], w_out_e[W_A:]

    h0_p, la_p = _prenorm_even(xp, scale_ep, shift_ep, g_e, w_lri, w_lr, b_lr, tm=512, rows_per_batch=seq)
    h0_s, la_s = _prenorm_even(xs, scale_es, shift_es, g_e, w_lri, w_lr, b_lr, tm=512, rows_per_batch=ms)
    (proj_p,) = _inproj(h0_p, w_e, tm=1024, tn=EV_N // 3, out_dtype=BF16, name="inproj_even")
    (proj_s,) = _inproj(h0_s, w_e, tm=512, tn=EV_N // 3, out_dtype=F32, name="inproj_even")

    ii = jnp.arange(WINDOW)
    ss = jnp.arange(2 * WINDOW)
    bucket_p = _bucket_or_masked(WINDOW + ii[:, None] - ss[None, :])
    mixed_a_p, k_last = _swa_prompt(proj_p, rel_bias, sinks_even[0], qn2, kn2, bucket_p, batch=batch, seq=seq)
    mixed_b_p, gla_p = _gla_prompt(proj_p, la_p, gla_g, batch=batch, seq=seq, rows=256)
    g_o = norm_g_odd[0].reshape(1, D_MODEL)
    y1_p, h1_p = _outproj([mixed_a_p, mixed_b_p], [w_out_a, w_out_b], xp, gate_ep, (scale_op, shift_op, g_o),
                          tm=512, rows_per_batch=seq)

    proj_p3 = proj_p.reshape(batch, seq, EV_N)
    swa_k_p = k_last.reshape(1, batch, WINDOW, H_A_KV, HD_A)
    swa_v_p = proj_p3[:, seq - WINDOW:, EV_KV + W_K: EV_KV + 2 * W_K].astype(F32).reshape(
        1, batch, WINDOW, H_A_KV, HD_A)

    rr = np.arange(H_A * SAMPLE_PAD)
    row_head = 8 * (rr // 64) + 2 * ((rr % 32) // 8) + (rr % 64) // 32
    tt = jnp.asarray(rr % SAMPLE_PAD)
    jj = jnp.arange(WINDOW)
    live = (tt < ntok)[:, None]
    bkc = jnp.where(live, _bucket_or_masked(WINDOW + tt[:, None] - jj[None, :]), -1)
    bkn = jnp.where(live & (jj[None, :] < ntok), _bucket_or_masked(tt[:, None] - jj[None, :]), -1)
    relrows = rel_bias.T[row_head]
    sinkrows = jnp.broadcast_to(sinks_even[0][row_head][:, None], (H_A * SAMPLE_PAD, 128))
    w_buf = cache_swa_k.shape[2]
    mixed_a_s, k_cache_s, v_cache_s = _swa_sample(
        proj_s, cache_swa_k[0].reshape(nseq, w_buf, W_K), cache_swa_v[0].reshape(nseq, w_buf, W_K),
        bkc, bkn, relrows, sinkrows, qn2, kn2, sb=16, ntok=ntok, unroll=4)
    mixed_b_s, gla_s = _gla_sample(proj_s, la_s, gla_g, state_gla[0], sb=8, ntok=ntok, unroll=4)
    y1_s, h1_s = _outproj([mixed_a_s, mixed_b_s], [w_out_a, w_out_b], xs, gate_es, (scale_os, shift_os, g_o),
                          tm=256, rows_per_batch=ms)

    w_o = w_in_odd[0].astype(BF16)
    ret_g = ret_g_odd[0].reshape(1, DV_C)
    w_out_o = w_out_odd[0].astype(BF16)
    cos_p, sin_p = _rotary_tables(jnp.arange(seq))
    pos_s = PAST_LEN + jnp.minimum(jnp.arange(SAMPLE_PAD), ntok - 1)
    cos_s, sin_s = _rotary_tables(jnp.tile(pos_s, nseq))

    (projo_s,) = _inproj(h1_s, w_o, (cos_s, sin_s), tm=512, tn=H_C * DK_C, out_dtype=F32, name="inproj_odd")
    ret_job = _ret_sample_job(projo_s, ret_g, state_ret[0], ntok=ntok)
    projo_p, o_s, ret_s = _inproj(h1_p, w_o, (cos_p, sin_p), ret_job, tm=256, tn=H_C * DK_C, out_dtype=BF16,
                                  name="inproj_odd_ret_sample")
    o_p, ret_p = _ret_prompt(projo_p, ret_g, batch=batch, seq=seq)
    (y2_p,) = _outproj([o_p], [w_out_o], y1_p, gate_op, tm=256, rows_per_batch=seq)
    (y2_s,) = _outproj([o_s], [w_out_o], y1_s, gate_os, tm=256, rows_per_batch=ms)

    y_prompt = y2_p.reshape(batch, seq, D_MODEL)
    y_sample = y2_s.reshape(nseq, SAMPLE_PAD, D_MODEL)[:, :ntok]
    return (y_prompt, y_sample, swa_k_p, swa_v_p, gla_p[None], ret_p[None],
            k_cache_s.reshape(1, nseq, w_buf, H_A_KV, HD_A), v_cache_s.reshape(1, nseq, w_buf, H_A_KV, HD_A),
            gla_s[None], ret_s[None])
```

```python
import functools
import math
from typing import Callable, NamedTuple

import numpy as np
import jax
import jax.numpy as jnp
from jax import lax
from jax.experimental import pallas as pl
from jax.experimental.pallas import tpu as pltpu

F32 = jnp.float32
BF16 = jnp.bfloat16

D_MODEL = 2048
WINDOW = 128
HD_A = 64
H_A = 16
H_A_KV = 2
G_A = 8
W_A = 1024
NUM_BUCKETS = 32
MAX_DISTANCE = 128
H_B = 4
DV_B = 256
DK_B = 128
W_B = 1024
GLA_RANK = 16
GLA_TAU = 16.0
GLA_CHUNK = 64
H_C = 8
DK_C = 256
DV_C = 512
W_C = 4096
RET_CHUNK = 128
ROPE_BASE = 10000.0
EPS = 1e-6
NEG_INF = -1e30
PAST_LEN = 8192

EV_QA, EV_GA, EV_VB, EV_GB, EV_QB, EV_KB, EV_KV = 0, 1024, 2048, 3072, 4096, 4608, 5120
EV_N = 5376
OD_N = 12288
W_K = H_A_KV * HD_A
SAMPLE_PAD = 8
MXU_N = 256
VMEM_LIMIT = 56 * 1024 * 1024

LOG_GAMMA = [float(np.log1p(-np.exp2(np.float32(-5.0 - h)))) for h in range(H_C)]


def _cparams(sem):
    return pltpu.CompilerParams(dimension_semantics=sem, vmem_limit_bytes=VMEM_LIMIT)


def _silu(x):
    return x / (1.0 + jnp.exp(-x))


def _dot(a, b):
    return jnp.dot(a, b, preferred_element_type=F32)


def _dot_nt(a, b):
    return lax.dot_general(a, b, (((1,), (1,)), ((), ())), preferred_element_type=F32)


def _dot_tn(a, b):
    return lax.dot_general(a, b, (((0,), (0,)), ((), ())), preferred_element_type=F32)


def _cumsum_rows(x):
    c = x.shape[0]
    r = lax.broadcasted_iota(jnp.int32, (c, c), 0)
    s = lax.broadcasted_iota(jnp.int32, (c, c), 1)
    tri = (r >= s).astype(BF16)
    hi = x.astype(BF16)
    r1 = x - hi.astype(F32)
    mid = r1.astype(BF16)
    lo = (r1 - mid.astype(F32)).astype(BF16)
    return _dot(tri, hi) + _dot(tri, mid) + _dot(tri, lo)


def _row_to_col(r):
    n = r.shape[1]
    ri = lax.broadcasted_iota(jnp.int32, (n, n), 0)
    ci = lax.broadcasted_iota(jnp.int32, (n, n), 1)
    return jnp.sum(jnp.where(ri == ci, jnp.broadcast_to(r, (n, n)), 0.0), axis=1, keepdims=True)


def _ada_kernel(c_ref, w_ref, b_ref, o_ref):
    sc = _silu(c_ref[...]).astype(BF16)
    o_ref[...] = _dot(sc, w_ref[...].astype(BF16)) + b_ref[...]


def _ada_mod(c_all, w, b):
    m = c_all.shape[0]
    n = w.shape[1]
    tn = 768
    return pl.pallas_call(
        _ada_kernel,
        grid=(n // tn,),
        in_specs=[pl.BlockSpec((m, D_MODEL), lambda j: (0, 0)),
                  pl.BlockSpec((D_MODEL, tn), lambda j: (0, j)),
                  pl.BlockSpec((1, tn), lambda j: (0, j))],
        out_specs=pl.BlockSpec((m, tn), lambda j: (0, j)),
        out_shape=jax.ShapeDtypeStruct((m, n), F32),
        compiler_params=_cparams(("arbitrary",)),
        name="ada_mod",
    )(c_all, w, b.reshape(1, n))


def _norm_mod(x, g, scale, shift):
    ms = jnp.mean(x * x, axis=-1, keepdims=True)
    y = x * lax.rsqrt(ms + EPS) * g
    return y * (1.0 + scale) + shift


def _log_sigmoid(z):
    return jnp.minimum(z, 0.0) - jnp.log(1.0 + jnp.exp(-jnp.abs(z)))


def _mod_spec(arr, tm, rows_per_batch):
    if arr.ndim == 3:
        return pl.BlockSpec((None, 1, D_MODEL), lambda i: (i // (rows_per_batch // tm), 0, 0))
    return pl.BlockSpec((tm, D_MODEL), lambda i: (i, 0))


def _prenorm_even_kernel(x_ref, scale_ref, shift_ref, g_ref, wlri_ref, wlr_ref, blr_ref, h_ref, la_ref):
    hb = _norm_mod(x_ref[...], g_ref[...], scale_ref[...], shift_ref[...]).astype(BF16)
    h_ref[...] = hb
    lr = _dot_nt(hb, wlri_ref[...])
    z = _dot(lr.astype(BF16), wlr_ref[...]) + blr_ref[...]
    la_ref[...] = _log_sigmoid(z) / GLA_TAU


def _prenorm_even(x, scale, shift, g, wlri, wlr, blr, *, tm, rows_per_batch):
    m = x.shape[0]
    const = lambda shape: pl.BlockSpec(shape, lambda i: (0, 0))
    return pl.pallas_call(
        _prenorm_even_kernel,
        grid=(m // tm,),
        in_specs=[pl.BlockSpec((tm, D_MODEL), lambda i: (i, 0)),
                  _mod_spec(scale, tm, rows_per_batch), _mod_spec(shift, tm, rows_per_batch),
                  const((1, D_MODEL)), const((128, D_MODEL)), const((128, H_B * DK_B)), const((1, H_B * DK_B))],
        out_specs=[pl.BlockSpec((tm, D_MODEL), lambda i: (i, 0)),
                   pl.BlockSpec((tm, H_B * DK_B), lambda i: (i, 0))],
        out_shape=[jax.ShapeDtypeStruct((m, D_MODEL), BF16),
                   jax.ShapeDtypeStruct((m, H_B * DK_B), F32)],
        compiler_params=_cparams(("arbitrary",)),
        name="prenorm_even",
    )(x, scale, shift, g, wlri, wlr, blr)


class _SideJob(NamedTuple):
    njobs: int
    args: list
    in_specs: Callable
    out_specs: Callable
    out_shape: list
    body: Callable


def _inproj_kernel(*refs, rot_tiles, side, nsteps):
    nmain = 4 if rot_tiles else 2
    if side is not None:
        njobs, nsi, nso, body = side
        side_in = refs[nmain:nmain + nsi]
        side_out = refs[nmain + nsi + 1:nmain + nsi + 1 + nso]
        step = pl.program_id(0) * pl.num_programs(1) + pl.program_id(1)
        block = (step * njobs) // nsteps
        prev_block = ((step - 1) * njobs) // nsteps

        @pl.when((step == 0) | (block != prev_block))
        def _():
            body(side_in, side_out)

        refs = refs[:nmain] + (refs[nmain + nsi],)
    h_ref, w_ref, *rest = refs
    o_ref = rest[-1]
    tn = o_ref.shape[1]

    def plain():
        for c in range(tn // MXU_N):
            cs = slice(c * MXU_N, (c + 1) * MXU_N)
            o_ref[:, cs] = _dot(h_ref[...], w_ref[:, cs]).astype(o_ref.dtype)

    if rot_tiles == 0:
        plain()
        return
    cos_ref, sin_ref = rest[:2]
    j = pl.program_id(0)
    half = DK_C // 2

    @pl.when(j < rot_tiles)
    def _():
        mult = jnp.where(j < rot_tiles // 2, 1.0, DK_C ** -0.5)
        cos = cos_ref[...] * mult
        sin = sin_ref[...] * mult
        for c in range(tn // DK_C):
            acc = _dot(h_ref[...], w_ref[:, c * DK_C:(c + 1) * DK_C])
            x1 = acc[:, :half]
            x2 = acc[:, half:]
            o_ref[:, c * DK_C:c * DK_C + half] = (x1 * cos - x2 * sin).astype(o_ref.dtype)
            o_ref[:, c * DK_C + half:(c + 1) * DK_C] = (x2 * cos + x1 * sin).astype(o_ref.dtype)

    pl.when(j >= rot_tiles)(plain)


def _inproj(h, w, rot=None, side=None, *, tm, tn, out_dtype, name):
    m = h.shape[0]
    n = w.shape[1]
    ni = m // tm
    nsteps = (n // tn) * ni
    in_specs = [pl.BlockSpec((tm, D_MODEL), lambda j, i: (i, 0)),
                pl.BlockSpec((D_MODEL, tn), lambda j, i: (0, j))]
    args = [h, w]
    rot_tiles = 0
    if rot is not None:
        nblk = rot[0].shape[0] // tm
        in_specs += [pl.BlockSpec((tm, DK_C // 2), lambda j, i: (i % nblk, 0))] * 2
        args += list(rot)
        rot_tiles = 2 * (H_C * DK_C) // tn
    out_specs = [pl.BlockSpec((tm, tn), lambda j, i: (i, j))]
    out_shape = [jax.ShapeDtypeStruct((m, n), out_dtype)]
    side_static = None
    if side is not None:
        assert side.njobs <= nsteps
        block = lambda j, i: ((j * ni + i) * side.njobs) // nsteps
        in_specs += side.in_specs(block)
        args += side.args
        out_specs += side.out_specs(block)
        out_shape += side.out_shape
        side_static = (side.njobs, len(side.args), len(side.out_shape), side.body)
    return pl.pallas_call(
        functools.partial(_inproj_kernel, rot_tiles=rot_tiles, side=side_static, nsteps=nsteps),
        grid=(n // tn, ni),
        in_specs=in_specs,
        out_specs=out_specs,
        out_shape=out_shape,
        compiler_params=_cparams(("arbitrary", "arbitrary")),
        name=name,
    )(*args)


def _outproj_kernel(*refs, nin, with_next):
    m_refs, w_refs = refs[:nin], refs[nin:2 * nin]
    if with_next:
        x_ref, gate_ref, nscale_ref, nshift_ref, ng_ref, o_ref, h_ref = refs[2 * nin:]
    else:
        x_ref, gate_ref, o_ref = refs[2 * nin:]
    for c in range(o_ref.shape[1] // MXU_N):
        cs = slice(c * MXU_N, (c + 1) * MXU_N)
        acc = _dot(m_refs[0][...].astype(BF16), w_refs[0][:, cs])
        for m_ref, w_ref in zip(m_refs[1:], w_refs[1:]):
            acc += _dot(m_ref[...].astype(BF16), w_ref[:, cs])
        o_ref[:, cs] = x_ref[:, cs] + gate_ref[:, cs] * acc
    if with_next:
        h_ref[...] = _norm_mod(o_ref[...], ng_ref[...], nscale_ref[...], nshift_ref[...]).astype(BF16)


def _outproj(mixed, ws, x, gate, next_mod=None, *, tm, rows_per_batch):
    m = x.shape[0]
    row = pl.BlockSpec((tm, D_MODEL), lambda i: (i, 0))
    in_specs = ([pl.BlockSpec((tm, a.shape[1]), lambda i: (i, 0)) for a in mixed]
                + [pl.BlockSpec(w.shape, lambda i: (0, 0), pipeline_mode=pl.Buffered(1)) for w in ws]
                + [row, _mod_spec(gate, tm, rows_per_batch)])
    args = [*mixed, *ws, x, gate]
    out_specs = [row]
    out_shape = [jax.ShapeDtypeStruct((m, D_MODEL), F32)]
    if next_mod is not None:
        nscale, nshift, ng = next_mod
        in_specs += [_mod_spec(nscale, tm, rows_per_batch), _mod_spec(nshift, tm, rows_per_batch),
                     pl.BlockSpec((1, D_MODEL), lambda i: (0, 0))]
        args += [nscale, nshift, ng]
        out_specs.append(row)
        out_shape.append(jax.ShapeDtypeStruct((m, D_MODEL), BF16))
    return pl.pallas_call(
        functools.partial(_outproj_kernel, nin=len(mixed), with_next=next_mod is not None),
        grid=(m // tm,),
        in_specs=in_specs,
        out_specs=out_specs,
        out_shape=out_shape,
        compiler_params=_cparams(("arbitrary",)),
        name="outproj",
    )(*args)


def _seg_rms(x, g2, seg_ones):
    x2 = x * x
    hi = x2.astype(BF16)
    lo = (x2 - hi.astype(F32)).astype(BF16)
    ss = _dot(hi, seg_ones) + _dot(lo, seg_ones)
    return x * lax.rsqrt(ss * (1.0 / HD_A) + EPS) * g2


def _dup_halves(x, lo_half):
    sw = pltpu.roll(x, HD_A, 1)
    return [jnp.where(lo_half, x, sw), jnp.where(lo_half, sw, x)]


def _rows_to_cols(x):
    pad = jnp.zeros((x.shape[1] - x.shape[0], x.shape[1]), x.dtype)
    return jnp.concatenate([x, pad], axis=0).T


def _seg_ones():
    r = lax.broadcasted_iota(jnp.int32, (2 * HD_A, 2 * HD_A), 0)
    c = lax.broadcasted_iota(jnp.int32, (2 * HD_A, 2 * HD_A), 1)
    return ((r < HD_A) == (c < HD_A)).astype(BF16)


def _swa_prompt_kernel(relb_ref, sinks_ref, bucket_ref, q_ref, ga_ref, kv_ref, qn_ref, kn_ref,
                       o_ref, knew_ref, bias_ref, kprev_ref, vprev_ref, s_ref, pe_ref):
    b = pl.program_id(0)
    i = pl.program_id(1)

    @pl.when((b == 0) & (i == 0))
    def _():
        bk = bucket_ref[...]
        for h in range(H_A):
            bias_ref[h] = jnp.full((WINDOW, 2 * WINDOW), NEG_INF, F32)
        for bb in range(NUM_BUCKETS):
            hit = bk == bb
            for h in range(H_A):
                bias_ref[h] = jnp.where(hit, relb_ref[bb, h], bias_ref[h])
        col = lax.broadcasted_iota(jnp.int32, (WINDOW, 2 * WINDOW), 1)
        for h in range(H_A):
            bias_ref[H_A + h] = jnp.where(col >= WINDOW, bias_ref[h], NEG_INF)

    @pl.when(i == 0)
    def _():
        kprev_ref[...] = jnp.zeros(kprev_ref.shape, BF16)
        vprev_ref[...] = jnp.zeros(vprev_ref.shape, BF16)

    seg_ones = _seg_ones()
    lt = 2 * HD_A
    tpg = G_A // 2
    lo_half = lax.broadcasted_iota(jnp.int32, (WINDOW, lt), 1) < HD_A
    kv = kv_ref[...].astype(F32)
    kc = _seg_rms(kv[:, :W_K], kn_ref[...], seg_ones)
    k_dup = _dup_halves(kc, lo_half)
    v_dup = _dup_halves(kv[:, W_K:], lo_half)
    base = jnp.where(i == 0, H_A, 0)
    qn2 = qn_ref[...] * HD_A ** -0.5
    gq = tpg * WINDOW
    lo_g = lax.broadcasted_iota(jnp.int32, (gq, lt), 1) < HD_A
    grows = [slice(g * 2 * gq, (g + 1) * 2 * gq) for g in range(H_A_KV)]
    for g in range(H_A_KV):
        kk = jnp.concatenate([kprev_ref[g], k_dup[g].astype(BF16)], axis=0)
        xg = jnp.concatenate([q_ref[:, (g * tpg + p) * lt:(g * tpg + p + 1) * lt] for p in range(tpg)],
                             axis=0).astype(F32)
        xn = _seg_rms(xg, qn2, seg_ones)
        q_all = jnp.concatenate([jnp.where(lo_g, xn, 0.0), jnp.where(lo_g, 0.0, xn)], axis=0).astype(BF16)
        s_ref[grows[g], :] = _dot_nt(q_all, kk)
    sink_terms = []
    for g in range(H_A_KV):
        for n in range(G_A):
            h = g * G_A + 2 * (n % tpg) + n // tpg
            rows = slice((g * G_A + n) * WINDOW, (g * G_A + n + 1) * WINDOW)
            s = s_ref[rows, :] + bias_ref[base + h]
            sink = sinks_ref[h]
            m = jnp.maximum(jnp.max(s, axis=-1, keepdims=True), sink)
            pe_ref[rows, :] = jnp.exp(s - m).astype(BF16)
            sink_terms.append(jnp.exp(sink - m))
    o_ext = []
    for g in range(H_A_KV):
        vv = jnp.concatenate([vprev_ref[g], v_dup[g].astype(BF16)], axis=0)
        vv_ext = jnp.concatenate([vv, jnp.ones(vv.shape, BF16)], axis=1)
        o_ext.append(_dot(pe_ref[grows[g], :], vv_ext))
    for g in range(H_A_KV):
        for p in range(tpg):
            halves = []
            for a in range(2):
                n = a * tpg + p
                rows = slice(n * WINDOW, (n + 1) * WINDOW)
                halves.append(o_ext[g][rows, :lt] / (o_ext[g][rows, lt:] + sink_terms[g * G_A + n]))
            oa = jnp.where(lo_half, halves[0], halves[1])
            cols = slice((g * tpg + p) * lt, (g * tpg + p + 1) * lt)
            o_ref[:, cols] = (oa * _silu(ga_ref[:, cols].astype(F32))).astype(o_ref.dtype)
    for g in range(H_A_KV):
        kprev_ref[g] = k_dup[g].astype(BF16)
        vprev_ref[g] = v_dup[g].astype(BF16)

    @pl.when(i == pl.num_programs(1) - 1)
    def _():
        knew_ref[...] = kc.T


def _swa_prompt(proj, rel_bias, sinks, qn2, kn2, bucket, *, batch, seq):
    nb = seq // WINDOW
    rb = lambda b, i: b * nb + i
    smem = pl.BlockSpec(memory_space=pltpu.SMEM)
    return pl.pallas_call(
        _swa_prompt_kernel,
        grid=(batch, nb),
        in_specs=[smem, smem,
                  pl.BlockSpec((WINDOW, 2 * WINDOW), lambda b, i: (0, 0)),
                  pl.BlockSpec((WINDOW, W_A), lambda b, i: (rb(b, i), EV_QA // W_A)),
                  pl.BlockSpec((WINDOW, W_A), lambda b, i: (rb(b, i), EV_GA // W_A)),
                  pl.BlockSpec((WINDOW, 2 * W_K), lambda b, i: (rb(b, i), EV_KV // (2 * W_K))),
                  pl.BlockSpec((1, 2 * HD_A), lambda b, i: (0, 0)),
                  pl.BlockSpec((1, 2 * HD_A), lambda b, i: (0, 0))],
        out_specs=[pl.BlockSpec((WINDOW, W_A), lambda b, i: (rb(b, i), 0)),
                   pl.BlockSpec((None, WINDOW, W_K), lambda b, i: (b, 0, 0))],
        out_shape=[jax.ShapeDtypeStruct((batch * seq, W_A), BF16),
                   jax.ShapeDtypeStruct((batch, WINDOW, W_K), F32)],
        scratch_shapes=[pltpu.VMEM((2 * H_A, WINDOW, 2 * WINDOW), F32),
                        pltpu.VMEM((H_A_KV, WINDOW, W_K), BF16), pltpu.VMEM((H_A_KV, WINDOW, W_K), BF16),
                        pltpu.VMEM((H_A * WINDOW, 2 * WINDOW), F32), pltpu.VMEM((H_A * WINDOW, 2 * WINDOW), BF16)],
        compiler_params=_cparams(("arbitrary", "arbitrary")),
        name="swa_prompt",
    )(rel_bias, sinks, bucket, proj, proj, proj, qn2, kn2)


def _swa_sample_kernel(bkc_ref, bkn_ref, relrows_ref, sinkrows_ref, q_ref, ga_ref, kvn_ref, ck_ref, cv_ref,
                       qn_ref, kn_ref, o_ref, ko_ref, vo_ref, biasc_ref, biasn_ref, *, sb, ntok, unroll):
    @pl.when(pl.program_id(0) == 0)
    def _():
        bkc = bkc_ref[...]
        bkn = bkn_ref[...]
        rr = relrows_ref[...]
        bc = jnp.full(bkc.shape, NEG_INF, F32)
        bn = jnp.full(bkn.shape, NEG_INF, F32)
        for bb in range(NUM_BUCKETS):
            val = rr[:, bb:bb + 1]
            bc = jnp.where(bkc == bb, val, bc)
            bn = jnp.where(bkn == bb, val, bn)
        biasc_ref[...] = bc
        biasn_ref[...] = bn

    seg_ones = _seg_ones()
    lt = 2 * HD_A
    nrow = G_A * SAMPLE_PAD
    grow = nrow // H_A_KV
    lo_q = lax.broadcasted_iota(jnp.int32, (grow, lt), 1) < HD_A
    lo_n = lax.broadcasted_iota(jnp.int32, (SAMPLE_PAD, lt), 1) < HD_A
    lane_pos = lax.broadcasted_iota(jnp.int32, (W_K, WINDOW), 1)
    qn2 = qn_ref[...] * HD_A ** -0.5
    kn2 = kn_ref[...]
    sink = sinkrows_ref[:, 0:1]

    def body(it, carry):
        seqs = [it * unroll + u for u in range(unroll)]
        rows = [pl.ds(pl.multiple_of(s * SAMPLE_PAD, SAMPLE_PAD), SAMPLE_PAD) for s in seqs]
        kvn = [kvn_ref[r, :] for r in rows]
        kn_all = _seg_rms(jnp.concatenate([x[:, :W_K] for x in kvn], axis=0), kn2, seg_ones)
        q2_all = jnp.concatenate([q_ref[r, p * lt:(p + 1) * lt] for r in rows for p in range(G_A)], axis=0)
        xn_all = _seg_rms(q2_all, qn2, seg_ones)
        sc, sn, kn = [], [], []
        for u, s in enumerate(seqs):
            kn.append(kn_all[u * SAMPLE_PAD:(u + 1) * SAMPLE_PAD])
            kn_dup = _dup_halves(kn[u], lo_n)
            scg, sng = [], []
            for g in range(H_A_KV):
                xn = xn_all[u * nrow + g * grow:u * nrow + (g + 1) * grow]
                q4 = jnp.concatenate([jnp.where(lo_q, xn, 0.0), jnp.where(lo_q, 0.0, xn)], axis=0).astype(BF16)
                kt_g = ck_ref[s, g * HD_A:(g + 1) * HD_A, :].astype(BF16)
                scg.append(_dot(q4, jnp.concatenate([kt_g, kt_g], axis=0)))
                sng.append(_dot_nt(q4, kn_dup[g].astype(BF16)))
            sc.append(jnp.concatenate(scg, axis=0))
            sn.append(jnp.concatenate(sng, axis=0))
        pc, pn, sink_terms = [], [], []
        for u in range(unroll):
            scu = sc[u] + biasc_ref[...]
            snu = sn[u] + biasn_ref[:, :SAMPLE_PAD]
            m = jnp.maximum(jnp.maximum(jnp.max(scu, axis=-1, keepdims=True),
                                        jnp.max(snu, axis=-1, keepdims=True)), sink)
            pc.append(jnp.exp(scu - m).astype(BF16))
            pn.append(jnp.exp(snu - m).astype(BF16))
            sink_terms.append(jnp.exp(sink - m))
        o_ext = []
        for u, s in enumerate(seqs):
            vn_dup = _dup_halves(kvn[u][:, W_K:], lo_n)
            og = []
            for g in range(H_A_KV):
                vt_g = cv_ref[s, g * HD_A:(g + 1) * HD_A, :].astype(BF16)
                vt_ext = jnp.concatenate([vt_g, vt_g, jnp.ones((lt, WINDOW), BF16)], axis=0)
                vn_ext = jnp.concatenate([vn_dup[g].astype(BF16), jnp.ones((SAMPLE_PAD, lt), BF16)], axis=1)
                gr = slice(g * 2 * grow, (g + 1) * 2 * grow)
                og.append(_dot_nt(pc[u][gr], vt_ext) + _dot(pn[u][gr], vn_ext))
            o_ext.append(jnp.concatenate(og, axis=0))
        for u, s in enumerate(seqs):
            o4 = o_ext[u][:, :lt] / (o_ext[u][:, lt:] + sink_terms[u])
            xg = ga_ref[rows[u], :]
            for p in range(G_A):
                g, pp = divmod(p, G_A // 2)
                r0 = g * 2 * grow + pp * SAMPLE_PAD
                o2 = jnp.where(lo_n, o4[r0:r0 + SAMPLE_PAD], o4[r0 + grow:r0 + grow + SAMPLE_PAD])
                o_ref[rows[u], p * lt:(p + 1) * lt] = (o2 * _silu(xg[:, p * lt:(p + 1) * lt])).astype(o_ref.dtype)
            ko_ref[s] = pltpu.roll(jnp.where(lane_pos < ntok, _rows_to_cols(kn[u]), ck_ref[s]), WINDOW - ntok, 1)
            vo_ref[s] = pltpu.roll(jnp.where(lane_pos < ntok, _rows_to_cols(kvn[u][:, W_K:]), cv_ref[s]),
                                   WINDOW - ntok, 1)
        return carry

    lax.fori_loop(0, sb // unroll, body, 0)


def _swa_sample(proj, cache_k, cache_v, bkc, bkn, relrows, sinkrows, qn2, kn2, *, sb, ntok, unroll):
    nseq = cache_k.shape[0]
    rows = sb * SAMPLE_PAD
    full = lambda shape: pl.BlockSpec(shape, lambda i: tuple(0 for _ in shape))
    cache = pl.BlockSpec((sb, W_K, WINDOW), lambda i: (i, 0, 0))
    return pl.pallas_call(
        functools.partial(_swa_sample_kernel, sb=sb, ntok=ntok, unroll=unroll),
        grid=(nseq // sb,),
        in_specs=[full(bkc.shape), full(bkn.shape), full(relrows.shape), full(sinkrows.shape),
                  pl.BlockSpec((rows, W_A), lambda i: (i, EV_QA // W_A)),
                  pl.BlockSpec((rows, W_A), lambda i: (i, EV_GA // W_A)),
                  pl.BlockSpec((rows, 2 * W_K), lambda i: (i, EV_KV // (2 * W_K))),
                  cache, cache, full((1, 2 * HD_A)), full((1, 2 * HD_A))],
        out_specs=[pl.BlockSpec((rows, W_A), lambda i: (i, 0)), cache, cache],
        out_shape=[jax.ShapeDtypeStruct((nseq * SAMPLE_PAD, W_A), F32),
                   jax.ShapeDtypeStruct(cache_k.shape, F32),
                   jax.ShapeDtypeStruct(cache_v.shape, F32)],
        scratch_shapes=[pltpu.VMEM(bkc.shape, F32), pltpu.VMEM(bkn.shape, F32)],
        compiler_params=_cparams(("arbitrary",)),
        name="swa_sample",
    )(bkc, bkn, relrows, sinkrows, proj, proj, proj, cache_k, cache_v, qn2, kn2)


def _gla_pre(q, k, v, la, n_valid):
    c = q.shape[0]
    bcum = _cumsum_rows(la)
    rr = lax.broadcasted_iota(jnp.int32, (c, c), 0)
    cc = lax.broadcasted_iota(jnp.int32, (c, c), 1)
    causal = rr >= cc
    row = lax.broadcasted_iota(jnp.int32, (c, 1), 0)
    qts, kts, kds, vbs, dcols = [], [], [], [], []
    for h in range(H_B):
        ks = slice(h * DK_B, (h + 1) * DK_B)
        bc = bcum[:, ks]
        qts.append(((q[:, ks] * DK_B ** -0.5) * jnp.exp(bc)).astype(BF16))
        kts.append((k[:, ks] * jnp.exp(-bc)).astype(BF16))
        blast = bc[n_valid - 1:n_valid, :]
        kd = k[:, ks] * jnp.exp(blast - bc)
        if n_valid < c:
            kd = jnp.where(row < n_valid, kd, 0.0)
        kds.append(kd.astype(BF16))
        vbs.append(v[:, h * DV_B:(h + 1) * DV_B].astype(BF16))
        dcols.append(_row_to_col(jnp.exp(blast)))
    scores = [_dot_nt(qts[h], kts[h]) for h in range(H_B)]
    upds = [_dot_tn(kds[h], vbs[h]) for h in range(H_B)]
    return [(qts[h], jnp.where(causal, scores[h], 0.0).astype(BF16), vbs[h], upds[h], dcols[h])
            for h in range(H_B)]


def _gla_post(pre, gb, gla_g, states):
    outs = [_dot(a, vb) + _dot(qt, st.astype(BF16)) for (qt, a, vb, _, _), st in zip(pre, states)]
    new_states = [dcol * st + upd for (_, _, _, upd, dcol), st in zip(pre, states)]
    gated = []
    for h, o in enumerate(outs):
        on = o * lax.rsqrt(jnp.mean(o * o, axis=-1, keepdims=True) + EPS) * gla_g
        gated.append(on * _silu(gb[:, h * DV_B:(h + 1) * DV_B]))
    return jnp.concatenate(gated, axis=1), new_states


def _gla_prompt_kernel(q_ref, k_ref, v_ref, gb_ref, la_ref, g_ref, o_ref, s_ref, *, nchunk):
    @pl.when(pl.program_id(1) == 0)
    def _():
        s_ref[...] = jnp.zeros(s_ref.shape, F32)

    gla_g = g_ref[...]
    chunk_rows = [slice(c * GLA_CHUNK, (c + 1) * GLA_CHUNK) for c in range(nchunk)]
    pres = [_gla_pre(q_ref[r, :].astype(F32), k_ref[r, :].astype(F32), v_ref[r, :].astype(F32), la_ref[r, :],
                     GLA_CHUNK) for r in chunk_rows]
    states = [s_ref[h] for h in range(H_B)]
    for r, pre in zip(chunk_rows, pres):
        out, states = _gla_post(pre, gb_ref[r, :].astype(F32), gla_g, states)
        o_ref[r, :] = out.astype(o_ref.dtype)
    for h in range(H_B):
        s_ref[h] = states[h]


def _gla_prompt(proj, la, gla_g, *, batch, seq, rows):
    nstep = seq // rows
    rb = lambda b, i: b * nstep + i
    return pl.pallas_call(
        functools.partial(_gla_prompt_kernel, nchunk=rows // GLA_CHUNK),
        grid=(batch, nstep),
        in_specs=[pl.BlockSpec((rows, H_B * DK_B), lambda b, i: (rb(b, i), EV_QB // (H_B * DK_B))),
                  pl.BlockSpec((rows, H_B * DK_B), lambda b, i: (rb(b, i), EV_KB // (H_B * DK_B))),
                  pl.BlockSpec((rows, W_B), lambda b, i: (rb(b, i), EV_VB // W_B)),
                  pl.BlockSpec((rows, W_B), lambda b, i: (rb(b, i), EV_GB // W_B)),
                  pl.BlockSpec((rows, H_B * DK_B), lambda b, i: (rb(b, i), 0)),
                  pl.BlockSpec((1, DV_B), lambda b, i: (0, 0))],
        out_specs=[pl.BlockSpec((rows, W_B), lambda b, i: (rb(b, i), 0)),
                   pl.BlockSpec((None, H_B, DK_B, DV_B), lambda b, i: (b, 0, 0, 0))],
        out_shape=[jax.ShapeDtypeStruct((batch * seq, W_B), BF16),
                   jax.ShapeDtypeStruct((batch, H_B, DK_B, DV_B), F32)],
        compiler_params=_cparams(("arbitrary", "arbitrary")),
        name="gla_prompt",
    )(proj, proj, proj, proj, la, gla_g)


def _gla_sample_kernel(q_ref, k_ref, v_ref, gb_ref, la_ref, g_ref, s_in_ref, o_ref, s_ref, *, sb, ntok, unroll):
    gla_g = g_ref[...]

    def body(it, carry):
        seqs = [it * unroll + u for u in range(unroll)]
        rows = [pl.ds(pl.multiple_of(s * SAMPLE_PAD, SAMPLE_PAD), SAMPLE_PAD) for s in seqs]
        pres = [_gla_pre(q_ref[r, :], k_ref[r, :], v_ref[r, :], la_ref[r, :], ntok) for r in rows]
        for s, r, pre in zip(seqs, rows, pres):
            out, new_states = _gla_post(pre, gb_ref[r, :], gla_g, [s_in_ref[s, h] for h in range(H_B)])
            o_ref[r, :] = out.astype(o_ref.dtype)
            for h in range(H_B):
                s_ref[s, h] = new_states[h]
        return carry

    lax.fori_loop(0, sb // unroll, body, 0)


def _gla_sample(proj, la, gla_g, state, *, sb, ntok, unroll):
    nseq = state.shape[0]
    rows = sb * SAMPLE_PAD
    return pl.pallas_call(
        functools.partial(_gla_sample_kernel, sb=sb, ntok=ntok, unroll=unroll),
        grid=(nseq // sb,),
        in_specs=[pl.BlockSpec((rows, H_B * DK_B), lambda i: (i, EV_QB // (H_B * DK_B))),
                  pl.BlockSpec((rows, H_B * DK_B), lambda i: (i, EV_KB // (H_B * DK_B))),
                  pl.BlockSpec((rows, W_B), lambda i: (i, EV_VB // W_B)),
                  pl.BlockSpec((rows, W_B), lambda i: (i, EV_GB // W_B)),
                  pl.BlockSpec((rows, H_B * DK_B), lambda i: (i, 0)),
                  pl.BlockSpec((1, DV_B), lambda i: (0, 0)),
                  pl.BlockSpec((sb, H_B, DK_B, DV_B), lambda i: (i, 0, 0, 0))],
        out_specs=[pl.BlockSpec((rows, W_B), lambda i: (i, 0)),
                   pl.BlockSpec((sb, H_B, DK_B, DV_B), lambda i: (i, 0, 0, 0))],
        out_shape=[jax.ShapeDtypeStruct((nseq * SAMPLE_PAD, W_B), BF16),
                   jax.ShapeDtypeStruct(state.shape, F32)],
        compiler_params=_cparams(("arbitrary",)),
        name="gla_sample",
    )(proj, proj, proj, proj, la, gla_g, state)


def _ret_chunk(q_ref, k_ref, v_ref, g_ref, retg, state_in_ref, state_ref, o_ref, n_valid):
    c = q_ref.shape[0]
    ri = lax.broadcasted_iota(jnp.int32, (c, c), 0)
    ci = lax.broadcasted_iota(jnp.int32, (c, c), 1)
    dist = (ri - ci).astype(F32)
    row = lax.broadcasted_iota(jnp.int32, (c, 1), 0)
    rowf = row.astype(F32)
    a_heads = []
    for h in range(H_C):
        lg = LOG_GAMMA[h]
        ks = slice(h * DK_C, (h + 1) * DK_C)
        decay = jnp.where(ri >= ci, jnp.exp(jnp.maximum(dist, 0.0) * lg), 0.0)
        a_heads.append((_dot_nt(q_ref[:, ks].astype(BF16), k_ref[:, ks].astype(BF16)) * decay).astype(BF16))
    for h in range(H_C):
        lg = LOG_GAMMA[h]
        ks = slice(h * DK_C, (h + 1) * DK_C)
        vs = slice(h * DV_C, (h + 1) * DV_C)
        qb = q_ref[:, ks].astype(BF16)
        kf = k_ref[:, ks].astype(F32)
        vb = v_ref[:, vs].astype(BF16)
        st = state_in_ref[h]
        inner = jnp.exp((rowf + 1.0) * lg)
        o = _dot(a_heads[h], vb) + inner * _dot(qb, st.astype(BF16))
        kd = kf * jnp.exp((n_valid - 1.0 - rowf) * lg)
        if n_valid < c:
            kd = jnp.where(row < n_valid, kd, 0.0)
        state_ref[h] = math.exp(n_valid * lg) * st + _dot_tn(kd.astype(BF16), vb)
        on = o * lax.rsqrt(jnp.mean(o * o, axis=-1, keepdims=True) + EPS) * retg
        o_ref[:, vs] = (on * _silu(g_ref[:, vs].astype(F32))).astype(o_ref.dtype)


def _ret_prompt_kernel(q_ref, k_ref, v_ref, g_ref, retg_ref, o_ref, s_ref):
    @pl.when(pl.program_id(1) == 0)
    def _():
        s_ref[...] = jnp.zeros(s_ref.shape, F32)

    _ret_chunk(q_ref, k_ref, v_ref, g_ref, retg_ref[...], s_ref, s_ref, o_ref, RET_CHUNK)


def _ret_prompt(proj, ret_g, *, batch, seq):
    nstep = seq // RET_CHUNK
    rb = lambda b, i: b * nstep + i
    qk = H_C * DK_C
    return pl.pallas_call(
        _ret_prompt_kernel,
        grid=(batch, nstep),
        in_specs=[pl.BlockSpec((RET_CHUNK, qk), lambda b, i: (rb(b, i), 0)),
                  pl.BlockSpec((RET_CHUNK, qk), lambda b, i: (rb(b, i), 1)),
                  pl.BlockSpec((RET_CHUNK, W_C), lambda b, i: (rb(b, i), 1)),
                  pl.BlockSpec((RET_CHUNK, W_C), lambda b, i: (rb(b, i), 2)),
                  pl.BlockSpec((1, DV_C), lambda b, i: (0, 0))],
        out_specs=[pl.BlockSpec((RET_CHUNK, W_C), lambda b, i: (rb(b, i), 0)),
                   pl.BlockSpec((None, H_C, DK_C, DV_C), lambda b, i: (b, 0, 0, 0))],
        out_shape=[jax.ShapeDtypeStruct((batch * seq, W_C), BF16),
                   jax.ShapeDtypeStruct((batch, H_C, DK_C, DV_C), F32)],
        compiler_params=_cparams(("arbitrary", "arbitrary")),
        name="ret_prompt",
    )(proj, proj, proj, proj, ret_g)


def _ret_sample_job(proj, ret_g, state, *, ntok):
    nseq = state.shape[0]
    qk = H_C * DK_C
    state_spec = lambda blk: pl.BlockSpec((None, H_C, DK_C, DV_C), lambda j, i: (blk(j, i), 0, 0, 0))

    def body(in_refs, out_refs):
        q_ref, k_ref, v_ref, g_ref, retg_ref, s_in_ref = in_refs
        o_ref, s_ref = out_refs
        _ret_chunk(q_ref, k_ref, v_ref, g_ref, retg_ref[...], s_in_ref, s_ref, o_ref, ntok)

    return _SideJob(
        njobs=nseq,
        args=[proj, proj, proj, proj, ret_g, state],
        in_specs=lambda blk: [pl.BlockSpec((SAMPLE_PAD, qk), lambda j, i: (blk(j, i), 0)),
                              pl.BlockSpec((SAMPLE_PAD, qk), lambda j, i: (blk(j, i), 1)),
                              pl.BlockSpec((SAMPLE_PAD, W_C), lambda j, i: (blk(j, i), 1)),
                              pl.BlockSpec((SAMPLE_PAD, W_C), lambda j, i: (blk(j, i), 2)),
                              pl.BlockSpec((1, DV_C), lambda j, i: (0, 0)),
                              state_spec(blk)],
        out_specs=lambda blk: [pl.BlockSpec((SAMPLE_PAD, W_C), lambda j, i: (blk(j, i), 0)), state_spec(blk)],
        out_shape=[jax.ShapeDtypeStruct((nseq * SAMPLE_PAD, W_C), BF16),
                   jax.ShapeDtypeStruct(state.shape, F32)],
        body=body)


def _t5_bucket(dist):
    dist = jnp.maximum(dist, 0)
    max_exact = NUM_BUCKETS // 2
    log_ratio = jnp.log(jnp.maximum(dist, 1).astype(F32) / max_exact) / math.log(MAX_DISTANCE / max_exact)
    large = jnp.minimum(max_exact + (log_ratio * (NUM_BUCKETS - max_exact)).astype(jnp.int32), NUM_BUCKETS - 1)
    return jnp.where(dist < max_exact, dist, large)


def _bucket_or_masked(dist):
    return jnp.where((dist >= 0) & (dist <= WINDOW), _t5_bucket(dist), -1).astype(jnp.int32)


def _rotary_tables(pos):
    half = DK_C // 2
    inv = ROPE_BASE ** (-jnp.arange(half, dtype=F32) / half)
    ang = pos.astype(F32)[:, None] * inv[None, :]
    return jnp.cos(ang), jnp.sin(ang)


def _regroup_kernel(wt_ref, o_ref):
    o_ref[...] = wt_ref[...].T.astype(o_ref.dtype)


def _even_weight_layout(w_in):
    wt = jnp.swapaxes(w_in, 0, 1)
    src = lambda j: jnp.where(j < 4, j, jnp.where(j < 8, j + 1, jnp.where(j < 16, j + 5,
                                                                         jnp.where(j < 20, j - 7, 4))))
    main = pl.pallas_call(
        _regroup_kernel,
        grid=(EV_N // MXU_N,),
        in_specs=[pl.BlockSpec((MXU_N, D_MODEL), lambda j: (src(j), 0))],
        out_specs=pl.BlockSpec((D_MODEL, MXU_N), lambda j: (0, j)),
        out_shape=jax.ShapeDtypeStruct((D_MODEL, EV_N), BF16),
        compiler_params=_cparams(("arbitrary",)),
        name="regroup_even_weights",
    )(wt)
    lr_t = jnp.pad(wt[EV_N:], ((0, 128 - GLA_RANK), (0, 0))).astype(BF16)
    return main, lr_t


def _pad_tokens(a, ntok):
    pad = [(0, 0), (0, SAMPLE_PAD - ntok)] + [(0, 0)] * (a.ndim - 2)
    a = jnp.pad(a, pad)
    return a.reshape((a.shape[0] * SAMPLE_PAD,) + a.shape[2:])


def _rows(v, reps):
    return jnp.repeat(v, reps, axis=0)


def kernel(x_prompt, x_sample, cache_swa_k, cache_swa_v, state_gla, state_ret, c_prompt, c_sample, rel_bias,
           ada_w_even, ada_b_even, norm_g_even, w_in_even, w_lr_even, b_lr_even, qn_g_even, kn_g_even,
           sinks_even, gla_g_even, w_out_even, ada_w_odd, ada_b_odd, norm_g_odd, w_in_odd, ret_g_odd, w_out_odd):
    batch, seq, _ = x_prompt.shape
    nseq, ntok, _ = x_sample.shape
    mp = batch * seq

    c_all = jnp.concatenate([c_prompt, c_sample], axis=0)
    mod_e = _ada_mod(c_all, ada_w_even[0], ada_b_even[0])
    mod_o = _ada_mod(c_all, ada_w_odd[0], ada_b_odd[0])

    def split_mod(mod):
        shift, scale, gate = jnp.split(mod, 3, axis=1)
        p = tuple(a[:batch].reshape(batch, 1, D_MODEL) for a in (shift, scale, gate))
        s = tuple(_rows(a[batch:], SAMPLE_PAD) for a in (shift, scale, gate))
        return p, s

    (shift_ep, scale_ep, gate_ep), (shift_es, scale_es, gate_es) = split_mod(mod_e)
    (shift_op, scale_op, gate_op), (shift_os, scale_os, gate_os) = split_mod(mod_o)

    xp = x_prompt.reshape(mp, D_MODEL)
    xs = _pad_tokens(x_sample, ntok)
    ms = xs.shape[0]

    w_e, w_lri = _even_weight_layout(w_in_even[0])
    w_lr = jnp.pad(w_lr_even[0], ((0, 128 - GLA_RANK), (0, 0))).astype(BF16)
    b_lr = b_lr_even[0].reshape(1, -1)
    g_e = norm_g_even[0].reshape(1, D_MODEL)
    qn2 = jnp.tile(qn_g_even[0].reshape(1, HD_A), (1, 2))
    kn2 = jnp.tile(kn_g_even[0].reshape(1, HD_A), (1, 2))
    gla_g = gla_g_even[0].reshape(1, DV_B)
    w_out_e = w_out_even[0].astype(BF16)
    w_out_a, w_out_b = w_out_e[:W_A], w_out_e[W_A:]

    h0_p, la_p = _prenorm_even(xp, scale_ep, shift_ep, g_e, w_lri, w_lr, b_lr, tm=512, rows_per_batch=seq)
    h0_s, la_s = _prenorm_even(xs, scale_es, shift_es, g_e, w_lri, w_lr, b_lr, tm=512, rows_per_batch=ms)
    (proj_p,) = _inproj(h0_p, w_e, tm=1024, tn=EV_N // 3, out_dtype=BF16, name="inproj_even")
    (proj_s,) = _inproj(h0_s, w_e, tm=512, tn=EV_N // 3, out_dtype=F32, name="inproj_even")

    ii = jnp.arange(WINDOW)
    ss = jnp.arange(2 * WINDOW)
    bucket_p = _bucket_or_masked(WINDOW + ii[:, None] - ss[None, :])
    mixed_a_p, k_last = _swa_prompt(proj_p, rel_bias, sinks_even[0], qn2, kn2, bucket_p, batch=batch, seq=seq)
    mixed_b_p, gla_p = _gla_prompt(proj_p, la_p, gla_g, batch=batch, seq=seq, rows=256)
    g_o = norm_g_odd[0].reshape(1, D_MODEL)
    y1_p, h1_p = _outproj([mixed_a_p, mixed_b_p], [w_out_a, w_out_b], xp, gate_ep, (scale_op, shift_op, g_o),
                          tm=512, rows_per_batch=seq)

    w_buf = cache_swa_k.shape[2]
    to_fp = lambda c: jnp.transpose(c[0], (0, 2, 3, 1)).reshape(nseq, W_K, w_buf)
    from_fp = lambda c: jnp.transpose(c.reshape(c.shape[0], H_A_KV, HD_A, -1), (0, 3, 1, 2))[None]
    proj_p3 = proj_p.reshape(batch, seq, EV_N)
    swa_k_p = from_fp(k_last)
    swa_v_p = proj_p3[:, seq - WINDOW:, EV_KV + W_K: EV_KV + 2 * W_K].astype(F32).reshape(
        1, batch, WINDOW, H_A_KV, HD_A)

    rr = np.arange(H_A * SAMPLE_PAD)
    row_head = 8 * (rr // 64) + 2 * ((rr % 32) // 8) + (rr % 64) // 32
    tt = jnp.asarray(rr % SAMPLE_PAD)
    jj = jnp.arange(WINDOW)
    live = (tt < ntok)[:, None]
    bkc = jnp.where(live, _bucket_or_masked(WINDOW + tt[:, None] - jj[None, :]), -1)
    bkn = jnp.where(live & (jj[None, :] < ntok), _bucket_or_masked(tt[:, None] - jj[None, :]), -1)
    relrows = rel_bias.T[row_head]
    sinkrows = jnp.broadcast_to(sinks_even[0][row_head][:, None], (H_A * SAMPLE_PAD, 128))
    mixed_a_s, k_cache_s, v_cache_s = _swa_sample(
        proj_s, to_fp(cache_swa_k), to_fp(cache_swa_v),
        bkc, bkn, relrows, sinkrows, qn2, kn2, sb=16, ntok=ntok, unroll=4)
    mixed_b_s, gla_s = _gla_sample(proj_s, la_s, gla_g, state_gla[0], sb=8, ntok=ntok, unroll=4)
    y1_s, h1_s = _outproj([mixed_a_s, mixed_b_s], [w_out_a, w_out_b], xs, gate_es, (scale_os, shift_os, g_o),
                          tm=256, rows_per_batch=ms)

    w_o = w_in_odd[0].astype(BF16)
    ret_g = ret_g_odd[0].reshape(1, DV_C)
    w_out_o = w_out_odd[0].astype(BF16)
    cos_p, sin_p = _rotary_tables(jnp.arange(seq))
    pos_s = PAST_LEN + jnp.minimum(jnp.arange(SAMPLE_PAD), ntok - 1)
    cos_s, sin_s = _rotary_tables(jnp.tile(pos_s, nseq))

    (projo_s,) = _inproj(h1_s, w_o, (cos_s, sin_s), tm=512, tn=H_C * DK_C, out_dtype=F32, name="inproj_odd")
    ret_job = _ret_sample_job(projo_s, ret_g, state_ret[0], ntok=ntok)
    projo_p, o_s, ret_s = _inproj(h1_p, w_o, (cos_p, sin_p), ret_job, tm=256, tn=H_C * DK_C, out_dtype=BF16,
                                  name="inproj_odd_ret_sample")
    o_p, ret_p = _ret_prompt(projo_p, ret_g, batch=batch, seq=seq)
    (y2_p,) = _outproj([o_p], [w_out_o], y1_p, gate_op, tm=256, rows_per_batch=seq)
    (y2_s,) = _outproj([o_s], [w_out_o], y1_s, gate_os, tm=256, rows_per_batch=ms)

    y_prompt = y2_p.reshape(batch, seq, D_MODEL)
    y_sample = y2_s.reshape(nseq, SAMPLE_PAD, D_MODEL)[:, :ntok]
    return (y_prompt, y_sample, swa_k_p, swa_v_p, gla_p[None], ret_p[None],
            from_fp(k_cache_s), from_fp(v_cache_s),
            gla_s[None], ret_s[None])
```

```python
import functools
import math
from typing import Callable, NamedTuple

import numpy as np
import jax
import jax.numpy as jnp
from jax import lax
from jax.experimental import pallas as pl
from jax.experimental.pallas import tpu as pltpu

F32 = jnp.float32
BF16 = jnp.bfloat16

D_MODEL = 2048
WINDOW = 128
HD_A = 64
H_A = 16
H_A_KV = 2
G_A = 8
W_A = 1024
NUM_BUCKETS = 32
MAX_DISTANCE = 128
H_B = 4
DV_B = 256
DK_B = 128
W_B = 1024
GLA_RANK = 16
GLA_TAU = 16.0
GLA_CHUNK = 64
H_C = 8
DK_C = 256
DV_C = 512
W_C = 4096
RET_CHUNK = 128
RET_PROMPT_CHUNK = 256
ROPE_BASE = 10000.0
EPS = 1e-6
NEG_INF = -1e30
PAST_LEN = 8192

EV_QA, EV_GA, EV_VB, EV_GB, EV_QB, EV_KB, EV_KV = 0, 1024, 2048, 3072, 4096, 4608, 5120
EV_N = 5376
OD_N = 12288
W_K = H_A_KV * HD_A
SAMPLE_PAD = 8
MXU_N = 256
VMEM_LIMIT = 56 * 1024 * 1024

LOG_GAMMA = [float(np.log1p(-np.exp2(np.float32(-5.0 - h)))) for h in range(H_C)]


def _cparams(sem):
    return pltpu.CompilerParams(dimension_semantics=sem, vmem_limit_bytes=VMEM_LIMIT)


def _silu(x):
    t = 0.5 * x
    return t * (1.0 + jnp.tanh(t))


def _dot(a, b):
    return jnp.dot(a, b, preferred_element_type=F32)


def _dot_nt(a, b):
    return lax.dot_general(a, b, (((1,), (1,)), ((), ())), preferred_element_type=F32)


def _dot_tn(a, b):
    return lax.dot_general(a, b, (((0,), (0,)), ((), ())), preferred_element_type=F32)


def _cumsum_rows(x):
    c = x.shape[0]
    r = lax.broadcasted_iota(jnp.int32, (c, c), 0)
    s = lax.broadcasted_iota(jnp.int32, (c, c), 1)
    tri = (r >= s).astype(BF16)
    hi = x.astype(BF16)
    r1 = x - hi.astype(F32)
    mid = r1.astype(BF16)
    lo = (r1 - mid.astype(F32)).astype(BF16)
    return _dot(tri, hi) + _dot(tri, mid) + _dot(tri, lo)


def _row_to_col(r):
    n = r.shape[1]
    ri = lax.broadcasted_iota(jnp.int32, (n, n), 0)
    ci = lax.broadcasted_iota(jnp.int32, (n, n), 1)
    return jnp.sum(jnp.where(ri == ci, jnp.broadcast_to(r, (n, n)), 0.0), axis=1, keepdims=True)


def _ada_kernel(c_ref, w_ref, b_ref, o_ref):
    sc = _silu(c_ref[...]).astype(BF16)
    o_ref[...] = _dot(sc, w_ref[...].astype(BF16)) + b_ref[...]


def _ada_mod(c_all, w, b):
    m = c_all.shape[0]
    n = w.shape[1]
    tn = 768
    return pl.pallas_call(
        _ada_kernel,
        grid=(n // tn,),
        in_specs=[pl.BlockSpec((m, D_MODEL), lambda j: (0, 0)),
                  pl.BlockSpec((D_MODEL, tn), lambda j: (0, j)),
                  pl.BlockSpec((1, tn), lambda j: (0, j))],
        out_specs=pl.BlockSpec((m, tn), lambda j: (0, j)),
        out_shape=jax.ShapeDtypeStruct((m, n), F32),
        compiler_params=_cparams(("arbitrary",)),
        name="ada_mod",
    )(c_all, w, b.reshape(1, n))


def _norm_mod(x, g, scale, shift):
    ms = jnp.mean(x * x, axis=-1, keepdims=True)
    y = x * lax.rsqrt(ms + EPS) * g
    return y * (1.0 + scale) + shift


def _log_sigmoid(z):
    return jnp.minimum(z, 0.0) - jnp.log(1.0 + jnp.exp(-jnp.abs(z)))


def _mod_spec(arr, tm, rows_per_batch):
    if arr.ndim == 3:
        return pl.BlockSpec((None, 1, D_MODEL), lambda i: (i // (rows_per_batch // tm), 0, 0))
    return pl.BlockSpec((tm, D_MODEL), lambda i: (i, 0))


def _prenorm_even_kernel(x_ref, scale_ref, shift_ref, g_ref, wlri_ref, wlr_ref, blr_ref, h_ref, la_ref):
    hb = _norm_mod(x_ref[...], g_ref[...], scale_ref[...], shift_ref[...]).astype(BF16)
    h_ref[...] = hb
    lr = _dot_nt(hb, wlri_ref[...])
    z = _dot(lr.astype(BF16), wlr_ref[...]) + blr_ref[...]
    la_ref[...] = _log_sigmoid(z) / GLA_TAU


def _prenorm_even(x, scale, shift, g, wlri, wlr, blr, *, tm, rows_per_batch):
    m = x.shape[0]
    const = lambda shape: pl.BlockSpec(shape, lambda i: (0, 0))
    return pl.pallas_call(
        _prenorm_even_kernel,
        grid=(m // tm,),
        in_specs=[pl.BlockSpec((tm, D_MODEL), lambda i: (i, 0)),
                  _mod_spec(scale, tm, rows_per_batch), _mod_spec(shift, tm, rows_per_batch),
                  const((1, D_MODEL)), const((128, D_MODEL)), const((128, H_B * DK_B)), const((1, H_B * DK_B))],
        out_specs=[pl.BlockSpec((tm, D_MODEL), lambda i: (i, 0)),
                   pl.BlockSpec((tm, H_B * DK_B), lambda i: (i, 0))],
        out_shape=[jax.ShapeDtypeStruct((m, D_MODEL), BF16),
                   jax.ShapeDtypeStruct((m, H_B * DK_B), F32)],
        compiler_params=_cparams(("arbitrary",)),
        name="prenorm_even",
    )(x, scale, shift, g, wlri, wlr, blr)


class _SideJob(NamedTuple):
    njobs: int
    args: list
    in_specs: Callable
    out_specs: Callable
    out_shape: list
    units: int
    body: Callable


def _inproj_kernel(*refs, rot_tiles, side, nsteps):
    nmain = 4 if rot_tiles else 2
    j = pl.program_id(0)
    run_side = None
    if side is not None:
        njobs, nsi, nso, body = side
        side_in = refs[nmain:nmain + nsi]
        side_out = refs[nmain + nsi + 1:nmain + nsi + 1 + nso]
        step = j * pl.num_programs(1) + pl.program_id(1)
        block = (step * njobs) // nsteps
        prev_block = ((step - 1) * njobs) // nsteps
        run_side = (step == 0) | (block != prev_block)
        refs = refs[:nmain] + (refs[nmain + nsi],)
    h_ref, w_ref, *rest = refs
    o_ref = rest[-1]
    nchunk = o_ref.shape[1] // MXU_N
    half = DK_C // 2

    def plain_chunks():
        def chunk(c):
            cs = slice(c * MXU_N, (c + 1) * MXU_N)
            o_ref[:, cs] = _dot(h_ref[...], w_ref[:, cs]).astype(o_ref.dtype)
        return chunk

    def rotary_chunks():
        mult = jnp.where(j < rot_tiles // 2, 1.0, DK_C ** -0.5)
        cos = rest[0][...] * mult
        sin = rest[1][...] * mult

        def chunk(c):
            acc = _dot(h_ref[...], w_ref[:, c * DK_C:(c + 1) * DK_C])
            x1 = acc[:, :half]
            x2 = acc[:, half:]
            o_ref[:, c * DK_C:c * DK_C + half] = (x1 * cos - x2 * sin).astype(o_ref.dtype)
            o_ref[:, c * DK_C + half:(c + 1) * DK_C] = (x2 * cos + x1 * sin).astype(o_ref.dtype)
        return chunk

    def tile(make_chunk, with_side):
        def run():
            chunk = make_chunk()
            if with_side:
                body(side_in, side_out, chunk)
            else:
                for c in range(nchunk):
                    chunk(c)
        return run

    kinds = [(plain_chunks, j >= rot_tiles)]
    if rot_tiles:
        kinds.append((rotary_chunks, j < rot_tiles))
    for make_chunk, is_kind in kinds:
        if run_side is None:
            pl.when(is_kind)(tile(make_chunk, False))
        else:
            pl.when(is_kind & run_side)(tile(make_chunk, True))
            pl.when(is_kind & jnp.logical_not(run_side))(tile(make_chunk, False))


def _inproj(h, w, rot=None, side=None, *, tm, tn, out_dtype, name):
    m = h.shape[0]
    n = w.shape[1]
    ni = m // tm
    nsteps = (n // tn) * ni
    in_specs = [pl.BlockSpec((tm, D_MODEL), lambda j, i: (i, 0)),
                pl.BlockSpec((D_MODEL, tn), lambda j, i: (0, j))]
    args = [h, w]
    rot_tiles = 0
    if rot is not None:
        nblk = rot[0].shape[0] // tm
        in_specs += [pl.BlockSpec((tm, DK_C // 2), lambda j, i: (i % nblk, 0))] * 2
        args += list(rot)
        rot_tiles = 2 * (H_C * DK_C) // tn
    out_specs = [pl.BlockSpec((tm, tn), lambda j, i: (i, j))]
    out_shape = [jax.ShapeDtypeStruct((m, n), out_dtype)]
    side_static = None
    if side is not None:
        assert side.njobs <= nsteps and side.units == tn // MXU_N
        block = lambda j, i: ((j * ni + i) * side.njobs) // nsteps
        in_specs += side.in_specs(block)
        args += side.args
        out_specs += side.out_specs(block)
        out_shape += side.out_shape
        side_static = (side.njobs, len(side.args), len(side.out_shape), side.body)
    return pl.pallas_call(
        functools.partial(_inproj_kernel, rot_tiles=rot_tiles, side=side_static, nsteps=nsteps),
        grid=(n // tn, ni),
        in_specs=in_specs,
        out_specs=out_specs,
        out_shape=out_shape,
        compiler_params=_cparams(("arbitrary", "arbitrary")),
        name=name,
    )(*args)


def _outproj_kernel(*refs, nin, with_next):
    m_refs, w_refs = refs[:nin], refs[nin:2 * nin]
    if with_next:
        x_ref, gate_ref, nscale_ref, nshift_ref, ng_ref, o_ref, h_ref = refs[2 * nin:]
    else:
        x_ref, gate_ref, o_ref = refs[2 * nin:]
    for c in range(o_ref.shape[1] // MXU_N):
        cs = slice(c * MXU_N, (c + 1) * MXU_N)
        acc = _dot(m_refs[0][...].astype(BF16), w_refs[0][:, cs])
        for m_ref, w_ref in zip(m_refs[1:], w_refs[1:]):
            acc += _dot(m_ref[...].astype(BF16), w_ref[:, cs])
        o_ref[:, cs] = x_ref[:, cs] + gate_ref[:, cs] * acc
    if with_next:
        h_ref[...] = _norm_mod(o_ref[...], ng_ref[...], nscale_ref[...], nshift_ref[...]).astype(BF16)


def _outproj(mixed, ws, x, gate, next_mod=None, *, tm, rows_per_batch):
    m = x.shape[0]
    row = pl.BlockSpec((tm, D_MODEL), lambda i: (i, 0))
    in_specs = ([pl.BlockSpec((tm, a.shape[1]), lambda i: (i, 0)) for a in mixed]
                + [pl.BlockSpec(w.shape, lambda i: (0, 0), pipeline_mode=pl.Buffered(1)) for w in ws]
                + [row, _mod_spec(gate, tm, rows_per_batch)])
    args = [*mixed, *ws, x, gate]
    out_specs = [row]
    out_shape = [jax.ShapeDtypeStruct((m, D_MODEL), F32)]
    if next_mod is not None:
        nscale, nshift, ng = next_mod
        in_specs += [_mod_spec(nscale, tm, rows_per_batch), _mod_spec(nshift, tm, rows_per_batch),
                     pl.BlockSpec((1, D_MODEL), lambda i: (0, 0))]
        args += [nscale, nshift, ng]
        out_specs.append(row)
        out_shape.append(jax.ShapeDtypeStruct((m, D_MODEL), BF16))
    return pl.pallas_call(
        functools.partial(_outproj_kernel, nin=len(mixed), with_next=next_mod is not None),
        grid=(m // tm,),
        in_specs=in_specs,
        out_specs=out_specs,
        out_shape=out_shape,
        compiler_params=_cparams(("arbitrary",)),
        name="outproj",
    )(*args)


def _seg_rms(x, g2, seg_ones):
    x2 = x * x
    hi = x2.astype(BF16)
    lo = (x2 - hi.astype(F32)).astype(BF16)
    ss = _dot(hi, seg_ones) + _dot(lo, seg_ones)
    return x * lax.rsqrt(ss * (1.0 / HD_A) + EPS) * g2


def _dup_halves(x, lo_half):
    sw = pltpu.roll(x, HD_A, 1)
    return [jnp.where(lo_half, x, sw), jnp.where(lo_half, sw, x)]


def _rows_to_cols(x):
    pad = jnp.zeros((x.shape[1] - x.shape[0], x.shape[1]), x.dtype)
    return jnp.concatenate([x, pad], axis=0).T


def _seg_ones():
    r = lax.broadcasted_iota(jnp.int32, (2 * HD_A, 2 * HD_A), 0)
    c = lax.broadcasted_iota(jnp.int32, (2 * HD_A, 2 * HD_A), 1)
    return ((r < HD_A) == (c < HD_A)).astype(BF16)


def _swa_prompt_kernel(relb_ref, sinks_ref, bucket_ref, q_ref, ga_ref, kv_ref, qn_ref, kn_ref,
                       o_ref, knew_ref, bias_ref, kprev_ref, vprev_ref, s_ref, pe_ref):
    b = pl.program_id(0)
    i = pl.program_id(1)

    @pl.when((b == 0) & (i == 0))
    def _():
        bk = bucket_ref[...]
        for h in range(H_A):
            bias_ref[h] = jnp.full((WINDOW, 2 * WINDOW), NEG_INF, F32)
        for bb in range(NUM_BUCKETS):
            hit = bk == bb
            for h in range(H_A):
                bias_ref[h] = jnp.where(hit, relb_ref[bb, h], bias_ref[h])
        col = lax.broadcasted_iota(jnp.int32, (WINDOW, 2 * WINDOW), 1)
        for h in range(H_A):
            bias_ref[H_A + h] = jnp.where(col >= WINDOW, bias_ref[h], NEG_INF)

    @pl.when(i == 0)
    def _():
        kprev_ref[...] = jnp.zeros(kprev_ref.shape, BF16)
        vprev_ref[...] = jnp.zeros(vprev_ref.shape, BF16)

    seg_ones = _seg_ones()
    lt = 2 * HD_A
    tpg = G_A // 2
    lo_half = lax.broadcasted_iota(jnp.int32, (WINDOW, lt), 1) < HD_A
    kv = kv_ref[...].astype(F32)
    kc = _seg_rms(kv[:, :W_K], kn_ref[...], seg_ones)
    k_dup = _dup_halves(kc, lo_half)
    v_dup = _dup_halves(kv[:, W_K:], lo_half)
    base = jnp.where(i == 0, H_A, 0)
    qn2 = qn_ref[...] * HD_A ** -0.5
    gq = tpg * WINDOW
    lo_g = lax.broadcasted_iota(jnp.int32, (gq, lt), 1) < HD_A
    grows = [slice(g * 2 * gq, (g + 1) * 2 * gq) for g in range(H_A_KV)]
    for g in range(H_A_KV):
        kk = jnp.concatenate([kprev_ref[g], k_dup[g].astype(BF16)], axis=0)
        xg = jnp.concatenate([q_ref[:, (g * tpg + p) * lt:(g * tpg + p + 1) * lt] for p in range(tpg)],
                             axis=0).astype(F32)
        xn = _seg_rms(xg, qn2, seg_ones)
        q_all = jnp.concatenate([jnp.where(lo_g, xn, 0.0), jnp.where(lo_g, 0.0, xn)], axis=0).astype(BF16)
        s_ref[grows[g], :] = _dot_nt(q_all, kk)
    sink_terms = []
    for g in range(H_A_KV):
        for n in range(G_A):
            h = g * G_A + 2 * (n % tpg) + n // tpg
            rows = slice((g * G_A + n) * WINDOW, (g * G_A + n + 1) * WINDOW)
            s = s_ref[rows, :] + bias_ref[base + h]
            sink = sinks_ref[h]
            m = jnp.maximum(jnp.max(s, axis=-1, keepdims=True), sink)
            pe_ref[rows, :] = jnp.exp(s - m).astype(BF16)
            sink_terms.append(jnp.exp(sink - m))
    o_ext = []
    for g in range(H_A_KV):
        vv = jnp.concatenate([vprev_ref[g], v_dup[g].astype(BF16)], axis=0)
        vv_ext = jnp.concatenate([vv, jnp.ones(vv.shape, BF16)], axis=1)
        o_ext.append(_dot(pe_ref[grows[g], :], vv_ext))
    for g in range(H_A_KV):
        for p in range(tpg):
            halves = []
            for a in range(2):
                n = a * tpg + p
                rows = slice(n * WINDOW, (n + 1) * WINDOW)
                halves.append(o_ext[g][rows, :lt] / (o_ext[g][rows, lt:] + sink_terms[g * G_A + n]))
            oa = jnp.where(lo_half, halves[0], halves[1])
            cols = slice((g * tpg + p) * lt, (g * tpg + p + 1) * lt)
            o_ref[:, cols] = (oa * _silu(ga_ref[:, cols].astype(F32))).astype(o_ref.dtype)
    for g in range(H_A_KV):
        kprev_ref[g] = k_dup[g].astype(BF16)
        vprev_ref[g] = v_dup[g].astype(BF16)

    @pl.when(i == pl.num_programs(1) - 1)
    def _():
        knew_ref[...] = kc.T


def _swa_prompt(proj, rel_bias, sinks, qn2, kn2, bucket, *, batch, seq):
    nb = seq // WINDOW
    rb = lambda b, i: b * nb + i
    smem = pl.BlockSpec(memory_space=pltpu.SMEM)
    return pl.pallas_call(
        _swa_prompt_kernel,
        grid=(batch, nb),
        in_specs=[smem, smem,
                  pl.BlockSpec((WINDOW, 2 * WINDOW), lambda b, i: (0, 0)),
                  pl.BlockSpec((WINDOW, W_A), lambda b, i: (rb(b, i), EV_QA // W_A)),
                  pl.BlockSpec((WINDOW, W_A), lambda b, i: (rb(b, i), EV_GA // W_A)),
                  pl.BlockSpec((WINDOW, 2 * W_K), lambda b, i: (rb(b, i), EV_KV // (2 * W_K))),
                  pl.BlockSpec((1, 2 * HD_A), lambda b, i: (0, 0)),
                  pl.BlockSpec((1, 2 * HD_A), lambda b, i: (0, 0))],
        out_specs=[pl.BlockSpec((WINDOW, W_A), lambda b, i: (rb(b, i), 0)),
                   pl.BlockSpec((None, WINDOW, W_K), lambda b, i: (b, 0, 0))],
        out_shape=[jax.ShapeDtypeStruct((batch * seq, W_A), BF16),
                   jax.ShapeDtypeStruct((batch, WINDOW, W_K), F32)],
        scratch_shapes=[pltpu.VMEM((2 * H_A, WINDOW, 2 * WINDOW), F32),
                        pltpu.VMEM((H_A_KV, WINDOW, W_K), BF16), pltpu.VMEM((H_A_KV, WINDOW, W_K), BF16),
                        pltpu.VMEM((H_A * WINDOW, 2 * WINDOW), F32), pltpu.VMEM((H_A * WINDOW, 2 * WINDOW), BF16)],
        compiler_params=_cparams(("arbitrary", "arbitrary")),
        name="swa_prompt",
    )(rel_bias, sinks, bucket, proj, proj, proj, qn2, kn2)


def _swa_sample_kernel(bkc_ref, bkn_ref, relrows_ref, sinkrows_ref, q_ref, ga_ref, kvn_ref, ck_ref, cv_ref,
                       qn_ref, kn_ref, o_ref, ko_ref, vo_ref, biasc_ref, biasn_ref, *, sb, ntok, unroll):
    @pl.when(pl.program_id(0) == 0)
    def _():
        bkc = bkc_ref[...]
        bkn = bkn_ref[...]
        rr = relrows_ref[...]
        bc = jnp.full(bkc.shape, NEG_INF, F32)
        bn = jnp.full(bkn.shape, NEG_INF, F32)
        for bb in range(NUM_BUCKETS):
            val = rr[:, bb:bb + 1]
            bc = jnp.where(bkc == bb, val, bc)
            bn = jnp.where(bkn == bb, val, bn)
        biasc_ref[...] = bc
        biasn_ref[...] = bn

    seg_ones = _seg_ones()
    lt = 2 * HD_A
    nrow = G_A * SAMPLE_PAD
    grow = nrow // H_A_KV
    lo_q = lax.broadcasted_iota(jnp.int32, (grow, lt), 1) < HD_A
    lo_n = lax.broadcasted_iota(jnp.int32, (SAMPLE_PAD, lt), 1) < HD_A
    lane_pos = lax.broadcasted_iota(jnp.int32, (W_K, WINDOW), 1)
    qn2 = qn_ref[...] * HD_A ** -0.5
    kn2 = kn_ref[...]
    sink = sinkrows_ref[:, 0:1]

    def body(it, carry):
        seqs = [it * unroll + u for u in range(unroll)]
        rows = [pl.ds(pl.multiple_of(s * SAMPLE_PAD, SAMPLE_PAD), SAMPLE_PAD) for s in seqs]
        kvn = [kvn_ref[r, :] for r in rows]
        kn_all = _seg_rms(jnp.concatenate([x[:, :W_K] for x in kvn], axis=0), kn2, seg_ones)
        q2_all = jnp.concatenate([q_ref[r, p * lt:(p + 1) * lt] for r in rows for p in range(G_A)], axis=0)
        xn_all = _seg_rms(q2_all, qn2, seg_ones)
        sc, sn, kn = [], [], []
        for u, s in enumerate(seqs):
            kn.append(kn_all[u * SAMPLE_PAD:(u + 1) * SAMPLE_PAD])
            kn_dup = _dup_halves(kn[u], lo_n)
            scg, sng = [], []
            for g in range(H_A_KV):
                xn = xn_all[u * nrow + g * grow:u * nrow + (g + 1) * grow]
                q4 = jnp.concatenate([jnp.where(lo_q, xn, 0.0), jnp.where(lo_q, 0.0, xn)], axis=0).astype(BF16)
                kt_g = ck_ref[s, g * HD_A:(g + 1) * HD_A, :].astype(BF16)
                scg.append(_dot(q4, jnp.concatenate([kt_g, kt_g], axis=0)))
                sng.append(_dot_nt(q4, kn_dup[g].astype(BF16)))
            sc.append(jnp.concatenate(scg, axis=0))
            sn.append(jnp.concatenate(sng, axis=0))
        pc, pn, sink_terms = [], [], []
        for u in range(unroll):
            scu = sc[u] + biasc_ref[...]
            snu = sn[u] + biasn_ref[:, :SAMPLE_PAD]
            m = jnp.maximum(jnp.maximum(jnp.max(scu, axis=-1, keepdims=True),
                                        jnp.max(snu, axis=-1, keepdims=True)), sink)
            pc.append(jnp.exp(scu - m).astype(BF16))
            pn.append(jnp.exp(snu - m).astype(BF16))
            sink_terms.append(jnp.exp(sink - m))
        o_ext = []
        for u, s in enumerate(seqs):
            vn_dup = _dup_halves(kvn[u][:, W_K:], lo_n)
            og = []
            for g in range(H_A_KV):
                vt_g = cv_ref[s, g * HD_A:(g + 1) * HD_A, :].astype(BF16)
                vt_ext = jnp.concatenate([vt_g, vt_g, jnp.ones((lt, WINDOW), BF16)], axis=0)
                vn_ext = jnp.concatenate([vn_dup[g].astype(BF16), jnp.ones((SAMPLE_PAD, lt), BF16)], axis=1)
                gr = slice(g * 2 * grow, (g + 1) * 2 * grow)
                og.append(_dot_nt(pc[u][gr], vt_ext) + _dot(pn[u][gr], vn_ext))
            o_ext.append(jnp.concatenate(og, axis=0))
        for u, s in enumerate(seqs):
            o4 = o_ext[u][:, :lt] / (o_ext[u][:, lt:] + sink_terms[u])
            xg = ga_ref[rows[u], :]
            for p in range(G_A):
                g, pp = divmod(p, G_A // 2)
                r0 = g * 2 * grow + pp * SAMPLE_PAD
                o2 = jnp.where(lo_n, o4[r0:r0 + SAMPLE_PAD], o4[r0 + grow:r0 + grow + SAMPLE_PAD])
                o_ref[rows[u], p * lt:(p + 1) * lt] = (o2 * _silu(xg[:, p * lt:(p + 1) * lt])).astype(o_ref.dtype)
            ko_ref[s] = pltpu.roll(jnp.where(lane_pos < ntok, _rows_to_cols(kn[u]), ck_ref[s]), WINDOW - ntok, 1)
            vo_ref[s] = pltpu.roll(jnp.where(lane_pos < ntok, _rows_to_cols(kvn[u][:, W_K:]), cv_ref[s]),
                                   WINDOW - ntok, 1)
        return carry

    lax.fori_loop(0, sb // unroll, body, 0)


def _swa_sample(proj, cache_k, cache_v, bkc, bkn, relrows, sinkrows, qn2, kn2, *, sb, ntok, unroll):
    nseq = cache_k.shape[0]
    rows = sb * SAMPLE_PAD
    full = lambda shape: pl.BlockSpec(shape, lambda i: tuple(0 for _ in shape))
    cache = pl.BlockSpec((sb, W_K, WINDOW), lambda i: (i, 0, 0))
    return pl.pallas_call(
        functools.partial(_swa_sample_kernel, sb=sb, ntok=ntok, unroll=unroll),
        grid=(nseq // sb,),
        in_specs=[full(bkc.shape), full(bkn.shape), full(relrows.shape), full(sinkrows.shape),
                  pl.BlockSpec((rows, W_A), lambda i: (i, EV_QA // W_A)),
                  pl.BlockSpec((rows, W_A), lambda i: (i, EV_GA // W_A)),
                  pl.BlockSpec((rows, 2 * W_K), lambda i: (i, EV_KV // (2 * W_K))),
                  cache, cache, full((1, 2 * HD_A)), full((1, 2 * HD_A))],
        out_specs=[pl.BlockSpec((rows, W_A), lambda i: (i, 0)), cache, cache],
        out_shape=[jax.ShapeDtypeStruct((nseq * SAMPLE_PAD, W_A), F32),
                   jax.ShapeDtypeStruct(cache_k.shape, F32),
                   jax.ShapeDtypeStruct(cache_v.shape, F32)],
        scratch_shapes=[pltpu.VMEM(bkc.shape, F32), pltpu.VMEM(bkn.shape, F32)],
        compiler_params=_cparams(("arbitrary",)),
        name="swa_sample",
    )(bkc, bkn, relrows, sinkrows, proj, proj, proj, cache_k, cache_v, qn2, kn2)


def _gla_pre(q, k, v, la, n_valid):
    c = q.shape[0]
    bcum = _cumsum_rows(la)
    rr = lax.broadcasted_iota(jnp.int32, (c, c), 0)
    cc = lax.broadcasted_iota(jnp.int32, (c, c), 1)
    causal = rr >= cc
    row = lax.broadcasted_iota(jnp.int32, (c, 1), 0)
    qts, kts, kds, vbs, dcols = [], [], [], [], []
    for h in range(H_B):
        ks = slice(h * DK_B, (h + 1) * DK_B)
        bc = bcum[:, ks]
        qts.append(((q[:, ks] * DK_B ** -0.5) * jnp.exp(bc)).astype(BF16))
        kts.append((k[:, ks] * jnp.exp(-bc)).astype(BF16))
        blast = bc[n_valid - 1:n_valid, :]
        kd = k[:, ks] * jnp.exp(blast - bc)
        if n_valid < c:
            kd = jnp.where(row < n_valid, kd, 0.0)
        kds.append(kd.astype(BF16))
        vbs.append(v[:, h * DV_B:(h + 1) * DV_B].astype(BF16))
        dcols.append(_row_to_col(jnp.exp(blast)))
    scores = [_dot_nt(qts[h], kts[h]) for h in range(H_B)]
    upds = [_dot_tn(kds[h], vbs[h]) for h in range(H_B)]
    return [(qts[h], jnp.where(causal, scores[h], 0.0).astype(BF16), vbs[h], upds[h], dcols[h])
            for h in range(H_B)]


def _gla_post(pre, gb, gla_g, states):
    outs = [_dot(a, vb) + _dot(qt, st.astype(BF16)) for (qt, a, vb, _, _), st in zip(pre, states)]
    new_states = [dcol * st + upd for (_, _, _, upd, dcol), st in zip(pre, states)]
    gated = []
    for h, o in enumerate(outs):
        on = o * lax.rsqrt(jnp.mean(o * o, axis=-1, keepdims=True) + EPS) * gla_g
        gated.append(on * _silu(gb[:, h * DV_B:(h + 1) * DV_B]))
    return jnp.concatenate(gated, axis=1), new_states


def _gla_prompt_kernel(q_ref, k_ref, v_ref, gb_ref, la_ref, g_ref, o_ref, s_ref, *, nchunk):
    @pl.when(pl.program_id(1) == 0)
    def _():
        s_ref[...] = jnp.zeros(s_ref.shape, F32)

    gla_g = g_ref[...]
    chunk_rows = [slice(c * GLA_CHUNK, (c + 1) * GLA_CHUNK) for c in range(nchunk)]
    pres = [_gla_pre(q_ref[r, :].astype(F32), k_ref[r, :].astype(F32), v_ref[r, :].astype(F32), la_ref[r, :],
                     GLA_CHUNK) for r in chunk_rows]
    states = [s_ref[h] for h in range(H_B)]
    for r, pre in zip(chunk_rows, pres):
        out, states = _gla_post(pre, gb_ref[r, :].astype(F32), gla_g, states)
        o_ref[r, :] = out.astype(o_ref.dtype)
    for h in range(H_B):
        s_ref[h] = states[h]


def _gla_prompt(proj, la, gla_g, *, batch, seq, rows):
    nstep = seq // rows
    rb = lambda b, i: b * nstep + i
    return pl.pallas_call(
        functools.partial(_gla_prompt_kernel, nchunk=rows // GLA_CHUNK),
        grid=(batch, nstep),
        in_specs=[pl.BlockSpec((rows, H_B * DK_B), lambda b, i: (rb(b, i), EV_QB // (H_B * DK_B))),
                  pl.BlockSpec((rows, H_B * DK_B), lambda b, i: (rb(b, i), EV_KB // (H_B * DK_B))),
                  pl.BlockSpec((rows, W_B), lambda b, i: (rb(b, i), EV_VB // W_B)),
                  pl.BlockSpec((rows, W_B), lambda b, i: (rb(b, i), EV_GB // W_B)),
                  pl.BlockSpec((rows, H_B * DK_B), lambda b, i: (rb(b, i), 0)),
                  pl.BlockSpec((1, DV_B), lambda b, i: (0, 0))],
        out_specs=[pl.BlockSpec((rows, W_B), lambda b, i: (rb(b, i), 0)),
                   pl.BlockSpec((None, H_B, DK_B, DV_B), lambda b, i: (b, 0, 0, 0))],
        out_shape=[jax.ShapeDtypeStruct((batch * seq, W_B), BF16),
                   jax.ShapeDtypeStruct((batch, H_B, DK_B, DV_B), F32)],
        compiler_params=_cparams(("arbitrary", "arbitrary")),
        name="gla_prompt",
    )(proj, proj, proj, proj, la, gla_g)


def _gla_sample_kernel(q_ref, k_ref, v_ref, gb_ref, la_ref, g_ref, s_in_ref, o_ref, s_ref, *, sb, ntok, unroll):
    gla_g = g_ref[...]

    def body(it, carry):
        seqs = [it * unroll + u for u in range(unroll)]
        rows = [pl.ds(pl.multiple_of(s * SAMPLE_PAD, SAMPLE_PAD), SAMPLE_PAD) for s in seqs]
        pres = [_gla_pre(q_ref[r, :], k_ref[r, :], v_ref[r, :], la_ref[r, :], ntok) for r in rows]
        for s, r, pre in zip(seqs, rows, pres):
            out, new_states = _gla_post(pre, gb_ref[r, :], gla_g, [s_in_ref[s, h] for h in range(H_B)])
            o_ref[r, :] = out.astype(o_ref.dtype)
            for h in range(H_B):
                s_ref[s, h] = new_states[h]
        return carry

    lax.fori_loop(0, sb // unroll, body, 0)


def _gla_sample(proj, la, gla_g, state, *, sb, ntok, unroll):
    nseq = state.shape[0]
    rows = sb * SAMPLE_PAD
    return pl.pallas_call(
        functools.partial(_gla_sample_kernel, sb=sb, ntok=ntok, unroll=unroll),
        grid=(nseq // sb,),
        in_specs=[pl.BlockSpec((rows, H_B * DK_B), lambda i: (i, EV_QB // (H_B * DK_B))),
                  pl.BlockSpec((rows, H_B * DK_B), lambda i: (i, EV_KB // (H_B * DK_B))),
                  pl.BlockSpec((rows, W_B), lambda i: (i, EV_VB // W_B)),
                  pl.BlockSpec((rows, W_B), lambda i: (i, EV_GB // W_B)),
                  pl.BlockSpec((rows, H_B * DK_B), lambda i: (i, 0)),
                  pl.BlockSpec((1, DV_B), lambda i: (0, 0)),
                  pl.BlockSpec((sb, H_B, DK_B, DV_B), lambda i: (i, 0, 0, 0))],
        out_specs=[pl.BlockSpec((rows, W_B), lambda i: (i, 0)),
                   pl.BlockSpec((sb, H_B, DK_B, DV_B), lambda i: (i, 0, 0, 0))],
        out_shape=[jax.ShapeDtypeStruct((nseq * SAMPLE_PAD, W_B), BF16),
                   jax.ShapeDtypeStruct(state.shape, F32)],
        compiler_params=_cparams(("arbitrary",)),
        name="gla_sample",
    )(proj, proj, proj, proj, la, gla_g, state)


def _ret_decay(c, h):
    ri = lax.broadcasted_iota(jnp.int32, (c, c), 0)
    ci = lax.broadcasted_iota(jnp.int32, (c, c), 1)
    dist = (ri - ci).astype(F32)
    return jnp.where(ri >= ci, jnp.exp(jnp.maximum(dist, 0.0) * LOG_GAMMA[h]), 0.0)


def _ret_chunk(q_ref, k_ref, v_ref, g_ref, retg, state_in_ref, state_ref, o_ref, n_valid, after_head=None,
               decay_ref=None):
    c = q_ref.shape[0]
    row = lax.broadcasted_iota(jnp.int32, (c, 1), 0)
    rowf = row.astype(F32)
    a_heads = []
    for h in range(H_C):
        ks = slice(h * DK_C, (h + 1) * DK_C)
        decay = _ret_decay(c, h) if decay_ref is None else decay_ref[h]
        a_heads.append((_dot_nt(q_ref[:, ks].astype(BF16), k_ref[:, ks].astype(BF16)) * decay).astype(BF16))
    for h in range(H_C):
        lg = LOG_GAMMA[h]
        ks = slice(h * DK_C, (h + 1) * DK_C)
        vs = slice(h * DV_C, (h + 1) * DV_C)
        kf = k_ref[:, ks].astype(F32)
        vb = v_ref[:, vs].astype(BF16)
        st = state_in_ref[h]
        q_in = (q_ref[:, ks].astype(F32) * jnp.exp((rowf + 1.0) * lg)).astype(BF16)
        o = _dot(a_heads[h], vb) + _dot(q_in, st.astype(BF16))
        kd = kf * jnp.exp((n_valid - 1.0 - rowf) * lg)
        if n_valid < c:
            kd = jnp.where(row < n_valid, kd, 0.0)
        state_ref[h] = math.exp(n_valid * lg) * st + _dot_tn(kd.astype(BF16), vb)
        on = o * lax.rsqrt(jnp.mean(o * o, axis=-1, keepdims=True) + EPS) * retg
        o_ref[:, vs] = (on * _silu(g_ref[:, vs].astype(F32))).astype(o_ref.dtype)
        if after_head is not None:
            after_head(h)


def _ret_prompt_kernel(q_ref, k_ref, v_ref, g_ref, retg_ref, o_ref, s_ref, decay_ref):
    chunk = q_ref.shape[0]

    @pl.when((pl.program_id(0) == 0) & (pl.program_id(1) == 0))
    def _():
        for h in range(H_C):
            decay_ref[h] = _ret_decay(chunk, h)

    @pl.when(pl.program_id(1) == 0)
    def _():
        s_ref[...] = jnp.zeros(s_ref.shape, F32)

    _ret_chunk(q_ref, k_ref, v_ref, g_ref, retg_ref[...], s_ref, s_ref, o_ref, chunk, decay_ref=decay_ref)


def _ret_prompt(proj, ret_g, *, batch, seq):
    chunk = RET_PROMPT_CHUNK if seq % RET_PROMPT_CHUNK == 0 else RET_CHUNK
    nstep = seq // chunk
    rb = lambda b, i: b * nstep + i
    qk = H_C * DK_C
    return pl.pallas_call(
        _ret_prompt_kernel,
        grid=(batch, nstep),
        in_specs=[pl.BlockSpec((chunk, qk), lambda b, i: (rb(b, i), 0)),
                  pl.BlockSpec((chunk, qk), lambda b, i: (rb(b, i), 1)),
                  pl.BlockSpec((chunk, W_C), lambda b, i: (rb(b, i), 1)),
                  pl.BlockSpec((chunk, W_C), lambda b, i: (rb(b, i), 2)),
                  pl.BlockSpec((1, DV_C), lambda b, i: (0, 0))],
        out_specs=[pl.BlockSpec((chunk, W_C), lambda b, i: (rb(b, i), 0)),
                   pl.BlockSpec((None, H_C, DK_C, DV_C), lambda b, i: (b, 0, 0, 0))],
        out_shape=[jax.ShapeDtypeStruct((batch * seq, W_C), BF16),
                   jax.ShapeDtypeStruct((batch, H_C, DK_C, DV_C), F32)],
        scratch_shapes=[pltpu.VMEM((H_C, chunk, chunk), F32)],
        compiler_params=_cparams(("arbitrary", "arbitrary")),
        name="ret_prompt",
    )(proj, proj, proj, proj, ret_g)


def _ret_sample_job(proj, ret_g, state, *, ntok):
    nseq = state.shape[0]
    qk = H_C * DK_C
    state_spec = lambda blk: pl.BlockSpec((None, H_C, DK_C, DV_C), lambda j, i: (blk(j, i), 0, 0, 0))

    def body(in_refs, out_refs, host_chunk):
        q_ref, k_ref, v_ref, g_ref, retg_ref, s_in_ref = in_refs
        o_ref, s_ref = out_refs
        _ret_chunk(q_ref, k_ref, v_ref, g_ref, retg_ref[...], s_in_ref, s_ref, o_ref, ntok, after_head=host_chunk)

    return _SideJob(
        njobs=nseq,
        units=H_C,
        args=[proj, proj, proj, proj, ret_g, state],
        in_specs=lambda blk: [pl.BlockSpec((SAMPLE_PAD, qk), lambda j, i: (blk(j, i), 0)),
                              pl.BlockSpec((SAMPLE_PAD, qk), lambda j, i: (blk(j, i), 1)),
                              pl.BlockSpec((SAMPLE_PAD, W_C), lambda j, i: (blk(j, i), 1)),
                              pl.BlockSpec((SAMPLE_PAD, W_C), lambda j, i: (blk(j, i), 2)),
                              pl.BlockSpec((1, DV_C), lambda j, i: (0, 0)),
                              state_spec(blk)],
        out_specs=lambda blk: [pl.BlockSpec((SAMPLE_PAD, W_C), lambda j, i: (blk(j, i), 0)), state_spec(blk)],
        out_shape=[jax.ShapeDtypeStruct((nseq * SAMPLE_PAD, W_C), BF16),
                   jax.ShapeDtypeStruct(state.shape, F32)],
        body=body)


def _t5_bucket(dist):
    dist = jnp.maximum(dist, 0)
    max_exact = NUM_BUCKETS // 2
    log_ratio = jnp.log(jnp.maximum(dist, 1).astype(F32) / max_exact) / math.log(MAX_DISTANCE / max_exact)
    large = jnp.minimum(max_exact + (log_ratio * (NUM_BUCKETS - max_exact)).astype(jnp.int32), NUM_BUCKETS - 1)
    return jnp.where(dist < max_exact, dist, large)


def _bucket_or_masked(dist):
    return jnp.where((dist >= 0) & (dist <= WINDOW), _t5_bucket(dist), -1).astype(jnp.int32)


def _rotary_tables(pos):
    half = DK_C // 2
    inv = ROPE_BASE ** (-jnp.arange(half, dtype=F32) / half)
    ang = pos.astype(F32)[:, None] * inv[None, :]
    return jnp.cos(ang), jnp.sin(ang)


def _regroup_kernel(wt_ref, o_ref):
    o_ref[...] = wt_ref[...].T.astype(o_ref.dtype)


def _even_weight_layout(w_in):
    wt = jnp.swapaxes(w_in, 0, 1)
    src = lambda j: jnp.where(j < 4, j, jnp.where(j < 8, j + 1, jnp.where(j < 16, j + 5,
                                                                         jnp.where(j < 20, j - 7, 4))))
    main = pl.pallas_call(
        _regroup_kernel,
        grid=(EV_N // MXU_N,),
        in_specs=[pl.BlockSpec((MXU_N, D_MODEL), lambda j: (src(j), 0))],
        out_specs=pl.BlockSpec((D_MODEL, MXU_N), lambda j: (0, j)),
        out_shape=jax.ShapeDtypeStruct((D_MODEL, EV_N), BF16),
        compiler_params=_cparams(("arbitrary",)),
        name="regroup_even_weights",
    )(wt)
    lr_t = jnp.pad(wt[EV_N:], ((0, 128 - GLA_RANK), (0, 0))).astype(BF16)
    return main, lr_t


def _pad_tokens(a, ntok):
    pad = [(0, 0), (0, SAMPLE_PAD - ntok)] + [(0, 0)] * (a.ndim - 2)
    a = jnp.pad(a, pad)
    return a.reshape((a.shape[0] * SAMPLE_PAD,) + a.shape[2:])


def _rows(v, reps):
    return jnp.repeat(v, reps, axis=0)


def kernel(x_prompt, x_sample, cache_swa_k, cache_swa_v, state_gla, state_ret, c_prompt, c_sample, rel_bias,
           ada_w_even, ada_b_even, norm_g_even, w_in_even, w_lr_even, b_lr_even, qn_g_even, kn_g_even,
           sinks_even, gla_g_even, w_out_even, ada_w_odd, ada_b_odd, norm_g_odd, w_in_odd, ret_g_odd, w_out_odd):
    batch, seq, _ = x_prompt.shape
    nseq, ntok, _ = x_sample.shape
    mp = batch * seq

    c_all = jnp.concatenate([c_prompt, c_sample], axis=0)
    mod_e = _ada_mod(c_all, ada_w_even[0], ada_b_even[0])
    mod_o = _ada_mod(c_all, ada_w_odd[0], ada_b_odd[0])

    def split_mod(mod):
        shift, scale, gate = jnp.split(mod, 3, axis=1)
        p = tuple(a[:batch].reshape(batch, 1, D_MODEL) for a in (shift, scale, gate))
        s = tuple(_rows(a[batch:], SAMPLE_PAD) for a in (shift, scale, gate))
        return p, s

    (shift_ep, scale_ep, gate_ep), (shift_es, scale_es, gate_es) = split_mod(mod_e)
    (shift_op, scale_op, gate_op), (shift_os, scale_os, gate_os) = split_mod(mod_o)

    xp = x_prompt.reshape(mp, D_MODEL)
    xs = _pad_tokens(x_sample, ntok)
    ms = xs.shape[0]

    w_e, w_lri = _even_weight_layout(w_in_even[0])
    w_lr = jnp.pad(w_lr_even[0], ((0, 128 - GLA_RANK), (0, 0))).astype(BF16)
    b_lr = b_lr_even[0].reshape(1, -1)
    g_e = norm_g_even[0].reshape(1, D_MODEL)
    qn2 = jnp.tile(qn_g_even[0].reshape(1, HD_A), (1, 2))
    kn2 = jnp.tile(kn_g_even[0].reshape(1, HD_A), (1, 2))
    gla_g = gla_g_even[0].reshape(1, DV_B)
    w_out_e = w_out_even[0].astype(BF16)
    w_out_a, w_out_b = w_out_e[:W_A], w_out_e[W_A:]

    h0_p, la_p = _prenorm_even(xp, scale_ep, shift_ep, g_e, w_lri, w_lr, b_lr, tm=512, rows_per_batch=seq)
    h0_s, la_s = _prenorm_even(xs, scale_es, shift_es, g_e, w_lri, w_lr, b_lr, tm=512, rows_per_batch=ms)
    (proj_p,) = _inproj(h0_p, w_e, tm=1024, tn=EV_N // 3, out_dtype=BF16, name="inproj_even")
    (proj_s,) = _inproj(h0_s, w_e, tm=512, tn=EV_N // 3, out_dtype=F32, name="inproj_even")

    ii = jnp.arange(WINDOW)
    ss = jnp.arange(2 * WINDOW)
    bucket_p = _bucket_or_masked(WINDOW + ii[:, None] - ss[None, :])
    mixed_a_p, k_last = _swa_prompt(proj_p, rel_bias, sinks_even[0], qn2, kn2, bucket_p, batch=batch, seq=seq)
    mixed_b_p, gla_p = _gla_prompt(proj_p, la_p, gla_g, batch=batch, seq=seq, rows=256)
    g_o = norm_g_odd[0].reshape(1, D_MODEL)
    y1_p, h1_p = _outproj([mixed_a_p, mixed_b_p], [w_out_a, w_out_b], xp, gate_ep, (scale_op, shift_op, g_o),
                          tm=512, rows_per_batch=seq)

    w_buf = cache_swa_k.shape[2]
    to_fp = lambda c: jnp.transpose(c[0], (0, 2, 3, 1)).reshape(nseq, W_K, w_buf)
    from_fp = lambda c: jnp.transpose(c.reshape(c.shape[0], H_A_KV, HD_A, -1), (0, 3, 1, 2))[None]
    proj_p3 = proj_p.reshape(batch, seq, EV_N)
    swa_k_p = from_fp(k_last)
    swa_v_p = proj_p3[:, seq - WINDOW:, EV_KV + W_K: EV_KV + 2 * W_K].astype(F32).reshape(
        1, batch, WINDOW, H_A_KV, HD_A)

    rr = np.arange(H_A * SAMPLE_PAD)
    row_head = 8 * (rr // 64) + 2 * ((rr % 32) // 8) + (rr % 64) // 32
    tt = jnp.asarray(rr % SAMPLE_PAD)
    jj = jnp.arange(WINDOW)
    live = (tt < ntok)[:, None]
    bkc = jnp.where(live, _bucket_or_masked(WINDOW + tt[:, None] - jj[None, :]), -1)
    bkn = jnp.where(live & (jj[None, :] < ntok), _bucket_or_masked(tt[:, None] - jj[None, :]), -1)
    relrows = rel_bias.T[row_head]
    sinkrows = jnp.broadcast_to(sinks_even[0][row_head][:, None], (H_A * SAMPLE_PAD, 128))
    mixed_a_s, k_cache_s, v_cache_s = _swa_sample(
        proj_s, to_fp(cache_swa_k), to_fp(cache_swa_v),
        bkc, bkn, relrows, sinkrows, qn2, kn2, sb=16, ntok=ntok, unroll=4)
    mixed_b_s, gla_s = _gla_sample(proj_s, la_s, gla_g, state_gla[0], sb=8, ntok=ntok, unroll=4)
    y1_s, h1_s = _outproj([mixed_a_s, mixed_b_s], [w_out_a, w_out_b], xs, gate_es, (scale_os, shift_os, g_o),
                          tm=256, rows_per_batch=ms)

    w_o = w_in_odd[0].astype(BF16)
    ret_g = ret_g_odd[0].reshape(1, DV_C)
    w_out_o = w_out_odd[0].astype(BF16)
    cos_p, sin_p = _rotary_tables(jnp.arange(seq))
    pos_s = PAST_LEN + jnp.minimum(jnp.arange(SAMPLE_PAD), ntok - 1)
    cos_s, sin_s = _rotary_tables(jnp.tile(pos_s, nseq))

    (projo_s,) = _inproj(h1_s, w_o, (cos_s, sin_s), tm=512, tn=H_C * DK_C, out_dtype=F32, name="inproj_odd")
    ret_job = _ret_sample_job(projo_s, ret_g, state_ret[0], ntok=ntok)
    projo_p, o_s, ret_s = _inproj(h1_p, w_o, (cos_p, sin_p), ret_job, tm=256, tn=H_C * DK_C, out_dtype=BF16,
                                  name="inproj_odd_ret_sample")
    o_p, ret_p = _ret_prompt(projo_p, ret_g, batch=batch, seq=seq)
    (y2_p,) = _outproj([o_p], [w_out_o], y1_p, gate_op, tm=256, rows_per_batch=seq)
    (y2_s,) = _outproj([o_s], [w_out_o], y1_s, gate_os, tm=256, rows_per_batch=ms)

    y_prompt = y2_p.reshape(batch, seq, D_MODEL)
    y_sample = y2_s.reshape(nseq, SAMPLE_PAD, D_MODEL)[:, :ntok]
    return (y_prompt, y_sample, swa_k_p, swa_v_p, gla_p[None], ret_p[None],
            from_fp(k_cache_s), from_fp(v_cache_s),
            gla_s[None], ret_s[None])
```

```python
import functools
import math
from typing import Callable, NamedTuple

import numpy as np
import jax
import jax.numpy as jnp
from jax import lax
from jax.experimental import pallas as pl
from jax.experimental.pallas import tpu as pltpu

F32 = jnp.float32
BF16 = jnp.bfloat16

D_MODEL = 2048
WINDOW = 128
HD_A = 64
H_A = 16
H_A_KV = 2
G_A = 8
W_A = 1024
NUM_BUCKETS = 32
MAX_DISTANCE = 128
H_B = 4
DV_B = 256
DK_B = 128
W_B = 1024
GLA_RANK = 16
GLA_TAU = 16.0
GLA_CHUNK = 64
H_C = 8
DK_C = 256
DV_C = 512
W_C = 4096
RET_CHUNK = 128
RET_PROMPT_CHUNK = 256
ROPE_BASE = 10000.0
EPS = 1e-6
NEG_INF = -1e30
PAST_LEN = 8192

EV_QA, EV_GA, EV_VB, EV_GB, EV_QB, EV_KB, EV_KV = 0, 1024, 2048, 3072, 4096, 4608, 5120
EV_N = 5376
OD_N = 12288
W_K = H_A_KV * HD_A
SAMPLE_PAD = 8
MXU_N = 256
VMEM_LIMIT = 56 * 1024 * 1024

LOG_GAMMA = [float(np.log1p(-np.exp2(np.float32(-5.0 - h)))) for h in range(H_C)]


def _cparams(sem):
    return pltpu.CompilerParams(dimension_semantics=sem, vmem_limit_bytes=VMEM_LIMIT)


def _silu(x):
    t = 0.5 * x
    return t * (1.0 + jnp.tanh(t))


def _dot(a, b):
    return jnp.dot(a, b, preferred_element_type=F32)


def _dot_nt(a, b):
    return lax.dot_general(a, b, (((1,), (1,)), ((), ())), preferred_element_type=F32)


def _dot_tn(a, b):
    return lax.dot_general(a, b, (((0,), (0,)), ((), ())), preferred_element_type=F32)


def _cumsum_rows(x):
    c = x.shape[0]
    r = lax.broadcasted_iota(jnp.int32, (c, c), 0)
    s = lax.broadcasted_iota(jnp.int32, (c, c), 1)
    tri = (r >= s).astype(BF16)
    hi = x.astype(BF16)
    r1 = x - hi.astype(F32)
    mid = r1.astype(BF16)
    lo = (r1 - mid.astype(F32)).astype(BF16)
    return _dot(tri, hi) + _dot(tri, mid) + _dot(tri, lo)


def _row_to_col(r):
    n = r.shape[1]
    ri = lax.broadcasted_iota(jnp.int32, (n, n), 0)
    ci = lax.broadcasted_iota(jnp.int32, (n, n), 1)
    return jnp.sum(jnp.where(ri == ci, jnp.broadcast_to(r, (n, n)), 0.0), axis=1, keepdims=True)


def _ada_kernel(c_ref, w_ref, b_ref, o_ref):
    sc = _silu(c_ref[...]).astype(BF16)
    o_ref[...] = _dot(sc, w_ref[...].astype(BF16)) + b_ref[...]


def _ada_mod(c_all, w, b):
    m = c_all.shape[0]
    n = w.shape[1]
    tn = 768
    return pl.pallas_call(
        _ada_kernel,
        grid=(n // tn,),
        in_specs=[pl.BlockSpec((m, D_MODEL), lambda j: (0, 0)),
                  pl.BlockSpec((D_MODEL, tn), lambda j: (0, j)),
                  pl.BlockSpec((1, tn), lambda j: (0, j))],
        out_specs=pl.BlockSpec((m, tn), lambda j: (0, j)),
        out_shape=jax.ShapeDtypeStruct((m, n), F32),
        compiler_params=_cparams(("arbitrary",)),
        name="ada_mod",
    )(c_all, w, b.reshape(1, n))


def _norm_mod(x, g, scale, shift):
    ms = jnp.mean(x * x, axis=-1, keepdims=True)
    y = x * lax.rsqrt(ms + EPS) * g
    return y * (1.0 + scale) + shift


def _log_sigmoid(z):
    return jnp.minimum(z, 0.0) - jnp.log(1.0 + jnp.exp(-jnp.abs(z)))


def _mod_spec(arr, tm, rows_per_batch):
    if arr.ndim == 3:
        return pl.BlockSpec((None, 1, D_MODEL), lambda i: (i // (rows_per_batch // tm), 0, 0))
    return pl.BlockSpec((tm, D_MODEL), lambda i: (i, 0))


def _prenorm_even_kernel(x_ref, scale_ref, shift_ref, g_ref, wlri_ref, wlr_ref, blr_ref, h_ref, la_ref):
    hb = _norm_mod(x_ref[...], g_ref[...], scale_ref[...], shift_ref[...]).astype(BF16)
    h_ref[...] = hb
    lr = _dot_nt(hb, wlri_ref[...])
    z = _dot(lr.astype(BF16), wlr_ref[...]) + blr_ref[...]
    la_ref[...] = _log_sigmoid(z) / GLA_TAU


def _prenorm_even(x, scale, shift, g, wlri, wlr, blr, *, tm, rows_per_batch):
    m = x.shape[0]
    const = lambda shape: pl.BlockSpec(shape, lambda i: (0, 0))
    return pl.pallas_call(
        _prenorm_even_kernel,
        grid=(m // tm,),
        in_specs=[pl.BlockSpec((tm, D_MODEL), lambda i: (i, 0)),
                  _mod_spec(scale, tm, rows_per_batch), _mod_spec(shift, tm, rows_per_batch),
                  const((1, D_MODEL)), const((128, D_MODEL)), const((128, H_B * DK_B)), const((1, H_B * DK_B))],
        out_specs=[pl.BlockSpec((tm, D_MODEL), lambda i: (i, 0)),
                   pl.BlockSpec((tm, H_B * DK_B), lambda i: (i, 0))],
        out_shape=[jax.ShapeDtypeStruct((m, D_MODEL), BF16),
                   jax.ShapeDtypeStruct((m, H_B * DK_B), F32)],
        compiler_params=_cparams(("arbitrary",)),
        name="prenorm_even",
    )(x, scale, shift, g, wlri, wlr, blr)


class _SideJob(NamedTuple):
    njobs: int
    args: list
    in_specs: Callable
    out_specs: Callable
    out_shape: list
    units: int
    body: Callable


def _inproj_kernel(*refs, rot_tiles, side, nsteps):
    nmain = 4 if rot_tiles else 2
    j = pl.program_id(0)
    run_side = None
    if side is not None:
        njobs, nsi, nso, body = side
        side_in = refs[nmain:nmain + nsi]
        side_out = refs[nmain + nsi + 1:nmain + nsi + 1 + nso]
        step = j * pl.num_programs(1) + pl.program_id(1)
        block = (step * njobs) // nsteps
        prev_block = ((step - 1) * njobs) // nsteps
        run_side = (step == 0) | (block != prev_block)
        refs = refs[:nmain] + (refs[nmain + nsi],)
    h_ref, w_ref, *rest = refs
    o_ref = rest[-1]
    nchunk = o_ref.shape[1] // MXU_N
    half = DK_C // 2

    def plain_chunks():
        def chunk(c):
            cs = slice(c * MXU_N, (c + 1) * MXU_N)
            o_ref[:, cs] = _dot(h_ref[...], w_ref[:, cs]).astype(o_ref.dtype)
        return chunk

    def rotary_chunks():
        mult = jnp.where(j < rot_tiles // 2, 1.0, DK_C ** -0.5)
        cos = rest[0][...] * mult
        sin = rest[1][...] * mult

        def chunk(c):
            acc = _dot(h_ref[...], w_ref[:, c * DK_C:(c + 1) * DK_C])
            x1 = acc[:, :half]
            x2 = acc[:, half:]
            o_ref[:, c * DK_C:c * DK_C + half] = (x1 * cos - x2 * sin).astype(o_ref.dtype)
            o_ref[:, c * DK_C + half:(c + 1) * DK_C] = (x2 * cos + x1 * sin).astype(o_ref.dtype)
        return chunk

    def tile(make_chunk, with_side):
        def run():
            chunk = make_chunk()
            if with_side:
                body(side_in, side_out, chunk)
            else:
                for c in range(nchunk):
                    chunk(c)
        return run

    kinds = [(plain_chunks, j >= rot_tiles)]
    if rot_tiles:
        kinds.append((rotary_chunks, j < rot_tiles))
    for make_chunk, is_kind in kinds:
        if run_side is None:
            pl.when(is_kind)(tile(make_chunk, False))
        else:
            pl.when(is_kind & run_side)(tile(make_chunk, True))
            pl.when(is_kind & jnp.logical_not(run_side))(tile(make_chunk, False))


def _inproj(h, w, rot=None, side=None, *, tm, tn, out_dtype, name):
    m = h.shape[0]
    n = w.shape[1]
    ni = m // tm
    nsteps = (n // tn) * ni
    in_specs = [pl.BlockSpec((tm, D_MODEL), lambda j, i: (i, 0)),
                pl.BlockSpec((D_MODEL, tn), lambda j, i: (0, j))]
    args = [h, w]
    rot_tiles = 0
    if rot is not None:
        nblk = rot[0].shape[0] // tm
        in_specs += [pl.BlockSpec((tm, DK_C // 2), lambda j, i: (i % nblk, 0))] * 2
        args += list(rot)
        rot_tiles = 2 * (H_C * DK_C) // tn
    out_specs = [pl.BlockSpec((tm, tn), lambda j, i: (i, j))]
    out_shape = [jax.ShapeDtypeStruct((m, n), out_dtype)]
    side_static = None
    if side is not None:
        assert side.njobs <= nsteps and side.units == tn // MXU_N
        block = lambda j, i: ((j * ni + i) * side.njobs) // nsteps
        in_specs += side.in_specs(block)
        args += side.args
        out_specs += side.out_specs(block)
        out_shape += side.out_shape
        side_static = (side.njobs, len(side.args), len(side.out_shape), side.body)
    return pl.pallas_call(
        functools.partial(_inproj_kernel, rot_tiles=rot_tiles, side=side_static, nsteps=nsteps),
        grid=(n // tn, ni),
        in_specs=in_specs,
        out_specs=out_specs,
        out_shape=out_shape,
        compiler_params=_cparams(("arbitrary", "arbitrary")),
        name=name,
    )(*args)


def _outproj_kernel(*refs, nin, with_next):
    m_refs, w_refs = refs[:nin], refs[nin:2 * nin]
    if with_next:
        x_ref, gate_ref, nscale_ref, nshift_ref, ng_ref, o_ref, h_ref = refs[2 * nin:]
    else:
        x_ref, gate_ref, o_ref = refs[2 * nin:]
    for c in range(o_ref.shape[1] // MXU_N):
        cs = slice(c * MXU_N, (c + 1) * MXU_N)
        acc = _dot(m_refs[0][...].astype(BF16), w_refs[0][:, cs])
        for m_ref, w_ref in zip(m_refs[1:], w_refs[1:]):
            acc += _dot(m_ref[...].astype(BF16), w_ref[:, cs])
        o_ref[:, cs] = x_ref[:, cs] + gate_ref[:, cs] * acc
    if with_next:
        h_ref[...] = _norm_mod(o_ref[...], ng_ref[...], nscale_ref[...], nshift_ref[...]).astype(BF16)


def _outproj(mixed, ws, x, gate, next_mod=None, *, tm, rows_per_batch):
    m = x.shape[0]
    row = pl.BlockSpec((tm, D_MODEL), lambda i: (i, 0))
    in_specs = ([pl.BlockSpec((tm, a.shape[1]), lambda i: (i, 0)) for a in mixed]
                + [pl.BlockSpec(w.shape, lambda i: (0, 0), pipeline_mode=pl.Buffered(1)) for w in ws]
                + [row, _mod_spec(gate, tm, rows_per_batch)])
    args = [*mixed, *ws, x, gate]
    out_specs = [row]
    out_shape = [jax.ShapeDtypeStruct((m, D_MODEL), F32)]
    if next_mod is not None:
        nscale, nshift, ng = next_mod
        in_specs += [_mod_spec(nscale, tm, rows_per_batch), _mod_spec(nshift, tm, rows_per_batch),
                     pl.BlockSpec((1, D_MODEL), lambda i: (0, 0))]
        args += [nscale, nshift, ng]
        out_specs.append(row)
        out_shape.append(jax.ShapeDtypeStruct((m, D_MODEL), BF16))
    return pl.pallas_call(
        functools.partial(_outproj_kernel, nin=len(mixed), with_next=next_mod is not None),
        grid=(m // tm,),
        in_specs=in_specs,
        out_specs=out_specs,
        out_shape=out_shape,
        compiler_params=_cparams(("arbitrary",)),
        name="outproj",
    )(*args)


def _seg_rms(x, g2, seg_ones):
    x2 = x * x
    hi = x2.astype(BF16)
    lo = (x2 - hi.astype(F32)).astype(BF16)
    ss = _dot(hi, seg_ones) + _dot(lo, seg_ones)
    return x * lax.rsqrt(ss * (1.0 / HD_A) + EPS) * g2


def _dup_halves(x, lo_half):
    sw = pltpu.roll(x, HD_A, 1)
    return [jnp.where(lo_half, x, sw), jnp.where(lo_half, sw, x)]


def _rows_to_cols(x):
    pad = jnp.zeros((x.shape[1] - x.shape[0], x.shape[1]), x.dtype)
    return jnp.concatenate([x, pad], axis=0).T


def _seg_ones():
    r = lax.broadcasted_iota(jnp.int32, (2 * HD_A, 2 * HD_A), 0)
    c = lax.broadcasted_iota(jnp.int32, (2 * HD_A, 2 * HD_A), 1)
    return ((r < HD_A) == (c < HD_A)).astype(BF16)


def _swa_prompt_kernel(relb_ref, sinks_ref, bucket_ref, q_ref, ga_ref, kv_ref, qn_ref, kn_ref,
                       o_ref, knew_ref, bias_ref, kprev_ref, vprev_ref, s_ref, pe_ref):
    b = pl.program_id(0)
    i = pl.program_id(1)

    @pl.when((b == 0) & (i == 0))
    def _():
        bk = bucket_ref[...]
        for h in range(H_A):
            bias_ref[h] = jnp.full((WINDOW, 2 * WINDOW), NEG_INF, F32)
        for bb in range(NUM_BUCKETS):
            hit = bk == bb
            for h in range(H_A):
                bias_ref[h] = jnp.where(hit, relb_ref[bb, h], bias_ref[h])
        col = lax.broadcasted_iota(jnp.int32, (WINDOW, 2 * WINDOW), 1)
        for h in range(H_A):
            bias_ref[H_A + h] = jnp.where(col >= WINDOW, bias_ref[h], NEG_INF)

    @pl.when(i == 0)
    def _():
        kprev_ref[...] = jnp.zeros(kprev_ref.shape, BF16)
        vprev_ref[...] = jnp.zeros(vprev_ref.shape, BF16)

    seg_ones = _seg_ones()
    lt = 2 * HD_A
    tpg = G_A // 2
    lo_half = lax.broadcasted_iota(jnp.int32, (WINDOW, lt), 1) < HD_A
    kv = kv_ref[...].astype(F32)
    kc = _seg_rms(kv[:, :W_K], kn_ref[...], seg_ones)
    k_dup = _dup_halves(kc, lo_half)
    v_dup = _dup_halves(kv[:, W_K:], lo_half)
    base = jnp.where(i == 0, H_A, 0)
    qn2 = qn_ref[...] * HD_A ** -0.5
    gq = tpg * WINDOW
    lo_g = lax.broadcasted_iota(jnp.int32, (gq, lt), 1) < HD_A
    grows = [slice(g * 2 * gq, (g + 1) * 2 * gq) for g in range(H_A_KV)]
    for g in range(H_A_KV):
        kk = jnp.concatenate([kprev_ref[g], k_dup[g].astype(BF16)], axis=0)
        xg = jnp.concatenate([q_ref[:, (g * tpg + p) * lt:(g * tpg + p + 1) * lt] for p in range(tpg)],
                             axis=0).astype(F32)
        xn = _seg_rms(xg, qn2, seg_ones)
        q_all = jnp.concatenate([jnp.where(lo_g, xn, 0.0), jnp.where(lo_g, 0.0, xn)], axis=0).astype(BF16)
        s_ref[grows[g], :] = _dot_nt(q_all, kk)
    sink_terms = []
    for g in range(H_A_KV):
        for n in range(G_A):
            h = g * G_A + 2 * (n % tpg) + n // tpg
            rows = slice((g * G_A + n) * WINDOW, (g * G_A + n + 1) * WINDOW)
            s = s_ref[rows, :] + bias_ref[base + h]
            sink = sinks_ref[h]
            m = jnp.maximum(jnp.max(s, axis=-1, keepdims=True), sink)
            pe_ref[rows, :] = jnp.exp(s - m).astype(BF16)
            sink_terms.append(jnp.exp(sink - m))
    o_ext = []
    for g in range(H_A_KV):
        vv = jnp.concatenate([vprev_ref[g], v_dup[g].astype(BF16)], axis=0)
        vv_ext = jnp.concatenate([vv, jnp.ones(vv.shape, BF16)], axis=1)
        o_ext.append(_dot(pe_ref[grows[g], :], vv_ext))
    for g in range(H_A_KV):
        for p in range(tpg):
            halves = []
            for a in range(2):
                n = a * tpg + p
                rows = slice(n * WINDOW, (n + 1) * WINDOW)
                halves.append(o_ext[g][rows, :lt] / (o_ext[g][rows, lt:] + sink_terms[g * G_A + n]))
            oa = jnp.where(lo_half, halves[0], halves[1])
            cols = slice((g * tpg + p) * lt, (g * tpg + p + 1) * lt)
            o_ref[:, cols] = (oa * _silu(ga_ref[:, cols].astype(F32))).astype(o_ref.dtype)
    for g in range(H_A_KV):
        kprev_ref[g] = k_dup[g].astype(BF16)
        vprev_ref[g] = v_dup[g].astype(BF16)

    @pl.when(i == pl.num_programs(1) - 1)
    def _():
        knew_ref[...] = kc.T


def _swa_prompt(proj, rel_bias, sinks, qn2, kn2, bucket, *, batch, seq):
    nb = seq // WINDOW
    rb = lambda b, i: b * nb + i
    smem = pl.BlockSpec(memory_space=pltpu.SMEM)
    return pl.pallas_call(
        _swa_prompt_kernel,
        grid=(batch, nb),
        in_specs=[smem, smem,
                  pl.BlockSpec((WINDOW, 2 * WINDOW), lambda b, i: (0, 0)),
                  pl.BlockSpec((WINDOW, W_A), lambda b, i: (rb(b, i), EV_QA // W_A)),
                  pl.BlockSpec((WINDOW, W_A), lambda b, i: (rb(b, i), EV_GA // W_A)),
                  pl.BlockSpec((WINDOW, 2 * W_K), lambda b, i: (rb(b, i), EV_KV // (2 * W_K))),
                  pl.BlockSpec((1, 2 * HD_A), lambda b, i: (0, 0)),
                  pl.BlockSpec((1, 2 * HD_A), lambda b, i: (0, 0))],
        out_specs=[pl.BlockSpec((WINDOW, W_A), lambda b, i: (rb(b, i), 0)),
                   pl.BlockSpec((None, WINDOW, W_K), lambda b, i: (b, 0, 0))],
        out_shape=[jax.ShapeDtypeStruct((batch * seq, W_A), BF16),
                   jax.ShapeDtypeStruct((batch, WINDOW, W_K), F32)],
        scratch_shapes=[pltpu.VMEM((2 * H_A, WINDOW, 2 * WINDOW), F32),
                        pltpu.VMEM((H_A_KV, WINDOW, W_K), BF16), pltpu.VMEM((H_A_KV, WINDOW, W_K), BF16),
                        pltpu.VMEM((H_A * WINDOW, 2 * WINDOW), F32), pltpu.VMEM((H_A * WINDOW, 2 * WINDOW), BF16)],
        compiler_params=_cparams(("arbitrary", "arbitrary")),
        name="swa_prompt",
    )(rel_bias, sinks, bucket, proj, proj, proj, qn2, kn2)


def _swa_sample_kernel(bkc_ref, bkn_ref, relrows_ref, sinkrows_ref, q_ref, ga_ref, kvn_ref, ck_ref, cv_ref,
                       qn_ref, kn_ref, o_ref, ko_ref, vo_ref, biasc_ref, biasn_ref, *, sb, ntok, unroll):
    @pl.when(pl.program_id(0) == 0)
    def _():
        bkc = bkc_ref[...]
        bkn = bkn_ref[...]
        rr = relrows_ref[...]
        bc = jnp.full(bkc.shape, NEG_INF, F32)
        bn = jnp.full(bkn.shape, NEG_INF, F32)
        for bb in range(NUM_BUCKETS):
            val = rr[:, bb:bb + 1]
            bc = jnp.where(bkc == bb, val, bc)
            bn = jnp.where(bkn == bb, val, bn)
        biasc_ref[...] = bc
        biasn_ref[...] = bn

    seg_ones = _seg_ones()
    lt = 2 * HD_A
    nrow = G_A * SAMPLE_PAD
    grow = nrow // H_A_KV
    lo_q = lax.broadcasted_iota(jnp.int32, (grow, lt), 1) < HD_A
    lo_n = lax.broadcasted_iota(jnp.int32, (SAMPLE_PAD, lt), 1) < HD_A
    lane_pos = lax.broadcasted_iota(jnp.int32, (W_K, WINDOW), 1)
    qn2 = qn_ref[...] * HD_A ** -0.5
    kn2 = kn_ref[...]
    sink = sinkrows_ref[:, 0:1]

    def body(it, carry):
        seqs = [it * unroll + u for u in range(unroll)]
        rows = [pl.ds(pl.multiple_of(s * SAMPLE_PAD, SAMPLE_PAD), SAMPLE_PAD) for s in seqs]
        kvn = [kvn_ref[r, :] for r in rows]
        kn_all = _seg_rms(jnp.concatenate([x[:, :W_K] for x in kvn], axis=0), kn2, seg_ones)
        q2_all = jnp.concatenate([q_ref[r, p * lt:(p + 1) * lt] for r in rows for p in range(G_A)], axis=0)
        xn_all = _seg_rms(q2_all, qn2, seg_ones)
        sc, sn, kn = [], [], []
        for u, s in enumerate(seqs):
            kn.append(kn_all[u * SAMPLE_PAD:(u + 1) * SAMPLE_PAD])
            kn_dup = _dup_halves(kn[u], lo_n)
            scg, sng = [], []
            for g in range(H_A_KV):
                xn = xn_all[u * nrow + g * grow:u * nrow + (g + 1) * grow]
                q4 = jnp.concatenate([jnp.where(lo_q, xn, 0.0), jnp.where(lo_q, 0.0, xn)], axis=0).astype(BF16)
                kt_g = ck_ref[s, g * HD_A:(g + 1) * HD_A, :].astype(BF16)
                scg.append(_dot(q4, jnp.concatenate([kt_g, kt_g], axis=0)))
                sng.append(_dot_nt(q4, kn_dup[g].astype(BF16)))
            sc.append(jnp.concatenate(scg, axis=0))
            sn.append(jnp.concatenate(sng, axis=0))
        pc, pn, sink_terms = [], [], []
        for u in range(unroll):
            scu = sc[u] + biasc_ref[...]
            snu = sn[u] + biasn_ref[:, :SAMPLE_PAD]
            m = jnp.maximum(jnp.maximum(jnp.max(scu, axis=-1, keepdims=True),
                                        jnp.max(snu, axis=-1, keepdims=True)), sink)
            pc.append(jnp.exp(scu - m).astype(BF16))
            pn.append(jnp.exp(snu - m).astype(BF16))
            sink_terms.append(jnp.exp(sink - m))
        o_ext = []
        for u, s in enumerate(seqs):
            vn_dup = _dup_halves(kvn[u][:, W_K:], lo_n)
            og = []
            for g in range(H_A_KV):
                vt_g = cv_ref[s, g * HD_A:(g + 1) * HD_A, :].astype(BF16)
                vt_ext = jnp.concatenate([vt_g, vt_g, jnp.ones((lt, WINDOW), BF16)], axis=0)
                vn_ext = jnp.concatenate([vn_dup[g].astype(BF16), jnp.ones((SAMPLE_PAD, lt), BF16)], axis=1)
                gr = slice(g * 2 * grow, (g + 1) * 2 * grow)
                og.append(_dot_nt(pc[u][gr], vt_ext) + _dot(pn[u][gr], vn_ext))
            o_ext.append(jnp.concatenate(og, axis=0))
        for u, s in enumerate(seqs):
            o4 = o_ext[u][:, :lt] / (o_ext[u][:, lt:] + sink_terms[u])
            xg = ga_ref[rows[u], :]
            for p in range(G_A):
                g, pp = divmod(p, G_A // 2)
                r0 = g * 2 * grow + pp * SAMPLE_PAD
                o2 = jnp.where(lo_n, o4[r0:r0 + SAMPLE_PAD], o4[r0 + grow:r0 + grow + SAMPLE_PAD])
                o_ref[rows[u], p * lt:(p + 1) * lt] = (o2 * _silu(xg[:, p * lt:(p + 1) * lt])).astype(o_ref.dtype)
            ko_ref[s] = pltpu.roll(jnp.where(lane_pos < ntok, _rows_to_cols(kn[u]), ck_ref[s]), WINDOW - ntok, 1)
            vo_ref[s] = pltpu.roll(jnp.where(lane_pos < ntok, _rows_to_cols(kvn[u][:, W_K:]), cv_ref[s]),
                                   WINDOW - ntok, 1)
        return carry

    lax.fori_loop(0, sb // unroll, body, 0)


def _swa_sample(proj, cache_k, cache_v, bkc, bkn, relrows, sinkrows, qn2, kn2, *, sb, ntok, unroll):
    nseq = cache_k.shape[0]
    rows = sb * SAMPLE_PAD
    full = lambda shape: pl.BlockSpec(shape, lambda i: tuple(0 for _ in shape))
    cache = pl.BlockSpec((sb, W_K, WINDOW), lambda i: (i, 0, 0))
    return pl.pallas_call(
        functools.partial(_swa_sample_kernel, sb=sb, ntok=ntok, unroll=unroll),
        grid=(nseq // sb,),
        in_specs=[full(bkc.shape), full(bkn.shape), full(relrows.shape), full(sinkrows.shape),
                  pl.BlockSpec((rows, W_A), lambda i: (i, EV_QA // W_A)),
                  pl.BlockSpec((rows, W_A), lambda i: (i, EV_GA // W_A)),
                  pl.BlockSpec((rows, 2 * W_K), lambda i: (i, EV_KV // (2 * W_K))),
                  cache, cache, full((1, 2 * HD_A)), full((1, 2 * HD_A))],
        out_specs=[pl.BlockSpec((rows, W_A), lambda i: (i, 0)), cache, cache],
        out_shape=[jax.ShapeDtypeStruct((nseq * SAMPLE_PAD, W_A), F32),
                   jax.ShapeDtypeStruct(cache_k.shape, F32),
                   jax.ShapeDtypeStruct(cache_v.shape, F32)],
        scratch_shapes=[pltpu.VMEM(bkc.shape, F32), pltpu.VMEM(bkn.shape, F32)],
        compiler_params=_cparams(("arbitrary",)),
        name="swa_sample",
    )(bkc, bkn, relrows, sinkrows, proj, proj, proj, cache_k, cache_v, qn2, kn2)


def _gla_pre(q, k, v, la, n_valid):
    c = q.shape[0]
    bcum = _cumsum_rows(la)
    rr = lax.broadcasted_iota(jnp.int32, (c, c), 0)
    cc = lax.broadcasted_iota(jnp.int32, (c, c), 1)
    causal = rr >= cc
    row = lax.broadcasted_iota(jnp.int32, (c, 1), 0)
    qts, kts, kds, vbs, dcols = [], [], [], [], []
    for h in range(H_B):
        ks = slice(h * DK_B, (h + 1) * DK_B)
        bc = bcum[:, ks]
        qts.append(((q[:, ks] * DK_B ** -0.5) * jnp.exp(bc)).astype(BF16))
        kts.append((k[:, ks] * jnp.exp(-bc)).astype(BF16))
        blast = bc[n_valid - 1:n_valid, :]
        kd = k[:, ks] * jnp.exp(blast - bc)
        if n_valid < c:
            kd = jnp.where(row < n_valid, kd, 0.0)
        kds.append(kd.astype(BF16))
        vbs.append(v[:, h * DV_B:(h + 1) * DV_B].astype(BF16))
        dcols.append(_row_to_col(jnp.exp(blast)))
    scores = [_dot_nt(qts[h], kts[h]) for h in range(H_B)]
    upds = [_dot_tn(kds[h], vbs[h]) for h in range(H_B)]
    return [(qts[h], jnp.where(causal, scores[h], 0.0).astype(BF16), vbs[h], upds[h], dcols[h])
            for h in range(H_B)]


def _gla_post(pre, gb, gla_g, states):
    outs = [_dot(a, vb) + _dot(qt, st.astype(BF16)) for (qt, a, vb, _, _), st in zip(pre, states)]
    new_states = [dcol * st + upd for (_, _, _, upd, dcol), st in zip(pre, states)]
    gated = []
    for h, o in enumerate(outs):
        on = o * lax.rsqrt(jnp.mean(o * o, axis=-1, keepdims=True) + EPS) * gla_g
        gated.append(on * _silu(gb[:, h * DV_B:(h + 1) * DV_B]))
    return jnp.concatenate(gated, axis=1), new_states


def _gla_prompt_kernel(q_ref, k_ref, v_ref, gb_ref, la_ref, g_ref, o_ref, s_ref, *, nchunk):
    @pl.when(pl.program_id(1) == 0)
    def _():
        s_ref[...] = jnp.zeros(s_ref.shape, F32)

    gla_g = g_ref[...]
    chunk_rows = [slice(c * GLA_CHUNK, (c + 1) * GLA_CHUNK) for c in range(nchunk)]
    pres = [_gla_pre(q_ref[r, :].astype(F32), k_ref[r, :].astype(F32), v_ref[r, :].astype(F32), la_ref[r, :],
                     GLA_CHUNK) for r in chunk_rows]
    states = [s_ref[h] for h in range(H_B)]
    for r, pre in zip(chunk_rows, pres):
        out, states = _gla_post(pre, gb_ref[r, :].astype(F32), gla_g, states)
        o_ref[r, :] = out.astype(o_ref.dtype)
    for h in range(H_B):
        s_ref[h] = states[h]


def _gla_prompt(proj, la, gla_g, *, batch, seq, rows):
    nstep = seq // rows
    rb = lambda b, i: b * nstep + i
    return pl.pallas_call(
        functools.partial(_gla_prompt_kernel, nchunk=rows // GLA_CHUNK),
        grid=(batch, nstep),
        in_specs=[pl.BlockSpec((rows, H_B * DK_B), lambda b, i: (rb(b, i), EV_QB // (H_B * DK_B))),
                  pl.BlockSpec((rows, H_B * DK_B), lambda b, i: (rb(b, i), EV_KB // (H_B * DK_B))),
                  pl.BlockSpec((rows, W_B), lambda b, i: (rb(b, i), EV_VB // W_B)),
                  pl.BlockSpec((rows, W_B), lambda b, i: (rb(b, i), EV_GB // W_B)),
                  pl.BlockSpec((rows, H_B * DK_B), lambda b, i: (rb(b, i), 0)),
                  pl.BlockSpec((1, DV_B), lambda b, i: (0, 0))],
        out_specs=[pl.BlockSpec((rows, W_B), lambda b, i: (rb(b, i), 0)),
                   pl.BlockSpec((None, H_B, DK_B, DV_B), lambda b, i: (b, 0, 0, 0))],
        out_shape=[jax.ShapeDtypeStruct((batch * seq, W_B), BF16),
                   jax.ShapeDtypeStruct((batch, H_B, DK_B, DV_B), F32)],
        compiler_params=_cparams(("arbitrary", "arbitrary")),
        name="gla_prompt",
    )(proj, proj, proj, proj, la, gla_g)


def _gla_sample_kernel(q_ref, k_ref, v_ref, gb_ref, la_ref, g_ref, s_in_ref, o_ref, s_ref, *, sb, ntok, unroll):
    gla_g = g_ref[...]

    def body(it, carry):
        seqs = [it * unroll + u for u in range(unroll)]
        rows = [pl.ds(pl.multiple_of(s * SAMPLE_PAD, SAMPLE_PAD), SAMPLE_PAD) for s in seqs]
        pres = [_gla_pre(q_ref[r, :], k_ref[r, :], v_ref[r, :], la_ref[r, :], ntok) for r in rows]
        for s, r, pre in zip(seqs, rows, pres):
            out, new_states = _gla_post(pre, gb_ref[r, :], gla_g, [s_in_ref[s, h] for h in range(H_B)])
            o_ref[r, :] = out.astype(o_ref.dtype)
            for h in range(H_B):
                s_ref[s, h] = new_states[h]
        return carry

    lax.fori_loop(0, sb // unroll, body, 0)


def _gla_sample(proj, la, gla_g, state, *, sb, ntok, unroll):
    nseq = state.shape[0]
    rows = sb * SAMPLE_PAD
    return pl.pallas_call(
        functools.partial(_gla_sample_kernel, sb=sb, ntok=ntok, unroll=unroll),
        grid=(nseq // sb,),
        in_specs=[pl.BlockSpec((rows, H_B * DK_B), lambda i: (i, EV_QB // (H_B * DK_B))),
                  pl.BlockSpec((rows, H_B * DK_B), lambda i: (i, EV_KB // (H_B * DK_B))),
                  pl.BlockSpec((rows, W_B), lambda i: (i, EV_VB // W_B)),
                  pl.BlockSpec((rows, W_B), lambda i: (i, EV_GB // W_B)),
                  pl.BlockSpec((rows, H_B * DK_B), lambda i: (i, 0)),
                  pl.BlockSpec((1, DV_B), lambda i: (0, 0)),
                  pl.BlockSpec((sb, H_B, DK_B, DV_B), lambda i: (i, 0, 0, 0))],
        out_specs=[pl.BlockSpec((rows, W_B), lambda i: (i, 0)),
                   pl.BlockSpec((sb, H_B, DK_B, DV_B), lambda i: (i, 0, 0, 0))],
        out_shape=[jax.ShapeDtypeStruct((nseq * SAMPLE_PAD, W_B), BF16),
                   jax.ShapeDtypeStruct(state.shape, F32)],
        compiler_params=_cparams(("arbitrary",)),
        name="gla_sample",
    )(proj, proj, proj, proj, la, gla_g, state)


def _ret_decay(c, h):
    ri = lax.broadcasted_iota(jnp.int32, (c, c), 0)
    ci = lax.broadcasted_iota(jnp.int32, (c, c), 1)
    dist = (ri - ci).astype(F32)
    return jnp.where(ri >= ci, jnp.exp(jnp.maximum(dist, 0.0) * LOG_GAMMA[h]), 0.0)


def _ret_chunk(q_ref, k_ref, v_ref, g_ref, retg, state_in_ref, state_ref, o_ref, n_valid, after_head=None,
               decay_ref=None):
    c = q_ref.shape[0]
    row = lax.broadcasted_iota(jnp.int32, (c, 1), 0)
    rowf = row.astype(F32)
    a_heads = []
    for h in range(H_C):
        ks = slice(h * DK_C, (h + 1) * DK_C)
        decay = _ret_decay(c, h) if decay_ref is None else decay_ref[h]
        a_heads.append((_dot_nt(q_ref[:, ks].astype(BF16), k_ref[:, ks].astype(BF16)) * decay).astype(BF16))
    for h in range(H_C):
        lg = LOG_GAMMA[h]
        ks = slice(h * DK_C, (h + 1) * DK_C)
        vs = slice(h * DV_C, (h + 1) * DV_C)
        kf = k_ref[:, ks].astype(F32)
        vb = v_ref[:, vs].astype(BF16)
        st = state_in_ref[h]
        q_in = (q_ref[:, ks].astype(F32) * jnp.exp((rowf + 1.0) * lg)).astype(BF16)
        o = _dot(a_heads[h], vb) + _dot(q_in, st.astype(BF16))
        kd = kf * jnp.exp((n_valid - 1.0 - rowf) * lg)
        if n_valid < c:
            kd = jnp.where(row < n_valid, kd, 0.0)
        state_ref[h] = math.exp(n_valid * lg) * st + _dot_tn(kd.astype(BF16), vb)
        on = o * lax.rsqrt(jnp.mean(o * o, axis=-1, keepdims=True) + EPS) * retg
        o_ref[:, vs] = (on * _silu(g_ref[:, vs].astype(F32))).astype(o_ref.dtype)
        if after_head is not None:
            after_head(h)


def _ret_prompt_kernel(q_ref, k_ref, v_ref, g_ref, retg_ref, o_ref, s_ref, decay_ref):
    chunk = q_ref.shape[0]

    @pl.when((pl.program_id(0) == 0) & (pl.program_id(1) == 0))
    def _():
        for h in range(H_C):
            decay_ref[h] = _ret_decay(chunk, h)

    @pl.when(pl.program_id(1) == 0)
    def _():
        s_ref[...] = jnp.zeros(s_ref.shape, F32)

    _ret_chunk(q_ref, k_ref, v_ref, g_ref, retg_ref[...], s_ref, s_ref, o_ref, chunk, decay_ref=decay_ref)


def _ret_prompt(proj, ret_g, *, batch, seq):
    chunk = RET_PROMPT_CHUNK if seq % RET_PROMPT_CHUNK == 0 else RET_CHUNK
    nstep = seq // chunk
    rb = lambda b, i: b * nstep + i
    qk = H_C * DK_C
    return pl.pallas_call(
        _ret_prompt_kernel,
        grid=(batch, nstep),
        in_specs=[pl.BlockSpec((chunk, qk), lambda b, i: (rb(b, i), 0)),
                  pl.BlockSpec((chunk, qk), lambda b, i: (rb(b, i), 1)),
                  pl.BlockSpec((chunk, W_C), lambda b, i: (rb(b, i), 1)),
                  pl.BlockSpec((chunk, W_C), lambda b, i: (rb(b, i), 2)),
                  pl.BlockSpec((1, DV_C), lambda b, i: (0, 0))],
        out_specs=[pl.BlockSpec((chunk, W_C), lambda b, i: (rb(b, i), 0)),
                   pl.BlockSpec((None, H_C, DK_C, DV_C), lambda b, i: (b, 0, 0, 0))],
        out_shape=[jax.ShapeDtypeStruct((batch * seq, W_C), BF16),
                   jax.ShapeDtypeStruct((batch, H_C, DK_C, DV_C), F32)],
        scratch_shapes=[pltpu.VMEM((H_C, chunk, chunk), F32)],
        compiler_params=_cparams(("arbitrary", "arbitrary")),
        name="ret_prompt",
    )(proj, proj, proj, proj, ret_g)


def _ret_sample_job(proj, ret_g, state, *, ntok):
    nseq = state.shape[0]
    qk = H_C * DK_C
    state_spec = lambda blk: pl.BlockSpec((None, H_C, DK_C, DV_C), lambda j, i: (blk(j, i), 0, 0, 0))

    def body(in_refs, out_refs, host_chunk):
        q_ref, k_ref, v_ref, g_ref, retg_ref, s_in_ref = in_refs
        o_ref, s_ref = out_refs
        _ret_chunk(q_ref, k_ref, v_ref, g_ref, retg_ref[...], s_in_ref, s_ref, o_ref, ntok, after_head=host_chunk)

    return _SideJob(
        njobs=nseq,
        units=H_C,
        args=[proj, proj, proj, proj, ret_g, state],
        in_specs=lambda blk: [pl.BlockSpec((SAMPLE_PAD, qk), lambda j, i: (blk(j, i), 0)),
                              pl.BlockSpec((SAMPLE_PAD, qk), lambda j, i: (blk(j, i), 1)),
                              pl.BlockSpec((SAMPLE_PAD, W_C), lambda j, i: (blk(j, i), 1)),
                              pl.BlockSpec((SAMPLE_PAD, W_C), lambda j, i: (blk(j, i), 2)),
                              pl.BlockSpec((1, DV_C), lambda j, i: (0, 0)),
                              state_spec(blk)],
        out_specs=lambda blk: [pl.BlockSpec((SAMPLE_PAD, W_C), lambda j, i: (blk(j, i), 0)), state_spec(blk)],
        out_shape=[jax.ShapeDtypeStruct((nseq * SAMPLE_PAD, W_C), BF16),
                   jax.ShapeDtypeStruct(state.shape, F32)],
        body=body)


def _t5_bucket(dist):
    dist = np.maximum(dist, 0)
    max_exact = NUM_BUCKETS // 2
    log_ratio = (np.log(np.maximum(dist, 1).astype(np.float32) / np.float32(max_exact))
                 / np.float32(math.log(MAX_DISTANCE / max_exact)))
    large = np.minimum(max_exact + (log_ratio * np.float32(NUM_BUCKETS - max_exact)).astype(np.int32),
                       NUM_BUCKETS - 1)
    return np.where(dist < max_exact, dist, large)


def _bucket_or_masked(dist):
    return np.where((dist >= 0) & (dist <= WINDOW), _t5_bucket(dist), -1).astype(np.int32)


def _rotary_tables(pos):
    half = DK_C // 2
    inv = ROPE_BASE ** (-jnp.arange(half, dtype=F32) / half)
    ang = pos.astype(F32)[:, None] * inv[None, :]
    return jnp.cos(ang), jnp.sin(ang)


def _regroup_kernel(wt_ref, o_ref):
    o_ref[...] = wt_ref[...].T.astype(o_ref.dtype)


def _even_weight_layout(w_in):
    wt = jnp.swapaxes(w_in, 0, 1)
    src = lambda j: jnp.where(j < 4, j, jnp.where(j < 8, j + 1, jnp.where(j < 16, j + 5,
                                                                         jnp.where(j < 20, j - 7, 4))))
    main = pl.pallas_call(
        _regroup_kernel,
        grid=(EV_N // MXU_N,),
        in_specs=[pl.BlockSpec((MXU_N, D_MODEL), lambda j: (src(j), 0))],
        out_specs=pl.BlockSpec((D_MODEL, MXU_N), lambda j: (0, j)),
        out_shape=jax.ShapeDtypeStruct((D_MODEL, EV_N), BF16),
        compiler_params=_cparams(("arbitrary",)),
        name="regroup_even_weights",
    )(wt)
    lr_t = jnp.pad(wt[EV_N:], ((0, 128 - GLA_RANK), (0, 0))).astype(BF16)
    return main, lr_t


def _pad_tokens(a, ntok):
    pad = [(0, 0), (0, SAMPLE_PAD - ntok)] + [(0, 0)] * (a.ndim - 2)
    a = jnp.pad(a, pad)
    return a.reshape((a.shape[0] * SAMPLE_PAD,) + a.shape[2:])


def _rows(v, reps):
    return jnp.repeat(v, reps, axis=0)


def kernel(x_prompt, x_sample, cache_swa_k, cache_swa_v, state_gla, state_ret, c_prompt, c_sample, rel_bias,
           ada_w_even, ada_b_even, norm_g_even, w_in_even, w_lr_even, b_lr_even, qn_g_even, kn_g_even,
           sinks_even, gla_g_even, w_out_even, ada_w_odd, ada_b_odd, norm_g_odd, w_in_odd, ret_g_odd, w_out_odd):
    batch, seq, _ = x_prompt.shape
    nseq, ntok, _ = x_sample.shape
    mp = batch * seq

    c_all = jnp.concatenate([c_prompt, c_sample], axis=0)
    mod_e = _ada_mod(c_all, ada_w_even[0], ada_b_even[0])
    mod_o = _ada_mod(c_all, ada_w_odd[0], ada_b_odd[0])

    def split_mod(mod):
        shift, scale, gate = jnp.split(mod, 3, axis=1)
        p = tuple(a[:batch].reshape(batch, 1, D_MODEL) for a in (shift, scale, gate))
        s = tuple(_rows(a[batch:], SAMPLE_PAD) for a in (shift, scale, gate))
        return p, s

    (shift_ep, scale_ep, gate_ep), (shift_es, scale_es, gate_es) = split_mod(mod_e)
    (shift_op, scale_op, gate_op), (shift_os, scale_os, gate_os) = split_mod(mod_o)

    xp = x_prompt.reshape(mp, D_MODEL)
    xs = _pad_tokens(x_sample, ntok)
    ms = xs.shape[0]

    w_e, w_lri = _even_weight_layout(w_in_even[0])
    w_lr = jnp.pad(w_lr_even[0], ((0, 128 - GLA_RANK), (0, 0))).astype(BF16)
    b_lr = b_lr_even[0].reshape(1, -1)
    g_e = norm_g_even[0].reshape(1, D_MODEL)
    qn2 = jnp.tile(qn_g_even[0].reshape(1, HD_A), (1, 2))
    kn2 = jnp.tile(kn_g_even[0].reshape(1, HD_A), (1, 2))
    gla_g = gla_g_even[0].reshape(1, DV_B)
    w_out_e = w_out_even[0].astype(BF16)
    w_out_a, w_out_b = w_out_e[:W_A], w_out_e[W_A:]

    h0_p, la_p = _prenorm_even(xp, scale_ep, shift_ep, g_e, w_lri, w_lr, b_lr, tm=512, rows_per_batch=seq)
    h0_s, la_s = _prenorm_even(xs, scale_es, shift_es, g_e, w_lri, w_lr, b_lr, tm=512, rows_per_batch=ms)
    (proj_p,) = _inproj(h0_p, w_e, tm=1024, tn=EV_N // 3, out_dtype=BF16, name="inproj_even")
    (proj_s,) = _inproj(h0_s, w_e, tm=512, tn=EV_N // 3, out_dtype=F32, name="inproj_even")

    ii = np.arange(WINDOW)
    ss = np.arange(2 * WINDOW)
    bucket_p = jnp.asarray(_bucket_or_masked(WINDOW + ii[:, None] - ss[None, :]))
    mixed_a_p, k_last = _swa_prompt(proj_p, rel_bias, sinks_even[0], qn2, kn2, bucket_p, batch=batch, seq=seq)
    mixed_b_p, gla_p = _gla_prompt(proj_p, la_p, gla_g, batch=batch, seq=seq, rows=256)
    g_o = norm_g_odd[0].reshape(1, D_MODEL)
    y1_p, h1_p = _outproj([mixed_a_p, mixed_b_p], [w_out_a, w_out_b], xp, gate_ep, (scale_op, shift_op, g_o),
                          tm=512, rows_per_batch=seq)

    w_buf = cache_swa_k.shape[2]
    to_fp = lambda c: jnp.transpose(c[0], (0, 2, 3, 1)).reshape(nseq, W_K, w_buf)
    from_fp = lambda c: jnp.transpose(c.reshape(c.shape[0], H_A_KV, HD_A, -1), (0, 3, 1, 2))[None]
    proj_p3 = proj_p.reshape(batch, seq, EV_N)
    swa_k_p = from_fp(k_last)
    swa_v_p = proj_p3[:, seq - WINDOW:, EV_KV + W_K: EV_KV + 2 * W_K].astype(F32).reshape(
        1, batch, WINDOW, H_A_KV, HD_A)

    rr = np.arange(H_A * SAMPLE_PAD)
    row_head = 8 * (rr // 64) + 2 * ((rr % 32) // 8) + (rr % 64) // 32
    tt = rr % SAMPLE_PAD
    jj = np.arange(WINDOW)
    live = (tt < ntok)[:, None]
    bkc = jnp.asarray(np.where(live, _bucket_or_masked(WINDOW + tt[:, None] - jj[None, :]), -1))
    bkn = jnp.asarray(np.where(live & (jj[None, :] < ntok), _bucket_or_masked(tt[:, None] - jj[None, :]), -1))
    relrows = rel_bias.T[row_head]
    sinkrows = jnp.broadcast_to(sinks_even[0][row_head][:, None], (H_A * SAMPLE_PAD, 128))
    mixed_a_s, k_cache_s, v_cache_s = _swa_sample(
        proj_s, to_fp(cache_swa_k), to_fp(cache_swa_v),
        bkc, bkn, relrows, sinkrows, qn2, kn2, sb=16, ntok=ntok, unroll=4)
    mixed_b_s, gla_s = _gla_sample(proj_s, la_s, gla_g, state_gla[0], sb=8, ntok=ntok, unroll=4)
    y1_s, h1_s = _outproj([mixed_a_s, mixed_b_s], [w_out_a, w_out_b], xs, gate_es, (scale_os, shift_os, g_o),
                          tm=256, rows_per_batch=ms)

    w_o = w_in_odd[0].astype(BF16)
    ret_g = ret_g_odd[0].reshape(1, DV_C)
    w_out_o = w_out_odd[0].astype(BF16)
    cos_p, sin_p = _rotary_tables(jnp.arange(seq))
    pos_s = PAST_LEN + jnp.minimum(jnp.arange(SAMPLE_PAD), ntok - 1)
    cos_s, sin_s = _rotary_tables(jnp.tile(pos_s, nseq))

    (projo_s,) = _inproj(h1_s, w_o, (cos_s, sin_s), tm=512, tn=H_C * DK_C, out_dtype=F32, name="inproj_odd")
    ret_job = _ret_sample_job(projo_s, ret_g, state_ret[0], ntok=ntok)
    projo_p, o_s, ret_s = _inproj(h1_p, w_o, (cos_p, sin_p), ret_job, tm=256, tn=H_C * DK_C, out_dtype=BF16,
                                  name="inproj_odd_ret_sample")
    o_p, ret_p = _ret_prompt(projo_p, ret_g, batch=batch, seq=seq)
    (y2_p,) = _outproj([o_p], [w_out_o], y1_p, gate_op, tm=256, rows_per_batch=seq)
    (y2_s,) = _outproj([o_s], [w_out_o], y1_s, gate_os, tm=256, rows_per_batch=ms)

    y_prompt = y2_p.reshape(batch, seq, D_MODEL)
    y_sample = y2_s.reshape(nseq, SAMPLE_PAD, D_MODEL)[:, :ntok]
    return (y_prompt, y_sample, swa_k_p, swa_v_p, gla_p[None], ret_p[None],
            from_fp(k_cache_s), from_fp(v_cache_s),
            gla_s[None], ret_s[None])
```

```python
import functools
import math
from typing import Callable, NamedTuple

import numpy as np
import jax
import jax.numpy as jnp
from jax import lax
from jax.experimental import pallas as pl
from jax.experimental.pallas import tpu as pltpu

F32 = jnp.float32
BF16 = jnp.bfloat16

D_MODEL = 2048
WINDOW = 128
HD_A = 64
H_A = 16
H_A_KV = 2
G_A = 8
W_A = 1024
NUM_BUCKETS = 32
MAX_DISTANCE = 128
H_B = 4
DV_B = 256
DK_B = 128
W_B = 1024
GLA_RANK = 16
GLA_TAU = 16.0
GLA_CHUNK = 64
H_C = 8
DK_C = 256
DV_C = 512
W_C = 4096
RET_CHUNK = 128
RET_PROMPT_CHUNK = 256
ROPE_BASE = 10000.0
EPS = 1e-6
NEG_INF = -1e30
PAST_LEN = 8192

EV_QA, EV_GA, EV_VB, EV_GB, EV_QB, EV_KB, EV_KV = 0, 1024, 2048, 3072, 4096, 4608, 5120
EV_N = 5376
OD_N = 12288
W_K = H_A_KV * HD_A
SAMPLE_PAD = 8
MXU_N = 256
VMEM_LIMIT = 56 * 1024 * 1024

LOG_GAMMA = [float(np.log1p(-np.exp2(np.float32(-5.0 - h)))) for h in range(H_C)]


def _cparams(sem):
    return pltpu.CompilerParams(dimension_semantics=sem, vmem_limit_bytes=VMEM_LIMIT)


def _silu(x):
    t = 0.5 * x
    return t * (1.0 + jnp.tanh(t))


def _dot(a, b):
    return jnp.dot(a, b, preferred_element_type=F32)


def _dot_nt(a, b):
    return lax.dot_general(a, b, (((1,), (1,)), ((), ())), preferred_element_type=F32)


def _dot_tn(a, b):
    return lax.dot_general(a, b, (((0,), (0,)), ((), ())), preferred_element_type=F32)


def _cumsum_rows(x):
    c = x.shape[0]
    r = lax.broadcasted_iota(jnp.int32, (c, c), 0)
    s = lax.broadcasted_iota(jnp.int32, (c, c), 1)
    tri = (r >= s).astype(BF16)
    hi = x.astype(BF16)
    r1 = x - hi.astype(F32)
    mid = r1.astype(BF16)
    lo = (r1 - mid.astype(F32)).astype(BF16)
    return _dot(tri, hi) + _dot(tri, mid) + _dot(tri, lo)


def _row_to_col(r):
    n = r.shape[1]
    ri = lax.broadcasted_iota(jnp.int32, (n, n), 0)
    ci = lax.broadcasted_iota(jnp.int32, (n, n), 1)
    return jnp.sum(jnp.where(ri == ci, jnp.broadcast_to(r, (n, n)), 0.0), axis=1, keepdims=True)


def _ada_kernel(c_ref, w_ref, b_ref, o_ref):
    sc = _silu(c_ref[...]).astype(BF16)
    o_ref[...] = _dot(sc, w_ref[...].astype(BF16)) + b_ref[...]


def _ada_mod(c_all, w, b):
    m = c_all.shape[0]
    n = w.shape[1]
    tn = 1536
    return pl.pallas_call(
        _ada_kernel,
        grid=(n // tn,),
        in_specs=[pl.BlockSpec((m, D_MODEL), lambda j: (0, 0)),
                  pl.BlockSpec((D_MODEL, tn), lambda j: (0, j)),
                  pl.BlockSpec((1, tn), lambda j: (0, j))],
        out_specs=pl.BlockSpec((m, tn), lambda j: (0, j)),
        out_shape=jax.ShapeDtypeStruct((m, n), F32),
        compiler_params=_cparams(("arbitrary",)),
        name="ada_mod",
    )(c_all, w, b.reshape(1, n))


def _norm_mod(x, g, scale, shift):
    ms = jnp.mean(x * x, axis=-1, keepdims=True)
    y = x * lax.rsqrt(ms + EPS) * g
    return y * (1.0 + scale) + shift


def _log_sigmoid(z):
    return jnp.minimum(z, 0.0) - jnp.log(1.0 + jnp.exp(-jnp.abs(z)))


def _mod_spec(arr, tm, rows_per_batch):
    if arr.ndim == 3:
        return pl.BlockSpec((None, 1, D_MODEL), lambda i: (i // (rows_per_batch // tm), 0, 0))
    return pl.BlockSpec((tm, D_MODEL), lambda i: (i, 0))


def _prenorm_even_kernel(x_ref, scale_ref, shift_ref, g_ref, wlri_ref, wlr_ref, blr_ref, h_ref, la_ref):
    hb = _norm_mod(x_ref[...], g_ref[...], scale_ref[...], shift_ref[...]).astype(BF16)
    h_ref[...] = hb
    lr = _dot_nt(hb, wlri_ref[...])
    z = _dot(lr.astype(BF16), wlr_ref[...]) + blr_ref[...]
    la_ref[...] = _log_sigmoid(z) / GLA_TAU


def _prenorm_even(x, scale, shift, g, wlri, wlr, blr, *, tm, rows_per_batch):
    m = x.shape[0]
    const = lambda shape: pl.BlockSpec(shape, lambda i: (0, 0))
    return pl.pallas_call(
        _prenorm_even_kernel,
        grid=(m // tm,),
        in_specs=[pl.BlockSpec((tm, D_MODEL), lambda i: (i, 0)),
                  _mod_spec(scale, tm, rows_per_batch), _mod_spec(shift, tm, rows_per_batch),
                  const((1, D_MODEL)), const((128, D_MODEL)), const((128, H_B * DK_B)), const((1, H_B * DK_B))],
        out_specs=[pl.BlockSpec((tm, D_MODEL), lambda i: (i, 0)),
                   pl.BlockSpec((tm, H_B * DK_B), lambda i: (i, 0))],
        out_shape=[jax.ShapeDtypeStruct((m, D_MODEL), BF16),
                   jax.ShapeDtypeStruct((m, H_B * DK_B), F32)],
        compiler_params=_cparams(("arbitrary",)),
        name="prenorm_even",
    )(x, scale, shift, g, wlri, wlr, blr)


class _SideJob(NamedTuple):
    njobs: int
    args: list
    in_specs: Callable
    out_specs: Callable
    out_shape: list
    units: int
    body: Callable


def _inproj_kernel(*refs, rot_tiles, side, nsteps):
    nmain = 4 if rot_tiles else 2
    j = pl.program_id(0)
    run_side = None
    if side is not None:
        njobs, nsi, nso, body = side
        side_in = refs[nmain:nmain + nsi]
        side_out = refs[nmain + nsi + 1:nmain + nsi + 1 + nso]
        step = j * pl.num_programs(1) + pl.program_id(1)
        block = (step * njobs) // nsteps
        prev_block = ((step - 1) * njobs) // nsteps
        run_side = (step == 0) | (block != prev_block)
        refs = refs[:nmain] + (refs[nmain + nsi],)
    h_ref, w_ref, *rest = refs
    o_ref = rest[-1]
    nchunk = o_ref.shape[1] // MXU_N
    half = DK_C // 2

    def plain_chunks():
        def chunk(c):
            cs = slice(c * MXU_N, (c + 1) * MXU_N)
            o_ref[:, cs] = _dot(h_ref[...], w_ref[:, cs]).astype(o_ref.dtype)
        return chunk

    def rotary_chunks():
        mult = jnp.where(j < rot_tiles // 2, 1.0, DK_C ** -0.5)
        cos = rest[0][...] * mult
        sin = rest[1][...] * mult

        def chunk(c):
            acc = _dot(h_ref[...], w_ref[:, c * DK_C:(c + 1) * DK_C])
            x1 = acc[:, :half]
            x2 = acc[:, half:]
            o_ref[:, c * DK_C:c * DK_C + half] = (x1 * cos - x2 * sin).astype(o_ref.dtype)
            o_ref[:, c * DK_C + half:(c + 1) * DK_C] = (x2 * cos + x1 * sin).astype(o_ref.dtype)
        return chunk

    def tile(make_chunk, with_side):
        def run():
            chunk = make_chunk()
            if with_side:
                body(side_in, side_out, chunk)
            else:
                for c in range(nchunk):
                    chunk(c)
        return run

    kinds = [(plain_chunks, j >= rot_tiles)]
    if rot_tiles:
        kinds.append((rotary_chunks, j < rot_tiles))
    for make_chunk, is_kind in kinds:
        if run_side is None:
            pl.when(is_kind)(tile(make_chunk, False))
        else:
            pl.when(is_kind & run_side)(tile(make_chunk, True))
            pl.when(is_kind & jnp.logical_not(run_side))(tile(make_chunk, False))


def _inproj(h, w, rot=None, side=None, *, tm, tn, out_dtype, name):
    m = h.shape[0]
    n = w.shape[1]
    ni = m // tm
    nsteps = (n // tn) * ni
    in_specs = [pl.BlockSpec((tm, D_MODEL), lambda j, i: (i, 0)),
                pl.BlockSpec((D_MODEL, tn), lambda j, i: (0, j))]
    args = [h, w]
    rot_tiles = 0
    if rot is not None:
        nblk = rot[0].shape[0] // tm
        in_specs += [pl.BlockSpec((tm, DK_C // 2), lambda j, i: (i % nblk, 0))] * 2
        args += list(rot)
        rot_tiles = 2 * (H_C * DK_C) // tn
    out_specs = [pl.BlockSpec((tm, tn), lambda j, i: (i, j))]
    out_shape = [jax.ShapeDtypeStruct((m, n), out_dtype)]
    side_static = None
    if side is not None:
        assert side.njobs <= nsteps and side.units == tn // MXU_N
        block = lambda j, i: ((j * ni + i) * side.njobs) // nsteps
        in_specs += side.in_specs(block)
        args += side.args
        out_specs += side.out_specs(block)
        out_shape += side.out_shape
        side_static = (side.njobs, len(side.args), len(side.out_shape), side.body)
    return pl.pallas_call(
        functools.partial(_inproj_kernel, rot_tiles=rot_tiles, side=side_static, nsteps=nsteps),
        grid=(n // tn, ni),
        in_specs=in_specs,
        out_specs=out_specs,
        out_shape=out_shape,
        compiler_params=_cparams(("arbitrary", "arbitrary")),
        name=name,
    )(*args)


def _outproj_kernel(*refs, nin, with_next):
    m_refs, w_refs = refs[:nin], refs[nin:2 * nin]
    if with_next:
        x_ref, gate_ref, nscale_ref, nshift_ref, ng_ref, o_ref, h_ref = refs[2 * nin:]
    else:
        x_ref, gate_ref, o_ref = refs[2 * nin:]
    for c in range(o_ref.shape[1] // MXU_N):
        cs = slice(c * MXU_N, (c + 1) * MXU_N)
        acc = _dot(m_refs[0][...].astype(BF16), w_refs[0][:, cs])
        for m_ref, w_ref in zip(m_refs[1:], w_refs[1:]):
            acc += _dot(m_ref[...].astype(BF16), w_ref[:, cs])
        o_ref[:, cs] = x_ref[:, cs] + gate_ref[:, cs] * acc
    if with_next:
        h_ref[...] = _norm_mod(o_ref[...], ng_ref[...], nscale_ref[...], nshift_ref[...]).astype(BF16)


def _outproj(mixed, ws, x, gate, next_mod=None, *, tm, rows_per_batch):
    m = x.shape[0]
    row = pl.BlockSpec((tm, D_MODEL), lambda i: (i, 0))
    in_specs = ([pl.BlockSpec((tm, a.shape[1]), lambda i: (i, 0)) for a in mixed]
                + [pl.BlockSpec(w.shape, lambda i: (0, 0), pipeline_mode=pl.Buffered(1)) for w in ws]
                + [row, _mod_spec(gate, tm, rows_per_batch)])
    args = [*mixed, *ws, x, gate]
    out_specs = [row]
    out_shape = [jax.ShapeDtypeStruct((m, D_MODEL), F32)]
    if next_mod is not None:
        nscale, nshift, ng = next_mod
        in_specs += [_mod_spec(nscale, tm, rows_per_batch), _mod_spec(nshift, tm, rows_per_batch),
                     pl.BlockSpec((1, D_MODEL), lambda i: (0, 0))]
        args += [nscale, nshift, ng]
        out_specs.append(row)
        out_shape.append(jax.ShapeDtypeStruct((m, D_MODEL), BF16))
    return pl.pallas_call(
        functools.partial(_outproj_kernel, nin=len(mixed), with_next=next_mod is not None),
        grid=(m // tm,),
        in_specs=in_specs,
        out_specs=out_specs,
        out_shape=out_shape,
        compiler_params=_cparams(("arbitrary",)),
        name="outproj",
    )(*args)


def _seg_rms(x, g2, seg_ones):
    x2 = x * x
    hi = x2.astype(BF16)
    lo = (x2 - hi.astype(F32)).astype(BF16)
    ss = _dot(hi, seg_ones) + _dot(lo, seg_ones)
    return x * lax.rsqrt(ss * (1.0 / HD_A) + EPS) * g2


def _dup_halves(x, lo_half):
    sw = pltpu.roll(x, HD_A, 1)
    return [jnp.where(lo_half, x, sw), jnp.where(lo_half, sw, x)]


def _rows_to_cols(x):
    pad = jnp.zeros((x.shape[1] - x.shape[0], x.shape[1]), x.dtype)
    return jnp.concatenate([x, pad], axis=0).T


def _seg_ones():
    r = lax.broadcasted_iota(jnp.int32, (2 * HD_A, 2 * HD_A), 0)
    c = lax.broadcasted_iota(jnp.int32, (2 * HD_A, 2 * HD_A), 1)
    return ((r < HD_A) == (c < HD_A)).astype(BF16)


def _swa_prompt_kernel(relb_ref, sinks_ref, bucket_ref, q_ref, ga_ref, kv_ref, qn_ref, kn_ref,
                       o_ref, knew_ref, bias_ref, kprev_ref, vprev_ref, s_ref, pe_ref):
    b = pl.program_id(0)
    i = pl.program_id(1)

    @pl.when((b == 0) & (i == 0))
    def _():
        bk = bucket_ref[...]
        for h in range(H_A):
            bias_ref[h] = jnp.full((WINDOW, 2 * WINDOW), NEG_INF, F32)
        for bb in range(NUM_BUCKETS):
            hit = bk == bb
            for h in range(H_A):
                bias_ref[h] = jnp.where(hit, relb_ref[bb, h], bias_ref[h])
        col = lax.broadcasted_iota(jnp.int32, (WINDOW, 2 * WINDOW), 1)
        for h in range(H_A):
            bias_ref[H_A + h] = jnp.where(col >= WINDOW, bias_ref[h], NEG_INF)

    @pl.when(i == 0)
    def _():
        kprev_ref[...] = jnp.zeros(kprev_ref.shape, BF16)
        vprev_ref[...] = jnp.zeros(vprev_ref.shape, BF16)

    seg_ones = _seg_ones()
    lt = 2 * HD_A
    tpg = G_A // 2
    lo_half = lax.broadcasted_iota(jnp.int32, (WINDOW, lt), 1) < HD_A
    kv = kv_ref[...].astype(F32)
    kc = _seg_rms(kv[:, :W_K], kn_ref[...], seg_ones)
    k_dup = _dup_halves(kc, lo_half)
    v_dup = _dup_halves(kv[:, W_K:], lo_half)
    base = jnp.where(i == 0, H_A, 0)
    qn2 = qn_ref[...] * HD_A ** -0.5
    gq = tpg * WINDOW
    lo_g = lax.broadcasted_iota(jnp.int32, (gq, lt), 1) < HD_A
    grows = [slice(g * 2 * gq, (g + 1) * 2 * gq) for g in range(H_A_KV)]
    for g in range(H_A_KV):
        kk = jnp.concatenate([kprev_ref[g], k_dup[g].astype(BF16)], axis=0)
        xg = jnp.concatenate([q_ref[:, (g * tpg + p) * lt:(g * tpg + p + 1) * lt] for p in range(tpg)],
                             axis=0).astype(F32)
        xn = _seg_rms(xg, qn2, seg_ones)
        q_all = jnp.concatenate([jnp.where(lo_g, xn, 0.0), jnp.where(lo_g, 0.0, xn)], axis=0).astype(BF16)
        s_ref[grows[g], :] = _dot_nt(q_all, kk)
    sink_terms = []
    for g in range(H_A_KV):
        for n in range(G_A):
            h = g * G_A + 2 * (n % tpg) + n // tpg
            rows = slice((g * G_A + n) * WINDOW, (g * G_A + n + 1) * WINDOW)
            s = s_ref[rows, :] + bias_ref[base + h]
            sink = sinks_ref[h]
            m = jnp.maximum(jnp.max(s, axis=-1, keepdims=True), sink)
            pe_ref[rows, :] = jnp.exp(s - m).astype(BF16)
            sink_terms.append(jnp.exp(sink - m))
    o_ext = []
    for g in range(H_A_KV):
        vv = jnp.concatenate([vprev_ref[g], v_dup[g].astype(BF16)], axis=0)
        vv_ext = jnp.concatenate([vv, jnp.ones(vv.shape, BF16)], axis=1)
        o_ext.append(_dot(pe_ref[grows[g], :], vv_ext))
    for g in range(H_A_KV):
        for p in range(tpg):
            halves = []
            for a in range(2):
                n = a * tpg + p
                rows = slice(n * WINDOW, (n + 1) * WINDOW)
                halves.append(o_ext[g][rows, :lt] / (o_ext[g][rows, lt:] + sink_terms[g * G_A + n]))
            oa = jnp.where(lo_half, halves[0], halves[1])
            cols = slice((g * tpg + p) * lt, (g * tpg + p + 1) * lt)
            o_ref[:, cols] = (oa * _silu(ga_ref[:, cols].astype(F32))).astype(o_ref.dtype)
    for g in range(H_A_KV):
        kprev_ref[g] = k_dup[g].astype(BF16)
        vprev_ref[g] = v_dup[g].astype(BF16)

    @pl.when(i == pl.num_programs(1) - 1)
    def _():
        knew_ref[...] = kc.T


def _swa_prompt(proj, rel_bias, sinks, qn2, kn2, bucket, *, batch, seq):
    nb = seq // WINDOW
    rb = lambda b, i: b * nb + i
    smem = pl.BlockSpec(memory_space=pltpu.SMEM)
    return pl.pallas_call(
        _swa_prompt_kernel,
        grid=(batch, nb),
        in_specs=[smem, smem,
                  pl.BlockSpec((WINDOW, 2 * WINDOW), lambda b, i: (0, 0)),
                  pl.BlockSpec((WINDOW, W_A), lambda b, i: (rb(b, i), EV_QA // W_A)),
                  pl.BlockSpec((WINDOW, W_A), lambda b, i: (rb(b, i), EV_GA // W_A)),
                  pl.BlockSpec((WINDOW, 2 * W_K), lambda b, i: (rb(b, i), EV_KV // (2 * W_K))),
                  pl.BlockSpec((1, 2 * HD_A), lambda b, i: (0, 0)),
                  pl.BlockSpec((1, 2 * HD_A), lambda b, i: (0, 0))],
        out_specs=[pl.BlockSpec((WINDOW, W_A), lambda b, i: (rb(b, i), 0)),
                   pl.BlockSpec((None, WINDOW, W_K), lambda b, i: (b, 0, 0))],
        out_shape=[jax.ShapeDtypeStruct((batch * seq, W_A), BF16),
                   jax.ShapeDtypeStruct((batch, WINDOW, W_K), F32)],
        scratch_shapes=[pltpu.VMEM((2 * H_A, WINDOW, 2 * WINDOW), F32),
                        pltpu.VMEM((H_A_KV, WINDOW, W_K), BF16), pltpu.VMEM((H_A_KV, WINDOW, W_K), BF16),
                        pltpu.VMEM((H_A * WINDOW, 2 * WINDOW), F32), pltpu.VMEM((H_A * WINDOW, 2 * WINDOW), BF16)],
        compiler_params=_cparams(("arbitrary", "arbitrary")),
        name="swa_prompt",
    )(rel_bias, sinks, bucket, proj, proj, proj, qn2, kn2)


def _swa_sample_kernel(bkc_ref, bkn_ref, relrows_ref, sinkrows_ref, q_ref, ga_ref, kvn_ref, ck_ref, cv_ref,
                       qn_ref, kn_ref, o_ref, ko_ref, vo_ref, biasc_ref, biasn_ref, *, sb, ntok, unroll):
    @pl.when(pl.program_id(0) == 0)
    def _():
        bkc = bkc_ref[...]
        bkn = bkn_ref[...]
        rr = relrows_ref[...]
        bc = jnp.full(bkc.shape, NEG_INF, F32)
        bn = jnp.full(bkn.shape, NEG_INF, F32)
        for bb in range(NUM_BUCKETS):
            val = rr[:, bb:bb + 1]
            bc = jnp.where(bkc == bb, val, bc)
            bn = jnp.where(bkn == bb, val, bn)
        biasc_ref[...] = bc
        biasn_ref[...] = bn

    seg_ones = _seg_ones()
    lt = 2 * HD_A
    nrow = G_A * SAMPLE_PAD
    grow = nrow // H_A_KV
    lo_q = lax.broadcasted_iota(jnp.int32, (grow, lt), 1) < HD_A
    lo_n = lax.broadcasted_iota(jnp.int32, (SAMPLE_PAD, lt), 1) < HD_A
    lane_pos = lax.broadcasted_iota(jnp.int32, (W_K, WINDOW), 1)
    qn2 = qn_ref[...] * HD_A ** -0.5
    kn2 = kn_ref[...]
    sink = sinkrows_ref[:, 0:1]

    def body(it, carry):
        seqs = [it * unroll + u for u in range(unroll)]
        rows = [pl.ds(pl.multiple_of(s * SAMPLE_PAD, SAMPLE_PAD), SAMPLE_PAD) for s in seqs]
        kvn = [kvn_ref[r, :] for r in rows]
        kn_all = _seg_rms(jnp.concatenate([x[:, :W_K] for x in kvn], axis=0), kn2, seg_ones)
        q2_all = jnp.concatenate([q_ref[r, p * lt:(p + 1) * lt] for r in rows for p in range(G_A)], axis=0)
        xn_all = _seg_rms(q2_all, qn2, seg_ones)
        sc, sn, kn = [], [], []
        for u, s in enumerate(seqs):
            kn.append(kn_all[u * SAMPLE_PAD:(u + 1) * SAMPLE_PAD])
            kn_dup = _dup_halves(kn[u], lo_n)
            scg, sng = [], []
            for g in range(H_A_KV):
                xn = xn_all[u * nrow + g * grow:u * nrow + (g + 1) * grow]
                q4 = jnp.concatenate([jnp.where(lo_q, xn, 0.0), jnp.where(lo_q, 0.0, xn)], axis=0).astype(BF16)
                kt_g = ck_ref[s, g * HD_A:(g + 1) * HD_A, :].astype(BF16)
                scg.append(_dot(q4, jnp.concatenate([kt_g, kt_g], axis=0)))
                sng.append(_dot_nt(q4, kn_dup[g].astype(BF16)))
            sc.append(jnp.concatenate(scg, axis=0))
            sn.append(jnp.concatenate(sng, axis=0))
        pc, pn, sink_terms = [], [], []
        for u in range(unroll):
            scu = sc[u] + biasc_ref[...]
            snu = sn[u] + biasn_ref[:, :SAMPLE_PAD]
            m = jnp.maximum(jnp.maximum(jnp.max(scu, axis=-1, keepdims=True),
                                        jnp.max(snu, axis=-1, keepdims=True)), sink)
            pc.append(jnp.exp(scu - m).astype(BF16))
            pn.append(jnp.exp(snu - m).astype(BF16))
            sink_terms.append(jnp.exp(sink - m))
        o_ext = []
        for u, s in enumerate(seqs):
            vn_dup = _dup_halves(kvn[u][:, W_K:], lo_n)
            og = []
            for g in range(H_A_KV):
                vt_g = cv_ref[s, g * HD_A:(g + 1) * HD_A, :].astype(BF16)
                vt_ext = jnp.concatenate([vt_g, vt_g, jnp.ones((lt, WINDOW), BF16)], axis=0)
                vn_ext = jnp.concatenate([vn_dup[g].astype(BF16), jnp.ones((SAMPLE_PAD, lt), BF16)], axis=1)
                gr = slice(g * 2 * grow, (g + 1) * 2 * grow)
                og.append(_dot_nt(pc[u][gr], vt_ext) + _dot(pn[u][gr], vn_ext))
            o_ext.append(jnp.concatenate(og, axis=0))
        for u, s in enumerate(seqs):
            o4 = o_ext[u][:, :lt] / (o_ext[u][:, lt:] + sink_terms[u])
            xg = ga_ref[rows[u], :]
            for p in range(G_A):
                g, pp = divmod(p, G_A // 2)
                r0 = g * 2 * grow + pp * SAMPLE_PAD
                o2 = jnp.where(lo_n, o4[r0:r0 + SAMPLE_PAD], o4[r0 + grow:r0 + grow + SAMPLE_PAD])
                o_ref[rows[u], p * lt:(p + 1) * lt] = (o2 * _silu(xg[:, p * lt:(p + 1) * lt])).astype(o_ref.dtype)
            ko_ref[s] = pltpu.roll(jnp.where(lane_pos < ntok, _rows_to_cols(kn[u]), ck_ref[s]), WINDOW - ntok, 1)
            vo_ref[s] = pltpu.roll(jnp.where(lane_pos < ntok, _rows_to_cols(kvn[u][:, W_K:]), cv_ref[s]),
                                   WINDOW - ntok, 1)
        return carry

    lax.fori_loop(0, sb // unroll, body, 0)


def _swa_sample(proj, cache_k, cache_v, bkc, bkn, relrows, sinkrows, qn2, kn2, *, sb, ntok, unroll):
    nseq = cache_k.shape[0]
    rows = sb * SAMPLE_PAD
    full = lambda shape: pl.BlockSpec(shape, lambda i: tuple(0 for _ in shape))
    cache = pl.BlockSpec((sb, W_K, WINDOW), lambda i: (i, 0, 0))
    return pl.pallas_call(
        functools.partial(_swa_sample_kernel, sb=sb, ntok=ntok, unroll=unroll),
        grid=(nseq // sb,),
        in_specs=[full(bkc.shape), full(bkn.shape), full(relrows.shape), full(sinkrows.shape),
                  pl.BlockSpec((rows, W_A), lambda i: (i, EV_QA // W_A)),
                  pl.BlockSpec((rows, W_A), lambda i: (i, EV_GA // W_A)),
                  pl.BlockSpec((rows, 2 * W_K), lambda i: (i, EV_KV // (2 * W_K))),
                  cache, cache, full((1, 2 * HD_A)), full((1, 2 * HD_A))],
        out_specs=[pl.BlockSpec((rows, W_A), lambda i: (i, 0)), cache, cache],
        out_shape=[jax.ShapeDtypeStruct((nseq * SAMPLE_PAD, W_A), F32),
                   jax.ShapeDtypeStruct(cache_k.shape, F32),
                   jax.ShapeDtypeStruct(cache_v.shape, F32)],
        scratch_shapes=[pltpu.VMEM(bkc.shape, F32), pltpu.VMEM(bkn.shape, F32)],
        compiler_params=_cparams(("arbitrary",)),
        name="swa_sample",
    )(bkc, bkn, relrows, sinkrows, proj, proj, proj, cache_k, cache_v, qn2, kn2)


def _gla_pre(q, k, v, la, n_valid):
    c = q.shape[0]
    bcum = _cumsum_rows(la)
    rr = lax.broadcasted_iota(jnp.int32, (c, c), 0)
    cc = lax.broadcasted_iota(jnp.int32, (c, c), 1)
    causal = rr >= cc
    row = lax.broadcasted_iota(jnp.int32, (c, 1), 0)
    qts, kts, kds, vbs, dcols = [], [], [], [], []
    for h in range(H_B):
        ks = slice(h * DK_B, (h + 1) * DK_B)
        bc = bcum[:, ks]
        qts.append(((q[:, ks] * DK_B ** -0.5) * jnp.exp(bc)).astype(BF16))
        kts.append((k[:, ks] * jnp.exp(-bc)).astype(BF16))
        blast = bc[n_valid - 1:n_valid, :]
        kd = k[:, ks] * jnp.exp(blast - bc)
        if n_valid < c:
            kd = jnp.where(row < n_valid, kd, 0.0)
        kds.append(kd.astype(BF16))
        vbs.append(v[:, h * DV_B:(h + 1) * DV_B].astype(BF16))
        dcols.append(_row_to_col(jnp.exp(blast)))
    scores = [_dot_nt(qts[h], kts[h]) for h in range(H_B)]
    upds = [_dot_tn(kds[h], vbs[h]) for h in range(H_B)]
    return [(qts[h], jnp.where(causal, scores[h], 0.0).astype(BF16), vbs[h], upds[h], dcols[h])
            for h in range(H_B)]


def _gla_post(pre, gb, gla_g, states):
    outs = [_dot(a, vb) + _dot(qt, st.astype(BF16)) for (qt, a, vb, _, _), st in zip(pre, states)]
    new_states = [dcol * st + upd for (_, _, _, upd, dcol), st in zip(pre, states)]
    gated = []
    for h, o in enumerate(outs):
        on = o * lax.rsqrt(jnp.mean(o * o, axis=-1, keepdims=True) + EPS) * gla_g
        gated.append(on * _silu(gb[:, h * DV_B:(h + 1) * DV_B]))
    return jnp.concatenate(gated, axis=1), new_states


def _gla_prompt_kernel(q_ref, k_ref, v_ref, gb_ref, la_ref, g_ref, o_ref, s_ref, *, nchunk):
    @pl.when(pl.program_id(1) == 0)
    def _():
        s_ref[...] = jnp.zeros(s_ref.shape, F32)

    gla_g = g_ref[...]
    chunk_rows = [slice(c * GLA_CHUNK, (c + 1) * GLA_CHUNK) for c in range(nchunk)]
    pres = [_gla_pre(q_ref[r, :].astype(F32), k_ref[r, :].astype(F32), v_ref[r, :].astype(F32), la_ref[r, :],
                     GLA_CHUNK) for r in chunk_rows]
    states = [s_ref[h] for h in range(H_B)]
    for r, pre in zip(chunk_rows, pres):
        out, states = _gla_post(pre, gb_ref[r, :].astype(F32), gla_g, states)
        o_ref[r, :] = out.astype(o_ref.dtype)
    for h in range(H_B):
        s_ref[h] = states[h]


def _gla_prompt(proj, la, gla_g, *, batch, seq, rows):
    nstep = seq // rows
    rb = lambda b, i: b * nstep + i
    return pl.pallas_call(
        functools.partial(_gla_prompt_kernel, nchunk=rows // GLA_CHUNK),
        grid=(batch, nstep),
        in_specs=[pl.BlockSpec((rows, H_B * DK_B), lambda b, i: (rb(b, i), EV_QB // (H_B * DK_B))),
                  pl.BlockSpec((rows, H_B * DK_B), lambda b, i: (rb(b, i), EV_KB // (H_B * DK_B))),
                  pl.BlockSpec((rows, W_B), lambda b, i: (rb(b, i), EV_VB // W_B)),
                  pl.BlockSpec((rows, W_B), lambda b, i: (rb(b, i), EV_GB // W_B)),
                  pl.BlockSpec((rows, H_B * DK_B), lambda b, i: (rb(b, i), 0)),
                  pl.BlockSpec((1, DV_B), lambda b, i: (0, 0))],
        out_specs=[pl.BlockSpec((rows, W_B), lambda b, i: (rb(b, i), 0)),
                   pl.BlockSpec((None, H_B, DK_B, DV_B), lambda b, i: (b, 0, 0, 0))],
        out_shape=[jax.ShapeDtypeStruct((batch * seq, W_B), BF16),
                   jax.ShapeDtypeStruct((batch, H_B, DK_B, DV_B), F32)],
        compiler_params=_cparams(("arbitrary", "arbitrary")),
        name="gla_prompt",
    )(proj, proj, proj, proj, la, gla_g)


def _gla_sample_kernel(q_ref, k_ref, v_ref, gb_ref, la_ref, g_ref, s_in_ref, o_ref, s_ref, *, sb, ntok, unroll):
    gla_g = g_ref[...]

    def body(it, carry):
        seqs = [it * unroll + u for u in range(unroll)]
        rows = [pl.ds(pl.multiple_of(s * SAMPLE_PAD, SAMPLE_PAD), SAMPLE_PAD) for s in seqs]
        pres = [_gla_pre(q_ref[r, :], k_ref[r, :], v_ref[r, :], la_ref[r, :], ntok) for r in rows]
        for s, r, pre in zip(seqs, rows, pres):
            out, new_states = _gla_post(pre, gb_ref[r, :], gla_g, [s_in_ref[s, h] for h in range(H_B)])
            o_ref[r, :] = out.astype(o_ref.dtype)
            for h in range(H_B):
                s_ref[s, h] = new_states[h]
        return carry

    lax.fori_loop(0, sb // unroll, body, 0)


def _gla_sample(proj, la, gla_g, state, *, sb, ntok, unroll):
    nseq = state.shape[0]
    rows = sb * SAMPLE_PAD
    return pl.pallas_call(
        functools.partial(_gla_sample_kernel, sb=sb, ntok=ntok, unroll=unroll),
        grid=(nseq // sb,),
        in_specs=[pl.BlockSpec((rows, H_B * DK_B), lambda i: (i, EV_QB // (H_B * DK_B))),
                  pl.BlockSpec((rows, H_B * DK_B), lambda i: (i, EV_KB // (H_B * DK_B))),
                  pl.BlockSpec((rows, W_B), lambda i: (i, EV_VB // W_B)),
                  pl.BlockSpec((rows, W_B), lambda i: (i, EV_GB // W_B)),
                  pl.BlockSpec((rows, H_B * DK_B), lambda i: (i, 0)),
                  pl.BlockSpec((1, DV_B), lambda i: (0, 0)),
                  pl.BlockSpec((sb, H_B, DK_B, DV_B), lambda i: (i, 0, 0, 0))],
        out_specs=[pl.BlockSpec((rows, W_B), lambda i: (i, 0)),
                   pl.BlockSpec((sb, H_B, DK_B, DV_B), lambda i: (i, 0, 0, 0))],
        out_shape=[jax.ShapeDtypeStruct((nseq * SAMPLE_PAD, W_B), BF16),
                   jax.ShapeDtypeStruct(state.shape, F32)],
        compiler_params=_cparams(("arbitrary",)),
        name="gla_sample",
    )(proj, proj, proj, proj, la, gla_g, state)


def _ret_decay(c, h):
    ri = lax.broadcasted_iota(jnp.int32, (c, c), 0)
    ci = lax.broadcasted_iota(jnp.int32, (c, c), 1)
    dist = (ri - ci).astype(F32)
    return jnp.where(ri >= ci, jnp.exp(jnp.maximum(dist, 0.0) * LOG_GAMMA[h]), 0.0)


def _ret_chunk(q_ref, k_ref, v_ref, g_ref, retg, state_in_ref, state_ref, o_ref, n_valid, after_head=None,
               decay_ref=None):
    c = q_ref.shape[0]
    row = lax.broadcasted_iota(jnp.int32, (c, 1), 0)
    rowf = row.astype(F32)
    a_heads = []
    for h in range(H_C):
        ks = slice(h * DK_C, (h + 1) * DK_C)
        decay = _ret_decay(c, h) if decay_ref is None else decay_ref[h]
        a_heads.append((_dot_nt(q_ref[:, ks].astype(BF16), k_ref[:, ks].astype(BF16)) * decay).astype(BF16))
    for h in range(H_C):
        lg = LOG_GAMMA[h]
        ks = slice(h * DK_C, (h + 1) * DK_C)
        vs = slice(h * DV_C, (h + 1) * DV_C)
        kf = k_ref[:, ks].astype(F32)
        vb = v_ref[:, vs].astype(BF16)
        st = state_in_ref[h]
        q_in = (q_ref[:, ks].astype(F32) * jnp.exp((rowf + 1.0) * lg)).astype(BF16)
        o = _dot(a_heads[h], vb) + _dot(q_in, st.astype(BF16))
        kd = kf * jnp.exp((n_valid - 1.0 - rowf) * lg)
        if n_valid < c:
            kd = jnp.where(row < n_valid, kd, 0.0)
        state_ref[h] = math.exp(n_valid * lg) * st + _dot_tn(kd.astype(BF16), vb)
        on = o * lax.rsqrt(jnp.mean(o * o, axis=-1, keepdims=True) + EPS) * retg
        o_ref[:, vs] = (on * _silu(g_ref[:, vs].astype(F32))).astype(o_ref.dtype)
        if after_head is not None:
            after_head(h)


def _ret_prompt_kernel(q_ref, k_ref, v_ref, g_ref, retg_ref, o_ref, s_ref, decay_ref):
    chunk = q_ref.shape[0]

    @pl.when((pl.program_id(0) == 0) & (pl.program_id(1) == 0))
    def _():
        for h in range(H_C):
            decay_ref[h] = _ret_decay(chunk, h)

    @pl.when(pl.program_id(1) == 0)
    def _():
        s_ref[...] = jnp.zeros(s_ref.shape, F32)

    _ret_chunk(q_ref, k_ref, v_ref, g_ref, retg_ref[...], s_ref, s_ref, o_ref, chunk, decay_ref=decay_ref)


def _ret_prompt(proj, ret_g, *, batch, seq):
    chunk = RET_PROMPT_CHUNK if seq % RET_PROMPT_CHUNK == 0 else RET_CHUNK
    nstep = seq // chunk
    rb = lambda b, i: b * nstep + i
    qk = H_C * DK_C
    return pl.pallas_call(
        _ret_prompt_kernel,
        grid=(batch, nstep),
        in_specs=[pl.BlockSpec((chunk, qk), lambda b, i: (rb(b, i), 0)),
                  pl.BlockSpec((chunk, qk), lambda b, i: (rb(b, i), 1)),
                  pl.BlockSpec((chunk, W_C), lambda b, i: (rb(b, i), 1)),
                  pl.BlockSpec((chunk, W_C), lambda b, i: (rb(b, i), 2)),
                  pl.BlockSpec((1, DV_C), lambda b, i: (0, 0))],
        out_specs=[pl.BlockSpec((chunk, W_C), lambda b, i: (rb(b, i), 0)),
                   pl.BlockSpec((None, H_C, DK_C, DV_C), lambda b, i: (b, 0, 0, 0))],
        out_shape=[jax.ShapeDtypeStruct((batch * seq, W_C), BF16),
                   jax.ShapeDtypeStruct((batch, H_C, DK_C, DV_C), F32)],
        scratch_shapes=[pltpu.VMEM((H_C, chunk, chunk), F32)],
        compiler_params=_cparams(("arbitrary", "arbitrary")),
        name="ret_prompt",
    )(proj, proj, proj, proj, ret_g)


def _ret_sample_job(proj, ret_g, state, *, ntok):
    nseq = state.shape[0]
    qk = H_C * DK_C
    state_spec = lambda blk: pl.BlockSpec((None, H_C, DK_C, DV_C), lambda j, i: (blk(j, i), 0, 0, 0))

    def body(in_refs, out_refs, host_chunk):
        q_ref, k_ref, v_ref, g_ref, retg_ref, s_in_ref = in_refs
        o_ref, s_ref = out_refs
        _ret_chunk(q_ref, k_ref, v_ref, g_ref, retg_ref[...], s_in_ref, s_ref, o_ref, ntok, after_head=host_chunk)

    return _SideJob(
        njobs=nseq,
        units=H_C,
        args=[proj, proj, proj, proj, ret_g, state],
        in_specs=lambda blk: [pl.BlockSpec((SAMPLE_PAD, qk), lambda j, i: (blk(j, i), 0)),
                              pl.BlockSpec((SAMPLE_PAD, qk), lambda j, i: (blk(j, i), 1)),
                              pl.BlockSpec((SAMPLE_PAD, W_C), lambda j, i: (blk(j, i), 1)),
                              pl.BlockSpec((SAMPLE_PAD, W_C), lambda j, i: (blk(j, i), 2)),
                              pl.BlockSpec((1, DV_C), lambda j, i: (0, 0)),
                              state_spec(blk)],
        out_specs=lambda blk: [pl.BlockSpec((SAMPLE_PAD, W_C), lambda j, i: (blk(j, i), 0)), state_spec(blk)],
        out_shape=[jax.ShapeDtypeStruct((nseq * SAMPLE_PAD, W_C), BF16),
                   jax.ShapeDtypeStruct(state.shape, F32)],
        body=body)


def _t5_bucket(dist):
    dist = np.maximum(dist, 0)
    max_exact = NUM_BUCKETS // 2
    log_ratio = (np.log(np.maximum(dist, 1).astype(np.float32) / np.float32(max_exact))
                 / np.float32(math.log(MAX_DISTANCE / max_exact)))
    large = np.minimum(max_exact + (log_ratio * np.float32(NUM_BUCKETS - max_exact)).astype(np.int32),
                       NUM_BUCKETS - 1)
    return np.where(dist < max_exact, dist, large)


def _bucket_or_masked(dist):
    return np.where((dist >= 0) & (dist <= WINDOW), _t5_bucket(dist), -1).astype(np.int32)


def _rotary_tables(pos):
    half = DK_C // 2
    inv = ROPE_BASE ** (-jnp.arange(half, dtype=F32) / half)
    ang = pos.astype(F32)[:, None] * inv[None, :]
    return jnp.cos(ang), jnp.sin(ang)


def _regroup_kernel(wt_ref, o_ref):
    o_ref[...] = wt_ref[...].T.astype(o_ref.dtype)


def _even_weight_layout(w_in):
    wt = jnp.swapaxes(w_in, 0, 1)
    src = lambda j: jnp.where(j < 4, j, jnp.where(j < 8, j + 1, jnp.where(j < 16, j + 5,
                                                                         jnp.where(j < 20, j - 7, 4))))
    main = pl.pallas_call(
        _regroup_kernel,
        grid=(EV_N // MXU_N,),
        in_specs=[pl.BlockSpec((MXU_N, D_MODEL), lambda j: (src(j), 0))],
        out_specs=pl.BlockSpec((D_MODEL, MXU_N), lambda j: (0, j)),
        out_shape=jax.ShapeDtypeStruct((D_MODEL, EV_N), BF16),
        compiler_params=_cparams(("arbitrary",)),
        name="regroup_even_weights",
    )(wt)
    lr_t = jnp.pad(wt[EV_N:], ((0, 128 - GLA_RANK), (0, 0))).astype(BF16)
    return main, lr_t


def _pad_tokens(a, ntok):
    pad = [(0, 0), (0, SAMPLE_PAD - ntok)] + [(0, 0)] * (a.ndim - 2)
    a = jnp.pad(a, pad)
    return a.reshape((a.shape[0] * SAMPLE_PAD,) + a.shape[2:])


def _rows(v, reps):
    return jnp.repeat(v, reps, axis=0)


def kernel(x_prompt, x_sample, cache_swa_k, cache_swa_v, state_gla, state_ret, c_prompt, c_sample, rel_bias,
           ada_w_even, ada_b_even, norm_g_even, w_in_even, w_lr_even, b_lr_even, qn_g_even, kn_g_even,
           sinks_even, gla_g_even, w_out_even, ada_w_odd, ada_b_odd, norm_g_odd, w_in_odd, ret_g_odd, w_out_odd):
    batch, seq, _ = x_prompt.shape
    nseq, ntok, _ = x_sample.shape
    mp = batch * seq

    c_all = jnp.concatenate([c_prompt, c_sample], axis=0)
    mod_e = _ada_mod(c_all, ada_w_even[0], ada_b_even[0])
    mod_o = _ada_mod(c_all, ada_w_odd[0], ada_b_odd[0])

    def split_mod(mod):
        shift, scale, gate = jnp.split(mod, 3, axis=1)
        p = tuple(a[:batch].reshape(batch, 1, D_MODEL) for a in (shift, scale, gate))
        s = tuple(_rows(a[batch:], SAMPLE_PAD) for a in (shift, scale, gate))
        return p, s

    (shift_ep, scale_ep, gate_ep), (shift_es, scale_es, gate_es) = split_mod(mod_e)
    (shift_op, scale_op, gate_op), (shift_os, scale_os, gate_os) = split_mod(mod_o)

    xp = x_prompt.reshape(mp, D_MODEL)
    xs = _pad_tokens(x_sample, ntok)
    ms = xs.shape[0]

    w_e, w_lri = _even_weight_layout(w_in_even[0])
    w_lr = jnp.pad(w_lr_even[0], ((0, 128 - GLA_RANK), (0, 0))).astype(BF16)
    b_lr = b_lr_even[0].reshape(1, -1)
    g_e = norm_g_even[0].reshape(1, D_MODEL)
    qn2 = jnp.tile(qn_g_even[0].reshape(1, HD_A), (1, 2))
    kn2 = jnp.tile(kn_g_even[0].reshape(1, HD_A), (1, 2))
    gla_g = gla_g_even[0].reshape(1, DV_B)
    w_out_e = w_out_even[0].astype(BF16)
    w_out_a, w_out_b = w_out_e[:W_A], w_out_e[W_A:]

    h0_p, la_p = _prenorm_even(xp, scale_ep, shift_ep, g_e, w_lri, w_lr, b_lr, tm=512, rows_per_batch=seq)
    h0_s, la_s = _prenorm_even(xs, scale_es, shift_es, g_e, w_lri, w_lr, b_lr, tm=512, rows_per_batch=ms)
    (proj_p,) = _inproj(h0_p, w_e, tm=1024, tn=EV_N // 3, out_dtype=BF16, name="inproj_even")
    (proj_s,) = _inproj(h0_s, w_e, tm=512, tn=EV_N // 3, out_dtype=F32, name="inproj_even")

    ii = np.arange(WINDOW)
    ss = np.arange(2 * WINDOW)
    bucket_p = jnp.asarray(_bucket_or_masked(WINDOW + ii[:, None] - ss[None, :]))
    mixed_a_p, k_last = _swa_prompt(proj_p, rel_bias, sinks_even[0], qn2, kn2, bucket_p, batch=batch, seq=seq)
    mixed_b_p, gla_p = _gla_prompt(proj_p, la_p, gla_g, batch=batch, seq=seq, rows=256)
    g_o = norm_g_odd[0].reshape(1, D_MODEL)
    y1_p, h1_p = _outproj([mixed_a_p, mixed_b_p], [w_out_a, w_out_b], xp, gate_ep, (scale_op, shift_op, g_o),
                          tm=512, rows_per_batch=seq)

    w_buf = cache_swa_k.shape[2]
    to_fp = lambda c: jnp.transpose(c[0], (0, 2, 3, 1)).reshape(nseq, W_K, w_buf)
    from_fp = lambda c: jnp.transpose(c.reshape(c.shape[0], H_A_KV, HD_A, -1), (0, 3, 1, 2))[None]
    proj_p3 = proj_p.reshape(batch, seq, EV_N)
    swa_k_p = from_fp(k_last)
    swa_v_p = proj_p3[:, seq - WINDOW:, EV_KV + W_K: EV_KV + 2 * W_K].astype(F32).reshape(
        1, batch, WINDOW, H_A_KV, HD_A)

    rr = np.arange(H_A * SAMPLE_PAD)
    row_head = 8 * (rr // 64) + 2 * ((rr % 32) // 8) + (rr % 64) // 32
    tt = rr % SAMPLE_PAD
    jj = np.arange(WINDOW)
    live = (tt < ntok)[:, None]
    bkc = jnp.asarray(np.where(live, _bucket_or_masked(WINDOW + tt[:, None] - jj[None, :]), -1))
    bkn = jnp.asarray(np.where(live & (jj[None, :] < ntok), _bucket_or_masked(tt[:, None] - jj[None, :]), -1))
    relrows = rel_bias.T[row_head]
    sinkrows = jnp.broadcast_to(sinks_even[0][row_head][:, None], (H_A * SAMPLE_PAD, 128))
    mixed_a_s, k_cache_s, v_cache_s = _swa_sample(
        proj_s, to_fp(cache_swa_k), to_fp(cache_swa_v),
        bkc, bkn, relrows, sinkrows, qn2, kn2, sb=16, ntok=ntok, unroll=4)
    mixed_b_s, gla_s = _gla_sample(proj_s, la_s, gla_g, state_gla[0], sb=8, ntok=ntok, unroll=4)
    y1_s, h1_s = _outproj([mixed_a_s, mixed_b_s], [w_out_a, w_out_b], xs, gate_es, (scale_os, shift_os, g_o),
                          tm=256, rows_per_batch=ms)

    w_o = w_in_odd[0].astype(BF16)
    ret_g = ret_g_odd[0].reshape(1, DV_C)
    w_out_o = w_out_odd[0].astype(BF16)
    cos_p, sin_p = _rotary_tables(jnp.arange(seq))
    pos_s = PAST_LEN + jnp.minimum(jnp.arange(SAMPLE_PAD), ntok - 1)
    cos_s, sin_s = _rotary_tables(jnp.tile(pos_s, nseq))

    (projo_s,) = _inproj(h1_s, w_o, (cos_s, sin_s), tm=512, tn=H_C * DK_C, out_dtype=F32, name="inproj_odd")
    ret_job = _ret_sample_job(projo_s, ret_g, state_ret[0], ntok=ntok)
    projo_p, o_s, ret_s = _inproj(h1_p, w_o, (cos_p, sin_p), ret_job, tm=256, tn=H_C * DK_C, out_dtype=BF16,
                                  name="inproj_odd_ret_sample")
    o_p, ret_p = _ret_prompt(projo_p, ret_g, batch=batch, seq=seq)
    (y2_p,) = _outproj([o_p], [w_out_o], y1_p, gate_op, tm=512, rows_per_batch=seq)
    (y2_s,) = _outproj([o_s], [w_out_o], y1_s, gate_os, tm=256, rows_per_batch=ms)

    y_prompt = y2_p.reshape(batch, seq, D_MODEL)
    y_sample = y2_s.reshape(nseq, SAMPLE_PAD, D_MODEL)[:, :ntok]
    return (y_prompt, y_sample, swa_k_p, swa_v_p, gla_p[None], ret_p[None],
            from_fp(k_cache_s), from_fp(v_cache_s),
            gla_s[None], ret_s[None])
```

```python
import functools
import math
from typing import Callable, NamedTuple

import numpy as np
import jax
import jax.numpy as jnp
from jax import lax
from jax.experimental import pallas as pl
from jax.experimental.pallas import tpu as pltpu

F32 = jnp.float32
BF16 = jnp.bfloat16

D_MODEL = 2048
WINDOW = 128
HD_A = 64
H_A = 16
H_A_KV = 2
G_A = 8
W_A = 1024
NUM_BUCKETS = 32
MAX_DISTANCE = 128
H_B = 4
DV_B = 256
DK_B = 128
W_B = 1024
GLA_RANK = 16
GLA_TAU = 16.0
GLA_CHUNK = 64
H_C = 8
DK_C = 256
DV_C = 512
W_C = 4096
RET_CHUNK = 128
RET_PROMPT_CHUNK = 256
ROPE_BASE = 10000.0
EPS = 1e-6
NEG_INF = -1e30
PAST_LEN = 8192

EV_QA, EV_GA, EV_VB, EV_GB, EV_QB, EV_KB, EV_KV = 0, 1024, 2048, 3072, 4096, 4608, 5120
EV_N = 5376
OD_N = 12288
W_K = H_A_KV * HD_A
SAMPLE_PAD = 8
MXU_N = 256
VMEM_LIMIT = 56 * 1024 * 1024

class _Tiles(NamedTuple):
    prenorm: int = 512
    inproj_prompt: int = 1024
    inproj_sample: int = 512
    inproj_side: int = 256
    outproj_prompt: int = 512
    outproj_sample: int = 256
    gla_rows: int = 256
    swa_sample_seqs: int = 32
    gla_sample_seqs: int = 16
    sample_unroll: int = 4


TILES = _Tiles()

LOG_GAMMA = [float(np.log1p(-np.exp2(np.float32(-5.0 - h)))) for h in range(H_C)]


def _cparams(sem):
    return pltpu.CompilerParams(dimension_semantics=sem, vmem_limit_bytes=VMEM_LIMIT)


def _silu(x):
    t = 0.5 * x
    return t * (1.0 + jnp.tanh(t))


def _dot(a, b):
    return jnp.dot(a, b, preferred_element_type=F32)


def _dot_nt(a, b):
    return lax.dot_general(a, b, (((1,), (1,)), ((), ())), preferred_element_type=F32)


def _dot_tn(a, b):
    return lax.dot_general(a, b, (((0,), (0,)), ((), ())), preferred_element_type=F32)


def _cumsum_rows(x):
    c = x.shape[0]
    r = lax.broadcasted_iota(jnp.int32, (c, c), 0)
    s = lax.broadcasted_iota(jnp.int32, (c, c), 1)
    tri = (r >= s).astype(BF16)
    hi = x.astype(BF16)
    r1 = x - hi.astype(F32)
    mid = r1.astype(BF16)
    lo = (r1 - mid.astype(F32)).astype(BF16)
    return _dot(tri, hi) + _dot(tri, mid) + _dot(tri, lo)


def _row_to_col(r):
    n = r.shape[1]
    ri = lax.broadcasted_iota(jnp.int32, (n, n), 0)
    ci = lax.broadcasted_iota(jnp.int32, (n, n), 1)
    return jnp.sum(jnp.where(ri == ci, jnp.broadcast_to(r, (n, n)), 0.0), axis=1, keepdims=True)


def _ada_kernel(c_ref, w_ref, b_ref, o_ref):
    sc = _silu(c_ref[...]).astype(BF16)
    o_ref[...] = _dot(sc, w_ref[...].astype(BF16)) + b_ref[...]


def _ada_mod(c_all, w, b):
    m = c_all.shape[0]
    n = w.shape[1]
    tn = 1536
    return pl.pallas_call(
        _ada_kernel,
        grid=(n // tn,),
        in_specs=[pl.BlockSpec((m, D_MODEL), lambda j: (0, 0)),
                  pl.BlockSpec((D_MODEL, tn), lambda j: (0, j)),
                  pl.BlockSpec((1, tn), lambda j: (0, j))],
        out_specs=pl.BlockSpec((m, tn), lambda j: (0, j)),
        out_shape=jax.ShapeDtypeStruct((m, n), F32),
        compiler_params=_cparams(("arbitrary",)),
        name="ada_mod",
    )(c_all, w, b.reshape(1, n))


def _norm_mod(x, g, scale, shift):
    ms = jnp.mean(x * x, axis=-1, keepdims=True)
    y = x * lax.rsqrt(ms + EPS) * g
    return y * (1.0 + scale) + shift


def _log_sigmoid(z):
    return jnp.minimum(z, 0.0) - jnp.log(1.0 + jnp.exp(-jnp.abs(z)))


def _mod_spec(arr, tm, rows_per_batch):
    if arr.ndim == 3:
        return pl.BlockSpec((None, 1, D_MODEL), lambda i: (i // (rows_per_batch // tm), 0, 0))
    return pl.BlockSpec((tm, D_MODEL), lambda i: (i, 0))


def _prenorm_even_kernel(x_ref, scale_ref, shift_ref, g_ref, wlri_ref, wlr_ref, blr_ref, h_ref, la_ref):
    hb = _norm_mod(x_ref[...], g_ref[...], scale_ref[...], shift_ref[...]).astype(BF16)
    h_ref[...] = hb
    lr = _dot_nt(hb, wlri_ref[...])
    z = _dot(lr.astype(BF16), wlr_ref[...]) + blr_ref[...]
    la_ref[...] = _log_sigmoid(z) / GLA_TAU


def _prenorm_even(x, scale, shift, g, wlri, wlr, blr, *, tm, rows_per_batch):
    m = x.shape[0]
    const = lambda shape: pl.BlockSpec(shape, lambda i: (0, 0))
    return pl.pallas_call(
        _prenorm_even_kernel,
        grid=(m // tm,),
        in_specs=[pl.BlockSpec((tm, D_MODEL), lambda i: (i, 0)),
                  _mod_spec(scale, tm, rows_per_batch), _mod_spec(shift, tm, rows_per_batch),
                  const((1, D_MODEL)), const((128, D_MODEL)), const((128, H_B * DK_B)), const((1, H_B * DK_B))],
        out_specs=[pl.BlockSpec((tm, D_MODEL), lambda i: (i, 0)),
                   pl.BlockSpec((tm, H_B * DK_B), lambda i: (i, 0))],
        out_shape=[jax.ShapeDtypeStruct((m, D_MODEL), BF16),
                   jax.ShapeDtypeStruct((m, H_B * DK_B), F32)],
        compiler_params=_cparams(("arbitrary",)),
        name="prenorm_even",
    )(x, scale, shift, g, wlri, wlr, blr)


class _SideJob(NamedTuple):
    njobs: int
    args: list
    in_specs: Callable
    out_specs: Callable
    out_shape: list
    units: int
    body: Callable


def _inproj_kernel(*refs, rot_tiles, side, nsteps):
    nmain = 4 if rot_tiles else 2
    j = pl.program_id(0)
    run_side = None
    if side is not None:
        njobs, nsi, nso, body = side
        side_in = refs[nmain:nmain + nsi]
        side_out = refs[nmain + nsi + 1:nmain + nsi + 1 + nso]
        step = j * pl.num_programs(1) + pl.program_id(1)
        block = (step * njobs) // nsteps
        prev_block = ((step - 1) * njobs) // nsteps
        run_side = (step == 0) | (block != prev_block)
        refs = refs[:nmain] + (refs[nmain + nsi],)
    h_ref, w_ref, *rest = refs
    o_ref = rest[-1]
    nchunk = o_ref.shape[1] // MXU_N
    half = DK_C // 2

    def plain_chunks():
        def chunk(c):
            cs = slice(c * MXU_N, (c + 1) * MXU_N)
            o_ref[:, cs] = _dot(h_ref[...], w_ref[:, cs]).astype(o_ref.dtype)
        return chunk

    def rotary_chunks():
        mult = jnp.where(j < rot_tiles // 2, 1.0, DK_C ** -0.5)
        cos = rest[0][...] * mult
        sin = rest[1][...] * mult

        def chunk(c):
            acc = _dot(h_ref[...], w_ref[:, c * DK_C:(c + 1) * DK_C])
            x1 = acc[:, :half]
            x2 = acc[:, half:]
            o_ref[:, c * DK_C:c * DK_C + half] = (x1 * cos - x2 * sin).astype(o_ref.dtype)
            o_ref[:, c * DK_C + half:(c + 1) * DK_C] = (x2 * cos + x1 * sin).astype(o_ref.dtype)
        return chunk

    def tile(make_chunk, with_side):
        def run():
            chunk = make_chunk()
            if with_side:
                body(side_in, side_out, chunk)
            else:
                for c in range(nchunk):
                    chunk(c)
        return run

    kinds = [(plain_chunks, j >= rot_tiles)]
    if rot_tiles:
        kinds.append((rotary_chunks, j < rot_tiles))
    for make_chunk, is_kind in kinds:
        if run_side is None:
            pl.when(is_kind)(tile(make_chunk, False))
        else:
            pl.when(is_kind & run_side)(tile(make_chunk, True))
            pl.when(is_kind & jnp.logical_not(run_side))(tile(make_chunk, False))


def _inproj(h, w, rot=None, side=None, *, tm, tn, out_dtype, name):
    m = h.shape[0]
    n = w.shape[1]
    ni = m // tm
    nsteps = (n // tn) * ni
    in_specs = [pl.BlockSpec((tm, D_MODEL), lambda j, i: (i, 0)),
                pl.BlockSpec((D_MODEL, tn), lambda j, i: (0, j))]
    args = [h, w]
    rot_tiles = 0
    if rot is not None:
        nblk = rot[0].shape[0] // tm
        in_specs += [pl.BlockSpec((tm, DK_C // 2), lambda j, i: (i % nblk, 0))] * 2
        args += list(rot)
        rot_tiles = 2 * (H_C * DK_C) // tn
    out_specs = [pl.BlockSpec((tm, tn), lambda j, i: (i, j))]
    out_shape = [jax.ShapeDtypeStruct((m, n), out_dtype)]
    side_static = None
    if side is not None:
        assert side.njobs <= nsteps and side.units == tn // MXU_N
        block = lambda j, i: ((j * ni + i) * side.njobs) // nsteps
        in_specs += side.in_specs(block)
        args += side.args
        out_specs += side.out_specs(block)
        out_shape += side.out_shape
        side_static = (side.njobs, len(side.args), len(side.out_shape), side.body)
    return pl.pallas_call(
        functools.partial(_inproj_kernel, rot_tiles=rot_tiles, side=side_static, nsteps=nsteps),
        grid=(n // tn, ni),
        in_specs=in_specs,
        out_specs=out_specs,
        out_shape=out_shape,
        compiler_params=_cparams(("arbitrary", "arbitrary")),
        name=name,
    )(*args)


def _outproj_kernel(*refs, nin, with_next):
    m_refs, w_refs = refs[:nin], refs[nin:2 * nin]
    if with_next:
        x_ref, gate_ref, nscale_ref, nshift_ref, ng_ref, o_ref, h_ref = refs[2 * nin:]
    else:
        x_ref, gate_ref, o_ref = refs[2 * nin:]
    for c in range(o_ref.shape[1] // MXU_N):
        cs = slice(c * MXU_N, (c + 1) * MXU_N)
        acc = _dot(m_refs[0][...].astype(BF16), w_refs[0][:, cs])
        for m_ref, w_ref in zip(m_refs[1:], w_refs[1:]):
            acc += _dot(m_ref[...].astype(BF16), w_ref[:, cs])
        o_ref[:, cs] = x_ref[:, cs] + gate_ref[:, cs] * acc
    if with_next:
        h_ref[...] = _norm_mod(o_ref[...], ng_ref[...], nscale_ref[...], nshift_ref[...]).astype(BF16)


def _outproj(mixed, ws, x, gate, next_mod=None, *, tm, rows_per_batch):
    m = x.shape[0]
    row = pl.BlockSpec((tm, D_MODEL), lambda i: (i, 0))
    in_specs = ([pl.BlockSpec((tm, a.shape[1]), lambda i: (i, 0)) for a in mixed]
                + [pl.BlockSpec(w.shape, lambda i: (0, 0), pipeline_mode=pl.Buffered(1)) for w in ws]
                + [row, _mod_spec(gate, tm, rows_per_batch)])
    args = [*mixed, *ws, x, gate]
    out_specs = [row]
    out_shape = [jax.ShapeDtypeStruct((m, D_MODEL), F32)]
    if next_mod is not None:
        nscale, nshift, ng = next_mod
        in_specs += [_mod_spec(nscale, tm, rows_per_batch), _mod_spec(nshift, tm, rows_per_batch),
                     pl.BlockSpec((1, D_MODEL), lambda i: (0, 0))]
        args += [nscale, nshift, ng]
        out_specs.append(row)
        out_shape.append(jax.ShapeDtypeStruct((m, D_MODEL), BF16))
    return pl.pallas_call(
        functools.partial(_outproj_kernel, nin=len(mixed), with_next=next_mod is not None),
        grid=(m // tm,),
        in_specs=in_specs,
        out_specs=out_specs,
        out_shape=out_shape,
        compiler_params=_cparams(("arbitrary",)),
        name="outproj",
    )(*args)


def _seg_rms(x, g2, seg_ones):
    x2 = x * x
    hi = x2.astype(BF16)
    lo = (x2 - hi.astype(F32)).astype(BF16)
    ss = _dot(hi, seg_ones) + _dot(lo, seg_ones)
    return x * lax.rsqrt(ss * (1.0 / HD_A) + EPS) * g2


def _dup_halves(x, lo_half):
    sw = pltpu.roll(x, HD_A, 1)
    return [jnp.where(lo_half, x, sw), jnp.where(lo_half, sw, x)]


def _rows_to_cols(x):
    pad = jnp.zeros((x.shape[1] - x.shape[0], x.shape[1]), x.dtype)
    return jnp.concatenate([x, pad], axis=0).T


def _seg_ones():
    r = lax.broadcasted_iota(jnp.int32, (2 * HD_A, 2 * HD_A), 0)
    c = lax.broadcasted_iota(jnp.int32, (2 * HD_A, 2 * HD_A), 1)
    return ((r < HD_A) == (c < HD_A)).astype(BF16)


def _swa_prompt_kernel(relb_ref, sinks_ref, bucket_ref, q_ref, ga_ref, kv_ref, qn_ref, kn_ref,
                       o_ref, knew_ref, bias_ref, kprev_ref, vprev_ref, s_ref, pe_ref):
    b = pl.program_id(0)
    i = pl.program_id(1)

    @pl.when((b == 0) & (i == 0))
    def _():
        bk = bucket_ref[...]
        for h in range(H_A):
            bias_ref[h] = jnp.full((WINDOW, 2 * WINDOW), NEG_INF, F32)
        for bb in range(NUM_BUCKETS):
            hit = bk == bb
            for h in range(H_A):
                bias_ref[h] = jnp.where(hit, relb_ref[bb, h], bias_ref[h])
        col = lax.broadcasted_iota(jnp.int32, (WINDOW, 2 * WINDOW), 1)
        for h in range(H_A):
            bias_ref[H_A + h] = jnp.where(col >= WINDOW, bias_ref[h], NEG_INF)

    @pl.when(i == 0)
    def _():
        kprev_ref[...] = jnp.zeros(kprev_ref.shape, BF16)
        vprev_ref[...] = jnp.zeros(vprev_ref.shape, BF16)

    seg_ones = _seg_ones()
    lt = 2 * HD_A
    tpg = G_A // 2
    lo_half = lax.broadcasted_iota(jnp.int32, (WINDOW, lt), 1) < HD_A
    kv = kv_ref[...].astype(F32)
    kc = _seg_rms(kv[:, :W_K], kn_ref[...], seg_ones)
    k_dup = _dup_halves(kc, lo_half)
    v_dup = _dup_halves(kv[:, W_K:], lo_half)
    base = jnp.where(i == 0, H_A, 0)
    qn2 = qn_ref[...] * HD_A ** -0.5
    gq = tpg * WINDOW
    lo_g = lax.broadcasted_iota(jnp.int32, (gq, lt), 1) < HD_A
    grows = [slice(g * 2 * gq, (g + 1) * 2 * gq) for g in range(H_A_KV)]
    for g in range(H_A_KV):
        kk = jnp.concatenate([kprev_ref[g], k_dup[g].astype(BF16)], axis=0)
        xg = jnp.concatenate([q_ref[:, (g * tpg + p) * lt:(g * tpg + p + 1) * lt] for p in range(tpg)],
                             axis=0).astype(F32)
        xn = _seg_rms(xg, qn2, seg_ones)
        q_all = jnp.concatenate([jnp.where(lo_g, xn, 0.0), jnp.where(lo_g, 0.0, xn)], axis=0).astype(BF16)
        s_ref[grows[g], :] = _dot_nt(q_all, kk)
    sink_terms = []
    for g in range(H_A_KV):
        for n in range(G_A):
            h = g * G_A + 2 * (n % tpg) + n // tpg
            rows = slice((g * G_A + n) * WINDOW, (g * G_A + n + 1) * WINDOW)
            s = s_ref[rows, :] + bias_ref[base + h]
            sink = sinks_ref[h]
            m = jnp.maximum(jnp.max(s, axis=-1, keepdims=True), sink)
            pe_ref[rows, :] = jnp.exp(s - m).astype(BF16)
            sink_terms.append(jnp.exp(sink - m))
    o_ext = []
    for g in range(H_A_KV):
        vv = jnp.concatenate([vprev_ref[g], v_dup[g].astype(BF16)], axis=0)
        vv_ext = jnp.concatenate([vv, jnp.ones(vv.shape, BF16)], axis=1)
        o_ext.append(_dot(pe_ref[grows[g], :], vv_ext))
    for g in range(H_A_KV):
        for p in range(tpg):
            halves = []
            for a in range(2):
                n = a * tpg + p
                rows = slice(n * WINDOW, (n + 1) * WINDOW)
                halves.append(o_ext[g][rows, :lt] / (o_ext[g][rows, lt:] + sink_terms[g * G_A + n]))
            oa = jnp.where(lo_half, halves[0], halves[1])
            cols = slice((g * tpg + p) * lt, (g * tpg + p + 1) * lt)
            o_ref[:, cols] = (oa * _silu(ga_ref[:, cols].astype(F32))).astype(o_ref.dtype)
    for g in range(H_A_KV):
        kprev_ref[g] = k_dup[g].astype(BF16)
        vprev_ref[g] = v_dup[g].astype(BF16)

    @pl.when(i == pl.num_programs(1) - 1)
    def _():
        knew_ref[...] = kc.T


def _swa_prompt(proj, rel_bias, sinks, qn2, kn2, bucket, *, batch, seq):
    nb = seq // WINDOW
    rb = lambda b, i: b * nb + i
    smem = pl.BlockSpec(memory_space=pltpu.SMEM)
    return pl.pallas_call(
        _swa_prompt_kernel,
        grid=(batch, nb),
        in_specs=[smem, smem,
                  pl.BlockSpec((WINDOW, 2 * WINDOW), lambda b, i: (0, 0)),
                  pl.BlockSpec((WINDOW, W_A), lambda b, i: (rb(b, i), EV_QA // W_A)),
                  pl.BlockSpec((WINDOW, W_A), lambda b, i: (rb(b, i), EV_GA // W_A)),
                  pl.BlockSpec((WINDOW, 2 * W_K), lambda b, i: (rb(b, i), EV_KV // (2 * W_K))),
                  pl.BlockSpec((1, 2 * HD_A), lambda b, i: (0, 0)),
                  pl.BlockSpec((1, 2 * HD_A), lambda b, i: (0, 0))],
        out_specs=[pl.BlockSpec((WINDOW, W_A), lambda b, i: (rb(b, i), 0)),
                   pl.BlockSpec((None, WINDOW, W_K), lambda b, i: (b, 0, 0))],
        out_shape=[jax.ShapeDtypeStruct((batch * seq, W_A), BF16),
                   jax.ShapeDtypeStruct((batch, WINDOW, W_K), F32)],
        scratch_shapes=[pltpu.VMEM((2 * H_A, WINDOW, 2 * WINDOW), F32),
                        pltpu.VMEM((H_A_KV, WINDOW, W_K), BF16), pltpu.VMEM((H_A_KV, WINDOW, W_K), BF16),
                        pltpu.VMEM((H_A * WINDOW, 2 * WINDOW), F32), pltpu.VMEM((H_A * WINDOW, 2 * WINDOW), BF16)],
        compiler_params=_cparams(("arbitrary", "arbitrary")),
        name="swa_prompt",
    )(rel_bias, sinks, bucket, proj, proj, proj, qn2, kn2)


def _swa_sample_kernel(bkc_ref, bkn_ref, relrows_ref, sinkrows_ref, q_ref, ga_ref, kvn_ref, ck_ref, cv_ref,
                       qn_ref, kn_ref, o_ref, ko_ref, vo_ref, biasc_ref, biasn_ref, *, sb, ntok, unroll):
    @pl.when(pl.program_id(0) == 0)
    def _():
        bkc = bkc_ref[...]
        bkn = bkn_ref[...]
        rr = relrows_ref[...]
        bc = jnp.full(bkc.shape, NEG_INF, F32)
        bn = jnp.full(bkn.shape, NEG_INF, F32)
        for bb in range(NUM_BUCKETS):
            val = rr[:, bb:bb + 1]
            bc = jnp.where(bkc == bb, val, bc)
            bn = jnp.where(bkn == bb, val, bn)
        biasc_ref[...] = bc
        biasn_ref[...] = bn

    seg_ones = _seg_ones()
    lt = 2 * HD_A
    nrow = G_A * SAMPLE_PAD
    grow = nrow // H_A_KV
    lo_q = lax.broadcasted_iota(jnp.int32, (grow, lt), 1) < HD_A
    lo_n = lax.broadcasted_iota(jnp.int32, (SAMPLE_PAD, lt), 1) < HD_A
    lane_pos = lax.broadcasted_iota(jnp.int32, (W_K, WINDOW), 1)
    qn2 = qn_ref[...] * HD_A ** -0.5
    kn2 = kn_ref[...]
    sink = sinkrows_ref[:, 0:1]

    def body(it, carry):
        seqs = [it * unroll + u for u in range(unroll)]
        rows = [pl.ds(pl.multiple_of(s * SAMPLE_PAD, SAMPLE_PAD), SAMPLE_PAD) for s in seqs]
        kvn = [kvn_ref[r, :] for r in rows]
        kn_all = _seg_rms(jnp.concatenate([x[:, :W_K] for x in kvn], axis=0), kn2, seg_ones)
        q2_all = jnp.concatenate([q_ref[r, p * lt:(p + 1) * lt] for r in rows for p in range(G_A)], axis=0)
        xn_all = _seg_rms(q2_all, qn2, seg_ones)
        sc, sn, kn = [], [], []
        for u, s in enumerate(seqs):
            kn.append(kn_all[u * SAMPLE_PAD:(u + 1) * SAMPLE_PAD])
            kn_dup = _dup_halves(kn[u], lo_n)
            scg, sng = [], []
            for g in range(H_A_KV):
                xn = xn_all[u * nrow + g * grow:u * nrow + (g + 1) * grow]
                q4 = jnp.concatenate([jnp.where(lo_q, xn, 0.0), jnp.where(lo_q, 0.0, xn)], axis=0).astype(BF16)
                kt_g = ck_ref[s, g * HD_A:(g + 1) * HD_A, :].astype(BF16)
                scg.append(_dot(q4, jnp.concatenate([kt_g, kt_g], axis=0)))
                sng.append(_dot_nt(q4, kn_dup[g].astype(BF16)))
            sc.append(jnp.concatenate(scg, axis=0))
            sn.append(jnp.concatenate(sng, axis=0))
        pc, pn, sink_terms = [], [], []
        for u in range(unroll):
            scu = sc[u] + biasc_ref[...]
            snu = sn[u] + biasn_ref[:, :SAMPLE_PAD]
            m = jnp.maximum(jnp.maximum(jnp.max(scu, axis=-1, keepdims=True),
                                        jnp.max(snu, axis=-1, keepdims=True)), sink)
            pc.append(jnp.exp(scu - m).astype(BF16))
            pn.append(jnp.exp(snu - m).astype(BF16))
            sink_terms.append(jnp.exp(sink - m))
        o_ext = []
        for u, s in enumerate(seqs):
            vn_dup = _dup_halves(kvn[u][:, W_K:], lo_n)
            og = []
            for g in range(H_A_KV):
                vt_g = cv_ref[s, g * HD_A:(g + 1) * HD_A, :].astype(BF16)
                vt_ext = jnp.concatenate([vt_g, vt_g, jnp.ones((lt, WINDOW), BF16)], axis=0)
                vn_ext = jnp.concatenate([vn_dup[g].astype(BF16), jnp.ones((SAMPLE_PAD, lt), BF16)], axis=1)
                gr = slice(g * 2 * grow, (g + 1) * 2 * grow)
                og.append(_dot_nt(pc[u][gr], vt_ext) + _dot(pn[u][gr], vn_ext))
            o_ext.append(jnp.concatenate(og, axis=0))
        for u, s in enumerate(seqs):
            o4 = o_ext[u][:, :lt] / (o_ext[u][:, lt:] + sink_terms[u])
            xg = ga_ref[rows[u], :]
            for p in range(G_A):
                g, pp = divmod(p, G_A // 2)
                r0 = g * 2 * grow + pp * SAMPLE_PAD
                o2 = jnp.where(lo_n, o4[r0:r0 + SAMPLE_PAD], o4[r0 + grow:r0 + grow + SAMPLE_PAD])
                o_ref[rows[u], p * lt:(p + 1) * lt] = (o2 * _silu(xg[:, p * lt:(p + 1) * lt])).astype(o_ref.dtype)
            ko_ref[s] = pltpu.roll(jnp.where(lane_pos < ntok, _rows_to_cols(kn[u]), ck_ref[s]), WINDOW - ntok, 1)
            vo_ref[s] = pltpu.roll(jnp.where(lane_pos < ntok, _rows_to_cols(kvn[u][:, W_K:]), cv_ref[s]),
                                   WINDOW - ntok, 1)
        return carry

    lax.fori_loop(0, sb // unroll, body, 0)


def _swa_sample(proj, cache_k, cache_v, bkc, bkn, relrows, sinkrows, qn2, kn2, *, sb, ntok, unroll):
    nseq = cache_k.shape[0]
    rows = sb * SAMPLE_PAD
    full = lambda shape: pl.BlockSpec(shape, lambda i: tuple(0 for _ in shape))
    cache = pl.BlockSpec((sb, W_K, WINDOW), lambda i: (i, 0, 0))
    return pl.pallas_call(
        functools.partial(_swa_sample_kernel, sb=sb, ntok=ntok, unroll=unroll),
        grid=(nseq // sb,),
        in_specs=[full(bkc.shape), full(bkn.shape), full(relrows.shape), full(sinkrows.shape),
                  pl.BlockSpec((rows, W_A), lambda i: (i, EV_QA // W_A)),
                  pl.BlockSpec((rows, W_A), lambda i: (i, EV_GA // W_A)),
                  pl.BlockSpec((rows, 2 * W_K), lambda i: (i, EV_KV // (2 * W_K))),
                  cache, cache, full((1, 2 * HD_A)), full((1, 2 * HD_A))],
        out_specs=[pl.BlockSpec((rows, W_A), lambda i: (i, 0)), cache, cache],
        out_shape=[jax.ShapeDtypeStruct((nseq * SAMPLE_PAD, W_A), F32),
                   jax.ShapeDtypeStruct(cache_k.shape, F32),
                   jax.ShapeDtypeStruct(cache_v.shape, F32)],
        scratch_shapes=[pltpu.VMEM(bkc.shape, F32), pltpu.VMEM(bkn.shape, F32)],
        compiler_params=_cparams(("arbitrary",)),
        name="swa_sample",
    )(bkc, bkn, relrows, sinkrows, proj, proj, proj, cache_k, cache_v, qn2, kn2)


def _gla_pre(q, k, v, la, n_valid):
    c = q.shape[0]
    bcum = _cumsum_rows(la)
    rr = lax.broadcasted_iota(jnp.int32, (c, c), 0)
    cc = lax.broadcasted_iota(jnp.int32, (c, c), 1)
    causal = rr >= cc
    row = lax.broadcasted_iota(jnp.int32, (c, 1), 0)
    qts, kts, kds, vbs, dcols = [], [], [], [], []
    for h in range(H_B):
        ks = slice(h * DK_B, (h + 1) * DK_B)
        bc = bcum[:, ks]
        qts.append(((q[:, ks] * DK_B ** -0.5) * jnp.exp(bc)).astype(BF16))
        kts.append((k[:, ks] * jnp.exp(-bc)).astype(BF16))
        blast = bc[n_valid - 1:n_valid, :]
        kd = k[:, ks] * jnp.exp(blast - bc)
        if n_valid < c:
            kd = jnp.where(row < n_valid, kd, 0.0)
        kds.append(kd.astype(BF16))
        vbs.append(v[:, h * DV_B:(h + 1) * DV_B].astype(BF16))
        dcols.append(_row_to_col(jnp.exp(blast)))
    scores = [_dot_nt(qts[h], kts[h]) for h in range(H_B)]
    upds = [_dot_tn(kds[h], vbs[h]) for h in range(H_B)]
    return [(qts[h], jnp.where(causal, scores[h], 0.0).astype(BF16), vbs[h], upds[h], dcols[h])
            for h in range(H_B)]


def _gla_post(pre, gb, gla_g, states):
    outs = [_dot(a, vb) + _dot(qt, st.astype(BF16)) for (qt, a, vb, _, _), st in zip(pre, states)]
    new_states = [dcol * st + upd for (_, _, _, upd, dcol), st in zip(pre, states)]
    gated = []
    for h, o in enumerate(outs):
        on = o * lax.rsqrt(jnp.mean(o * o, axis=-1, keepdims=True) + EPS) * gla_g
        gated.append(on * _silu(gb[:, h * DV_B:(h + 1) * DV_B]))
    return jnp.concatenate(gated, axis=1), new_states


def _gla_prompt_kernel(q_ref, k_ref, v_ref, gb_ref, la_ref, g_ref, o_ref, s_ref, *, nchunk):
    @pl.when(pl.program_id(1) == 0)
    def _():
        s_ref[...] = jnp.zeros(s_ref.shape, F32)

    gla_g = g_ref[...]
    chunk_rows = [slice(c * GLA_CHUNK, (c + 1) * GLA_CHUNK) for c in range(nchunk)]
    pres = [_gla_pre(q_ref[r, :].astype(F32), k_ref[r, :].astype(F32), v_ref[r, :].astype(F32), la_ref[r, :],
                     GLA_CHUNK) for r in chunk_rows]
    states = [s_ref[h] for h in range(H_B)]
    for r, pre in zip(chunk_rows, pres):
        out, states = _gla_post(pre, gb_ref[r, :].astype(F32), gla_g, states)
        o_ref[r, :] = out.astype(o_ref.dtype)
    for h in range(H_B):
        s_ref[h] = states[h]


def _gla_prompt(proj, la, gla_g, *, batch, seq, rows):
    nstep = seq // rows
    rb = lambda b, i: b * nstep + i
    return pl.pallas_call(
        functools.partial(_gla_prompt_kernel, nchunk=rows // GLA_CHUNK),
        grid=(batch, nstep),
        in_specs=[pl.BlockSpec((rows, H_B * DK_B), lambda b, i: (rb(b, i), EV_QB // (H_B * DK_B))),
                  pl.BlockSpec((rows, H_B * DK_B), lambda b, i: (rb(b, i), EV_KB // (H_B * DK_B))),
                  pl.BlockSpec((rows, W_B), lambda b, i: (rb(b, i), EV_VB // W_B)),
                  pl.BlockSpec((rows, W_B), lambda b, i: (rb(b, i), EV_GB // W_B)),
                  pl.BlockSpec((rows, H_B * DK_B), lambda b, i: (rb(b, i), 0)),
                  pl.BlockSpec((1, DV_B), lambda b, i: (0, 0))],
        out_specs=[pl.BlockSpec((rows, W_B), lambda b, i: (rb(b, i), 0)),
                   pl.BlockSpec((None, H_B, DK_B, DV_B), lambda b, i: (b, 0, 0, 0))],
        out_shape=[jax.ShapeDtypeStruct((batch * seq, W_B), BF16),
                   jax.ShapeDtypeStruct((batch, H_B, DK_B, DV_B), F32)],
        compiler_params=_cparams(("arbitrary", "arbitrary")),
        name="gla_prompt",
    )(proj, proj, proj, proj, la, gla_g)


def _gla_sample_kernel(q_ref, k_ref, v_ref, gb_ref, la_ref, g_ref, s_in_ref, o_ref, s_ref, *, sb, ntok, unroll):
    gla_g = g_ref[...]

    def body(it, carry):
        seqs = [it * unroll + u for u in range(unroll)]
        rows = [pl.ds(pl.multiple_of(s * SAMPLE_PAD, SAMPLE_PAD), SAMPLE_PAD) for s in seqs]
        pres = [_gla_pre(q_ref[r, :], k_ref[r, :], v_ref[r, :], la_ref[r, :], ntok) for r in rows]
        for s, r, pre in zip(seqs, rows, pres):
            out, new_states = _gla_post(pre, gb_ref[r, :], gla_g, [s_in_ref[s, h] for h in range(H_B)])
            o_ref[r, :] = out.astype(o_ref.dtype)
            for h in range(H_B):
                s_ref[s, h] = new_states[h]
        return carry

    lax.fori_loop(0, sb // unroll, body, 0)


def _gla_sample(proj, la, gla_g, state, *, sb, ntok, unroll):
    nseq = state.shape[0]
    rows = sb * SAMPLE_PAD
    return pl.pallas_call(
        functools.partial(_gla_sample_kernel, sb=sb, ntok=ntok, unroll=unroll),
        grid=(nseq // sb,),
        in_specs=[pl.BlockSpec((rows, H_B * DK_B), lambda i: (i, EV_QB // (H_B * DK_B))),
                  pl.BlockSpec((rows, H_B * DK_B), lambda i: (i, EV_KB // (H_B * DK_B))),
                  pl.BlockSpec((rows, W_B), lambda i: (i, EV_VB // W_B)),
                  pl.BlockSpec((rows, W_B), lambda i: (i, EV_GB // W_B)),
                  pl.BlockSpec((rows, H_B * DK_B), lambda i: (i, 0)),
                  pl.BlockSpec((1, DV_B), lambda i: (0, 0)),
                  pl.BlockSpec((sb, H_B, DK_B, DV_B), lambda i: (i, 0, 0, 0))],
        out_specs=[pl.BlockSpec((rows, W_B), lambda i: (i, 0)),
                   pl.BlockSpec((sb, H_B, DK_B, DV_B), lambda i: (i, 0, 0, 0))],
        out_shape=[jax.ShapeDtypeStruct((nseq * SAMPLE_PAD, W_B), BF16),
                   jax.ShapeDtypeStruct(state.shape, F32)],
        compiler_params=_cparams(("arbitrary",)),
        name="gla_sample",
    )(proj, proj, proj, proj, la, gla_g, state)


def _ret_decay(c, h):
    ri = lax.broadcasted_iota(jnp.int32, (c, c), 0)
    ci = lax.broadcasted_iota(jnp.int32, (c, c), 1)
    dist = (ri - ci).astype(F32)
    return jnp.where(ri >= ci, jnp.exp(jnp.maximum(dist, 0.0) * LOG_GAMMA[h]), 0.0)


def _ret_chunk(q_ref, k_ref, v_ref, g_ref, retg, state_in_ref, state_ref, o_ref, n_valid, after_head=None,
               decay_ref=None):
    c = q_ref.shape[0]
    row = lax.broadcasted_iota(jnp.int32, (c, 1), 0)
    rowf = row.astype(F32)
    a_heads = []
    for h in range(H_C):
        ks = slice(h * DK_C, (h + 1) * DK_C)
        decay = _ret_decay(c, h) if decay_ref is None else decay_ref[h]
        a_heads.append((_dot_nt(q_ref[:, ks].astype(BF16), k_ref[:, ks].astype(BF16)) * decay).astype(BF16))
    for h in range(H_C):
        lg = LOG_GAMMA[h]
        ks = slice(h * DK_C, (h + 1) * DK_C)
        vs = slice(h * DV_C, (h + 1) * DV_C)
        kf = k_ref[:, ks].astype(F32)
        vb = v_ref[:, vs].astype(BF16)
        st = state_in_ref[h]
        q_in = (q_ref[:, ks].astype(F32) * jnp.exp((rowf + 1.0) * lg)).astype(BF16)
        o = _dot(a_heads[h], vb) + _dot(q_in, st.astype(BF16))
        kd = kf * jnp.exp((n_valid - 1.0 - rowf) * lg)
        if n_valid < c:
            kd = jnp.where(row < n_valid, kd, 0.0)
        state_ref[h] = math.exp(n_valid * lg) * st + _dot_tn(kd.astype(BF16), vb)
        on = o * lax.rsqrt(jnp.mean(o * o, axis=-1, keepdims=True) + EPS) * retg
        o_ref[:, vs] = (on * _silu(g_ref[:, vs].astype(F32))).astype(o_ref.dtype)
        if after_head is not None:
            after_head(h)


def _ret_prompt_kernel(q_ref, k_ref, v_ref, g_ref, retg_ref, o_ref, s_ref, decay_ref):
    chunk = q_ref.shape[0]

    @pl.when((pl.program_id(0) == 0) & (pl.program_id(1) == 0))
    def _():
        for h in range(H_C):
            decay_ref[h] = _ret_decay(chunk, h)

    @pl.when(pl.program_id(1) == 0)
    def _():
        s_ref[...] = jnp.zeros(s_ref.shape, F32)

    _ret_chunk(q_ref, k_ref, v_ref, g_ref, retg_ref[...], s_ref, s_ref, o_ref, chunk, decay_ref=decay_ref)


def _ret_prompt(proj, ret_g, *, batch, seq):
    chunk = RET_PROMPT_CHUNK if seq % RET_PROMPT_CHUNK == 0 else RET_CHUNK
    nstep = seq // chunk
    rb = lambda b, i: b * nstep + i
    qk = H_C * DK_C
    return pl.pallas_call(
        _ret_prompt_kernel,
        grid=(batch, nstep),
        in_specs=[pl.BlockSpec((chunk, qk), lambda b, i: (rb(b, i), 0)),
                  pl.BlockSpec((chunk, qk), lambda b, i: (rb(b, i), 1)),
                  pl.BlockSpec((chunk, W_C), lambda b, i: (rb(b, i), 1)),
                  pl.BlockSpec((chunk, W_C), lambda b, i: (rb(b, i), 2)),
                  pl.BlockSpec((1, DV_C), lambda b, i: (0, 0))],
        out_specs=[pl.BlockSpec((chunk, W_C), lambda b, i: (rb(b, i), 0)),
                   pl.BlockSpec((None, H_C, DK_C, DV_C), lambda b, i: (b, 0, 0, 0))],
        out_shape=[jax.ShapeDtypeStruct((batch * seq, W_C), BF16),
                   jax.ShapeDtypeStruct((batch, H_C, DK_C, DV_C), F32)],
        scratch_shapes=[pltpu.VMEM((H_C, chunk, chunk), F32)],
        compiler_params=_cparams(("arbitrary", "arbitrary")),
        name="ret_prompt",
    )(proj, proj, proj, proj, ret_g)


def _ret_sample_job(proj, ret_g, state, *, ntok):
    nseq = state.shape[0]
    qk = H_C * DK_C
    state_spec = lambda blk: pl.BlockSpec((None, H_C, DK_C, DV_C), lambda j, i: (blk(j, i), 0, 0, 0))

    def body(in_refs, out_refs, host_chunk):
        q_ref, k_ref, v_ref, g_ref, retg_ref, s_in_ref = in_refs
        o_ref, s_ref = out_refs
        _ret_chunk(q_ref, k_ref, v_ref, g_ref, retg_ref[...], s_in_ref, s_ref, o_ref, ntok, after_head=host_chunk)

    return _SideJob(
        njobs=nseq,
        units=H_C,
        args=[proj, proj, proj, proj, ret_g, state],
        in_specs=lambda blk: [pl.BlockSpec((SAMPLE_PAD, qk), lambda j, i: (blk(j, i), 0)),
                              pl.BlockSpec((SAMPLE_PAD, qk), lambda j, i: (blk(j, i), 1)),
                              pl.BlockSpec((SAMPLE_PAD, W_C), lambda j, i: (blk(j, i), 1)),
                              pl.BlockSpec((SAMPLE_PAD, W_C), lambda j, i: (blk(j, i), 2)),
                              pl.BlockSpec((1, DV_C), lambda j, i: (0, 0)),
                              state_spec(blk)],
        out_specs=lambda blk: [pl.BlockSpec((SAMPLE_PAD, W_C), lambda j, i: (blk(j, i), 0)), state_spec(blk)],
        out_shape=[jax.ShapeDtypeStruct((nseq * SAMPLE_PAD, W_C), BF16),
                   jax.ShapeDtypeStruct(state.shape, F32)],
        body=body)


def _t5_bucket(dist):
    dist = np.maximum(dist, 0)
    max_exact = NUM_BUCKETS // 2
    log_ratio = (np.log(np.maximum(dist, 1).astype(np.float32) / np.float32(max_exact))
                 / np.float32(math.log(MAX_DISTANCE / max_exact)))
    large = np.minimum(max_exact + (log_ratio * np.float32(NUM_BUCKETS - max_exact)).astype(np.int32),
                       NUM_BUCKETS - 1)
    return np.where(dist < max_exact, dist, large)


def _bucket_or_masked(dist):
    return np.where((dist >= 0) & (dist <= WINDOW), _t5_bucket(dist), -1).astype(np.int32)


def _rotary_tables(pos):
    half = DK_C // 2
    inv = ROPE_BASE ** (-jnp.arange(half, dtype=F32) / half)
    ang = pos.astype(F32)[:, None] * inv[None, :]
    return jnp.cos(ang), jnp.sin(ang)


def _regroup_kernel(wt_ref, o_ref):
    o_ref[...] = wt_ref[...].T.astype(o_ref.dtype)


def _even_weight_layout(w_in):
    wt = jnp.swapaxes(w_in, 0, 1)
    src = lambda j: jnp.where(j < 4, j, jnp.where(j < 8, j + 1, jnp.where(j < 16, j + 5,
                                                                         jnp.where(j < 20, j - 7, 4))))
    main = pl.pallas_call(
        _regroup_kernel,
        grid=(EV_N // MXU_N,),
        in_specs=[pl.BlockSpec((MXU_N, D_MODEL), lambda j: (src(j), 0))],
        out_specs=pl.BlockSpec((D_MODEL, MXU_N), lambda j: (0, j)),
        out_shape=jax.ShapeDtypeStruct((D_MODEL, EV_N), BF16),
        compiler_params=_cparams(("arbitrary",)),
        name="regroup_even_weights",
    )(wt)
    lr_t = jnp.pad(wt[EV_N:], ((0, 128 - GLA_RANK), (0, 0))).astype(BF16)
    return main, lr_t


def _pad_tokens(a, ntok):
    pad = [(0, 0), (0, SAMPLE_PAD - ntok)] + [(0, 0)] * (a.ndim - 2)
    a = jnp.pad(a, pad)
    return a.reshape((a.shape[0] * SAMPLE_PAD,) + a.shape[2:])


def _rows(v, reps):
    return jnp.repeat(v, reps, axis=0)


def kernel(x_prompt, x_sample, cache_swa_k, cache_swa_v, state_gla, state_ret, c_prompt, c_sample, rel_bias,
           ada_w_even, ada_b_even, norm_g_even, w_in_even, w_lr_even, b_lr_even, qn_g_even, kn_g_even,
           sinks_even, gla_g_even, w_out_even, ada_w_odd, ada_b_odd, norm_g_odd, w_in_odd, ret_g_odd, w_out_odd):
    batch, seq, _ = x_prompt.shape
    nseq, ntok, _ = x_sample.shape
    mp = batch * seq

    c_all = jnp.concatenate([c_prompt, c_sample], axis=0)
    mod_e = _ada_mod(c_all, ada_w_even[0], ada_b_even[0])
    mod_o = _ada_mod(c_all, ada_w_odd[0], ada_b_odd[0])

    def split_mod(mod):
        shift, scale, gate = jnp.split(mod, 3, axis=1)
        p = tuple(a[:batch].reshape(batch, 1, D_MODEL) for a in (shift, scale, gate))
        s = tuple(_rows(a[batch:], SAMPLE_PAD) for a in (shift, scale, gate))
        return p, s

    (shift_ep, scale_ep, gate_ep), (shift_es, scale_es, gate_es) = split_mod(mod_e)
    (shift_op, scale_op, gate_op), (shift_os, scale_os, gate_os) = split_mod(mod_o)

    xp = x_prompt.reshape(mp, D_MODEL)
    xs = _pad_tokens(x_sample, ntok)
    ms = xs.shape[0]

    w_e, w_lri = _even_weight_layout(w_in_even[0])
    w_lr = jnp.pad(w_lr_even[0], ((0, 128 - GLA_RANK), (0, 0))).astype(BF16)
    b_lr = b_lr_even[0].reshape(1, -1)
    g_e = norm_g_even[0].reshape(1, D_MODEL)
    qn2 = jnp.tile(qn_g_even[0].reshape(1, HD_A), (1, 2))
    kn2 = jnp.tile(kn_g_even[0].reshape(1, HD_A), (1, 2))
    gla_g = gla_g_even[0].reshape(1, DV_B)
    w_out_e = w_out_even[0].astype(BF16)
    w_out_a, w_out_b = w_out_e[:W_A], w_out_e[W_A:]

    h0_p, la_p = _prenorm_even(xp, scale_ep, shift_ep, g_e, w_lri, w_lr, b_lr, tm=TILES.prenorm, rows_per_batch=seq)
    h0_s, la_s = _prenorm_even(xs, scale_es, shift_es, g_e, w_lri, w_lr, b_lr, tm=TILES.prenorm, rows_per_batch=ms)
    (proj_p,) = _inproj(h0_p, w_e, tm=TILES.inproj_prompt, tn=EV_N // 3, out_dtype=BF16, name="inproj_even")
    (proj_s,) = _inproj(h0_s, w_e, tm=TILES.inproj_sample, tn=EV_N // 3, out_dtype=F32, name="inproj_even")

    ii = np.arange(WINDOW)
    ss = np.arange(2 * WINDOW)
    bucket_p = jnp.asarray(_bucket_or_masked(WINDOW + ii[:, None] - ss[None, :]))
    mixed_a_p, k_last = _swa_prompt(proj_p, rel_bias, sinks_even[0], qn2, kn2, bucket_p, batch=batch, seq=seq)
    mixed_b_p, gla_p = _gla_prompt(proj_p, la_p, gla_g, batch=batch, seq=seq, rows=TILES.gla_rows)
    g_o = norm_g_odd[0].reshape(1, D_MODEL)
    y1_p, h1_p = _outproj([mixed_a_p, mixed_b_p], [w_out_a, w_out_b], xp, gate_ep, (scale_op, shift_op, g_o),
                          tm=TILES.outproj_prompt, rows_per_batch=seq)

    w_buf = cache_swa_k.shape[2]
    to_fp = lambda c: jnp.transpose(c[0], (0, 2, 3, 1)).reshape(nseq, W_K, w_buf)
    from_fp = lambda c: jnp.transpose(c.reshape(c.shape[0], H_A_KV, HD_A, -1), (0, 3, 1, 2))[None]
    proj_p3 = proj_p.reshape(batch, seq, EV_N)
    swa_k_p = from_fp(k_last)
    swa_v_p = proj_p3[:, seq - WINDOW:, EV_KV + W_K: EV_KV + 2 * W_K].astype(F32).reshape(
        1, batch, WINDOW, H_A_KV, HD_A)

    rr = np.arange(H_A * SAMPLE_PAD)
    row_head = 8 * (rr // 64) + 2 * ((rr % 32) // 8) + (rr % 64) // 32
    tt = rr % SAMPLE_PAD
    jj = np.arange(WINDOW)
    live = (tt < ntok)[:, None]
    bkc = jnp.asarray(np.where(live, _bucket_or_masked(WINDOW + tt[:, None] - jj[None, :]), -1))
    bkn = jnp.asarray(np.where(live & (jj[None, :] < ntok), _bucket_or_masked(tt[:, None] - jj[None, :]), -1))
    relrows = rel_bias.T[row_head]
    sinkrows = jnp.broadcast_to(sinks_even[0][row_head][:, None], (H_A * SAMPLE_PAD, 128))
    mixed_a_s, k_cache_s, v_cache_s = _swa_sample(
        proj_s, to_fp(cache_swa_k), to_fp(cache_swa_v),
        bkc, bkn, relrows, sinkrows, qn2, kn2, sb=TILES.swa_sample_seqs, ntok=ntok, unroll=TILES.sample_unroll)
    mixed_b_s, gla_s = _gla_sample(proj_s, la_s, gla_g, state_gla[0], sb=TILES.gla_sample_seqs, ntok=ntok,
                                   unroll=TILES.sample_unroll)
    y1_s, h1_s = _outproj([mixed_a_s, mixed_b_s], [w_out_a, w_out_b], xs, gate_es, (scale_os, shift_os, g_o),
                          tm=TILES.outproj_sample, rows_per_batch=ms)

    w_o = w_in_odd[0].astype(BF16)
    ret_g = ret_g_odd[0].reshape(1, DV_C)
    w_out_o = w_out_odd[0].astype(BF16)
    cos_p, sin_p = _rotary_tables(jnp.arange(seq))
    pos_s = PAST_LEN + jnp.minimum(jnp.arange(SAMPLE_PAD), ntok - 1)
    cos_s, sin_s = _rotary_tables(jnp.tile(pos_s, nseq))

    (projo_s,) = _inproj(h1_s, w_o, (cos_s, sin_s), tm=TILES.inproj_sample, tn=H_C * DK_C, out_dtype=F32,
                         name="inproj_odd")
    ret_job = _ret_sample_job(projo_s, ret_g, state_ret[0], ntok=ntok)
    projo_p, o_s, ret_s = _inproj(h1_p, w_o, (cos_p, sin_p), ret_job, tm=TILES.inproj_side, tn=H_C * DK_C,
                                  out_dtype=BF16,
                                  name="inproj_odd_ret_sample")
    o_p, ret_p = _ret_prompt(projo_p, ret_g, batch=batch, seq=seq)
    (y2_p,) = _outproj([o_p], [w_out_o], y1_p, gate_op, tm=TILES.outproj_prompt, rows_per_batch=seq)
    (y2_s,) = _outproj([o_s], [w_out_o], y1_s, gate_os, tm=TILES.outproj_sample, rows_per_batch=ms)

    y_prompt = y2_p.reshape(batch, seq, D_MODEL)
    y_sample = y2_s.reshape(nseq, SAMPLE_PAD, D_MODEL)[:, :ntok]
    return (y_prompt, y_sample, swa_k_p, swa_v_p, gla_p[None], ret_p[None],
            from_fp(k_cache_s), from_fp(v_cache_s),
            gla_s[None], ret_s[None])
```

```python
import functools
import math
from typing import Callable, NamedTuple

import numpy as np
import jax
import jax.numpy as jnp
from jax import lax
from jax.experimental import pallas as pl
from jax.experimental.pallas import tpu as pltpu

F32 = jnp.float32
BF16 = jnp.bfloat16

D_MODEL = 2048
WINDOW = 128
HD_A = 64
H_A = 16
H_A_KV = 2
G_A = 8
W_A = 1024
NUM_BUCKETS = 32
MAX_DISTANCE = 128
H_B = 4
DV_B = 256
DK_B = 128
W_B = 1024
GLA_RANK = 16
GLA_TAU = 16.0
GLA_CHUNK = 64
H_C = 8
DK_C = 256
DV_C = 512
W_C = 4096
RET_CHUNK = 128
RET_PROMPT_CHUNK = 256
ROPE_BASE = 10000.0
EPS = 1e-6
NEG_INF = -1e30
PAST_LEN = 8192

EV_QA, EV_GA, EV_VB, EV_GB, EV_QB, EV_KB, EV_KV = 0, 1024, 2048, 3072, 4096, 4608, 5120
EV_N = 5376
OD_N = 12288
W_K = H_A_KV * HD_A
SAMPLE_PAD = 8
MXU_N = 256
VMEM_LIMIT = 56 * 1024 * 1024

class _Tiles(NamedTuple):
    prenorm: int = 512
    inproj_prompt: int = 1024
    inproj_sample: int = 512
    inproj_side: int = 256
    outproj_prompt: int = 512
    outproj_sample: int = 256
    gla_rows: int = 256
    swa_sample_seqs: int = 32
    gla_sample_seqs: int = 16
    sample_unroll: int = 8


TILES = _Tiles()

LOG_GAMMA = [float(np.log1p(-np.exp2(np.float32(-5.0 - h)))) for h in range(H_C)]


def _cparams(sem):
    return pltpu.CompilerParams(dimension_semantics=sem, vmem_limit_bytes=VMEM_LIMIT)


def _silu(x):
    t = 0.5 * x
    return t * (1.0 + jnp.tanh(t))


def _dot(a, b):
    return jnp.dot(a, b, preferred_element_type=F32)


def _dot_nt(a, b):
    return lax.dot_general(a, b, (((1,), (1,)), ((), ())), preferred_element_type=F32)


def _dot_tn(a, b):
    return lax.dot_general(a, b, (((0,), (0,)), ((), ())), preferred_element_type=F32)


def _cumsum_rows(x):
    c = x.shape[0]
    r = lax.broadcasted_iota(jnp.int32, (c, c), 0)
    s = lax.broadcasted_iota(jnp.int32, (c, c), 1)
    tri = (r >= s).astype(BF16)
    hi = x.astype(BF16)
    r1 = x - hi.astype(F32)
    mid = r1.astype(BF16)
    lo = (r1 - mid.astype(F32)).astype(BF16)
    return _dot(tri, hi) + _dot(tri, mid) + _dot(tri, lo)


def _row_to_col(r):
    n = r.shape[1]
    ri = lax.broadcasted_iota(jnp.int32, (n, n), 0)
    ci = lax.broadcasted_iota(jnp.int32, (n, n), 1)
    return jnp.sum(jnp.where(ri == ci, jnp.broadcast_to(r, (n, n)), 0.0), axis=1, keepdims=True)


def _ada_kernel(c_ref, w_ref, b_ref, o_ref):
    sc = _silu(c_ref[...]).astype(BF16)
    o_ref[...] = _dot(sc, w_ref[...].astype(BF16)) + b_ref[...]


def _ada_mod(c_all, w, b):
    m = c_all.shape[0]
    n = w.shape[1]
    tn = 1536
    return pl.pallas_call(
        _ada_kernel,
        grid=(n // tn,),
        in_specs=[pl.BlockSpec((m, D_MODEL), lambda j: (0, 0)),
                  pl.BlockSpec((D_MODEL, tn), lambda j: (0, j)),
                  pl.BlockSpec((1, tn), lambda j: (0, j))],
        out_specs=pl.BlockSpec((m, tn), lambda j: (0, j)),
        out_shape=jax.ShapeDtypeStruct((m, n), F32),
        compiler_params=_cparams(("arbitrary",)),
        name="ada_mod",
    )(c_all, w, b.reshape(1, n))


def _norm_mod(x, g, scale, shift):
    ms = jnp.mean(x * x, axis=-1, keepdims=True)
    y = x * lax.rsqrt(ms + EPS) * g
    return y * (1.0 + scale) + shift


def _log_sigmoid(z):
    return jnp.minimum(z, 0.0) - jnp.log(1.0 + jnp.exp(-jnp.abs(z)))


def _mod_spec(arr, tm, rows_per_batch):
    if arr.ndim == 3:
        return pl.BlockSpec((None, 1, D_MODEL), lambda i: (i // (rows_per_batch // tm), 0, 0))
    return pl.BlockSpec((tm, D_MODEL), lambda i: (i, 0))


def _prenorm_even_kernel(x_ref, scale_ref, shift_ref, g_ref, wlri_ref, wlr_ref, blr_ref, h_ref, la_ref):
    hb = _norm_mod(x_ref[...], g_ref[...], scale_ref[...], shift_ref[...]).astype(BF16)
    h_ref[...] = hb
    lr = _dot_nt(hb, wlri_ref[...])
    z = _dot(lr.astype(BF16), wlr_ref[...]) + blr_ref[...]
    la_ref[...] = _log_sigmoid(z) / GLA_TAU


def _prenorm_even(x, scale, shift, g, wlri, wlr, blr, *, tm, rows_per_batch):
    m = x.shape[0]
    const = lambda shape: pl.BlockSpec(shape, lambda i: (0, 0))
    return pl.pallas_call(
        _prenorm_even_kernel,
        grid=(m // tm,),
        in_specs=[pl.BlockSpec((tm, D_MODEL), lambda i: (i, 0)),
                  _mod_spec(scale, tm, rows_per_batch), _mod_spec(shift, tm, rows_per_batch),
                  const((1, D_MODEL)), const((128, D_MODEL)), const((128, H_B * DK_B)), const((1, H_B * DK_B))],
        out_specs=[pl.BlockSpec((tm, D_MODEL), lambda i: (i, 0)),
                   pl.BlockSpec((tm, H_B * DK_B), lambda i: (i, 0))],
        out_shape=[jax.ShapeDtypeStruct((m, D_MODEL), BF16),
                   jax.ShapeDtypeStruct((m, H_B * DK_B), F32)],
        compiler_params=_cparams(("arbitrary",)),
        name="prenorm_even",
    )(x, scale, shift, g, wlri, wlr, blr)


class _SideJob(NamedTuple):
    njobs: int
    args: list
    in_specs: Callable
    out_specs: Callable
    out_shape: list
    units: int
    body: Callable


def _inproj_kernel(*refs, rot_tiles, side, nsteps):
    nmain = 4 if rot_tiles else 2
    j = pl.program_id(0)
    run_side = None
    if side is not None:
        njobs, nsi, nso, body = side
        side_in = refs[nmain:nmain + nsi]
        side_out = refs[nmain + nsi + 1:nmain + nsi + 1 + nso]
        step = j * pl.num_programs(1) + pl.program_id(1)
        block = (step * njobs) // nsteps
        prev_block = ((step - 1) * njobs) // nsteps
        run_side = (step == 0) | (block != prev_block)
        refs = refs[:nmain] + (refs[nmain + nsi],)
    h_ref, w_ref, *rest = refs
    o_ref = rest[-1]
    nchunk = o_ref.shape[1] // MXU_N
    half = DK_C // 2

    def plain_chunks():
        def chunk(c):
            cs = slice(c * MXU_N, (c + 1) * MXU_N)
            o_ref[:, cs] = _dot(h_ref[...], w_ref[:, cs]).astype(o_ref.dtype)
        return chunk

    def rotary_chunks():
        mult = jnp.where(j < rot_tiles // 2, 1.0, DK_C ** -0.5)
        cos = rest[0][...] * mult
        sin = rest[1][...] * mult

        def chunk(c):
            acc = _dot(h_ref[...], w_ref[:, c * DK_C:(c + 1) * DK_C])
            x1 = acc[:, :half]
            x2 = acc[:, half:]
            o_ref[:, c * DK_C:c * DK_C + half] = (x1 * cos - x2 * sin).astype(o_ref.dtype)
            o_ref[:, c * DK_C + half:(c + 1) * DK_C] = (x2 * cos + x1 * sin).astype(o_ref.dtype)
        return chunk

    def tile(make_chunk, with_side):
        def run():
            chunk = make_chunk()
            if with_side:
                body(side_in, side_out, chunk)
            else:
                for c in range(nchunk):
                    chunk(c)
        return run

    kinds = [(plain_chunks, j >= rot_tiles)]
    if rot_tiles:
        kinds.append((rotary_chunks, j < rot_tiles))
    for make_chunk, is_kind in kinds:
        if run_side is None:
            pl.when(is_kind)(tile(make_chunk, False))
        else:
            pl.when(is_kind & run_side)(tile(make_chunk, True))
            pl.when(is_kind & jnp.logical_not(run_side))(tile(make_chunk, False))


def _inproj(h, w, rot=None, side=None, *, tm, tn, out_dtype, name):
    m = h.shape[0]
    n = w.shape[1]
    ni = m // tm
    nsteps = (n // tn) * ni
    in_specs = [pl.BlockSpec((tm, D_MODEL), lambda j, i: (i, 0)),
                pl.BlockSpec((D_MODEL, tn), lambda j, i: (0, j))]
    args = [h, w]
    rot_tiles = 0
    if rot is not None:
        nblk = rot[0].shape[0] // tm
        in_specs += [pl.BlockSpec((tm, DK_C // 2), lambda j, i: (i % nblk, 0))] * 2
        args += list(rot)
        rot_tiles = 2 * (H_C * DK_C) // tn
    out_specs = [pl.BlockSpec((tm, tn), lambda j, i: (i, j))]
    out_shape = [jax.ShapeDtypeStruct((m, n), out_dtype)]
    side_static = None
    if side is not None:
        assert side.njobs <= nsteps and side.units == tn // MXU_N
        block = lambda j, i: ((j * ni + i) * side.njobs) // nsteps
        in_specs += side.in_specs(block)
        args += side.args
        out_specs += side.out_specs(block)
        out_shape += side.out_shape
        side_static = (side.njobs, len(side.args), len(side.out_shape), side.body)
    return pl.pallas_call(
        functools.partial(_inproj_kernel, rot_tiles=rot_tiles, side=side_static, nsteps=nsteps),
        grid=(n // tn, ni),
        in_specs=in_specs,
        out_specs=out_specs,
        out_shape=out_shape,
        compiler_params=_cparams(("arbitrary", "arbitrary")),
        name=name,
    )(*args)


def _outproj_kernel(*refs, nin, with_next):
    m_refs, w_refs = refs[:nin], refs[nin:2 * nin]
    if with_next:
        x_ref, gate_ref, nscale_ref, nshift_ref, ng_ref, o_ref, h_ref = refs[2 * nin:]
    else:
        x_ref, gate_ref, o_ref = refs[2 * nin:]
    for c in range(o_ref.shape[1] // MXU_N):
        cs = slice(c * MXU_N, (c + 1) * MXU_N)
        acc = _dot(m_refs[0][...].astype(BF16), w_refs[0][:, cs])
        for m_ref, w_ref in zip(m_refs[1:], w_refs[1:]):
            acc += _dot(m_ref[...].astype(BF16), w_ref[:, cs])
        o_ref[:, cs] = x_ref[:, cs] + gate_ref[:, cs] * acc
    if with_next:
        h_ref[...] = _norm_mod(o_ref[...], ng_ref[...], nscale_ref[...], nshift_ref[...]).astype(BF16)


def _outproj(mixed, ws, x, gate, next_mod=None, *, tm, rows_per_batch):
    m = x.shape[0]
    row = pl.BlockSpec((tm, D_MODEL), lambda i: (i, 0))
    in_specs = ([pl.BlockSpec((tm, a.shape[1]), lambda i: (i, 0)) for a in mixed]
                + [pl.BlockSpec(w.shape, lambda i: (0, 0), pipeline_mode=pl.Buffered(1)) for w in ws]
                + [row, _mod_spec(gate, tm, rows_per_batch)])
    args = [*mixed, *ws, x, gate]
    out_specs = [row]
    out_shape = [jax.ShapeDtypeStruct((m, D_MODEL), F32)]
    if next_mod is not None:
        nscale, nshift, ng = next_mod
        in_specs += [_mod_spec(nscale, tm, rows_per_batch), _mod_spec(nshift, tm, rows_per_batch),
                     pl.BlockSpec((1, D_MODEL), lambda i: (0, 0))]
        args += [nscale, nshift, ng]
        out_specs.append(row)
        out_shape.append(jax.ShapeDtypeStruct((m, D_MODEL), BF16))
    return pl.pallas_call(
        functools.partial(_outproj_kernel, nin=len(mixed), with_next=next_mod is not None),
        grid=(m // tm,),
        in_specs=in_specs,
        out_specs=out_specs,
        out_shape=out_shape,
        compiler_params=_cparams(("arbitrary",)),
        name="outproj",
    )(*args)


def _seg_rms(x, g2, seg_ones):
    x2 = x * x
    hi = x2.astype(BF16)
    lo = (x2 - hi.astype(F32)).astype(BF16)
    ss = _dot(hi, seg_ones) + _dot(lo, seg_ones)
    return x * lax.rsqrt(ss * (1.0 / HD_A) + EPS) * g2


def _dup_halves(x, lo_half):
    sw = pltpu.roll(x, HD_A, 1)
    return [jnp.where(lo_half, x, sw), jnp.where(lo_half, sw, x)]


def _rows_to_cols(x):
    pad = jnp.zeros((x.shape[1] - x.shape[0], x.shape[1]), x.dtype)
    return jnp.concatenate([x, pad], axis=0).T


def _seg_ones():
    r = lax.broadcasted_iota(jnp.int32, (2 * HD_A, 2 * HD_A), 0)
    c = lax.broadcasted_iota(jnp.int32, (2 * HD_A, 2 * HD_A), 1)
    return ((r < HD_A) == (c < HD_A)).astype(BF16)


def _swa_prompt_kernel(relb_ref, sinks_ref, bucket_ref, q_ref, ga_ref, kv_ref, qn_ref, kn_ref,
                       o_ref, knew_ref, bias_ref, kprev_ref, vprev_ref, s_ref, pe_ref):
    b = pl.program_id(0)
    i = pl.program_id(1)

    @pl.when((b == 0) & (i == 0))
    def _():
        bk = bucket_ref[...]
        for h in range(H_A):
            bias_ref[h] = jnp.full((WINDOW, 2 * WINDOW), NEG_INF, F32)
        for bb in range(NUM_BUCKETS):
            hit = bk == bb
            for h in range(H_A):
                bias_ref[h] = jnp.where(hit, relb_ref[bb, h], bias_ref[h])
        col = lax.broadcasted_iota(jnp.int32, (WINDOW, 2 * WINDOW), 1)
        for h in range(H_A):
            bias_ref[H_A + h] = jnp.where(col >= WINDOW, bias_ref[h], NEG_INF)

    @pl.when(i == 0)
    def _():
        kprev_ref[...] = jnp.zeros(kprev_ref.shape, BF16)
        vprev_ref[...] = jnp.zeros(vprev_ref.shape, BF16)

    seg_ones = _seg_ones()
    lt = 2 * HD_A
    tpg = G_A // 2
    lo_half = lax.broadcasted_iota(jnp.int32, (WINDOW, lt), 1) < HD_A
    kv = kv_ref[...].astype(F32)
    kc = _seg_rms(kv[:, :W_K], kn_ref[...], seg_ones)
    k_dup = _dup_halves(kc, lo_half)
    v_dup = _dup_halves(kv[:, W_K:], lo_half)
    base = jnp.where(i == 0, H_A, 0)
    qn2 = qn_ref[...] * HD_A ** -0.5
    gq = tpg * WINDOW
    lo_g = lax.broadcasted_iota(jnp.int32, (gq, lt), 1) < HD_A
    grows = [slice(g * 2 * gq, (g + 1) * 2 * gq) for g in range(H_A_KV)]
    for g in range(H_A_KV):
        kk = jnp.concatenate([kprev_ref[g], k_dup[g].astype(BF16)], axis=0)
        xg = jnp.concatenate([q_ref[:, (g * tpg + p) * lt:(g * tpg + p + 1) * lt] for p in range(tpg)],
                             axis=0).astype(F32)
        xn = _seg_rms(xg, qn2, seg_ones)
        q_all = jnp.concatenate([jnp.where(lo_g, xn, 0.0), jnp.where(lo_g, 0.0, xn)], axis=0).astype(BF16)
        s_ref[grows[g], :] = _dot_nt(q_all, kk)
    sink_terms = []
    for g in range(H_A_KV):
        for n in range(G_A):
            h = g * G_A + 2 * (n % tpg) + n // tpg
            rows = slice((g * G_A + n) * WINDOW, (g * G_A + n + 1) * WINDOW)
            s = s_ref[rows, :] + bias_ref[base + h]
            sink = sinks_ref[h]
            m = jnp.maximum(jnp.max(s, axis=-1, keepdims=True), sink)
            pe_ref[rows, :] = jnp.exp(s - m).astype(BF16)
            sink_terms.append(jnp.exp(sink - m))
    o_ext = []
    for g in range(H_A_KV):
        vv = jnp.concatenate([vprev_ref[g], v_dup[g].astype(BF16)], axis=0)
        vv_ext = jnp.concatenate([vv, jnp.ones(vv.shape, BF16)], axis=1)
        o_ext.append(_dot(pe_ref[grows[g], :], vv_ext))
    for g in range(H_A_KV):
        for p in range(tpg):
            halves = []
            for a in range(2):
                n = a * tpg + p
                rows = slice(n * WINDOW, (n + 1) * WINDOW)
                halves.append(o_ext[g][rows, :lt] / (o_ext[g][rows, lt:] + sink_terms[g * G_A + n]))
            oa = jnp.where(lo_half, halves[0], halves[1])
            cols = slice((g * tpg + p) * lt, (g * tpg + p + 1) * lt)
            o_ref[:, cols] = (oa * _silu(ga_ref[:, cols].astype(F32))).astype(o_ref.dtype)
    for g in range(H_A_KV):
        kprev_ref[g] = k_dup[g].astype(BF16)
        vprev_ref[g] = v_dup[g].astype(BF16)

    @pl.when(i == pl.num_programs(1) - 1)
    def _():
        knew_ref[...] = kc.T


def _swa_prompt(proj, rel_bias, sinks, qn2, kn2, bucket, *, batch, seq):
    nb = seq // WINDOW
    rb = lambda b, i: b * nb + i
    smem = pl.BlockSpec(memory_space=pltpu.SMEM)
    return pl.pallas_call(
        _swa_prompt_kernel,
        grid=(batch, nb),
        in_specs=[smem, smem,
                  pl.BlockSpec((WINDOW, 2 * WINDOW), lambda b, i: (0, 0)),
                  pl.BlockSpec((WINDOW, W_A), lambda b, i: (rb(b, i), EV_QA // W_A)),
                  pl.BlockSpec((WINDOW, W_A), lambda b, i: (rb(b, i), EV_GA // W_A)),
                  pl.BlockSpec((WINDOW, 2 * W_K), lambda b, i: (rb(b, i), EV_KV // (2 * W_K))),
                  pl.BlockSpec((1, 2 * HD_A), lambda b, i: (0, 0)),
                  pl.BlockSpec((1, 2 * HD_A), lambda b, i: (0, 0))],
        out_specs=[pl.BlockSpec((WINDOW, W_A), lambda b, i: (rb(b, i), 0)),
                   pl.BlockSpec((None, WINDOW, W_K), lambda b, i: (b, 0, 0))],
        out_shape=[jax.ShapeDtypeStruct((batch * seq, W_A), BF16),
                   jax.ShapeDtypeStruct((batch, WINDOW, W_K), F32)],
        scratch_shapes=[pltpu.VMEM((2 * H_A, WINDOW, 2 * WINDOW), F32),
                        pltpu.VMEM((H_A_KV, WINDOW, W_K), BF16), pltpu.VMEM((H_A_KV, WINDOW, W_K), BF16),
                        pltpu.VMEM((H_A * WINDOW, 2 * WINDOW), F32), pltpu.VMEM((H_A * WINDOW, 2 * WINDOW), BF16)],
        compiler_params=_cparams(("arbitrary", "arbitrary")),
        name="swa_prompt",
    )(rel_bias, sinks, bucket, proj, proj, proj, qn2, kn2)


def _swa_sample_kernel(bkc_ref, bkn_ref, relrows_ref, sinkrows_ref, q_ref, ga_ref, kvn_ref, ck_ref, cv_ref,
                       qn_ref, kn_ref, o_ref, ko_ref, vo_ref, biasc_ref, biasn_ref, *, sb, ntok, unroll):
    @pl.when(pl.program_id(0) == 0)
    def _():
        bkc = bkc_ref[...]
        bkn = bkn_ref[...]
        rr = relrows_ref[...]
        bc = jnp.full(bkc.shape, NEG_INF, F32)
        bn = jnp.full(bkn.shape, NEG_INF, F32)
        for bb in range(NUM_BUCKETS):
            val = rr[:, bb:bb + 1]
            bc = jnp.where(bkc == bb, val, bc)
            bn = jnp.where(bkn == bb, val, bn)
        biasc_ref[...] = bc
        biasn_ref[...] = bn

    seg_ones = _seg_ones()
    lt = 2 * HD_A
    nrow = G_A * SAMPLE_PAD
    grow = nrow // H_A_KV
    lo_q = lax.broadcasted_iota(jnp.int32, (grow, lt), 1) < HD_A
    lo_n = lax.broadcasted_iota(jnp.int32, (SAMPLE_PAD, lt), 1) < HD_A
    lane_pos = lax.broadcasted_iota(jnp.int32, (W_K, WINDOW), 1)
    qn2 = qn_ref[...] * HD_A ** -0.5
    kn2 = kn_ref[...]
    sink = sinkrows_ref[:, 0:1]

    def body(it, carry):
        seqs = [it * unroll + u for u in range(unroll)]
        rows = [pl.ds(pl.multiple_of(s * SAMPLE_PAD, SAMPLE_PAD), SAMPLE_PAD) for s in seqs]
        kvn = [kvn_ref[r, :] for r in rows]
        kn_all = _seg_rms(jnp.concatenate([x[:, :W_K] for x in kvn], axis=0), kn2, seg_ones)
        q2_all = jnp.concatenate([q_ref[r, p * lt:(p + 1) * lt] for r in rows for p in range(G_A)], axis=0)
        xn_all = _seg_rms(q2_all, qn2, seg_ones)
        sc, sn, kn = [], [], []
        for u, s in enumerate(seqs):
            kn.append(kn_all[u * SAMPLE_PAD:(u + 1) * SAMPLE_PAD])
            kn_dup = _dup_halves(kn[u], lo_n)
            scg, sng = [], []
            for g in range(H_A_KV):
                xn = xn_all[u * nrow + g * grow:u * nrow + (g + 1) * grow]
                q4 = jnp.concatenate([jnp.where(lo_q, xn, 0.0), jnp.where(lo_q, 0.0, xn)], axis=0).astype(BF16)
                kt_g = ck_ref[s, g * HD_A:(g + 1) * HD_A, :].astype(BF16)
                scg.append(_dot(q4, jnp.concatenate([kt_g, kt_g], axis=0)))
                sng.append(_dot_nt(q4, kn_dup[g].astype(BF16)))
            sc.append(jnp.concatenate(scg, axis=0))
            sn.append(jnp.concatenate(sng, axis=0))
        pc, pn, sink_terms = [], [], []
        for u in range(unroll):
            scu = sc[u] + biasc_ref[...]
            snu = sn[u] + biasn_ref[:, :SAMPLE_PAD]
            m = jnp.maximum(jnp.maximum(jnp.max(scu, axis=-1, keepdims=True),
                                        jnp.max(snu, axis=-1, keepdims=True)), sink)
            pc.append(jnp.exp(scu - m).astype(BF16))
            pn.append(jnp.exp(snu - m).astype(BF16))
            sink_terms.append(jnp.exp(sink - m))
        o_ext = []
        for u, s in enumerate(seqs):
            vn_dup = _dup_halves(kvn[u][:, W_K:], lo_n)
            og = []
            for g in range(H_A_KV):
                vt_g = cv_ref[s, g * HD_A:(g + 1) * HD_A, :].astype(BF16)
                vt_ext = jnp.concatenate([vt_g, vt_g, jnp.ones((lt, WINDOW), BF16)], axis=0)
                vn_ext = jnp.concatenate([vn_dup[g].astype(BF16), jnp.ones((SAMPLE_PAD, lt), BF16)], axis=1)
                gr = slice(g * 2 * grow, (g + 1) * 2 * grow)
                og.append(_dot_nt(pc[u][gr], vt_ext) + _dot(pn[u][gr], vn_ext))
            o_ext.append(jnp.concatenate(og, axis=0))
        for u, s in enumerate(seqs):
            o4 = o_ext[u][:, :lt] / (o_ext[u][:, lt:] + sink_terms[u])
            xg = ga_ref[rows[u], :]
            for p in range(G_A):
                g, pp = divmod(p, G_A // 2)
                r0 = g * 2 * grow + pp * SAMPLE_PAD
                o2 = jnp.where(lo_n, o4[r0:r0 + SAMPLE_PAD], o4[r0 + grow:r0 + grow + SAMPLE_PAD])
                o_ref[rows[u], p * lt:(p + 1) * lt] = (o2 * _silu(xg[:, p * lt:(p + 1) * lt])).astype(o_ref.dtype)
            ko_ref[s] = pltpu.roll(jnp.where(lane_pos < ntok, _rows_to_cols(kn[u]), ck_ref[s]), WINDOW - ntok, 1)
            vo_ref[s] = pltpu.roll(jnp.where(lane_pos < ntok, _rows_to_cols(kvn[u][:, W_K:]), cv_ref[s]),
                                   WINDOW - ntok, 1)
        return carry

    lax.fori_loop(0, sb // unroll, body, 0)


def _swa_sample(proj, cache_k, cache_v, bkc, bkn, relrows, sinkrows, qn2, kn2, *, sb, ntok, unroll):
    nseq = cache_k.shape[0]
    rows = sb * SAMPLE_PAD
    full = lambda shape: pl.BlockSpec(shape, lambda i: tuple(0 for _ in shape))
    cache = pl.BlockSpec((sb, W_K, WINDOW), lambda i: (i, 0, 0))
    return pl.pallas_call(
        functools.partial(_swa_sample_kernel, sb=sb, ntok=ntok, unroll=unroll),
        grid=(nseq // sb,),
        in_specs=[full(bkc.shape), full(bkn.shape), full(relrows.shape), full(sinkrows.shape),
                  pl.BlockSpec((rows, W_A), lambda i: (i, EV_QA // W_A)),
                  pl.BlockSpec((rows, W_A), lambda i: (i, EV_GA // W_A)),
                  pl.BlockSpec((rows, 2 * W_K), lambda i: (i, EV_KV // (2 * W_K))),
                  cache, cache, full((1, 2 * HD_A)), full((1, 2 * HD_A))],
        out_specs=[pl.BlockSpec((rows, W_A), lambda i: (i, 0)), cache, cache],
        out_shape=[jax.ShapeDtypeStruct((nseq * SAMPLE_PAD, W_A), F32),
                   jax.ShapeDtypeStruct(cache_k.shape, F32),
                   jax.ShapeDtypeStruct(cache_v.shape, F32)],
        scratch_shapes=[pltpu.VMEM(bkc.shape, F32), pltpu.VMEM(bkn.shape, F32)],
        compiler_params=_cparams(("arbitrary",)),
        name="swa_sample",
    )(bkc, bkn, relrows, sinkrows, proj, proj, proj, cache_k, cache_v, qn2, kn2)


def _gla_pre(q, k, v, la, n_valid):
    c = q.shape[0]
    bcum = _cumsum_rows(la)
    rr = lax.broadcasted_iota(jnp.int32, (c, c), 0)
    cc = lax.broadcasted_iota(jnp.int32, (c, c), 1)
    causal = rr >= cc
    row = lax.broadcasted_iota(jnp.int32, (c, 1), 0)
    qts, kts, kds, vbs, dcols = [], [], [], [], []
    for h in range(H_B):
        ks = slice(h * DK_B, (h + 1) * DK_B)
        bc = bcum[:, ks]
        qts.append(((q[:, ks] * DK_B ** -0.5) * jnp.exp(bc)).astype(BF16))
        kts.append((k[:, ks] * jnp.exp(-bc)).astype(BF16))
        blast = bc[n_valid - 1:n_valid, :]
        kd = k[:, ks] * jnp.exp(blast - bc)
        if n_valid < c:
            kd = jnp.where(row < n_valid, kd, 0.0)
        kds.append(kd.astype(BF16))
        vbs.append(v[:, h * DV_B:(h + 1) * DV_B].astype(BF16))
        dcols.append(_row_to_col(jnp.exp(blast)))
    scores = [_dot_nt(qts[h], kts[h]) for h in range(H_B)]
    upds = [_dot_tn(kds[h], vbs[h]) for h in range(H_B)]
    return [(qts[h], jnp.where(causal, scores[h], 0.0).astype(BF16), vbs[h], upds[h], dcols[h])
            for h in range(H_B)]


def _gla_post(pre, gb, gla_g, states):
    outs = [_dot(a, vb) + _dot(qt, st.astype(BF16)) for (qt, a, vb, _, _), st in zip(pre, states)]
    new_states = [dcol * st + upd for (_, _, _, upd, dcol), st in zip(pre, states)]
    gated = []
    for h, o in enumerate(outs):
        on = o * lax.rsqrt(jnp.mean(o * o, axis=-1, keepdims=True) + EPS) * gla_g
        gated.append(on * _silu(gb[:, h * DV_B:(h + 1) * DV_B]))
    return jnp.concatenate(gated, axis=1), new_states


def _gla_prompt_kernel(q_ref, k_ref, v_ref, gb_ref, la_ref, g_ref, o_ref, s_ref, *, nchunk):
    @pl.when(pl.program_id(1) == 0)
    def _():
        s_ref[...] = jnp.zeros(s_ref.shape, F32)

    gla_g = g_ref[...]
    chunk_rows = [slice(c * GLA_CHUNK, (c + 1) * GLA_CHUNK) for c in range(nchunk)]
    pres = [_gla_pre(q_ref[r, :].astype(F32), k_ref[r, :].astype(F32), v_ref[r, :].astype(F32), la_ref[r, :],
                     GLA_CHUNK) for r in chunk_rows]
    states = [s_ref[h] for h in range(H_B)]
    for r, pre in zip(chunk_rows, pres):
        out, states = _gla_post(pre, gb_ref[r, :].astype(F32), gla_g, states)
        o_ref[r, :] = out.astype(o_ref.dtype)
    for h in range(H_B):
        s_ref[h] = states[h]


def _gla_prompt(proj, la, gla_g, *, batch, seq, rows):
    nstep = seq // rows
    rb = lambda b, i: b * nstep + i
    return pl.pallas_call(
        functools.partial(_gla_prompt_kernel, nchunk=rows // GLA_CHUNK),
        grid=(batch, nstep),
        in_specs=[pl.BlockSpec((rows, H_B * DK_B), lambda b, i: (rb(b, i), EV_QB // (H_B * DK_B))),
                  pl.BlockSpec((rows, H_B * DK_B), lambda b, i: (rb(b, i), EV_KB // (H_B * DK_B))),
                  pl.BlockSpec((rows, W_B), lambda b, i: (rb(b, i), EV_VB // W_B)),
                  pl.BlockSpec((rows, W_B), lambda b, i: (rb(b, i), EV_GB // W_B)),
                  pl.BlockSpec((rows, H_B * DK_B), lambda b, i: (rb(b, i), 0)),
                  pl.BlockSpec((1, DV_B), lambda b, i: (0, 0))],
        out_specs=[pl.BlockSpec((rows, W_B), lambda b, i: (rb(b, i), 0)),
                   pl.BlockSpec((None, H_B, DK_B, DV_B), lambda b, i: (b, 0, 0, 0))],
        out_shape=[jax.ShapeDtypeStruct((batch * seq, W_B), BF16),
                   jax.ShapeDtypeStruct((batch, H_B, DK_B, DV_B), F32)],
        compiler_params=_cparams(("arbitrary", "arbitrary")),
        name="gla_prompt",
    )(proj, proj, proj, proj, la, gla_g)


def _gla_sample_kernel(q_ref, k_ref, v_ref, gb_ref, la_ref, g_ref, s_in_ref, o_ref, s_ref, *, sb, ntok, unroll):
    gla_g = g_ref[...]

    def body(it, carry):
        seqs = [it * unroll + u for u in range(unroll)]
        rows = [pl.ds(pl.multiple_of(s * SAMPLE_PAD, SAMPLE_PAD), SAMPLE_PAD) for s in seqs]
        pres = [_gla_pre(q_ref[r, :], k_ref[r, :], v_ref[r, :], la_ref[r, :], ntok) for r in rows]
        for s, r, pre in zip(seqs, rows, pres):
            out, new_states = _gla_post(pre, gb_ref[r, :], gla_g, [s_in_ref[s, h] for h in range(H_B)])
            o_ref[r, :] = out.astype(o_ref.dtype)
            for h in range(H_B):
                s_ref[s, h] = new_states[h]
        return carry

    lax.fori_loop(0, sb // unroll, body, 0)


def _gla_sample(proj, la, gla_g, state, *, sb, ntok, unroll):
    nseq = state.shape[0]
    rows = sb * SAMPLE_PAD
    return pl.pallas_call(
        functools.partial(_gla_sample_kernel, sb=sb, ntok=ntok, unroll=unroll),
        grid=(nseq // sb,),
        in_specs=[pl.BlockSpec((rows, H_B * DK_B), lambda i: (i, EV_QB // (H_B * DK_B))),
                  pl.BlockSpec((rows, H_B * DK_B), lambda i: (i, EV_KB // (H_B * DK_B))),
                  pl.BlockSpec((rows, W_B), lambda i: (i, EV_VB // W_B)),
                  pl.BlockSpec((rows, W_B), lambda i: (i, EV_GB // W_B)),
                  pl.BlockSpec((rows, H_B * DK_B), lambda i: (i, 0)),
                  pl.BlockSpec((1, DV_B), lambda i: (0, 0)),
                  pl.BlockSpec((sb, H_B, DK_B, DV_B), lambda i: (i, 0, 0, 0))],
        out_specs=[pl.BlockSpec((rows, W_B), lambda i: (i, 0)),
                   pl.BlockSpec((sb, H_B, DK_B, DV_B), lambda i: (i, 0, 0, 0))],
        out_shape=[jax.ShapeDtypeStruct((nseq * SAMPLE_PAD, W_B), BF16),
                   jax.ShapeDtypeStruct(state.shape, F32)],
        compiler_params=_cparams(("arbitrary",)),
        name="gla_sample",
    )(proj, proj, proj, proj, la, gla_g, state)


def _ret_decay(c, h):
    ri = lax.broadcasted_iota(jnp.int32, (c, c), 0)
    ci = lax.broadcasted_iota(jnp.int32, (c, c), 1)
    dist = (ri - ci).astype(F32)
    return jnp.where(ri >= ci, jnp.exp(jnp.maximum(dist, 0.0) * LOG_GAMMA[h]), 0.0)


def _ret_chunk(q_ref, k_ref, v_ref, g_ref, retg, state_in_ref, state_ref, o_ref, n_valid, after_head=None,
               decay_ref=None):
    c = q_ref.shape[0]
    row = lax.broadcasted_iota(jnp.int32, (c, 1), 0)
    rowf = row.astype(F32)
    a_heads = []
    for h in range(H_C):
        ks = slice(h * DK_C, (h + 1) * DK_C)
        decay = _ret_decay(c, h) if decay_ref is None else decay_ref[h]
        a_heads.append((_dot_nt(q_ref[:, ks].astype(BF16), k_ref[:, ks].astype(BF16)) * decay).astype(BF16))
    for h in range(H_C):
        lg = LOG_GAMMA[h]
        ks = slice(h * DK_C, (h + 1) * DK_C)
        vs = slice(h * DV_C, (h + 1) * DV_C)
        kf = k_ref[:, ks].astype(F32)
        vb = v_ref[:, vs].astype(BF16)
        st = state_in_ref[h]
        q_in = (q_ref[:, ks].astype(F32) * jnp.exp((rowf + 1.0) * lg)).astype(BF16)
        o = _dot(a_heads[h], vb) + _dot(q_in, st.astype(BF16))
        kd = kf * jnp.exp((n_valid - 1.0 - rowf) * lg)
        if n_valid < c:
            kd = jnp.where(row < n_valid, kd, 0.0)
        state_ref[h] = math.exp(n_valid * lg) * st + _dot_tn(kd.astype(BF16), vb)
        on = o * lax.rsqrt(jnp.mean(o * o, axis=-1, keepdims=True) + EPS) * retg
        o_ref[:, vs] = (on * _silu(g_ref[:, vs].astype(F32))).astype(o_ref.dtype)
        if after_head is not None:
            after_head(h)


def _ret_prompt_kernel(q_ref, k_ref, v_ref, g_ref, retg_ref, o_ref, s_ref, decay_ref):
    chunk = q_ref.shape[0]

    @pl.when((pl.program_id(0) == 0) & (pl.program_id(1) == 0))
    def _():
        for h in range(H_C):
            decay_ref[h] = _ret_decay(chunk, h)

    @pl.when(pl.program_id(1) == 0)
    def _():
        s_ref[...] = jnp.zeros(s_ref.shape, F32)

    _ret_chunk(q_ref, k_ref, v_ref, g_ref, retg_ref[...], s_ref, s_ref, o_ref, chunk, decay_ref=decay_ref)


def _ret_prompt(proj, ret_g, *, batch, seq):
    chunk = RET_PROMPT_CHUNK if seq % RET_PROMPT_CHUNK == 0 else RET_CHUNK
    nstep = seq // chunk
    rb = lambda b, i: b * nstep + i
    qk = H_C * DK_C
    return pl.pallas_call(
        _ret_prompt_kernel,
        grid=(batch, nstep),
        in_specs=[pl.BlockSpec((chunk, qk), lambda b, i: (rb(b, i), 0)),
                  pl.BlockSpec((chunk, qk), lambda b, i: (rb(b, i), 1)),
                  pl.BlockSpec((chunk, W_C), lambda b, i: (rb(b, i), 1)),
                  pl.BlockSpec((chunk, W_C), lambda b, i: (rb(b, i), 2)),
                  pl.BlockSpec((1, DV_C), lambda b, i: (0, 0))],
        out_specs=[pl.BlockSpec((chunk, W_C), lambda b, i: (rb(b, i), 0)),
                   pl.BlockSpec((None, H_C, DK_C, DV_C), lambda b, i: (b, 0, 0, 0))],
        out_shape=[jax.ShapeDtypeStruct((batch * seq, W_C), BF16),
                   jax.ShapeDtypeStruct((batch, H_C, DK_C, DV_C), F32)],
        scratch_shapes=[pltpu.VMEM((H_C, chunk, chunk), F32)],
        compiler_params=_cparams(("arbitrary", "arbitrary")),
        name="ret_prompt",
    )(proj, proj, proj, proj, ret_g)


def _ret_sample_job(proj, ret_g, state, *, ntok):
    nseq = state.shape[0]
    qk = H_C * DK_C
    state_spec = lambda blk: pl.BlockSpec((None, H_C, DK_C, DV_C), lambda j, i: (blk(j, i), 0, 0, 0))

    def body(in_refs, out_refs, host_chunk):
        q_ref, k_ref, v_ref, g_ref, retg_ref, s_in_ref = in_refs
        o_ref, s_ref = out_refs
        _ret_chunk(q_ref, k_ref, v_ref, g_ref, retg_ref[...], s_in_ref, s_ref, o_ref, ntok, after_head=host_chunk)

    return _SideJob(
        njobs=nseq,
        units=H_C,
        args=[proj, proj, proj, proj, ret_g, state],
        in_specs=lambda blk: [pl.BlockSpec((SAMPLE_PAD, qk), lambda j, i: (blk(j, i), 0)),
                              pl.BlockSpec((SAMPLE_PAD, qk), lambda j, i: (blk(j, i), 1)),
                              pl.BlockSpec((SAMPLE_PAD, W_C), lambda j, i: (blk(j, i), 1)),
                              pl.BlockSpec((SAMPLE_PAD, W_C), lambda j, i: (blk(j, i), 2)),
                              pl.BlockSpec((1, DV_C), lambda j, i: (0, 0)),
                              state_spec(blk)],
        out_specs=lambda blk: [pl.BlockSpec((SAMPLE_PAD, W_C), lambda j, i: (blk(j, i), 0)), state_spec(blk)],
        out_shape=[jax.ShapeDtypeStruct((nseq * SAMPLE_PAD, W_C), BF16),
                   jax.ShapeDtypeStruct(state.shape, F32)],
        body=body)


def _t5_bucket(dist):
    dist = np.maximum(dist, 0)
    max_exact = NUM_BUCKETS // 2
    log_ratio = (np.log(np.maximum(dist, 1).astype(np.float32) / np.float32(max_exact))
                 / np.float32(math.log(MAX_DISTANCE / max_exact)))
    large = np.minimum(max_exact + (log_ratio * np.float32(NUM_BUCKETS - max_exact)).astype(np.int32),
                       NUM_BUCKETS - 1)
    return np.where(dist < max_exact, dist, large)


def _bucket_or_masked(dist):
    return np.where((dist >= 0) & (dist <= WINDOW), _t5_bucket(dist), -1).astype(np.int32)


def _rotary_tables(pos):
    half = DK_C // 2
    inv = ROPE_BASE ** (-jnp.arange(half, dtype=F32) / half)
    ang = pos.astype(F32)[:, None] * inv[None, :]
    return jnp.cos(ang), jnp.sin(ang)


def _regroup_kernel(wt_ref, o_ref):
    o_ref[...] = wt_ref[...].T.astype(o_ref.dtype)


def _even_weight_layout(w_in):
    wt = jnp.swapaxes(w_in, 0, 1)
    src = lambda j: jnp.where(j < 4, j, jnp.where(j < 8, j + 1, jnp.where(j < 16, j + 5,
                                                                         jnp.where(j < 20, j - 7, 4))))
    main = pl.pallas_call(
        _regroup_kernel,
        grid=(EV_N // MXU_N,),
        in_specs=[pl.BlockSpec((MXU_N, D_MODEL), lambda j: (src(j), 0))],
        out_specs=pl.BlockSpec((D_MODEL, MXU_N), lambda j: (0, j)),
        out_shape=jax.ShapeDtypeStruct((D_MODEL, EV_N), BF16),
        compiler_params=_cparams(("arbitrary",)),
        name="regroup_even_weights",
    )(wt)
    lr_t = jnp.pad(wt[EV_N:], ((0, 128 - GLA_RANK), (0, 0))).astype(BF16)
    return main, lr_t


def _pad_tokens(a, ntok):
    pad = [(0, 0), (0, SAMPLE_PAD - ntok)] + [(0, 0)] * (a.ndim - 2)
    a = jnp.pad(a, pad)
    return a.reshape((a.shape[0] * SAMPLE_PAD,) + a.shape[2:])


def _rows(v, reps):
    return jnp.repeat(v, reps, axis=0)


def kernel(x_prompt, x_sample, cache_swa_k, cache_swa_v, state_gla, state_ret, c_prompt, c_sample, rel_bias,
           ada_w_even, ada_b_even, norm_g_even, w_in_even, w_lr_even, b_lr_even, qn_g_even, kn_g_even,
           sinks_even, gla_g_even, w_out_even, ada_w_odd, ada_b_odd, norm_g_odd, w_in_odd, ret_g_odd, w_out_odd):
    batch, seq, _ = x_prompt.shape
    nseq, ntok, _ = x_sample.shape
    mp = batch * seq

    c_all = jnp.concatenate([c_prompt, c_sample], axis=0)
    mod_e = _ada_mod(c_all, ada_w_even[0], ada_b_even[0])
    mod_o = _ada_mod(c_all, ada_w_odd[0], ada_b_odd[0])

    def split_mod(mod):
        shift, scale, gate = jnp.split(mod, 3, axis=1)
        p = tuple(a[:batch].reshape(batch, 1, D_MODEL) for a in (shift, scale, gate))
        s = tuple(_rows(a[batch:], SAMPLE_PAD) for a in (shift, scale, gate))
        return p, s

    (shift_ep, scale_ep, gate_ep), (shift_es, scale_es, gate_es) = split_mod(mod_e)
    (shift_op, scale_op, gate_op), (shift_os, scale_os, gate_os) = split_mod(mod_o)

    xp = x_prompt.reshape(mp, D_MODEL)
    xs = _pad_tokens(x_sample, ntok)
    ms = xs.shape[0]

    w_e, w_lri = _even_weight_layout(w_in_even[0])
    w_lr = jnp.pad(w_lr_even[0], ((0, 128 - GLA_RANK), (0, 0))).astype(BF16)
    b_lr = b_lr_even[0].reshape(1, -1)
    g_e = norm_g_even[0].reshape(1, D_MODEL)
    qn2 = jnp.tile(qn_g_even[0].reshape(1, HD_A), (1, 2))
    kn2 = jnp.tile(kn_g_even[0].reshape(1, HD_A), (1, 2))
    gla_g = gla_g_even[0].reshape(1, DV_B)
    w_out_e = w_out_even[0].astype(BF16)
    w_out_a, w_out_b = w_out_e[:W_A], w_out_e[W_A:]

    h0_p, la_p = _prenorm_even(xp, scale_ep, shift_ep, g_e, w_lri, w_lr, b_lr, tm=TILES.prenorm, rows_per_batch=seq)
    h0_s, la_s = _prenorm_even(xs, scale_es, shift_es, g_e, w_lri, w_lr, b_lr, tm=TILES.prenorm, rows_per_batch=ms)
    (proj_p,) = _inproj(h0_p, w_e, tm=TILES.inproj_prompt, tn=EV_N // 3, out_dtype=BF16, name="inproj_even")
    (proj_s,) = _inproj(h0_s, w_e, tm=TILES.inproj_sample, tn=EV_N // 3, out_dtype=F32, name="inproj_even")

    ii = np.arange(WINDOW)
    ss = np.arange(2 * WINDOW)
    bucket_p = jnp.asarray(_bucket_or_masked(WINDOW + ii[:, None] - ss[None, :]))
    mixed_a_p, k_last = _swa_prompt(proj_p, rel_bias, sinks_even[0], qn2, kn2, bucket_p, batch=batch, seq=seq)
    mixed_b_p, gla_p = _gla_prompt(proj_p, la_p, gla_g, batch=batch, seq=seq, rows=TILES.gla_rows)
    g_o = norm_g_odd[0].reshape(1, D_MODEL)
    y1_p, h1_p = _outproj([mixed_a_p, mixed_b_p], [w_out_a, w_out_b], xp, gate_ep, (scale_op, shift_op, g_o),
                          tm=TILES.outproj_prompt, rows_per_batch=seq)

    w_buf = cache_swa_k.shape[2]
    to_fp = lambda c: jnp.transpose(c[0], (0, 2, 3, 1)).reshape(nseq, W_K, w_buf)
    from_fp = lambda c: jnp.transpose(c.reshape(c.shape[0], H_A_KV, HD_A, -1), (0, 3, 1, 2))[None]
    proj_p3 = proj_p.reshape(batch, seq, EV_N)
    swa_k_p = from_fp(k_last)
    swa_v_p = proj_p3[:, seq - WINDOW:, EV_KV + W_K: EV_KV + 2 * W_K].astype(F32).reshape(
        1, batch, WINDOW, H_A_KV, HD_A)

    rr = np.arange(H_A * SAMPLE_PAD)
    row_head = 8 * (rr // 64) + 2 * ((rr % 32) // 8) + (rr % 64) // 32
    tt = rr % SAMPLE_PAD
    jj = np.arange(WINDOW)
    live = (tt < ntok)[:, None]
    bkc = jnp.asarray(np.where(live, _bucket_or_masked(WINDOW + tt[:, None] - jj[None, :]), -1))
    bkn = jnp.asarray(np.where(live & (jj[None, :] < ntok), _bucket_or_masked(tt[:, None] - jj[None, :]), -1))
    relrows = rel_bias.T[row_head]
    sinkrows = jnp.broadcast_to(sinks_even[0][row_head][:, None], (H_A * SAMPLE_PAD, 128))
    mixed_a_s, k_cache_s, v_cache_s = _swa_sample(
        proj_s, to_fp(cache_swa_k), to_fp(cache_swa_v),
        bkc, bkn, relrows, sinkrows, qn2, kn2, sb=TILES.swa_sample_seqs, ntok=ntok, unroll=TILES.sample_unroll)
    mixed_b_s, gla_s = _gla_sample(proj_s, la_s, gla_g, state_gla[0], sb=TILES.gla_sample_seqs, ntok=ntok,
                                   unroll=TILES.sample_unroll)
    y1_s, h1_s = _outproj([mixed_a_s, mixed_b_s], [w_out_a, w_out_b], xs, gate_es, (scale_os, shift_os, g_o),
                          tm=TILES.outproj_sample, rows_per_batch=ms)

    w_o = w_in_odd[0].astype(BF16)
    ret_g = ret_g_odd[0].reshape(1, DV_C)
    w_out_o = w_out_odd[0].astype(BF16)
    cos_p, sin_p = _rotary_tables(jnp.arange(seq))
    pos_s = PAST_LEN + jnp.minimum(jnp.arange(SAMPLE_PAD), ntok - 1)
    cos_s, sin_s = _rotary_tables(jnp.tile(pos_s, nseq))

    (projo_s,) = _inproj(h1_s, w_o, (cos_s, sin_s), tm=TILES.inproj_sample, tn=H_C * DK_C, out_dtype=F32,
                         name="inproj_odd")
    ret_job = _ret_sample_job(projo_s, ret_g, state_ret[0], ntok=ntok)
    projo_p, o_s, ret_s = _inproj(h1_p, w_o, (cos_p, sin_p), ret_job, tm=TILES.inproj_side, tn=H_C * DK_C,
                                  out_dtype=BF16,
                                  name="inproj_odd_ret_sample")
    o_p, ret_p = _ret_prompt(projo_p, ret_g, batch=batch, seq=seq)
    (y2_p,) = _outproj([o_p], [w_out_o], y1_p, gate_op, tm=TILES.outproj_prompt, rows_per_batch=seq)
    (y2_s,) = _outproj([o_s], [w_out_o], y1_s, gate_os, tm=TILES.outproj_sample, rows_per_batch=ms)

    y_prompt = y2_p.reshape(batch, seq, D_MODEL)
    y_sample = y2_s.reshape(nseq, SAMPLE_PAD, D_MODEL)[:, :ntok]
    return (y_prompt, y_sample, swa_k_p, swa_v_p, gla_p[None], ret_p[None],
            from_fp(k_cache_s), from_fp(v_cache_s),
            gla_s[None], ret_s[None])
```

```python
import functools
import math
from typing import Callable, NamedTuple

import numpy as np
import jax
import jax.numpy as jnp
from jax import lax
from jax.experimental import pallas as pl
from jax.experimental.pallas import tpu as pltpu

F32 = jnp.float32
BF16 = jnp.bfloat16

D_MODEL = 2048
WINDOW = 128
HD_A = 64
H_A = 16
H_A_KV = 2
G_A = 8
W_A = 1024
NUM_BUCKETS = 32
MAX_DISTANCE = 128
H_B = 4
DV_B = 256
DK_B = 128
W_B = 1024
GLA_RANK = 16
GLA_TAU = 16.0
GLA_CHUNK = 64
H_C = 8
DK_C = 256
DV_C = 512
W_C = 4096
RET_CHUNK = 128
RET_PROMPT_CHUNK = 256
ROPE_BASE = 10000.0
EPS = 1e-6
NEG_INF = -1e30
PAST_LEN = 8192

EV_QA, EV_GA, EV_VB, EV_GB, EV_QB, EV_KB, EV_KV = 0, 1024, 2048, 3072, 4096, 4608, 5120
EV_N = 5376
OD_N = 12288
W_K = H_A_KV * HD_A
SAMPLE_PAD = 8
MXU_N = 256
VMEM_LIMIT = 56 * 1024 * 1024

class _Tiles(NamedTuple):
    prenorm: int = 512
    inproj_prompt: int = 1024
    inproj_sample: int = 512
    inproj_side: int = 256
    outproj_prompt: int = 512
    outproj_sample: int = 256
    gla_rows: int = 256
    swa_sample_seqs: int = 32
    gla_sample_seqs: int = 16
    sample_unroll: int = 8


TILES = _Tiles()

LOG_GAMMA = [float(np.log1p(-np.exp2(np.float32(-5.0 - h)))) for h in range(H_C)]


def _cparams(sem):
    return pltpu.CompilerParams(dimension_semantics=sem, vmem_limit_bytes=VMEM_LIMIT)


def _silu(x):
    t = 0.5 * x
    return t * (1.0 + jnp.tanh(t))


def _dot(a, b):
    return jnp.dot(a, b, preferred_element_type=F32)


def _dot_nt(a, b):
    return lax.dot_general(a, b, (((1,), (1,)), ((), ())), preferred_element_type=F32)


def _dot_tn(a, b):
    return lax.dot_general(a, b, (((0,), (0,)), ((), ())), preferred_element_type=F32)


def _cumsum_rows(x):
    c = x.shape[0]
    r = lax.broadcasted_iota(jnp.int32, (c, c), 0)
    s = lax.broadcasted_iota(jnp.int32, (c, c), 1)
    tri = (r >= s).astype(BF16)
    hi = x.astype(BF16)
    r1 = x - hi.astype(F32)
    mid = r1.astype(BF16)
    lo = (r1 - mid.astype(F32)).astype(BF16)
    return _dot(tri, hi) + _dot(tri, mid) + _dot(tri, lo)


def _row_to_col(r):
    n = r.shape[1]
    ri = lax.broadcasted_iota(jnp.int32, (n, n), 0)
    ci = lax.broadcasted_iota(jnp.int32, (n, n), 1)
    return jnp.sum(jnp.where(ri == ci, jnp.broadcast_to(r, (n, n)), 0.0), axis=1, keepdims=True)


def _ada_kernel(c_ref, w_ref, b_ref, o_ref):
    sc = _silu(c_ref[...]).astype(BF16)
    o_ref[...] = _dot(sc, w_ref[...].astype(BF16)) + b_ref[...]


def _ada_mod(c_all, w, b):
    m = c_all.shape[0]
    n = w.shape[1]
    tn = 1536
    return pl.pallas_call(
        _ada_kernel,
        grid=(n // tn,),
        in_specs=[pl.BlockSpec((m, D_MODEL), lambda j: (0, 0)),
                  pl.BlockSpec((D_MODEL, tn), lambda j: (0, j)),
                  pl.BlockSpec((1, tn), lambda j: (0, j))],
        out_specs=pl.BlockSpec((m, tn), lambda j: (0, j)),
        out_shape=jax.ShapeDtypeStruct((m, n), F32),
        compiler_params=_cparams(("arbitrary",)),
        name="ada_mod",
    )(c_all, w, b.reshape(1, n))


def _norm_mod(x, g, scale, shift):
    ms = jnp.mean(x * x, axis=-1, keepdims=True)
    y = x * lax.rsqrt(ms + EPS) * g
    return y * (1.0 + scale) + shift


def _log_sigmoid(z):
    return jnp.minimum(z, 0.0) - jnp.log(1.0 + jnp.exp(-jnp.abs(z)))


def _mod_spec(arr, tm, rows_per_batch):
    if arr.ndim == 3:
        return pl.BlockSpec((None, 1, D_MODEL), lambda i: (i // (rows_per_batch // tm), 0, 0))
    return pl.BlockSpec((tm, D_MODEL), lambda i: (i, 0))


def _prenorm_even_kernel(x_ref, scale_ref, shift_ref, g_ref, wlri_ref, wlr_ref, blr_ref, h_ref, la_ref):
    hb = _norm_mod(x_ref[...], g_ref[...], scale_ref[...], shift_ref[...]).astype(BF16)
    h_ref[...] = hb
    lr = _dot_nt(hb, wlri_ref[...])
    z = _dot(lr.astype(BF16), wlr_ref[...]) + blr_ref[...]
    la_ref[...] = _log_sigmoid(z) / GLA_TAU


def _prenorm_even(x, scale, shift, g, wlri, wlr, blr, *, tm, rows_per_batch):
    m = x.shape[0]
    const = lambda shape: pl.BlockSpec(shape, lambda i: (0, 0))
    return pl.pallas_call(
        _prenorm_even_kernel,
        grid=(m // tm,),
        in_specs=[pl.BlockSpec((tm, D_MODEL), lambda i: (i, 0)),
                  _mod_spec(scale, tm, rows_per_batch), _mod_spec(shift, tm, rows_per_batch),
                  const((1, D_MODEL)), const((128, D_MODEL)), const((128, H_B * DK_B)), const((1, H_B * DK_B))],
        out_specs=[pl.BlockSpec((tm, D_MODEL), lambda i: (i, 0)),
                   pl.BlockSpec((tm, H_B * DK_B), lambda i: (i, 0))],
        out_shape=[jax.ShapeDtypeStruct((m, D_MODEL), BF16),
                   jax.ShapeDtypeStruct((m, H_B * DK_B), F32)],
        compiler_params=_cparams(("arbitrary",)),
        name="prenorm_even",
    )(x, scale, shift, g, wlri, wlr, blr)


class _SideJob(NamedTuple):
    njobs: int
    args: list
    in_specs: Callable
    out_specs: Callable
    out_shape: list
    units: int
    body: Callable


def _inproj_kernel(*refs, rot_tiles, side, nsteps):
    nmain = 4 if rot_tiles else 2
    j = pl.program_id(0)
    run_side = None
    if side is not None:
        njobs, nsi, nso, body = side
        side_in = refs[nmain:nmain + nsi]
        side_out = refs[nmain + nsi + 1:nmain + nsi + 1 + nso]
        step = j * pl.num_programs(1) + pl.program_id(1)
        block = (step * njobs) // nsteps
        prev_block = ((step - 1) * njobs) // nsteps
        run_side = (step == 0) | (block != prev_block)
        refs = refs[:nmain] + (refs[nmain + nsi],)
    h_ref, w_ref, *rest = refs
    o_ref = rest[-1]
    nchunk = o_ref.shape[1] // MXU_N
    half = DK_C // 2

    def plain_chunks():
        def chunk(c):
            cs = slice(c * MXU_N, (c + 1) * MXU_N)
            o_ref[:, cs] = _dot(h_ref[...], w_ref[:, cs]).astype(o_ref.dtype)
        return chunk

    def rotary_chunks():
        mult = jnp.where(j < rot_tiles // 2, 1.0, DK_C ** -0.5)
        cos = rest[0][...] * mult
        sin = rest[1][...] * mult

        def chunk(c):
            acc = _dot(h_ref[...], w_ref[:, c * DK_C:(c + 1) * DK_C])
            x1 = acc[:, :half]
            x2 = acc[:, half:]
            o_ref[:, c * DK_C:c * DK_C + half] = (x1 * cos - x2 * sin).astype(o_ref.dtype)
            o_ref[:, c * DK_C + half:(c + 1) * DK_C] = (x2 * cos + x1 * sin).astype(o_ref.dtype)
        return chunk

    def tile(make_chunk, with_side):
        def run():
            chunk = make_chunk()
            if with_side:
                body(side_in, side_out, chunk)
            else:
                for c in range(nchunk):
                    chunk(c)
        return run

    kinds = [(plain_chunks, j >= rot_tiles)]
    if rot_tiles:
        kinds.append((rotary_chunks, j < rot_tiles))
    for make_chunk, is_kind in kinds:
        if run_side is None:
            pl.when(is_kind)(tile(make_chunk, False))
        else:
            pl.when(is_kind & run_side)(tile(make_chunk, True))
            pl.when(is_kind & jnp.logical_not(run_side))(tile(make_chunk, False))


def _inproj(h, w, rot=None, side=None, *, tm, tn, out_dtype, name):
    m = h.shape[0]
    n = w.shape[1]
    ni = m // tm
    nsteps = (n // tn) * ni
    in_specs = [pl.BlockSpec((tm, D_MODEL), lambda j, i: (i, 0)),
                pl.BlockSpec((D_MODEL, tn), lambda j, i: (0, j))]
    args = [h, w]
    rot_tiles = 0
    if rot is not None:
        nblk = rot[0].shape[0] // tm
        in_specs += [pl.BlockSpec((tm, DK_C // 2), lambda j, i: (i % nblk, 0))] * 2
        args += list(rot)
        rot_tiles = 2 * (H_C * DK_C) // tn
    out_specs = [pl.BlockSpec((tm, tn), lambda j, i: (i, j))]
    out_shape = [jax.ShapeDtypeStruct((m, n), out_dtype)]
    side_static = None
    if side is not None:
        assert side.njobs <= nsteps and side.units == tn // MXU_N
        block = lambda j, i: ((j * ni + i) * side.njobs) // nsteps
        in_specs += side.in_specs(block)
        args += side.args
        out_specs += side.out_specs(block)
        out_shape += side.out_shape
        side_static = (side.njobs, len(side.args), len(side.out_shape), side.body)
    return pl.pallas_call(
        functools.partial(_inproj_kernel, rot_tiles=rot_tiles, side=side_static, nsteps=nsteps),
        grid=(n // tn, ni),
        in_specs=in_specs,
        out_specs=out_specs,
        out_shape=out_shape,
        compiler_params=_cparams(("arbitrary", "arbitrary")),
        name=name,
    )(*args)


def _outproj_kernel(*refs, nin, with_next):
    m_refs, w_refs = refs[:nin], refs[nin:2 * nin]
    if with_next:
        x_ref, gate_ref, nscale_ref, nshift_ref, ng_ref, o_ref, h_ref = refs[2 * nin:]
    else:
        x_ref, gate_ref, o_ref = refs[2 * nin:]
    for c in range(o_ref.shape[1] // MXU_N):
        cs = slice(c * MXU_N, (c + 1) * MXU_N)
        acc = _dot(m_refs[0][...].astype(BF16), w_refs[0][:, cs])
        for m_ref, w_ref in zip(m_refs[1:], w_refs[1:]):
            acc += _dot(m_ref[...].astype(BF16), w_ref[:, cs])
        o_ref[:, cs] = x_ref[:, cs] + gate_ref[:, cs] * acc
    if with_next:
        h_ref[...] = _norm_mod(o_ref[...], ng_ref[...], nscale_ref[...], nshift_ref[...]).astype(BF16)


def _outproj(mixed, ws, x, gate, next_mod=None, *, tm, rows_per_batch):
    m = x.shape[0]
    row = pl.BlockSpec((tm, D_MODEL), lambda i: (i, 0))
    in_specs = ([pl.BlockSpec((tm, a.shape[1]), lambda i: (i, 0)) for a in mixed]
                + [pl.BlockSpec(w.shape, lambda i: (0, 0), pipeline_mode=pl.Buffered(1)) for w in ws]
                + [row, _mod_spec(gate, tm, rows_per_batch)])
    args = [*mixed, *ws, x, gate]
    out_specs = [row]
    out_shape = [jax.ShapeDtypeStruct((m, D_MODEL), F32)]
    if next_mod is not None:
        nscale, nshift, ng = next_mod
        in_specs += [_mod_spec(nscale, tm, rows_per_batch), _mod_spec(nshift, tm, rows_per_batch),
                     pl.BlockSpec((1, D_MODEL), lambda i: (0, 0))]
        args += [nscale, nshift, ng]
        out_specs.append(row)
        out_shape.append(jax.ShapeDtypeStruct((m, D_MODEL), BF16))
    return pl.pallas_call(
        functools.partial(_outproj_kernel, nin=len(mixed), with_next=next_mod is not None),
        grid=(m // tm,),
        in_specs=in_specs,
        out_specs=out_specs,
        out_shape=out_shape,
        compiler_params=_cparams(("arbitrary",)),
        name="outproj",
    )(*args)


def _seg_rms(x, g2, seg_ones):
    x2 = x * x
    hi = x2.astype(BF16)
    lo = (x2 - hi.astype(F32)).astype(BF16)
    ss = _dot(hi, seg_ones) + _dot(lo, seg_ones)
    return x * lax.rsqrt(ss * (1.0 / HD_A) + EPS) * g2


def _dup_halves(x, lo_half):
    sw = pltpu.roll(x, HD_A, 1)
    return [jnp.where(lo_half, x, sw), jnp.where(lo_half, sw, x)]


def _rows_to_cols(x):
    pad = jnp.zeros((x.shape[1] - x.shape[0], x.shape[1]), x.dtype)
    return jnp.concatenate([x, pad], axis=0).T


def _seg_ones():
    r = lax.broadcasted_iota(jnp.int32, (2 * HD_A, 2 * HD_A), 0)
    c = lax.broadcasted_iota(jnp.int32, (2 * HD_A, 2 * HD_A), 1)
    return ((r < HD_A) == (c < HD_A)).astype(BF16)


def _swa_prompt_kernel(relb_ref, sinks_ref, bucket_ref, q_ref, ga_ref, kv_ref, qn_ref, kn_ref,
                       o_ref, knew_ref, bias_ref, kprev_ref, vprev_ref, s_ref, pe_ref):
    b = pl.program_id(0)
    i = pl.program_id(1)

    @pl.when((b == 0) & (i == 0))
    def _():
        bk = bucket_ref[...]
        for h in range(H_A):
            bias_ref[h] = jnp.full((WINDOW, 2 * WINDOW), NEG_INF, F32)
        for bb in range(NUM_BUCKETS):
            hit = bk == bb
            for h in range(H_A):
                bias_ref[h] = jnp.where(hit, relb_ref[bb, h], bias_ref[h])
        col = lax.broadcasted_iota(jnp.int32, (WINDOW, 2 * WINDOW), 1)
        for h in range(H_A):
            bias_ref[H_A + h] = jnp.where(col >= WINDOW, bias_ref[h], NEG_INF)

    @pl.when(i == 0)
    def _():
        kprev_ref[...] = jnp.zeros(kprev_ref.shape, BF16)
        vprev_ref[...] = jnp.zeros(vprev_ref.shape, BF16)

    seg_ones = _seg_ones()
    lt = 2 * HD_A
    tpg = G_A // 2
    lo_half = lax.broadcasted_iota(jnp.int32, (WINDOW, lt), 1) < HD_A
    kv = kv_ref[...].astype(F32)
    kc = _seg_rms(kv[:, :W_K], kn_ref[...], seg_ones)
    k_dup = _dup_halves(kc, lo_half)
    v_dup = _dup_halves(kv[:, W_K:], lo_half)
    base = jnp.where(i == 0, H_A, 0)
    qn2 = qn_ref[...] * HD_A ** -0.5
    gq = tpg * WINDOW
    lo_g = lax.broadcasted_iota(jnp.int32, (gq, lt), 1) < HD_A
    grows = [slice(g * 2 * gq, (g + 1) * 2 * gq) for g in range(H_A_KV)]
    for g in range(H_A_KV):
        kk = jnp.concatenate([kprev_ref[g], k_dup[g].astype(BF16)], axis=0)
        xg = jnp.concatenate([q_ref[:, (g * tpg + p) * lt:(g * tpg + p + 1) * lt] for p in range(tpg)],
                             axis=0).astype(F32)
        xn = _seg_rms(xg, qn2, seg_ones)
        q_all = jnp.concatenate([jnp.where(lo_g, xn, 0.0), jnp.where(lo_g, 0.0, xn)], axis=0).astype(BF16)
        s_ref[grows[g], :] = _dot_nt(q_all, kk)
    sink_terms = []
    for g in range(H_A_KV):
        for n in range(G_A):
            h = g * G_A + 2 * (n % tpg) + n // tpg
            rows = slice((g * G_A + n) * WINDOW, (g * G_A + n + 1) * WINDOW)
            s = s_ref[rows, :] + bias_ref[base + h]
            sink = sinks_ref[h]
            m = jnp.maximum(jnp.max(s, axis=-1, keepdims=True), sink)
            pe_ref[rows, :] = jnp.exp(s - m).astype(BF16)
            sink_terms.append(jnp.exp(sink - m))
    o_ext = []
    for g in range(H_A_KV):
        vv = jnp.concatenate([vprev_ref[g], v_dup[g].astype(BF16)], axis=0)
        vv_ext = jnp.concatenate([vv, jnp.ones(vv.shape, BF16)], axis=1)
        o_ext.append(_dot(pe_ref[grows[g], :], vv_ext))
    for g in range(H_A_KV):
        for p in range(tpg):
            halves = []
            for a in range(2):
                n = a * tpg + p
                rows = slice(n * WINDOW, (n + 1) * WINDOW)
                halves.append(o_ext[g][rows, :lt] / (o_ext[g][rows, lt:] + sink_terms[g * G_A + n]))
            oa = jnp.where(lo_half, halves[0], halves[1])
            cols = slice((g * tpg + p) * lt, (g * tpg + p + 1) * lt)
            o_ref[:, cols] = (oa * _silu(ga_ref[:, cols].astype(F32))).astype(o_ref.dtype)
    for g in range(H_A_KV):
        kprev_ref[g] = k_dup[g].astype(BF16)
        vprev_ref[g] = v_dup[g].astype(BF16)

    @pl.when(i == pl.num_programs(1) - 1)
    def _():
        knew_ref[...] = kc.T


def _swa_prompt(proj, rel_bias, sinks, qn2, kn2, bucket, *, batch, seq):
    nb = seq // WINDOW
    rb = lambda b, i: b * nb + i
    smem = pl.BlockSpec(memory_space=pltpu.SMEM)
    return pl.pallas_call(
        _swa_prompt_kernel,
        grid=(batch, nb),
        in_specs=[smem, smem,
                  pl.BlockSpec((WINDOW, 2 * WINDOW), lambda b, i: (0, 0)),
                  pl.BlockSpec((WINDOW, W_A), lambda b, i: (rb(b, i), EV_QA // W_A)),
                  pl.BlockSpec((WINDOW, W_A), lambda b, i: (rb(b, i), EV_GA // W_A)),
                  pl.BlockSpec((WINDOW, 2 * W_K), lambda b, i: (rb(b, i), EV_KV // (2 * W_K))),
                  pl.BlockSpec((1, 2 * HD_A), lambda b, i: (0, 0)),
                  pl.BlockSpec((1, 2 * HD_A), lambda b, i: (0, 0))],
        out_specs=[pl.BlockSpec((WINDOW, W_A), lambda b, i: (rb(b, i), 0)),
                   pl.BlockSpec((None, WINDOW, W_K), lambda b, i: (b, 0, 0))],
        out_shape=[jax.ShapeDtypeStruct((batch * seq, W_A), BF16),
                   jax.ShapeDtypeStruct((batch, WINDOW, W_K), F32)],
        scratch_shapes=[pltpu.VMEM((2 * H_A, WINDOW, 2 * WINDOW), F32),
                        pltpu.VMEM((H_A_KV, WINDOW, W_K), BF16), pltpu.VMEM((H_A_KV, WINDOW, W_K), BF16),
                        pltpu.VMEM((H_A * WINDOW, 2 * WINDOW), F32), pltpu.VMEM((H_A * WINDOW, 2 * WINDOW), BF16)],
        compiler_params=_cparams(("arbitrary", "arbitrary")),
        name="swa_prompt",
    )(rel_bias, sinks, bucket, proj, proj, proj, qn2, kn2)


def _swa_sample_kernel(bkc_ref, bkn_ref, relrows_ref, sinkrows_ref, q_ref, ga_ref, kvn_ref, ck_ref, cv_ref,
                       qn_ref, kn_ref, o_ref, ko_ref, vo_ref, biasc_ref, biasn_ref, *, sb, ntok, unroll):
    @pl.when(pl.program_id(0) == 0)
    def _():
        bkc = bkc_ref[...]
        bkn = bkn_ref[...]
        rr = relrows_ref[...]
        bc = jnp.full(bkc.shape, NEG_INF, F32)
        bn = jnp.full(bkn.shape, NEG_INF, F32)
        for bb in range(NUM_BUCKETS):
            val = rr[:, bb:bb + 1]
            bc = jnp.where(bkc == bb, val, bc)
            bn = jnp.where(bkn == bb, val, bn)
        biasc_ref[...] = bc
        biasn_ref[...] = bn

    seg_ones = _seg_ones()
    lt = 2 * HD_A
    nrow = G_A * SAMPLE_PAD
    grow = nrow // H_A_KV
    lo_q = lax.broadcasted_iota(jnp.int32, (grow, lt), 1) < HD_A
    lo_n = lax.broadcasted_iota(jnp.int32, (SAMPLE_PAD, lt), 1) < HD_A
    lane_pos = lax.broadcasted_iota(jnp.int32, (W_K, WINDOW), 1)
    qn2 = qn_ref[...] * HD_A ** -0.5
    kn2 = kn_ref[...]
    sink = sinkrows_ref[:, 0:1]

    def body(it, carry):
        seqs = [it * unroll + u for u in range(unroll)]
        rows = [pl.ds(pl.multiple_of(s * SAMPLE_PAD, SAMPLE_PAD), SAMPLE_PAD) for s in seqs]
        kvn = [kvn_ref[r, :] for r in rows]
        kn_all = _seg_rms(jnp.concatenate([x[:, :W_K] for x in kvn], axis=0), kn2, seg_ones)
        q2_all = jnp.concatenate([q_ref[r, p * lt:(p + 1) * lt] for r in rows for p in range(G_A)], axis=0)
        xn_all = _seg_rms(q2_all, qn2, seg_ones)
        sc, sn, kn = [], [], []
        for u, s in enumerate(seqs):
            kn.append(kn_all[u * SAMPLE_PAD:(u + 1) * SAMPLE_PAD])
            kn_dup = _dup_halves(kn[u], lo_n)
            scg, sng = [], []
            for g in range(H_A_KV):
                xn = xn_all[u * nrow + g * grow:u * nrow + (g + 1) * grow]
                q4 = jnp.concatenate([jnp.where(lo_q, xn, 0.0), jnp.where(lo_q, 0.0, xn)], axis=0).astype(BF16)
                kt_g = ck_ref[s, g * HD_A:(g + 1) * HD_A, :].astype(BF16)
                scg.append(_dot(q4, jnp.concatenate([kt_g, kt_g], axis=0)))
                sng.append(_dot_nt(q4, kn_dup[g].astype(BF16)))
            sc.append(jnp.concatenate(scg, axis=0))
            sn.append(jnp.concatenate(sng, axis=0))
        pc, pn, sink_terms = [], [], []
        for u in range(unroll):
            scu = sc[u] + biasc_ref[...]
            snu = sn[u] + biasn_ref[:, :SAMPLE_PAD]
            m = jnp.maximum(jnp.maximum(jnp.max(scu, axis=-1, keepdims=True),
                                        jnp.max(snu, axis=-1, keepdims=True)), sink)
            pc.append(jnp.exp(scu - m).astype(BF16))
            pn.append(jnp.exp(snu - m).astype(BF16))
            sink_terms.append(jnp.exp(sink - m))
        o_ext = []
        for u, s in enumerate(seqs):
            vn_dup = _dup_halves(kvn[u][:, W_K:], lo_n)
            og = []
            for g in range(H_A_KV):
                vt_g = cv_ref[s, g * HD_A:(g + 1) * HD_A, :].astype(BF16)
                vt_ext = jnp.concatenate([vt_g, vt_g, jnp.ones((lt, WINDOW), BF16)], axis=0)
                vn_ext = jnp.concatenate([vn_dup[g].astype(BF16), jnp.ones((SAMPLE_PAD, lt), BF16)], axis=1)
                gr = slice(g * 2 * grow, (g + 1) * 2 * grow)
                og.append(_dot_nt(pc[u][gr], vt_ext) + _dot(pn[u][gr], vn_ext))
            o_ext.append(jnp.concatenate(og, axis=0))
        for u, s in enumerate(seqs):
            o4 = o_ext[u][:, :lt] / (o_ext[u][:, lt:] + sink_terms[u])
            xg = ga_ref[rows[u], :]
            for p in range(G_A):
                g, pp = divmod(p, G_A // 2)
                r0 = g * 2 * grow + pp * SAMPLE_PAD
                o2 = jnp.where(lo_n, o4[r0:r0 + SAMPLE_PAD], o4[r0 + grow:r0 + grow + SAMPLE_PAD])
                o_ref[rows[u], p * lt:(p + 1) * lt] = (o2 * _silu(xg[:, p * lt:(p + 1) * lt])).astype(o_ref.dtype)
            ko_ref[s] = pltpu.roll(jnp.where(lane_pos < ntok, _rows_to_cols(kn[u]), ck_ref[s]), WINDOW - ntok, 1)
            vo_ref[s] = pltpu.roll(jnp.where(lane_pos < ntok, _rows_to_cols(kvn[u][:, W_K:]), cv_ref[s]),
                                   WINDOW - ntok, 1)
        return carry

    lax.fori_loop(0, sb // unroll, body, 0)


def _swa_sample(proj, cache_k, cache_v, bkc, bkn, relrows, sinkrows, qn2, kn2, *, sb, ntok, unroll):
    nseq = cache_k.shape[0]
    rows = sb * SAMPLE_PAD
    full = lambda shape: pl.BlockSpec(shape, lambda i: tuple(0 for _ in shape))
    cache = pl.BlockSpec((sb, W_K, WINDOW), lambda i: (i, 0, 0))
    return pl.pallas_call(
        functools.partial(_swa_sample_kernel, sb=sb, ntok=ntok, unroll=unroll),
        grid=(nseq // sb,),
        in_specs=[full(bkc.shape), full(bkn.shape), full(relrows.shape), full(sinkrows.shape),
                  pl.BlockSpec((rows, W_A), lambda i: (i, EV_QA // W_A)),
                  pl.BlockSpec((rows, W_A), lambda i: (i, EV_GA // W_A)),
                  pl.BlockSpec((rows, 2 * W_K), lambda i: (i, EV_KV // (2 * W_K))),
                  cache, cache, full((1, 2 * HD_A)), full((1, 2 * HD_A))],
        out_specs=[pl.BlockSpec((rows, W_A), lambda i: (i, 0)), cache, cache],
        out_shape=[jax.ShapeDtypeStruct((nseq * SAMPLE_PAD, W_A), F32),
                   jax.ShapeDtypeStruct(cache_k.shape, F32),
                   jax.ShapeDtypeStruct(cache_v.shape, F32)],
        scratch_shapes=[pltpu.VMEM(bkc.shape, F32), pltpu.VMEM(bkn.shape, F32)],
        compiler_params=_cparams(("arbitrary",)),
        name="swa_sample",
    )(bkc, bkn, relrows, sinkrows, proj, proj, proj, cache_k, cache_v, qn2, kn2)


def _gla_pre(q, k, v, la, n_valid):
    c = q.shape[0]
    bcum = _cumsum_rows(la)
    rr = lax.broadcasted_iota(jnp.int32, (c, c), 0)
    cc = lax.broadcasted_iota(jnp.int32, (c, c), 1)
    causal = rr >= cc
    row = lax.broadcasted_iota(jnp.int32, (c, 1), 0)
    qts, kts, kds, vbs, dcols = [], [], [], [], []
    for h in range(H_B):
        ks = slice(h * DK_B, (h + 1) * DK_B)
        bc = bcum[:, ks]
        qts.append(((q[:, ks] * DK_B ** -0.5) * jnp.exp(bc)).astype(BF16))
        kts.append((k[:, ks] * jnp.exp(-bc)).astype(BF16))
        blast = bc[n_valid - 1:n_valid, :]
        kd = k[:, ks] * jnp.exp(blast - bc)
        if n_valid < c:
            kd = jnp.where(row < n_valid, kd, 0.0)
        kds.append(kd.astype(BF16))
        vbs.append(v[:, h * DV_B:(h + 1) * DV_B].astype(BF16))
        dcols.append(_row_to_col(jnp.exp(blast)))
    scores = [_dot_nt(qts[h], kts[h]) for h in range(H_B)]
    upds = [_dot_tn(kds[h], vbs[h]) for h in range(H_B)]
    return [(qts[h], jnp.where(causal, scores[h], 0.0).astype(BF16), vbs[h], upds[h], dcols[h])
            for h in range(H_B)]


def _gla_post(pre, gb, gla_g, states):
    outs = [_dot(a, vb) + _dot(qt, st.astype(BF16)) for (qt, a, vb, _, _), st in zip(pre, states)]
    new_states = [dcol * st + upd for (_, _, _, upd, dcol), st in zip(pre, states)]
    gated = []
    for h, o in enumerate(outs):
        on = o * lax.rsqrt(jnp.mean(o * o, axis=-1, keepdims=True) + EPS) * gla_g
        gated.append(on * _silu(gb[:, h * DV_B:(h + 1) * DV_B]))
    return jnp.concatenate(gated, axis=1), new_states


def _gla_prompt_kernel(q_ref, k_ref, v_ref, gb_ref, la_ref, g_ref, o_ref, s_ref, *, nchunk):
    @pl.when(pl.program_id(1) == 0)
    def _():
        s_ref[...] = jnp.zeros(s_ref.shape, F32)

    gla_g = g_ref[...]
    chunk_rows = [slice(c * GLA_CHUNK, (c + 1) * GLA_CHUNK) for c in range(nchunk)]
    pres = [_gla_pre(q_ref[r, :].astype(F32), k_ref[r, :].astype(F32), v_ref[r, :].astype(F32), la_ref[r, :],
                     GLA_CHUNK) for r in chunk_rows]
    states = [s_ref[h] for h in range(H_B)]
    for r, pre in zip(chunk_rows, pres):
        out, states = _gla_post(pre, gb_ref[r, :].astype(F32), gla_g, states)
        o_ref[r, :] = out.astype(o_ref.dtype)
    for h in range(H_B):
        s_ref[h] = states[h]


def _gla_prompt(proj, la, gla_g, *, batch, seq, rows):
    nstep = seq // rows
    rb = lambda b, i: b * nstep + i
    return pl.pallas_call(
        functools.partial(_gla_prompt_kernel, nchunk=rows // GLA_CHUNK),
        grid=(batch, nstep),
        in_specs=[pl.BlockSpec((rows, H_B * DK_B), lambda b, i: (rb(b, i), EV_QB // (H_B * DK_B))),
                  pl.BlockSpec((rows, H_B * DK_B), lambda b, i: (rb(b, i), EV_KB // (H_B * DK_B))),
                  pl.BlockSpec((rows, W_B), lambda b, i: (rb(b, i), EV_VB // W_B)),
                  pl.BlockSpec((rows, W_B), lambda b, i: (rb(b, i), EV_GB // W_B)),
                  pl.BlockSpec((rows, H_B * DK_B), lambda b, i: (rb(b, i), 0)),
                  pl.BlockSpec((1, DV_B), lambda b, i: (0, 0))],
        out_specs=[pl.BlockSpec((rows, W_B), lambda b, i: (rb(b, i), 0)),
                   pl.BlockSpec((None, H_B, DK_B, DV_B), lambda b, i: (b, 0, 0, 0))],
        out_shape=[jax.ShapeDtypeStruct((batch * seq, W_B), BF16),
                   jax.ShapeDtypeStruct((batch, H_B, DK_B, DV_B), F32)],
        compiler_params=_cparams(("arbitrary", "arbitrary")),
        name="gla_prompt",
    )(proj, proj, proj, proj, la, gla_g)


def _gla_sample_kernel(q_ref, k_ref, v_ref, gb_ref, la_ref, g_ref, s_in_ref, o_ref, s_ref, *, sb, ntok, unroll):
    gla_g = g_ref[...]

    def body(it, carry):
        seqs = [it * unroll + u for u in range(unroll)]
        rows = [pl.ds(pl.multiple_of(s * SAMPLE_PAD, SAMPLE_PAD), SAMPLE_PAD) for s in seqs]
        pres = [_gla_pre(q_ref[r, :], k_ref[r, :], v_ref[r, :], la_ref[r, :], ntok) for r in rows]
        for s, r, pre in zip(seqs, rows, pres):
            out, new_states = _gla_post(pre, gb_ref[r, :], gla_g, [s_in_ref[s, h] for h in range(H_B)])
            o_ref[r, :] = out.astype(o_ref.dtype)
            for h in range(H_B):
                s_ref[s, h] = new_states[h]
        return carry

    lax.fori_loop(0, sb // unroll, body, 0)


def _gla_sample(proj, la, gla_g, state, *, sb, ntok, unroll):
    nseq = state.shape[0]
    rows = sb * SAMPLE_PAD
    return pl.pallas_call(
        functools.partial(_gla_sample_kernel, sb=sb, ntok=ntok, unroll=unroll),
        grid=(nseq // sb,),
        in_specs=[pl.BlockSpec((rows, H_B * DK_B), lambda i: (i, EV_QB // (H_B * DK_B))),
                  pl.BlockSpec((rows, H_B * DK_B), lambda i: (i, EV_KB // (H_B * DK_B))),
                  pl.BlockSpec((rows, W_B), lambda i: (i, EV_VB // W_B)),
                  pl.BlockSpec((rows, W_B), lambda i: (i, EV_GB // W_B)),
                  pl.BlockSpec((rows, H_B * DK_B), lambda i: (i, 0)),
                  pl.BlockSpec((1, DV_B), lambda i: (0, 0)),
                  pl.BlockSpec((sb, H_B, DK_B, DV_B), lambda i: (i, 0, 0, 0))],
        out_specs=[pl.BlockSpec((rows, W_B), lambda i: (i, 0)),
                   pl.BlockSpec((sb, H_B, DK_B, DV_B), lambda i: (i, 0, 0, 0))],
        out_shape=[jax.ShapeDtypeStruct((nseq * SAMPLE_PAD, W_B), BF16),
                   jax.ShapeDtypeStruct(state.shape, F32)],
        compiler_params=_cparams(("arbitrary",)),
        name="gla_sample",
    )(proj, proj, proj, proj, la, gla_g, state)


def _ret_decay(c, h):
    ri = lax.broadcasted_iota(jnp.int32, (c, c), 0)
    ci = lax.broadcasted_iota(jnp.int32, (c, c), 1)
    dist = (ri - ci).astype(F32)
    return jnp.where(ri >= ci, jnp.exp(jnp.maximum(dist, 0.0) * LOG_GAMMA[h]), 0.0)


def _ret_chunk(q_ref, k_ref, v_ref, g_ref, retg, state_in_ref, state_ref, o_ref, n_valid, after_head=None,
               decay_ref=None):
    c = q_ref.shape[0]
    row = lax.broadcasted_iota(jnp.int32, (c, 1), 0)
    rowf = row.astype(F32)
    a_heads = []
    for h in range(H_C):
        ks = slice(h * DK_C, (h + 1) * DK_C)
        decay = _ret_decay(c, h) if decay_ref is None else decay_ref[h]
        a_heads.append((_dot_nt(q_ref[:, ks].astype(BF16), k_ref[:, ks].astype(BF16)) * decay).astype(BF16))
    for h in range(H_C):
        lg = LOG_GAMMA[h]
        ks = slice(h * DK_C, (h + 1) * DK_C)
        vs = slice(h * DV_C, (h + 1) * DV_C)
        kf = k_ref[:, ks].astype(F32)
        vb = v_ref[:, vs].astype(BF16)
        st = state_in_ref[h]
        q_in = (q_ref[:, ks].astype(F32) * jnp.exp((rowf + 1.0) * lg)).astype(BF16)
        o = _dot(a_heads[h], vb) + _dot(q_in, st.astype(BF16))
        kd = kf * jnp.exp((n_valid - 1.0 - rowf) * lg)
        if n_valid < c:
            kd = jnp.where(row < n_valid, kd, 0.0)
        state_ref[h] = math.exp(n_valid * lg) * st + _dot_tn(kd.astype(BF16), vb)
        on = o * lax.rsqrt(jnp.mean(o * o, axis=-1, keepdims=True) + EPS) * retg
        o_ref[:, vs] = (on * _silu(g_ref[:, vs].astype(F32))).astype(o_ref.dtype)
        if after_head is not None:
            after_head(h)


def _ret_prompt_kernel(q_ref, k_ref, v_ref, g_ref, retg_ref, o_ref, s_ref, decay_ref):
    chunk = q_ref.shape[0]

    @pl.when((pl.program_id(0) == 0) & (pl.program_id(1) == 0))
    def _():
        for h in range(H_C):
            decay_ref[h] = _ret_decay(chunk, h)

    @pl.when(pl.program_id(1) == 0)
    def _():
        s_ref[...] = jnp.zeros(s_ref.shape, F32)

    _ret_chunk(q_ref, k_ref, v_ref, g_ref, retg_ref[...], s_ref, s_ref, o_ref, chunk, decay_ref=decay_ref)


def _ret_prompt(proj, ret_g, *, batch, seq):
    chunk = RET_PROMPT_CHUNK if seq % RET_PROMPT_CHUNK == 0 else RET_CHUNK
    nstep = seq // chunk
    rb = lambda b, i: b * nstep + i
    qk = H_C * DK_C
    return pl.pallas_call(
        _ret_prompt_kernel,
        grid=(batch, nstep),
        in_specs=[pl.BlockSpec((chunk, qk), lambda b, i: (rb(b, i), 0)),
                  pl.BlockSpec((chunk, qk), lambda b, i: (rb(b, i), 1)),
                  pl.BlockSpec((chunk, W_C), lambda b, i: (rb(b, i), 1)),
                  pl.BlockSpec((chunk, W_C), lambda b, i: (rb(b, i), 2)),
                  pl.BlockSpec((1, DV_C), lambda b, i: (0, 0))],
        out_specs=[pl.BlockSpec((chunk, W_C), lambda b, i: (rb(b, i), 0)),
                   pl.BlockSpec((None, H_C, DK_C, DV_C), lambda b, i: (b, 0, 0, 0))],
        out_shape=[jax.ShapeDtypeStruct((batch * seq, W_C), BF16),
                   jax.ShapeDtypeStruct((batch, H_C, DK_C, DV_C), F32)],
        scratch_shapes=[pltpu.VMEM((H_C, chunk, chunk), F32)],
        compiler_params=_cparams(("arbitrary", "arbitrary")),
        name="ret_prompt",
    )(proj, proj, proj, proj, ret_g)


def _ret_sample_job(proj, ret_g, state, *, ntok):
    nseq = state.shape[0]
    qk = H_C * DK_C
    state_spec = lambda blk: pl.BlockSpec((None, H_C, DK_C, DV_C), lambda j, i: (blk(j, i), 0, 0, 0))

    nsplit = 4
    hps = H_C // nsplit
    part_spec = lambda blk, p: pl.BlockSpec((None, hps, DK_C, DV_C), lambda j, i: (blk(j, i), p, 0, 0))

    class _SplitState:
        def __init__(self, parts):
            self.parts = parts

        def __getitem__(self, h):
            return self.parts[h // hps][h % hps]

    def body(in_refs, out_refs, host_chunk):
        q_ref, k_ref, v_ref, g_ref, retg_ref, *s_parts = in_refs
        o_ref, s_ref = out_refs
        _ret_chunk(q_ref, k_ref, v_ref, g_ref, retg_ref[...], _SplitState(s_parts), s_ref, o_ref, ntok,
                   after_head=host_chunk)

    return _SideJob(
        njobs=nseq,
        units=H_C,
        args=[proj, proj, proj, proj, ret_g] + [state] * nsplit,
        in_specs=lambda blk: [pl.BlockSpec((SAMPLE_PAD, qk), lambda j, i: (blk(j, i), 0)),
                              pl.BlockSpec((SAMPLE_PAD, qk), lambda j, i: (blk(j, i), 1)),
                              pl.BlockSpec((SAMPLE_PAD, W_C), lambda j, i: (blk(j, i), 1)),
                              pl.BlockSpec((SAMPLE_PAD, W_C), lambda j, i: (blk(j, i), 2)),
                              pl.BlockSpec((1, DV_C), lambda j, i: (0, 0))]
                             + [part_spec(blk, p) for p in range(nsplit)],
        out_specs=lambda blk: [pl.BlockSpec((SAMPLE_PAD, W_C), lambda j, i: (blk(j, i), 0)), state_spec(blk)],
        out_shape=[jax.ShapeDtypeStruct((nseq * SAMPLE_PAD, W_C), BF16),
                   jax.ShapeDtypeStruct(state.shape, F32)],
        body=body)


def _t5_bucket(dist):
    dist = np.maximum(dist, 0)
    max_exact = NUM_BUCKETS // 2
    log_ratio = (np.log(np.maximum(dist, 1).astype(np.float32) / np.float32(max_exact))
                 / np.float32(math.log(MAX_DISTANCE / max_exact)))
    large = np.minimum(max_exact + (log_ratio * np.float32(NUM_BUCKETS - max_exact)).astype(np.int32),
                       NUM_BUCKETS - 1)
    return np.where(dist < max_exact, dist, large)


def _bucket_or_masked(dist):
    return np.where((dist >= 0) & (dist <= WINDOW), _t5_bucket(dist), -1).astype(np.int32)


def _rotary_tables(pos):
    half = DK_C // 2
    inv = ROPE_BASE ** (-jnp.arange(half, dtype=F32) / half)
    ang = pos.astype(F32)[:, None] * inv[None, :]
    return jnp.cos(ang), jnp.sin(ang)


def _regroup_kernel(wt_ref, o_ref):
    o_ref[...] = wt_ref[...].T.astype(o_ref.dtype)


def _even_weight_layout(w_in):
    wt = jnp.swapaxes(w_in, 0, 1)
    src = lambda j: jnp.where(j < 4, j, jnp.where(j < 8, j + 1, jnp.where(j < 16, j + 5,
                                                                         jnp.where(j < 20, j - 7, 4))))
    main = pl.pallas_call(
        _regroup_kernel,
        grid=(EV_N // MXU_N,),
        in_specs=[pl.BlockSpec((MXU_N, D_MODEL), lambda j: (src(j), 0))],
        out_specs=pl.BlockSpec((D_MODEL, MXU_N), lambda j: (0, j)),
        out_shape=jax.ShapeDtypeStruct((D_MODEL, EV_N), BF16),
        compiler_params=_cparams(("arbitrary",)),
        name="regroup_even_weights",
    )(wt)
    lr_t = jnp.pad(wt[EV_N:], ((0, 128 - GLA_RANK), (0, 0))).astype(BF16)
    return main, lr_t


def _pad_tokens(a, ntok):
    pad = [(0, 0), (0, SAMPLE_PAD - ntok)] + [(0, 0)] * (a.ndim - 2)
    a = jnp.pad(a, pad)
    return a.reshape((a.shape[0] * SAMPLE_PAD,) + a.shape[2:])


def _rows(v, reps):
    return jnp.repeat(v, reps, axis=0)


def kernel(x_prompt, x_sample, cache_swa_k, cache_swa_v, state_gla, state_ret, c_prompt, c_sample, rel_bias,
           ada_w_even, ada_b_even, norm_g_even, w_in_even, w_lr_even, b_lr_even, qn_g_even, kn_g_even,
           sinks_even, gla_g_even, w_out_even, ada_w_odd, ada_b_odd, norm_g_odd, w_in_odd, ret_g_odd, w_out_odd):
    batch, seq, _ = x_prompt.shape
    nseq, ntok, _ = x_sample.shape
    mp = batch * seq

    c_all = jnp.concatenate([c_prompt, c_sample], axis=0)
    mod_e = _ada_mod(c_all, ada_w_even[0], ada_b_even[0])
    mod_o = _ada_mod(c_all, ada_w_odd[0], ada_b_odd[0])

    def split_mod(mod):
        shift, scale, gate = jnp.split(mod, 3, axis=1)
        p = tuple(a[:batch].reshape(batch, 1, D_MODEL) for a in (shift, scale, gate))
        s = tuple(_rows(a[batch:], SAMPLE_PAD) for a in (shift, scale, gate))
        return p, s

    (shift_ep, scale_ep, gate_ep), (shift_es, scale_es, gate_es) = split_mod(mod_e)
    (shift_op, scale_op, gate_op), (shift_os, scale_os, gate_os) = split_mod(mod_o)

    xp = x_prompt.reshape(mp, D_MODEL)
    xs = _pad_tokens(x_sample, ntok)
    ms = xs.shape[0]

    w_e, w_lri = _even_weight_layout(w_in_even[0])
    w_lr = jnp.pad(w_lr_even[0], ((0, 128 - GLA_RANK), (0, 0))).astype(BF16)
    b_lr = b_lr_even[0].reshape(1, -1)
    g_e = norm_g_even[0].reshape(1, D_MODEL)
    qn2 = jnp.tile(qn_g_even[0].reshape(1, HD_A), (1, 2))
    kn2 = jnp.tile(kn_g_even[0].reshape(1, HD_A), (1, 2))
    gla_g = gla_g_even[0].reshape(1, DV_B)
    w_out_e = w_out_even[0].astype(BF16)
    w_out_a, w_out_b = w_out_e[:W_A], w_out_e[W_A:]

    h0_p, la_p = _prenorm_even(xp, scale_ep, shift_ep, g_e, w_lri, w_lr, b_lr, tm=TILES.prenorm, rows_per_batch=seq)
    h0_s, la_s = _prenorm_even(xs, scale_es, shift_es, g_e, w_lri, w_lr, b_lr, tm=TILES.prenorm, rows_per_batch=ms)
    (proj_p,) = _inproj(h0_p, w_e, tm=TILES.inproj_prompt, tn=EV_N // 3, out_dtype=BF16, name="inproj_even")
    (proj_s,) = _inproj(h0_s, w_e, tm=TILES.inproj_sample, tn=EV_N // 3, out_dtype=F32, name="inproj_even")

    ii = np.arange(WINDOW)
    ss = np.arange(2 * WINDOW)
    bucket_p = jnp.asarray(_bucket_or_masked(WINDOW + ii[:, None] - ss[None, :]))
    mixed_a_p, k_last = _swa_prompt(proj_p, rel_bias, sinks_even[0], qn2, kn2, bucket_p, batch=batch, seq=seq)
    mixed_b_p, gla_p = _gla_prompt(proj_p, la_p, gla_g, batch=batch, seq=seq, rows=TILES.gla_rows)
    g_o = norm_g_odd[0].reshape(1, D_MODEL)
    y1_p, h1_p = _outproj([mixed_a_p, mixed_b_p], [w_out_a, w_out_b], xp, gate_ep, (scale_op, shift_op, g_o),
                          tm=TILES.outproj_prompt, rows_per_batch=seq)

    w_buf = cache_swa_k.shape[2]
    to_fp = lambda c: jnp.transpose(c[0], (0, 2, 3, 1)).reshape(nseq, W_K, w_buf)
    from_fp = lambda c: jnp.transpose(c.reshape(c.shape[0], H_A_KV, HD_A, -1), (0, 3, 1, 2))[None]
    proj_p3 = proj_p.reshape(batch, seq, EV_N)
    swa_k_p = from_fp(k_last)
    swa_v_p = proj_p3[:, seq - WINDOW:, EV_KV + W_K: EV_KV + 2 * W_K].astype(F32).reshape(
        1, batch, WINDOW, H_A_KV, HD_A)

    rr = np.arange(H_A * SAMPLE_PAD)
    row_head = 8 * (rr // 64) + 2 * ((rr % 32) // 8) + (rr % 64) // 32
    tt = rr % SAMPLE_PAD
    jj = np.arange(WINDOW)
    live = (tt < ntok)[:, None]
    bkc = jnp.asarray(np.where(live, _bucket_or_masked(WINDOW + tt[:, None] - jj[None, :]), -1))
    bkn = jnp.asarray(np.where(live & (jj[None, :] < ntok), _bucket_or_masked(tt[:, None] - jj[None, :]), -1))
    relrows = rel_bias.T[row_head]
    sinkrows = jnp.broadcast_to(sinks_even[0][row_head][:, None], (H_A * SAMPLE_PAD, 128))
    mixed_a_s, k_cache_s, v_cache_s = _swa_sample(
        proj_s, to_fp(cache_swa_k), to_fp(cache_swa_v),
        bkc, bkn, relrows, sinkrows, qn2, kn2, sb=TILES.swa_sample_seqs, ntok=ntok, unroll=TILES.sample_unroll)
    mixed_b_s, gla_s = _gla_sample(proj_s, la_s, gla_g, state_gla[0], sb=TILES.gla_sample_seqs, ntok=ntok,
                                   unroll=TILES.sample_unroll)
    y1_s, h1_s = _outproj([mixed_a_s, mixed_b_s], [w_out_a, w_out_b], xs, gate_es, (scale_os, shift_os, g_o),
                          tm=TILES.outproj_sample, rows_per_batch=ms)

    w_o = w_in_odd[0].astype(BF16)
    ret_g = ret_g_odd[0].reshape(1, DV_C)
    w_out_o = w_out_odd[0].astype(BF16)
    cos_p, sin_p = _rotary_tables(jnp.arange(seq))
    pos_s = PAST_LEN + jnp.minimum(jnp.arange(SAMPLE_PAD), ntok - 1)
    cos_s, sin_s = _rotary_tables(jnp.tile(pos_s, nseq))

    (projo_s,) = _inproj(h1_s, w_o, (cos_s, sin_s), tm=TILES.inproj_sample, tn=H_C * DK_C, out_dtype=F32,
                         name="inproj_odd")
    ret_job = _ret_sample_job(projo_s, ret_g, state_ret[0], ntok=ntok)
    projo_p, o_s, ret_s = _inproj(h1_p, w_o, (cos_p, sin_p), ret_job, tm=TILES.inproj_side, tn=H_C * DK_C,
                                  out_dtype=BF16,
                                  name="inproj_odd_ret_sample")
    o_p, ret_p = _ret_prompt(projo_p, ret_g, batch=batch, seq=seq)
    (y2_p,) = _outproj([o_p], [w_out_o], y1_p, gate_op, tm=TILES.outproj_prompt, rows_per_batch=seq)
    (y2_s,) = _outproj([o_s], [w_out_o], y1_s, gate_os, tm=TILES.outproj_sample, rows_per_batch=ms)

    y_prompt = y2_p.reshape(batch, seq, D_MODEL)
    y_sample = y2_s.reshape(nseq, SAMPLE_PAD, D_MODEL)[:, :ntok]
    return (y_prompt, y_sample, swa_k_p, swa_v_p, gla_p[None], ret_p[None],
            from_fp(k_cache_s), from_fp(v_cache_s),
            gla_s[None], ret_s[None])
```
